```python
import math
import jax
import jax.numpy as jnp
from jax import lax
import numpy as np

D_MODEL = 1024
BATCH = 4
SEQ = 4096
DEPTH = 2

CTX_LEN = 256
GRID_W = 64
NORM_EPS = 1e-6

HY_WIDTH = 1024
HY_SHORT = 3
HY_EMB = 33
HY_BANDS = (HY_EMB - 1) // 2
HY_HIDDEN = 64
HY_TARGET = 1e-2
HY_SHORT_DECAY_PCT = 0.3
HY_LONG_DECAY_PCT = 1.5

ATT_HEADS = 8
ATT_KV_HEADS = 2
HEAD_DIM = 128
ATT_WIDTH = ATT_HEADS * HEAD_DIM
Q_BLOCK = 128
ROPE_THETA = 10000.0

EVEN_MIX = HY_WIDTH + ATT_WIDTH
EVEN_SPLITS = (3 * HY_WIDTH, HY_WIDTH, ATT_WIDTH, ATT_KV_HEADS * HEAD_DIM, ATT_KV_HEADS * HEAD_DIM, ATT_WIDTH)
EVEN_IN = sum(EVEN_SPLITS)

ML_HEADS = 8
ML_QK = 128
ML_V = 256
ML_WIDTH = ML_HEADS * ML_V
ML_CHUNK = 64
ML_SHORT = 3
ODD_SPLITS = (ML_HEADS * ML_QK, ML_HEADS * ML_QK, ML_WIDTH, ML_WIDTH, ML_WIDTH, 4 * ML_HEADS)
ODD_IN = sum(ODD_SPLITS)

N_EVEN = (DEPTH + 1) // 2
N_ODD = DEPTH // 2

kernel_name = 'hybrid_hyena_gqa_mlstm_prefix_dit'


def _split(a, sizes):
    out, start = [], 0
    for s in sizes:
        out.append(a[..., start:start + s])
        start += s
    return out


def rms_norm(x, g):
    xf = x.astype(jnp.float32)
    y = xf * lax.rsqrt(jnp.mean(xf * xf, axis=-1, keepdims=True) + NORM_EPS)
    return (y * g.astype(jnp.float32)).astype(x.dtype)


def dw_conv(u, w, b):
    ch = u.shape[-1]
    k = w.shape[0]
    y = lax.conv_general_dilated(u, w[:, None, :].astype(u.dtype), window_strides=(1,),
                                 padding=[(k // 2, k // 2)], dimension_numbers=('NWC', 'WIO', 'NWC'),
                                 feature_group_count=ch)
    return y + b.astype(u.dtype)


def adaln(cond, w_mod, b_mod):
    m = jax.nn.silu(cond) @ w_mod + b_mod
    return _split(m, (D_MODEL, D_MODEL, D_MODEL))


def axial_rope(x, rows, cols):
    f32 = jnp.float32
    half = x.shape[-1] // 2
    nf = half // 2
    inv = ROPE_THETA ** (-jnp.arange(nf, dtype=f32) / nf)

    def rot(xa, pos):
        ang = pos.astype(f32)[:, None] * inv
        cos = jnp.cos(ang)[None, :, None, :]
        sin = jnp.sin(ang)[None, :, None, :]
        x1 = xa[..., :nf].astype(f32)
        x2 = xa[..., nf:].astype(f32)
        return jnp.concatenate([x1 * cos - x2 * sin, x1 * sin + x2 * cos], axis=-1)

    out = jnp.concatenate([rot(x[..., :half], rows), rot(x[..., half:], cols)], axis=-1)
    return out.astype(x.dtype)


def hyena_filter(n, w1, b1, freq, w2, b2, w3):
    f32 = jnp.float32
    t = jnp.linspace(0.0, 1.0, n, dtype=f32)[:, None]
    bands = jnp.linspace(1e-4, HY_BANDS - 1, HY_BANDS, dtype=f32)
    ang = (2.0 * math.pi / n) * jnp.arange(n, dtype=f32)[:, None] * bands
    z = jnp.concatenate([t, jnp.cos(ang), -jnp.sin(ang)], axis=-1)
    fr = freq.astype(f32)
    hdn = jnp.sin(fr * (z @ w1.astype(f32) + b1.astype(f32)))
    hdn = jnp.sin(fr * (hdn @ w2.astype(f32) + b2.astype(f32)))
    h = (hdn @ w3.astype(f32)).reshape(n, 2, HY_WIDTH)
    max_decay = math.log(HY_TARGET) / HY_SHORT_DECAY_PCT
    min_decay = math.log(HY_TARGET) / HY_LONG_DECAY_PCT
    deltas = jnp.linspace(min_decay, max_decay, HY_WIDTH, dtype=f32)
    h = h * jnp.exp(-t * jnp.abs(deltas))[:, None, :]
    two_sided = jnp.concatenate([h[:, 0], jnp.zeros((1, HY_WIDTH), f32), h[:0:-1, 1]], axis=0)
    return two_sided / jnp.sum(jnp.abs(two_sided), axis=0, keepdims=True)


def bidir_long_conv(u, filt):
    n = u.shape[1]
    uf = jnp.fft.rfft(u.astype(jnp.float32), n=2 * n, axis=1)
    ff = jnp.fft.rfft(filt, axis=0)
    return jnp.fft.irfft(uf * ff[None], n=2 * n, axis=1)[:, :n].astype(u.dtype)


def hyena_mixer(xv, conv_w, conv_b, f_w1, f_b1, f_freq, f_w2, f_b2, f_w3, hy_bias):
    n = xv.shape[1]
    x0, x1, v = _split(dw_conv(xv, conv_w, conv_b), (HY_WIDTH, HY_WIDTH, HY_WIDTH))
    filt = hyena_filter(n, f_w1, f_b1, f_freq, f_w2, f_b2, f_w3)
    g = v * x1
    return x0 * (bidir_long_conv(g, filt) + g * hy_bias.astype(g.dtype))


def block_attention(q, k, v):
    B, n, H, Dh = q.shape
    kvh = k.shape[2]
    grp = H // kvh
    nb = n // Q_BLOCK
    qb = jnp.moveaxis(q.reshape(B, nb, Q_BLOCK, kvh, grp, Dh), 1, 0)
    scale = Dh ** -0.5

    def attend(qblk):
        s = jnp.einsum('bqkgd,btkd->bkgqt', qblk, k, preferred_element_type=jnp.float32) * scale
        p = jax.nn.softmax(s, axis=-1).astype(v.dtype)
        return jnp.einsum('bkgqt,btkd->bqkgd', p, v)

    out = lax.map(attend, qb)
    return jnp.moveaxis(out, 0, 1).reshape(B, n, H * Dh)


def even_mixer(h_lat, h_ctx, rows, cols, need_ctx, w_in, conv_w, conv_b, f_w1, f_b1, f_freq,
               f_w2, f_b2, f_w3, hy_bias, q_norm, k_norm, w_out):
    def project(h):
        B, n = h.shape[:2]
        xv, g_hy, q, k, v, g_att = _split(h @ w_in, EVEN_SPLITS)
        q = rms_norm(q.reshape(B, n, ATT_HEADS, HEAD_DIM), q_norm)
        k = rms_norm(k.reshape(B, n, ATT_KV_HEADS, HEAD_DIM), k_norm)
        v = v.reshape(B, n, ATT_KV_HEADS, HEAD_DIM)
        return xv, g_hy, q, k, v, g_att

    def hyena(xv):
        return hyena_mixer(xv, conv_w, conv_b, f_w1, f_b1, f_freq, f_w2, f_b2, f_w3, hy_bias)

    def combine(hy, att, g_hy, g_att):
        return jnp.concatenate([hy * jax.nn.silu(g_hy), att * jax.nn.silu(g_att)], axis=-1) @ w_out

    xv_l, gh_l, q_l, k_l, v_l, ga_l = project(h_lat)
    xv_c, gh_c, q_c, k_c, v_c, ga_c = project(h_ctx)
    q_l = axial_rope(q_l, rows, cols)
    k_l = axial_rope(k_l, rows, cols)
    att_l = block_attention(q_l, jnp.concatenate([k_l, k_c], axis=1), jnp.concatenate([v_l, v_c], axis=1))
    y_lat = combine(hyena(xv_l), att_l, gh_l, ga_l)
    y_ctx = None
    if need_ctx:
        y_ctx = combine(hyena(xv_c), block_attention(q_c, k_c, v_c), gh_c, ga_c)
    return y_lat, y_ctx


def mlstm_chunkwise(q, k, v, i_pre, f_pre, state):
    f32 = jnp.float32
    B, H, n, _ = q.shape
    nc = n // ML_CHUNK

    def chunks(a):
        a = a.astype(f32)
        return jnp.moveaxis(a.reshape(B, H, nc, ML_CHUNK, *a.shape[3:]), 2, 0)

    lower = jnp.tril(jnp.ones((ML_CHUNK, ML_CHUNK), dtype=bool))

    def step(carry, inp):
        C, nvec, m = carry
        qc, kc, vc, ic, lfc = inp
        b = jnp.cumsum(lfc, axis=-1)
        d = jnp.where(lower, b[..., :, None] - b[..., None, :] + ic[..., None, :], -jnp.inf)
        inter = b + m[..., None]
        m_row = jnp.maximum(inter, jnp.max(d, axis=-1))
        s = jnp.einsum('bhtd,bhsd->bhts', qc, kc) * jnp.exp(d - m_row[..., None])
        w_prev = jnp.exp(inter - m_row)
        num = jnp.einsum('bhts,bhsv->bhtv', s, vc) + w_prev[..., None] * jnp.einsum('bhvd,bhtd->bhtv', C, qc)
        den = jnp.sum(s, axis=-1) + w_prev * jnp.einsum('bhd,bhtd->bht', nvec, qc)
        h = num / jnp.maximum(jnp.abs(den), jnp.exp(-m_row))[..., None]
        b_end = b[..., -1]
        g = b_end[..., None] - b + ic
        m_new = jnp.maximum(b_end + m, jnp.max(g, axis=-1))
        a = jnp.exp(g - m_new[..., None])
        a_prev = jnp.exp(b_end + m - m_new)
        C = a_prev[..., None, None] * C + jnp.einsum('bhsv,bhsd->bhvd', vc * a[..., None], kc)
        nvec = a_prev[..., None] * nvec + jnp.einsum('bhs,bhsd->bhd', a, kc)
        return (C, nvec, m_new), h

    xs = (chunks(q), chunks(k), chunks(v), chunks(i_pre), chunks(jax.nn.log_sigmoid(f_pre.astype(f32))))
    state, hs = lax.scan(step, state, xs)
    return state, jnp.moveaxis(hs, 0, 2).reshape(B, H, n, v.shape[-1])


def odd_mixer(h_lat, h_ctx, need_ctx, w_in, conv_w, conv_b, gate_b, head_norm, w_out):
    f32 = jnp.float32

    def project(h):
        B, n = h.shape[:2]
        q, k, v, o, z, gates = _split(h @ w_in, ODD_SPLITS)
        qk = jax.nn.silu(dw_conv(jnp.concatenate([q, k], axis=-1), conv_w, conv_b))
        q, k = _split(qk, (ML_HEADS * ML_QK, ML_HEADS * ML_QK))
        q = q.reshape(B, n, ML_HEADS, ML_QK).transpose(0, 2, 1, 3)
        k = k.reshape(B, n, ML_HEADS, ML_QK).transpose(0, 2, 1, 3) * (ML_QK ** -0.5)
        v = v.reshape(B, n, ML_HEADS, ML_V).transpose(0, 2, 1, 3)
        gates = (gates + gate_b).astype(f32).reshape(B, n, 4, ML_HEADS).transpose(2, 0, 3, 1)
        return q, k, v, o, z, gates

    def flip(a):
        return jnp.flip(a, axis=2)

    ql, kl, vl, ol, zl, gl = project(h_lat)
    qc, kc, vc, oc, zc, gc = project(h_ctx)
    B = h_lat.shape[0]
    zero = (jnp.zeros((B, ML_HEADS, ML_V, ML_QK), f32), jnp.zeros((B, ML_HEADS, ML_QK), f32),
            jnp.zeros((B, ML_HEADS), f32))
    st_f, hc_f = mlstm_chunkwise(qc, kc, vc, gc[0], gc[1], zero)
    _, hl_f = mlstm_chunkwise(ql, kl, vl, gl[0], gl[1], st_f)
    st_b, hc_b = mlstm_chunkwise(flip(qc), flip(kc), flip(vc), flip(gc[2]), flip(gc[3]), zero)
    _, hl_b = mlstm_chunkwise(flip(ql), flip(kl), flip(vl), flip(gl[2]), flip(gl[3]), st_b)

    def finish(h_f, h_b, o, z):
        Bn, _, n, _ = h_f.shape
        h = (h_f + h_b).transpose(0, 2, 1, 3).reshape(Bn, n, ML_WIDTH).astype(o.dtype) * jax.nn.sigmoid(o)
        h = rms_norm(h.reshape(Bn, n, ML_HEADS, ML_V), head_norm.reshape(ML_HEADS, ML_V)).reshape(Bn, n, ML_WIDTH)
        return (h * jax.nn.silu(z)) @ w_out

    y_lat = finish(hl_f, flip(hl_b), ol, zl)
    y_ctx = finish(hc_f, flip(hc_b), oc, zc) if need_ctx else None
    return y_lat, y_ctx


def setup_inputs(seed: int = 0) -> dict:
    key = jax.random.key(seed)
    keys = jax.random.split(key, 32)
    f32 = jnp.float32

    def nrm(i, shape, scale):
        return jax.random.normal(keys[i], shape, f32) * scale

    D = D_MODEL
    forget_init = jnp.linspace(3.0, 6.0, ML_HEADS, dtype=f32)
    o_gate_b = jnp.concatenate([
        nrm(24, (N_ODD, ML_HEADS), 0.1),
        forget_init + nrm(25, (N_ODD, ML_HEADS), 0.1),
        nrm(26, (N_ODD, ML_HEADS), 0.1),
        forget_init + nrm(27, (N_ODD, ML_HEADS), 0.1)], axis=-1)
    return {
        'x': nrm(0, (BATCH, SEQ, D), 1.0),
        'c': nrm(1, (BATCH, D), 1.0),
        'ctx': nrm(2, (BATCH, CTX_LEN, D), 1.0),
        'c_ctx': nrm(3, (D,), 1.0),
        'w_mod': nrm(4, (DEPTH, D, 3 * D), D ** -0.5),
        'b_mod': nrm(5, (DEPTH, 3 * D), 0.02),
        'g_pre': 1.0 + nrm(6, (DEPTH, D), 0.02),
        'g_post': 1.0 + nrm(7, (DEPTH, D), 0.02),
        'e_w_in': nrm(8, (N_EVEN, D, EVEN_IN), D ** -0.5),
        'e_conv_w': nrm(9, (N_EVEN, HY_SHORT, 3 * HY_WIDTH), HY_SHORT ** -0.5),
        'e_conv_b': nrm(10, (N_EVEN, 3 * HY_WIDTH), 0.02),
        'e_filt_w1': nrm(11, (N_EVEN, HY_EMB, HY_HIDDEN), HY_EMB ** -0.5),
        'e_filt_b1': nrm(12, (N_EVEN, HY_HIDDEN), 0.1),
        'e_filt_freq': 1.0 + nrm(13, (N_EVEN, HY_HIDDEN), 0.02),
        'e_filt_w2': nrm(14, (N_EVEN, HY_HIDDEN, HY_HIDDEN), HY_HIDDEN ** -0.5),
        'e_filt_b2': nrm(15, (N_EVEN, HY_HIDDEN), 0.1),
        'e_filt_w3': nrm(16, (N_EVEN, HY_HIDDEN, 2 * HY_WIDTH), HY_HIDDEN ** -0.5),
        'e_hy_bias': nrm(17, (N_EVEN, HY_WIDTH), 0.1),
        'e_q_norm': 1.0 + nrm(18, (N_EVEN, HEAD_DIM), 0.02),
        'e_k_norm': 1.0 + nrm(19, (N_EVEN, HEAD_DIM), 0.02),
        'e_w_out': nrm(20, (N_EVEN, EVEN_MIX, D), EVEN_MIX ** -0.5),
        'o_w_in': nrm(21, (N_ODD, D, ODD_IN), D ** -0.5),
        'o_conv_w': nrm(22, (N_ODD, ML_SHORT, 2 * ML_HEADS * ML_QK), ML_SHORT ** -0.5),
        'o_conv_b': nrm(23, (N_ODD, 2 * ML_HEADS * ML_QK), 0.02),
        'o_gate_b': o_gate_b,
        'o_head_norm': 1.0 + nrm(28, (N_ODD, ML_WIDTH), 0.02),
        'o_w_out': nrm(29, (N_ODD, ML_WIDTH, D), ML_WIDTH ** -0.5),
    }


def reference(x, c, ctx, c_ctx, w_mod, b_mod, g_pre, g_post, e_w_in, e_conv_w, e_conv_b, e_filt_w1,
              e_filt_b1, e_filt_freq, e_filt_w2, e_filt_b2, e_filt_w3, e_hy_bias, e_q_norm, e_k_norm,
              e_w_out, o_w_in, o_conv_w, o_conv_b, o_gate_b, o_head_norm, o_w_out):
    n_lat = x.shape[1]
    ROWS = n_lat // GRID_W
    rows = jnp.repeat(jnp.arange(ROWS, dtype=jnp.int32), GRID_W)
    cols = jnp.tile(jnp.arange(GRID_W, dtype=jnp.int32), ROWS)
    x_lat, x_ctx = x, ctx
    for layer in range(DEPTH):
        need_ctx = layer < DEPTH - 1
        sh_l, sc_l, ga_l = adaln(c, w_mod[layer], b_mod[layer])
        sh_c, sc_c, ga_c = adaln(c_ctx, w_mod[layer], b_mod[layer])
        h_lat = rms_norm(x_lat, g_pre[layer]) * (1 + sc_l[:, None]) + sh_l[:, None]
        h_ctx = rms_norm(x_ctx, g_pre[layer]) * (1 + sc_c) + sh_c
        j = layer // 2
        if layer % 2 == 0:
            y_lat, y_ctx = even_mixer(h_lat, h_ctx, rows, cols, need_ctx, e_w_in[j], e_conv_w[j], e_conv_b[j],
                                      e_filt_w1[j], e_filt_b1[j], e_filt_freq[j], e_filt_w2[j], e_filt_b2[j],
                                      e_filt_w3[j], e_hy_bias[j], e_q_norm[j], e_k_norm[j], e_w_out[j])
        else:
            y_lat, y_ctx = odd_mixer(h_lat, h_ctx, need_ctx, o_w_in[j], o_conv_w[j], o_conv_b[j], o_gate_b[j],
                                     o_head_norm[j], o_w_out[j])
        x_lat = x_lat + ga_l[:, None] * rms_norm(y_lat, g_post[layer])
        if need_ctx:
            x_ctx = x_ctx + ga_c * rms_norm(y_ctx, g_post[layer])
    return x_lat
```

```python
import functools
import math

import jax
import jax.numpy as jnp
from jax import lax
from jax.experimental import pallas as pl
from jax.experimental.pallas import tpu as pltpu

D_MODEL = 1024
GRID_W = 64
NORM_EPS = 1e-6

HY_WIDTH = 1024
HY_EMB = 33
HY_BANDS = (HY_EMB - 1) // 2
HY_TARGET = 1e-2
HY_SHORT_DECAY_PCT = 0.3
HY_LONG_DECAY_PCT = 1.5

ATT_HEADS = 8
ATT_KV_HEADS = 2
HEAD_DIM = 128
ATT_WIDTH = ATT_HEADS * HEAD_DIM
Q_BLOCK = 128
ROPE_THETA = 10000.0
EVEN_SPLITS = (3 * HY_WIDTH, HY_WIDTH, ATT_WIDTH, ATT_KV_HEADS * HEAD_DIM, ATT_KV_HEADS * HEAD_DIM, ATT_WIDTH)

ML_HEADS = 8
ML_QK = 128
ML_V = 256
ML_WIDTH = ML_HEADS * ML_V
ML_CHUNK = 64
ODD_SPLITS = (ML_HEADS * ML_QK, ML_HEADS * ML_QK, ML_WIDTH, ML_WIDTH, ML_WIDTH, 4 * ML_HEADS)

VMEM_LIMIT_BYTES = 48 * 1024 * 1024


def _split(a, sizes):
    out, start = [], 0
    for s in sizes:
        out.append(a[..., start:start + s])
        start += s
    return out


def rms_norm(x, g):
    xf = x.astype(jnp.float32)
    y = xf * lax.rsqrt(jnp.mean(xf * xf, axis=-1, keepdims=True) + NORM_EPS)
    return (y * g.astype(jnp.float32)).astype(x.dtype)


def _norm_proj_kernel(x_ref, g_ref, sc_ref, sh_ref, w_ref, o_ref, h_ref):
    @pl.when(pl.program_id(1) == 0)
    def _():
        x = x_ref[...]
        y = x * lax.rsqrt(jnp.mean(x * x, axis=-1, keepdims=True) + NORM_EPS)
        h = y * g_ref[...] * (1.0 + sc_ref[0]) + sh_ref[0]
        h_ref[...] = h.astype(h_ref.dtype)

    o_ref[...] = jnp.dot(h_ref[...], w_ref[...], preferred_element_type=jnp.float32)


def _col_tile(n):
    for t in (1024, 896, 768, 640, 512, 384, 256, 128):
        if n % t == 0:
            return t
    raise ValueError(n)


def norm_proj(x, g, scale, shift, w, *, tm=512):
    B, n, D = x.shape
    n_out = w.shape[1]
    N = -(-n_out // 128) * 128
    w = jnp.pad(w.astype(jnp.bfloat16), ((0, 0), (0, N - n_out)))
    tm = min(tm, n)
    tn = _col_tile(N)
    per_b = n // tm
    x2 = x.reshape(B * n, D)
    out = pl.pallas_call(
        _norm_proj_kernel,
        grid=(B * n // tm, N // tn),
        in_specs=[
            pl.BlockSpec((tm, D), lambda i, j: (i, 0)),
            pl.BlockSpec((1, D), lambda i, j: (0, 0)),
            pl.BlockSpec((1, 1, D), lambda i, j: (i // per_b, 0, 0)),
            pl.BlockSpec((1, 1, D), lambda i, j: (i // per_b, 0, 0)),
            pl.BlockSpec((D, tn), lambda i, j: (0, j)),
        ],
        out_specs=pl.BlockSpec((tm, tn), lambda i, j: (i, j)),
        out_shape=jax.ShapeDtypeStruct((B * n, N), jnp.float32),
        scratch_shapes=[pltpu.VMEM((tm, D), jnp.bfloat16)],
        compiler_params=pltpu.CompilerParams(
            dimension_semantics=("arbitrary", "arbitrary"), vmem_limit_bytes=VMEM_LIMIT_BYTES),
        name="norm_proj",
    )(x2, g.reshape(1, D), scale.reshape(B, 1, D), shift.reshape(B, 1, D), w)
    return out.reshape(B, n, N)[..., :n_out]


def dw_conv(u, w, b):
    ch = u.shape[-1]
    k = w.shape[0]
    y = lax.conv_general_dilated(u, w[:, None, :].astype(u.dtype), window_strides=(1,),
                                 padding=[(k // 2, k // 2)], dimension_numbers=('NWC', 'WIO', 'NWC'),
                                 feature_group_count=ch)
    return y + b.astype(u.dtype)


def adaln(cond, w_mod, b_mod):
    m = jax.nn.silu(cond) @ w_mod + b_mod
    return _split(m, (D_MODEL, D_MODEL, D_MODEL))


def axial_rope(x, rows, cols):
    f32 = jnp.float32
    half = x.shape[-1] // 2
    nf = half // 2
    inv = ROPE_THETA ** (-jnp.arange(nf, dtype=f32) / nf)

    def rot(xa, pos):
        ang = pos.astype(f32)[:, None] * inv
        cos = jnp.cos(ang)[None, :, None, :]
        sin = jnp.sin(ang)[None, :, None, :]
        x1 = xa[..., :nf].astype(f32)
        x2 = xa[..., nf:].astype(f32)
        return jnp.concatenate([x1 * cos - x2 * sin, x1 * sin + x2 * cos], axis=-1)

    out = jnp.concatenate([rot(x[..., :half], rows), rot(x[..., half:], cols)], axis=-1)
    return out.astype(x.dtype)


def hyena_filter(n, w1, b1, freq, w2, b2, w3):
    f32 = jnp.float32
    t = jnp.linspace(0.0, 1.0, n, dtype=f32)[:, None]
    bands = jnp.linspace(1e-4, HY_BANDS - 1, HY_BANDS, dtype=f32)
    ang = (2.0 * math.pi / n) * jnp.arange(n, dtype=f32)[:, None] * bands
    z = jnp.concatenate([t, jnp.cos(ang), -jnp.sin(ang)], axis=-1)
    fr = freq.astype(f32)
    hdn = jnp.sin(fr * (z @ w1.astype(f32) + b1.astype(f32)))
    hdn = jnp.sin(fr * (hdn @ w2.astype(f32) + b2.astype(f32)))
    h = (hdn @ w3.astype(f32)).reshape(n, 2, HY_WIDTH)
    max_decay = math.log(HY_TARGET) / HY_SHORT_DECAY_PCT
    min_decay = math.log(HY_TARGET) / HY_LONG_DECAY_PCT
    deltas = jnp.linspace(min_decay, max_decay, HY_WIDTH, dtype=f32)
    h = h * jnp.exp(-t * jnp.abs(deltas))[:, None, :]
    two_sided = jnp.concatenate([h[:, 0], jnp.zeros((1, HY_WIDTH), f32), h[:0:-1, 1]], axis=0)
    return two_sided / jnp.sum(jnp.abs(two_sided), axis=0, keepdims=True)


def bidir_long_conv(u, filt):
    n = u.shape[1]
    uf = jnp.fft.rfft(u.astype(jnp.float32), n=2 * n, axis=1)
    ff = jnp.fft.rfft(filt, axis=0)
    return jnp.fft.irfft(uf * ff[None], n=2 * n, axis=1)[:, :n].astype(u.dtype)


def hyena_mixer(xv, conv_w, conv_b, f_w1, f_b1, f_freq, f_w2, f_b2, f_w3, hy_bias):
    n = xv.shape[1]
    x0, x1, v = _split(dw_conv(xv, conv_w, conv_b), (HY_WIDTH, HY_WIDTH, HY_WIDTH))
    filt = hyena_filter(n, f_w1, f_b1, f_freq, f_w2, f_b2, f_w3)
    g = v * x1
    return x0 * (bidir_long_conv(g, filt) + g * hy_bias.astype(g.dtype))


def block_attention(q, k, v):
    B, n, H, Dh = q.shape
    kvh = k.shape[2]
    grp = H // kvh
    nb = n // Q_BLOCK
    qb = jnp.moveaxis(q.reshape(B, nb, Q_BLOCK, kvh, grp, Dh), 1, 0)
    scale = Dh ** -0.5

    def attend(qblk):
        s = jnp.einsum('bqkgd,btkd->bkgqt', qblk, k, preferred_element_type=jnp.float32) * scale
        p = jax.nn.softmax(s, axis=-1).astype(v.dtype)
        return jnp.einsum('bkgqt,btkd->bqkgd', p, v)

    out = lax.map(attend, qb)
    return jnp.moveaxis(out, 0, 1).reshape(B, n, H * Dh)


def even_mixer(p_lat, p_ctx, rows, cols, need_ctx, conv_w, conv_b, f_w1, f_b1, f_freq,
               f_w2, f_b2, f_w3, hy_bias, q_norm, k_norm, w_out):
    def unpack(p):
        B, n = p.shape[:2]
        xv, g_hy, q, k, v, g_att = _split(p, EVEN_SPLITS)
        q = rms_norm(q.reshape(B, n, ATT_HEADS, HEAD_DIM), q_norm)
        k = rms_norm(k.reshape(B, n, ATT_KV_HEADS, HEAD_DIM), k_norm)
        v = v.reshape(B, n, ATT_KV_HEADS, HEAD_DIM)
        return xv, g_hy, q, k, v, g_att

    def hyena(xv):
        return hyena_mixer(xv, conv_w, conv_b, f_w1, f_b1, f_freq, f_w2, f_b2, f_w3, hy_bias)

    def combine(hy, att, g_hy, g_att):
        return jnp.concatenate([hy * jax.nn.silu(g_hy), att * jax.nn.silu(g_att)], axis=-1) @ w_out

    xv_l, gh_l, q_l, k_l, v_l, ga_l = unpack(p_lat)
    xv_c, gh_c, q_c, k_c, v_c, ga_c = unpack(p_ctx)
    q_l = axial_rope(q_l, rows, cols)
    k_l = axial_rope(k_l, rows, cols)
    att_l = block_attention(q_l, jnp.concatenate([k_l, k_c], axis=1), jnp.concatenate([v_l, v_c], axis=1))
    y_lat = combine(hyena(xv_l), att_l, gh_l, ga_l)
    y_ctx = None
    if need_ctx:
        y_ctx = combine(hyena(xv_c), block_attention(q_c, k_c, v_c), gh_c, ga_c)
    return y_lat, y_ctx


def mlstm_chunkwise(q, k, v, i_pre, f_pre, state):
    f32 = jnp.float32
    B, H, n, _ = q.shape
    nc = n // ML_CHUNK

    def chunks(a):
        a = a.astype(f32)
        return jnp.moveaxis(a.reshape(B, H, nc, ML_CHUNK, *a.shape[3:]), 2, 0)

    lower = jnp.tril(jnp.ones((ML_CHUNK, ML_CHUNK), dtype=bool))

    def step(carry, inp):
        C, nvec, m = carry
        qc, kc, vc, ic, lfc = inp
        b = jnp.cumsum(lfc, axis=-1)
        d = jnp.where(lower, b[..., :, None] - b[..., None, :] + ic[..., None, :], -jnp.inf)
        inter = b + m[..., None]
        m_row = jnp.maximum(inter, jnp.max(d, axis=-1))
        s = jnp.einsum('bhtd,bhsd->bhts', qc, kc) * jnp.exp(d - m_row[..., None])
        w_prev = jnp.exp(inter - m_row)
        num = jnp.einsum('bhts,bhsv->bhtv', s, vc) + w_prev[..., None] * jnp.einsum('bhvd,bhtd->bhtv', C, qc)
        den = jnp.sum(s, axis=-1) + w_prev * jnp.einsum('bhd,bhtd->bht', nvec, qc)
        h = num / jnp.maximum(jnp.abs(den), jnp.exp(-m_row))[..., None]
        b_end = b[..., -1]
        g = b_end[..., None] - b + ic
        m_new = jnp.maximum(b_end + m, jnp.max(g, axis=-1))
        a = jnp.exp(g - m_new[..., None])
        a_prev = jnp.exp(b_end + m - m_new)
        C = a_prev[..., None, None] * C + jnp.einsum('bhsv,bhsd->bhvd', vc * a[..., None], kc)
        nvec = a_prev[..., None] * nvec + jnp.einsum('bhs,bhsd->bhd', a, kc)
        return (C, nvec, m_new), h

    xs = (chunks(q), chunks(k), chunks(v), chunks(i_pre), chunks(jax.nn.log_sigmoid(f_pre.astype(f32))))
    state, hs = lax.scan(step, state, xs)
    return state, jnp.moveaxis(hs, 0, 2).reshape(B, H, n, v.shape[-1])


def odd_mixer(p_lat, p_ctx, need_ctx, conv_w, conv_b, gate_b, head_norm, w_out):
    f32 = jnp.float32

    def unpack(p):
        B, n = p.shape[:2]
        q, k, v, o, z, gates = _split(p, ODD_SPLITS)
        qk = jax.nn.silu(dw_conv(jnp.concatenate([q, k], axis=-1), conv_w, conv_b))
        q, k = _split(qk, (ML_HEADS * ML_QK, ML_HEADS * ML_QK))
        q = q.reshape(B, n, ML_HEADS, ML_QK).transpose(0, 2, 1, 3)
        k = k.reshape(B, n, ML_HEADS, ML_QK).transpose(0, 2, 1, 3) * (ML_QK ** -0.5)
        v = v.reshape(B, n, ML_HEADS, ML_V).transpose(0, 2, 1, 3)
        gates = (gates + gate_b).astype(f32).reshape(B, n, 4, ML_HEADS).transpose(2, 0, 3, 1)
        return q, k, v, o, z, gates

    def flip(a):
        return jnp.flip(a, axis=2)

    ql, kl, vl, ol, zl, gl = unpack(p_lat)
    qc, kc, vc, oc, zc, gc = unpack(p_ctx)
    B = p_lat.shape[0]
    zero = (jnp.zeros((B, ML_HEADS, ML_V, ML_QK), f32), jnp.zeros((B, ML_HEADS, ML_QK), f32),
            jnp.zeros((B, ML_HEADS), f32))
    st_f, hc_f = mlstm_chunkwise(qc, kc, vc, gc[0], gc[1], zero)
    _, hl_f = mlstm_chunkwise(ql, kl, vl, gl[0], gl[1], st_f)
    st_b, hc_b = mlstm_chunkwise(flip(qc), flip(kc), flip(vc), flip(gc[2]), flip(gc[3]), zero)
    _, hl_b = mlstm_chunkwise(flip(ql), flip(kl), flip(vl), flip(gl[2]), flip(gl[3]), st_b)

    def finish(h_f, h_b, o, z):
        Bn, _, n, _ = h_f.shape
        h = (h_f + h_b).transpose(0, 2, 1, 3).reshape(Bn, n, ML_WIDTH).astype(o.dtype) * jax.nn.sigmoid(o)
        h = rms_norm(h.reshape(Bn, n, ML_HEADS, ML_V), head_norm.reshape(ML_HEADS, ML_V)).reshape(Bn, n, ML_WIDTH)
        return (h * jax.nn.silu(z)) @ w_out

    y_lat = finish(hl_f, flip(hl_b), ol, zl)
    y_ctx = finish(hc_f, flip(hc_b), oc, zc) if need_ctx else None
    return y_lat, y_ctx


def kernel(x, c, ctx, c_ctx, w_mod, b_mod, g_pre, g_post, e_w_in, e_conv_w, e_conv_b, e_filt_w1,
           e_filt_b1, e_filt_freq, e_filt_w2, e_filt_b2, e_filt_w3, e_hy_bias, e_q_norm, e_k_norm,
           e_w_out, o_w_in, o_conv_w, o_conv_b, o_gate_b, o_head_norm, o_w_out):
    B, n_lat, D = x.shape
    depth = w_mod.shape[0]
    n_rows = n_lat // GRID_W
    rows = jnp.repeat(jnp.arange(n_rows, dtype=jnp.int32), GRID_W)
    cols = jnp.tile(jnp.arange(GRID_W, dtype=jnp.int32), n_rows)
    x_lat, x_ctx = x, ctx
    for layer in range(depth):
        need_ctx = layer < depth - 1
        sh_l, sc_l, ga_l = adaln(c, w_mod[layer], b_mod[layer])
        sh_c, sc_c, ga_c = adaln(c_ctx, w_mod[layer], b_mod[layer])
        j = layer // 2
        w_in = e_w_in[j] if layer % 2 == 0 else o_w_in[j]
        p_lat = norm_proj(x_lat, g_pre[layer], sc_l, sh_l, w_in)
        p_ctx = norm_proj(x_ctx, g_pre[layer], jnp.broadcast_to(sc_c, (B, D)), jnp.broadcast_to(sh_c, (B, D)), w_in)
        if layer % 2 == 0:
            y_lat, y_ctx = even_mixer(p_lat, p_ctx, rows, cols, need_ctx, e_conv_w[j], e_conv_b[j],
                                      e_filt_w1[j], e_filt_b1[j], e_filt_freq[j], e_filt_w2[j], e_filt_b2[j],
                                      e_filt_w3[j], e_hy_bias[j], e_q_norm[j], e_k_norm[j], e_w_out[j])
        else:
            y_lat, y_ctx = odd_mixer(p_lat, p_ctx, need_ctx, o_conv_w[j], o_conv_b[j], o_gate_b[j],
                                     o_head_norm[j], o_w_out[j])
        x_lat = x_lat + ga_l[:, None] * rms_norm(y_lat, g_post[layer])
        if need_ctx:
            x_ctx = x_ctx + ga_c * rms_norm(y_ctx, g_post[layer])
    return x_lat
```

```python
import functools
import math

import numpy as np
import jax
import jax.numpy as jnp
from jax import lax
from jax.experimental import pallas as pl
from jax.experimental.pallas import tpu as pltpu

f32 = jnp.float32
bf16 = jnp.bfloat16
HIGHEST = lax.Precision.HIGHEST

D_MODEL = 1024
GRID_W = 64
NORM_EPS = 1e-6

HY_WIDTH = 1024
HY_EMB = 33
HY_BANDS = (HY_EMB - 1) // 2
HY_HIDDEN = 64
HY_TARGET = 1e-2
HY_SHORT_DECAY_PCT = 0.3
HY_LONG_DECAY_PCT = 1.5

ATT_HEADS = 8
ATT_KV_HEADS = 2
ATT_GROUP = ATT_HEADS // ATT_KV_HEADS
HEAD_DIM = 128
ATT_WIDTH = ATT_HEADS * HEAD_DIM
ATT_KV_WIDTH = ATT_KV_HEADS * HEAD_DIM
ROPE_THETA = 10000.0
EVEN_OFF_XV = 0
EVEN_OFF_GHY = 3 * HY_WIDTH
EVEN_OFF_Q = EVEN_OFF_GHY + HY_WIDTH
EVEN_OFF_K = EVEN_OFF_Q + ATT_WIDTH
EVEN_OFF_V = EVEN_OFF_K + ATT_KV_WIDTH
EVEN_OFF_GATT = EVEN_OFF_V + ATT_KV_WIDTH
EVEN_IN = EVEN_OFF_GATT + ATT_WIDTH

ML_HEADS = 8
ML_QK = 128
ML_V = 256
ML_QK_WIDTH = ML_HEADS * ML_QK
ML_WIDTH = ML_HEADS * ML_V
ODD_OFF_Q = 0
ODD_OFF_K = ML_QK_WIDTH
ODD_OFF_V = 2 * ML_QK_WIDTH
ODD_OFF_O = ODD_OFF_V + ML_WIDTH
ODD_OFF_Z = ODD_OFF_O + ML_WIDTH
ODD_OFF_GATES = ODD_OFF_Z + ML_WIDTH
ODD_MAIN = ODD_OFF_GATES

ROW_TILE = 256
LANE = 128
VMEM_LIMIT_BYTES = 48 * 1024 * 1024

FFT_L1 = 64
FFT_L2 = 128


def _params(*sem):
    return pltpu.CompilerParams(dimension_semantics=sem, vmem_limit_bytes=VMEM_LIMIT_BYTES)


def _adaln_kernel(c_ref, w_ref, b_ref, o_ref):
    c = c_ref[...]
    s = c * jax.nn.sigmoid(c)
    o_ref[0] = jnp.dot(s, w_ref[0], preferred_element_type=f32, precision=HIGHEST) + b_ref[0]


def adaln_all(cond, w_mod, b_mod, *, tn=768):
    depth, D, N = w_mod.shape
    return pl.pallas_call(
        _adaln_kernel,
        grid=(depth, N // tn),
        in_specs=[
            pl.BlockSpec((8, D), lambda l, j: (0, 0)),
            pl.BlockSpec((1, D, tn), lambda l, j: (l, 0, j)),
            pl.BlockSpec((1, 1, tn), lambda l, j: (l, 0, j)),
        ],
        out_specs=pl.BlockSpec((1, 8, tn), lambda l, j: (l, 0, j)),
        out_shape=jax.ShapeDtypeStruct((depth, 8, N), f32),
        compiler_params=_params("arbitrary", "arbitrary"),
        name="adaln",
    )(cond, w_mod, b_mod.reshape(depth, 1, N))


def _mod_row(i, tiles_per_batch, ctx_row):
    lat_tiles = tiles_per_batch - 1
    return jnp.where(i % tiles_per_batch == lat_tiles, ctx_row, i // tiles_per_batch)


def _norm_mod_kernel(x_ref, g_ref, m_ref, o_ref):
    x = x_ref[...]
    D = x.shape[-1]
    y = x * lax.rsqrt(jnp.mean(x * x, axis=-1, keepdims=True) + NORM_EPS)
    shift = m_ref[0, :, 0:D]
    scale = m_ref[0, :, D:2 * D]
    o_ref[...] = (y * g_ref[...] * (1.0 + scale) + shift).astype(o_ref.dtype)


def norm_mod(x_all, g, mods, tiles_per_batch, ctx_row):
    R, D = x_all.shape
    return pl.pallas_call(
        _norm_mod_kernel,
        grid=(R // ROW_TILE,),
        in_specs=[
            pl.BlockSpec((ROW_TILE, D), lambda i: (i, 0)),
            pl.BlockSpec((1, D), lambda i: (0, 0)),
            pl.BlockSpec((1, 1, 3 * D), lambda i: (_mod_row(i, tiles_per_batch, ctx_row), 0, 0)),
        ],
        out_specs=pl.BlockSpec((ROW_TILE, D), lambda i: (i, 0)),
        out_shape=jax.ShapeDtypeStruct((R, D), bf16),
        compiler_params=_params("arbitrary"),
        name="norm_mod",
    )(x_all, g.reshape(1, D), mods)


def _matmul_kernel(a_ref, b_ref, o_ref):
    o_ref[...] = jnp.dot(a_ref[...], b_ref[...], preferred_element_type=f32).astype(o_ref.dtype)


def _proj_row_tile(rows):
    return next(t for t in (1024, 512, ROW_TILE) if rows % t == 0)


def matmul(a, b, *, tm, tn, out_dtype=f32, name="matmul"):
    M, K = a.shape
    N = b.shape[1]
    assert M % tm == 0 and N % tn == 0, (M, N, tm, tn)
    return pl.pallas_call(
        _matmul_kernel,
        grid=(M // tm, N // tn),
        in_specs=[pl.BlockSpec((tm, K), lambda i, j: (i, 0)),
                  pl.BlockSpec((K, tn), lambda i, j: (0, j))],
        out_specs=pl.BlockSpec((tm, tn), lambda i, j: (i, j)),
        out_shape=jax.ShapeDtypeStruct((M, N), out_dtype),
        compiler_params=_params("arbitrary", "arbitrary"),
        name=name,
    )(a, b)


def _conv3(x, prev_row, next_row, w, b):
    tm = x.shape[0]
    row = lax.broadcasted_iota(jnp.int32, x.shape, 0)
    xm = jnp.where(row == 0, prev_row, pltpu.roll(x, 1, 0))
    xp = jnp.where(row == tm - 1, next_row, pltpu.roll(x, tm - 1, 0))
    return w[0:1] * xm + w[1:2] * x + w[2:3] * xp + b


def _seq_edges(i, tiles_per_batch):
    r = i % tiles_per_batch
    lat_tiles = tiles_per_batch - 1
    first = jnp.logical_or(r == 0, r == lat_tiles)
    last = jnp.logical_or(r == lat_tiles - 1, r == lat_tiles)
    return first, last


def _halo_rows(prev_ref, next_ref, first, last):
    prev_row = jnp.where(first, 0.0, prev_ref[7:8, :])
    next_row = jnp.where(last, 0.0, next_ref[0:1, :])
    return prev_row, next_row


def _halo_specs(col_block, tc, n_row_blocks8):
    per = ROW_TILE // 8
    prev = pl.BlockSpec((8, tc), lambda i, c: (jnp.maximum(i * per - 1, 0), col_block(c)))
    nxt = pl.BlockSpec((8, tc), lambda i, c: (jnp.minimum((i + 1) * per, n_row_blocks8 - 1), col_block(c)))
    return prev, nxt


def _hyena_pre_kernel(tiles_per_batch, x0_ref, x0p_ref, x0n_ref, x1_ref, x1p_ref, x1n_ref,
                      v_ref, vp_ref, vn_ref, w0_ref, w1_ref, w2_ref, b0_ref, b1_ref, b2_ref,
                      x0_out, g_out, gz_out):
    first, last = _seq_edges(pl.program_id(0), tiles_per_batch)

    def conv(x_ref, p_ref, n_ref, w_ref, b_ref):
        prev_row, next_row = _halo_rows(p_ref, n_ref, first, last)
        return _conv3(x_ref[...], prev_row, next_row, w_ref[...], b_ref[...])

    x0 = conv(x0_ref, x0p_ref, x0n_ref, w0_ref, b0_ref)
    x1 = conv(x1_ref, x1p_ref, x1n_ref, w1_ref, b1_ref)
    v = conv(v_ref, vp_ref, vn_ref, w2_ref, b2_ref)
    g = v * x1
    x0_out[...] = x0
    g_out[...] = g
    gz_out[...] = g.astype(gz_out.dtype)


def hyena_pre(p, conv_w, conv_b, tiles_per_batch, *, tc=512):
    R = p.shape[0]
    W = HY_WIDTH
    nb = W // tc
    n8 = R // 8
    specs = []
    for part in range(3):
        col = functools.partial(lambda c, part: part * nb + c, part=part)
        main = pl.BlockSpec((ROW_TILE, tc), functools.partial(lambda i, c, col: (i, col(c)), col=col))
        prev, nxt = _halo_specs(col, tc, n8)
        specs += [main, prev, nxt]
    wspecs = [pl.BlockSpec((3, tc), functools.partial(lambda i, c, part: (0, part * nb + c), part=part))
              for part in range(3)]
    bspecs = [pl.BlockSpec((1, tc), functools.partial(lambda i, c, part: (0, part * nb + c), part=part))
              for part in range(3)]
    out_spec = pl.BlockSpec((ROW_TILE, tc), lambda i, c: (i, c))
    args = [p] * 9 + [conv_w] * 3 + [conv_b.reshape(1, -1)] * 3
    return pl.pallas_call(
        functools.partial(_hyena_pre_kernel, tiles_per_batch),
        grid=(R // ROW_TILE, nb),
        in_specs=specs + wspecs + bspecs,
        out_specs=[out_spec, out_spec, out_spec],
        out_shape=[jax.ShapeDtypeStruct((R, W), f32), jax.ShapeDtypeStruct((R, W), f32),
                   jax.ShapeDtypeStruct((R, W), bf16)],
        compiler_params=_params("arbitrary", "arbitrary"),
        name="hyena_pre",
    )(*args)


def _filter_kernel(n, rows, bands_ref, w1t_ref, w1c_ref, w1s_ref, b1_ref, fr_ref, w2_ref, b2_ref,
                   w3_ref, dl_ref, o_ref, nrm_ref):
    step = pl.program_id(0)
    j = step * rows + lax.broadcasted_iota(jnp.int32, (rows, 1), 0)
    d = jnp.where(j < n, j, 2 * n - j)
    valid = j != n
    df = d.astype(f32)
    t = df / float(n - 1)
    ang = (2.0 * math.pi / n) * df * bands_ref[...]
    fr = fr_ref[...]
    z1 = (t * w1t_ref[...]
          + jnp.dot(jnp.cos(ang), w1c_ref[...], preferred_element_type=f32, precision=HIGHEST)
          - jnp.dot(jnp.sin(ang), w1s_ref[...], preferred_element_type=f32, precision=HIGHEST)
          + b1_ref[...])
    hdn = jnp.sin(fr * z1)
    hdn = jnp.sin(fr * (jnp.dot(hdn, w2_ref[...], preferred_element_type=f32, precision=HIGHEST) + b2_ref[...]))
    h = jnp.dot(hdn, w3_ref[0], preferred_element_type=f32, precision=HIGHEST)
    h = h * jnp.exp(-t * jnp.abs(dl_ref[...]))
    h = jnp.where(valid, h, 0.0)
    o_ref[...] = h

    @pl.when(step == 0)
    def _():
        nrm_ref[...] = jnp.zeros_like(nrm_ref)

    nrm_ref[...] += jnp.sum(jnp.abs(h), axis=0, keepdims=True)


def hyena_filter(n, w1, b1, freq, w2, b2, w3, *, rows=256):
    W = HY_WIDTH
    Hd = HY_HIDDEN
    bands = jnp.linspace(1e-4, HY_BANDS - 1, HY_BANDS, dtype=f32).reshape(1, HY_BANDS)
    max_decay = math.log(HY_TARGET) / HY_SHORT_DECAY_PCT
    min_decay = math.log(HY_TARGET) / HY_LONG_DECAY_PCT
    deltas = jnp.linspace(min_decay, max_decay, W, dtype=f32).reshape(1, W)
    steps = 2 * n // rows
    half_steps = n // rows
    full = lambda s: (0, 0)
    w3r = w3.reshape(Hd, 2, W).transpose(1, 0, 2)
    return pl.pallas_call(
        functools.partial(_filter_kernel, n, rows),
        grid=(steps,),
        in_specs=[
            pl.BlockSpec((1, HY_BANDS), full),
            pl.BlockSpec((1, Hd), full),
            pl.BlockSpec((HY_BANDS, Hd), full),
            pl.BlockSpec((HY_BANDS, Hd), full),
            pl.BlockSpec((1, Hd), full),
            pl.BlockSpec((1, Hd), full),
            pl.BlockSpec((Hd, Hd), full),
            pl.BlockSpec((1, Hd), full),
            pl.BlockSpec((1, Hd, W), lambda s: (jnp.where(s * rows < n, 0, 1), 0, 0)),
            pl.BlockSpec((1, W), full),
        ],
        out_specs=[pl.BlockSpec((rows, W), lambda s: (s, 0)), pl.BlockSpec((1, W), full)],
        out_shape=[jax.ShapeDtypeStruct((2 * n, W), f32), jax.ShapeDtypeStruct((1, W), f32)],
        compiler_params=_params("arbitrary"),
        name="hyena_filter",
    )(bands, w1[0:1], w1[1:1 + HY_BANDS], w1[1 + HY_BANDS:], b1.reshape(1, Hd), freq.reshape(1, Hd),
      w2, b2.reshape(1, Hd), w3r, deltas)


@functools.lru_cache(maxsize=None)
def _fft_constants():
    L1, L2 = FFT_L1, FFT_L2
    L = L1 * L2
    k1 = np.arange(L1)
    nh = np.arange(L1 // 2)
    th = 2.0 * np.pi * np.outer(k1, nh) / L1
    m1r = np.zeros((2 * L1, L1 // 2))
    m1i = np.zeros((2 * L1, L1 // 2))
    m1r[0::2], m1r[1::2] = np.cos(th), -np.sin(th)
    m1i[0::2], m1i[1::2] = np.sin(th), np.cos(th)
    thf = 2.0 * np.pi * np.outer(k1, np.arange(L1)) / L1
    m1f = np.zeros((2 * L1, L1))
    m1f[0::2], m1f[1::2] = np.cos(thf), -np.sin(thf)
    n2 = np.arange(L2)
    k2 = np.arange(L2)
    m = (k1[:, None, None] * n2[None, None, :] + L1 * k2[None, :, None] * n2[None, None, :]) % L
    ph = 2.0 * np.pi * m / L
    gr, gi = np.cos(ph), -np.sin(ph)
    g = np.concatenate([np.concatenate([gr, -gi], axis=2), np.concatenate([gi, gr], axis=2)], axis=1)
    gt = np.transpose(g, (0, 2, 1))
    thi = 2.0 * np.pi * np.outer(nh, k1) / L1
    minv = np.zeros((2, L1 // 2, 2 * L1))
    minv[0][:, 0::2], minv[0][:, 1::2] = np.cos(thi), -np.sin(thi)
    minv[1][:, 0::2], minv[1][:, 1::2] = np.sin(thi), np.cos(thi)
    cast = lambda a: np.asarray(a, dtype=np.float32)
    return dict(m1r=cast(m1r), m1i=cast(m1i), m1f=cast(m1f), g=cast(g), gt=cast(gt), minv=cast(minv))


@functools.lru_cache(maxsize=None)
def _ctx_dft_constants(n):
    L = 2 * n
    k = np.arange(L)
    th = 2.0 * np.pi * np.outer(k, np.arange(n)) / L
    c, s = np.cos(th), np.sin(th)
    mc = np.concatenate([np.concatenate([c, s], axis=1), np.concatenate([-s, c], axis=1)], axis=0)
    thf = 2.0 * np.pi * np.outer(k, np.arange(L)) / L
    mf = np.concatenate([np.cos(thf), -np.sin(thf)], axis=0)
    ct, st = c.T, s.T
    minv = np.concatenate([np.concatenate([ct, -st], axis=1), np.concatenate([st, ct], axis=1)], axis=0)
    cast = lambda a: np.asarray(a, dtype=np.float32)
    return dict(mc=cast(mc), mf=cast(mf), minv=cast(minv))


def _bf16_constants(consts):
    return {k: jnp.asarray(v).astype(bf16) for k, v in consts.items()}


def _fft1_kernel(re_ref, im_ref, mr_ref, mi_ref, o_ref):
    o_ref[0] = (jnp.dot(mr_ref[...], re_ref[0], preferred_element_type=f32)
                + jnp.dot(mi_ref[...], im_ref[0], preferred_element_type=f32)).astype(o_ref.dtype)


def fft_step1(gz3, m1r, m1i, *, tcol=8192):
    B, _, ncol = gz3.shape
    half = FFT_L1 // 2
    return pl.pallas_call(
        _fft1_kernel,
        grid=(B // 2, ncol // tcol),
        in_specs=[pl.BlockSpec((1, half, tcol), lambda p, j: (2 * p, 0, j)),
                  pl.BlockSpec((1, half, tcol), lambda p, j: (2 * p + 1, 0, j)),
                  pl.BlockSpec((2 * FFT_L1, half), lambda p, j: (0, 0)),
                  pl.BlockSpec((2 * FFT_L1, half), lambda p, j: (0, 0))],
        out_specs=pl.BlockSpec((1, 2 * FFT_L1, tcol), lambda p, j: (p, 0, j)),
        out_shape=jax.ShapeDtypeStruct((B // 2, 2 * FFT_L1, ncol), bf16),
        compiler_params=_params("arbitrary", "arbitrary"),
        name="fft_step1",
    )(gz3, gz3, m1r, m1i)


def _fft1_filter_kernel(x_ref, m_ref, o_ref):
    o_ref[...] = jnp.dot(m_ref[...], x_ref[...].astype(bf16), preferred_element_type=f32).astype(o_ref.dtype)


def fft_step1_filter(ts2, m1f, *, tcol=8192):
    _, ncol = ts2.shape
    return pl.pallas_call(
        _fft1_filter_kernel,
        grid=(ncol // tcol,),
        in_specs=[pl.BlockSpec((FFT_L1, tcol), lambda j: (0, j)),
                  pl.BlockSpec((2 * FFT_L1, FFT_L1), lambda j: (0, 0))],
        out_specs=pl.BlockSpec((2 * FFT_L1, tcol), lambda j: (0, j)),
        out_shape=jax.ShapeDtypeStruct((2 * FFT_L1, ncol), bf16),
        compiler_params=_params("arbitrary"),
        name="fft_step1_filter",
    )(ts2, m1f)


def _cmul(yr, yi, hr, hi):
    return yr * hr - yi * hi, yr * hi + yi * hr


def _fft2_filter_kernel(kc, a_ref, g_ref, nrm_ref, o_ref):
    scale = 1.0 / (nrm_ref[...] * float(FFT_L1 * FFT_L2))
    for kk in range(kc):
        o_ref[kk] = jnp.dot(g_ref[kk], a_ref[kk], preferred_element_type=f32) * scale


def fft_step2_filter(af, g, nrm, *, kc=8, ct=256):
    L1, R2, W = af.shape
    return pl.pallas_call(
        functools.partial(_fft2_filter_kernel, kc),
        grid=(L1 // kc, W // ct),
        in_specs=[pl.BlockSpec((kc, R2, ct), lambda k, c: (k, 0, c)),
                  pl.BlockSpec((kc, R2, R2), lambda k, c: (k, 0, 0)),
                  pl.BlockSpec((1, ct), lambda k, c: (0, c))],
        out_specs=pl.BlockSpec((kc, R2, ct), lambda k, c: (k, 0, c)),
        out_shape=jax.ShapeDtypeStruct((L1, R2, W), f32),
        compiler_params=_params("arbitrary", "arbitrary"),
        name="fft_step2_filter",
    )(af, g, nrm)


def _fft2_kernel(kc, a_ref, g_ref, gt_ref, h_ref, o_ref):
    half = FFT_L2
    for kk in range(kc):
        y = jnp.dot(g_ref[kk], a_ref[0, kk], preferred_element_type=f32)
        pr, pi = _cmul(y[:half], y[half:], h_ref[kk, :half], h_ref[kk, half:])
        pcat = jnp.concatenate([pr, pi], axis=0).astype(bf16)
        o_ref[0, kk] = jnp.dot(gt_ref[kk], pcat, preferred_element_type=f32).astype(o_ref.dtype)


def fft_step2(a4, g, gt, hf, *, kc=8, ct=256):
    P, L1, R2, W = a4.shape
    return pl.pallas_call(
        functools.partial(_fft2_kernel, kc),
        grid=(L1 // kc, W // ct, P),
        in_specs=[pl.BlockSpec((1, kc, R2, ct), lambda k, c, p: (p, k, 0, c)),
                  pl.BlockSpec((kc, R2, R2), lambda k, c, p: (k, 0, 0)),
                  pl.BlockSpec((kc, R2, R2), lambda k, c, p: (k, 0, 0)),
                  pl.BlockSpec((kc, R2, ct), lambda k, c, p: (k, 0, c))],
        out_specs=pl.BlockSpec((1, kc, R2, ct), lambda k, c, p: (p, k, 0, c)),
        out_shape=jax.ShapeDtypeStruct((P, L1, R2, W), bf16),
        compiler_params=_params("arbitrary", "arbitrary", "arbitrary"),
        name="fft_step2",
    )(a4, g, gt, hf)


def _fft3_kernel(b_ref, m_ref, o_ref):
    o_ref[0] = jnp.dot(m_ref[0], b_ref[0], preferred_element_type=f32)


def fft_step3(b3, minv, rows_total, *, tcol=8192):
    P, _, ncol = b3.shape
    half = FFT_L1 // 2
    return pl.pallas_call(
        _fft3_kernel,
        grid=(2 * P, ncol // tcol),
        in_specs=[pl.BlockSpec((1, 2 * FFT_L1, tcol), lambda b, j: (b // 2, 0, j)),
                  pl.BlockSpec((1, half, 2 * FFT_L1), lambda b, j: (b % 2, 0, 0))],
        out_specs=pl.BlockSpec((1, half, tcol), lambda b, j: (b, 0, j)),
        out_shape=jax.ShapeDtypeStruct((2 * P, rows_total, ncol), f32),
        compiler_params=_params("arbitrary", "arbitrary"),
        name="fft_step3",
    )(b3, minv)


def _ctx_filter_kernel(n, ts_ref, m_ref, nrm_ref, o_ref):
    scale = 1.0 / (nrm_ref[...] * float(2 * n))
    o_ref[...] = jnp.dot(m_ref[...], ts_ref[...].astype(bf16), preferred_element_type=f32) * scale


def ctx_filter_spectrum(ts, mf, nrm, *, ct=256):
    L, W = ts.shape
    return pl.pallas_call(
        functools.partial(_ctx_filter_kernel, L // 2),
        grid=(W // ct,),
        in_specs=[pl.BlockSpec((L, ct), lambda c: (0, c)),
                  pl.BlockSpec((2 * L, L), lambda c: (0, 0)),
                  pl.BlockSpec((1, ct), lambda c: (0, c))],
        out_specs=pl.BlockSpec((2 * L, ct), lambda c: (0, c)),
        out_shape=jax.ShapeDtypeStruct((2 * L, W), f32),
        compiler_params=_params("arbitrary"),
        name="ctx_filter_spectrum",
    )(ts, mf, nrm)


def _ctx_conv_kernel(n, re_ref, im_ref, mc_ref, minv_ref, h_ref, y_in_ref, o_ref):
    del y_in_ref
    L = 2 * n
    z = jnp.concatenate([re_ref[0], im_ref[0]], axis=0)
    y = jnp.dot(mc_ref[...], z, preferred_element_type=f32)
    pr, pi = _cmul(y[:L], y[L:], h_ref[:L], h_ref[L:])
    pcat = jnp.concatenate([pr, pi], axis=0).astype(bf16)
    out = jnp.dot(minv_ref[...], pcat, preferred_element_type=f32)
    o_ref[0, 0] = out[:n]
    o_ref[0, 1] = out[n:]


def ctx_long_conv(gz, yconv, hfc, consts, n_lat, n_ctx, *, ct=256):
    B, T, W = gz.shape
    blk = n_lat // n_ctx
    L = 2 * n_ctx
    y4 = yconv.reshape(B // 2, 2, T, W)
    out = pl.pallas_call(
        functools.partial(_ctx_conv_kernel, n_ctx),
        grid=(B // 2, W // ct),
        in_specs=[pl.BlockSpec((1, n_ctx, ct), lambda p, c: (2 * p, blk, c)),
                  pl.BlockSpec((1, n_ctx, ct), lambda p, c: (2 * p + 1, blk, c)),
                  pl.BlockSpec((2 * L, L), lambda p, c: (0, 0)),
                  pl.BlockSpec((L, 2 * L), lambda p, c: (0, 0)),
                  pl.BlockSpec((2 * L, ct), lambda p, c: (0, c)),
                  pl.BlockSpec(memory_space=pl.ANY)],
        out_specs=pl.BlockSpec((1, 2, n_ctx, ct), lambda p, c: (p, 0, blk, c)),
        out_shape=jax.ShapeDtypeStruct(y4.shape, f32),
        input_output_aliases={5: 0},
        compiler_params=_params("arbitrary", "arbitrary"),
        name="ctx_long_conv",
    )(gz, gz, consts["mc"], consts["minv"], hfc, y4)
    return out.reshape(B, T, W)


def hyena_long_conv(gz, n_lat, n_ctx, f_w1, f_b1, f_freq, f_w2, f_b2, f_w3):
    B, T, W = gz.shape
    cst = _bf16_constants(_fft_constants())
    L1, L2 = FFT_L1, FFT_L2
    assert 2 * n_lat == L1 * L2 and T % L2 == 0 and B % 2 == 0
    ts, nrm = hyena_filter(n_lat, f_w1, f_b1, f_freq, f_w2, f_b2, f_w3)
    af = fft_step1_filter(ts.reshape(L1, L2 * W), cst["m1f"])
    hf = fft_step2_filter(af.reshape(L1, 2 * L2, W), cst["g"], nrm)
    a = fft_step1(gz.reshape(B, T // L2, L2 * W), cst["m1r"], cst["m1i"])
    bq = fft_step2(a.reshape(B // 2, L1, 2 * L2, W), cst["g"], cst["gt"], hf)
    y = fft_step3(bq.reshape(B // 2, 2 * L1, L2 * W), cst["minv"], T // L2).reshape(B, T, W)
    ccst = _bf16_constants(_ctx_dft_constants(n_ctx))
    ts_c, nrm_c = hyena_filter(n_ctx, f_w1, f_b1, f_freq, f_w2, f_b2, f_w3)
    hfc = ctx_filter_spectrum(ts_c, ccst["mf"], nrm_c)
    return ctx_long_conv(gz, y, hfc, ccst, n_lat, n_ctx)


@functools.lru_cache(maxsize=None)
def _rope_tables(n_lat, n_ctx):
    half = HEAD_DIM // 2
    nf = half // 2
    inv = ROPE_THETA ** (-np.arange(nf, dtype=np.float64) / nf)
    t = np.arange(n_lat)
    pos = np.stack([t // GRID_W, t % GRID_W], axis=1).astype(np.float64)
    ang = pos[:, :, None] * inv[None, None, :]
    cos = np.concatenate([np.cos(ang), np.cos(ang)], axis=2).reshape(n_lat, HEAD_DIM)
    sin = np.concatenate([-np.sin(ang), np.sin(ang)], axis=2).reshape(n_lat, HEAD_DIM)
    cos = np.concatenate([cos, np.ones((n_ctx, HEAD_DIM))], axis=0)
    sin = np.concatenate([sin, np.zeros((n_ctx, HEAD_DIM))], axis=0)
    return np.asarray(cos, np.float32), np.asarray(sin, np.float32)


def _norm_rope(x, w, cos, sin, lo_lane):
    y = x * lax.rsqrt(jnp.mean(x * x, axis=-1, keepdims=True) + NORM_EPS) * w
    nf = HEAD_DIM // 4
    partner = jnp.where(lo_lane, pltpu.roll(y, HEAD_DIM - nf, 1), pltpu.roll(y, nf, 1))
    return y * cos + partner * sin


def _qkv_prep_kernel(q_ref, k_ref, v_ref, cos_ref, sin_ref, qn_ref, kn_ref, q_out, kt_out, v_out):
    cos = cos_ref[...]
    sin = sin_ref[...]
    lane = lax.broadcasted_iota(jnp.int32, cos.shape, 1)
    lo_lane = (lane % (HEAD_DIM // 2)) < (HEAD_DIM // 4)
    scale = HEAD_DIM ** -0.5
    for h in range(ATT_HEADS):
        sl = slice(h * HEAD_DIM, (h + 1) * HEAD_DIM)
        q_out[:, sl] = (_norm_rope(q_ref[:, sl], qn_ref[...], cos, sin, lo_lane) * scale).astype(q_out.dtype)
    for h in range(ATT_KV_HEADS):
        sl = slice(h * HEAD_DIM, (h + 1) * HEAD_DIM)
        k = _norm_rope(k_ref[:, sl], kn_ref[...], cos, sin, lo_lane)
        kt_out[0, sl, :] = k.T.astype(kt_out.dtype)
    v_out[...] = v_ref[...].astype(v_out.dtype)


def qkv_prep(p, q_norm, k_norm, B, n_lat, n_ctx):
    R = p.shape[0]
    T = n_lat + n_ctx
    tpb = T // ROW_TILE
    cos, sin = _rope_tables(n_lat, n_ctx)
    return pl.pallas_call(
        _qkv_prep_kernel,
        grid=(R // ROW_TILE,),
        in_specs=[pl.BlockSpec((ROW_TILE, ATT_WIDTH), lambda i: (i, EVEN_OFF_Q // ATT_WIDTH)),
                  pl.BlockSpec((ROW_TILE, ATT_KV_WIDTH), lambda i: (i, EVEN_OFF_K // ATT_KV_WIDTH)),
                  pl.BlockSpec((ROW_TILE, ATT_KV_WIDTH), lambda i: (i, EVEN_OFF_V // ATT_KV_WIDTH)),
                  pl.BlockSpec((ROW_TILE, HEAD_DIM), lambda i: (i % tpb, 0)),
                  pl.BlockSpec((ROW_TILE, HEAD_DIM), lambda i: (i % tpb, 0)),
                  pl.BlockSpec((1, HEAD_DIM), lambda i: (0, 0)),
                  pl.BlockSpec((1, HEAD_DIM), lambda i: (0, 0))],
        out_specs=[pl.BlockSpec((ROW_TILE, ATT_WIDTH), lambda i: (i, 0)),
                   pl.BlockSpec((1, ATT_KV_WIDTH, ROW_TILE), lambda i: (i // tpb, 0, i % tpb)),
                   pl.BlockSpec((ROW_TILE, ATT_KV_WIDTH), lambda i: (i, 0))],
        out_shape=[jax.ShapeDtypeStruct((R, ATT_WIDTH), bf16),
                   jax.ShapeDtypeStruct((B, ATT_KV_WIDTH, T), bf16),
                   jax.ShapeDtypeStruct((R, ATT_KV_WIDTH), bf16)],
        compiler_params=_params("arbitrary"),
        name="qkv_prep",
    )(p, p, p, jnp.asarray(cos), jnp.asarray(sin), q_norm.reshape(1, HEAD_DIM), k_norm.reshape(1, HEAD_DIM))


def _softmax_pv(q, kt, v):
    s = jnp.dot(q, kt, preferred_element_type=f32)
    m = jnp.max(s, axis=-1, keepdims=True)
    e = jnp.exp(s - m)
    l = jnp.sum(e, axis=-1, keepdims=True)
    return jnp.dot(e.astype(v.dtype), v, preferred_element_type=f32) / l


def _attention_kernel(n_lat, tq, q_tiles_lat, q_ref, kt_ref, v_ref, o_ref):
    q = jnp.concatenate([q_ref[:, h * HEAD_DIM:(h + 1) * HEAD_DIM] for h in range(ATT_GROUP)], axis=0)

    def store(o):
        for h in range(ATT_GROUP):
            o_ref[:, h * HEAD_DIM:(h + 1) * HEAD_DIM] = o[h * tq:(h + 1) * tq].astype(o_ref.dtype)

    @pl.when(pl.program_id(2) < q_tiles_lat)
    def _():
        store(_softmax_pv(q, kt_ref[0], v_ref[0]))

    @pl.when(pl.program_id(2) >= q_tiles_lat)
    def _():
        store(_softmax_pv(q, kt_ref[0, :, n_lat:], v_ref[0, n_lat:, :]))


def attention(q, kt, v, B, n_lat, n_ctx, *, tq=128):
    R = q.shape[0]
    T = n_lat + n_ctx
    qt = T // tq
    gw = ATT_GROUP * HEAD_DIM
    return pl.pallas_call(
        functools.partial(_attention_kernel, n_lat, tq, n_lat // tq),
        grid=(B, ATT_KV_HEADS, qt),
        in_specs=[pl.BlockSpec((tq, gw), lambda b, g, i: (b * qt + i, g)),
                  pl.BlockSpec((1, HEAD_DIM, T), lambda b, g, i: (b, g, 0)),
                  pl.BlockSpec((1, T, HEAD_DIM), lambda b, g, i: (b, 0, g))],
        out_specs=pl.BlockSpec((tq, gw), lambda b, g, i: (b * qt + i, g)),
        out_shape=jax.ShapeDtypeStruct((R, ATT_WIDTH), bf16),
        compiler_params=_params("arbitrary", "arbitrary", "arbitrary"),
        name="attention",
    )(q, kt, v.reshape(B, T, ATT_KV_WIDTH))


def _post_residual(x, y, g_post, gate):
    yn = y * lax.rsqrt(jnp.mean(y * y, axis=-1, keepdims=True) + NORM_EPS) * g_post
    return x + gate * yn


def _silu(x):
    return x * jax.nn.sigmoid(x)


def _even_out_kernel(x0_ref, g_ref, yc_ref, ghy_ref, att_ref, ga0_ref, ga1_ref, bias_ref, w_ref,
                     gp_ref, m_ref, x_ref, o_ref):
    D = x_ref.shape[-1]
    g = g_ref[...]
    hy = x0_ref[...] * (yc_ref[...] + g * bias_ref[...]) * _silu(ghy_ref[...])
    g_att = jnp.concatenate([ga0_ref[...], ga1_ref[...]], axis=1)
    at = att_ref[...].astype(f32) * _silu(g_att)
    lhs = jnp.concatenate([hy, at], axis=1).astype(bf16)
    y = jnp.dot(lhs, w_ref[...], preferred_element_type=f32)
    o_ref[...] = _post_residual(x_ref[...], y, gp_ref[...], m_ref[0, :, 2 * D:3 * D])


def even_out(x0, g, yconv, p, att, hy_bias, w_out, g_post, mods, x_all, tiles_per_batch, ctx_row):
    R, D = x_all.shape
    W = HY_WIDTH
    hw = ATT_WIDTH // 2
    row = lambda i: (i, 0)
    full = lambda i: (0, 0)
    return pl.pallas_call(
        _even_out_kernel,
        grid=(R // ROW_TILE,),
        in_specs=[pl.BlockSpec((ROW_TILE, W), row),
                  pl.BlockSpec((ROW_TILE, W), row),
                  pl.BlockSpec((ROW_TILE, W), row),
                  pl.BlockSpec((ROW_TILE, W), lambda i: (i, EVEN_OFF_GHY // W)),
                  pl.BlockSpec((ROW_TILE, ATT_WIDTH), row),
                  pl.BlockSpec((ROW_TILE, hw), lambda i: (i, EVEN_OFF_GATT // hw)),
                  pl.BlockSpec((ROW_TILE, hw), lambda i: (i, EVEN_OFF_GATT // hw + 1)),
                  pl.BlockSpec((1, W), full),
                  pl.BlockSpec((W + ATT_WIDTH, D), full),
                  pl.BlockSpec((1, D), full),
                  pl.BlockSpec((1, 1, 3 * D), lambda i: (_mod_row(i, tiles_per_batch, ctx_row), 0, 0)),
                  pl.BlockSpec((ROW_TILE, D), row)],
        out_specs=pl.BlockSpec((ROW_TILE, D), row),
        out_shape=jax.ShapeDtypeStruct((R, D), f32),
        compiler_params=_params("arbitrary"),
        name="even_out",
    )(x0, g, yconv, p, att, p, p, hy_bias.reshape(1, W), w_out.astype(bf16), g_post.reshape(1, D), mods, x_all)


def even_layer(x_all, mods, B, n_lat, n_ctx, g_pre, g_post, w_in, conv_w, conv_b, f_w1, f_b1, f_freq,
               f_w2, f_b2, f_w3, hy_bias, q_norm, k_norm, w_out):
    T = n_lat + n_ctx
    tpb = T // ROW_TILE
    h = norm_mod(x_all, g_pre, mods, tpb, B)
    p = matmul(h, w_in.astype(bf16), tm=_proj_row_tile(B * T), tn=512, name="even_in_proj")
    x0, g, gz = hyena_pre(p, conv_w, conv_b, tpb)
    yconv = hyena_long_conv(gz.reshape(B, T, HY_WIDTH), n_lat, n_ctx, f_w1, f_b1, f_freq, f_w2, f_b2, f_w3)
    q, kt, v = qkv_prep(p, q_norm, k_norm, B, n_lat, n_ctx)
    att = attention(q, kt, v, B, n_lat, n_ctx)
    return even_out(x0, g, yconv.reshape(B * T, HY_WIDTH), p, att, hy_bias, w_out, g_post, mods, x_all, tpb, B)


def _mlstm_prep_kernel(tiles_per_batch, q_ref, qp_ref, qn_ref, k_ref, kp_ref, kn_ref,
                       wq_ref, wk_ref, bq_ref, bk_ref, q_out, kt_out):
    first, last = _seq_edges(pl.program_id(0), tiles_per_batch)
    prev_row, next_row = _halo_rows(qp_ref, qn_ref, first, last)
    q = _silu(_conv3(q_ref[...], prev_row, next_row, wq_ref[...], bq_ref[...]))
    q_out[...] = q.astype(q_out.dtype)
    prev_row, next_row = _halo_rows(kp_ref, kn_ref, first, last)
    k = _silu(_conv3(k_ref[...], prev_row, next_row, wk_ref[...], bk_ref[...])) * (ML_QK ** -0.5)
    for h in range(k.shape[1] // ML_QK):
        sl = slice(h * ML_QK, (h + 1) * ML_QK)
        kt_out[0, sl, :] = k[:, sl].T.astype(kt_out.dtype)


def mlstm_prep(p, conv_w, conv_b, B, T, *, tc=512):
    R = p.shape[0]
    tpb = T // ROW_TILE
    nb = ML_QK_WIDTH // tc
    n8 = R // 8
    qcol = lambda c: c
    kcol = lambda c: nb + c
    qprev, qnext = _halo_specs(qcol, tc, n8)
    kprev, knext = _halo_specs(kcol, tc, n8)
    return pl.pallas_call(
        functools.partial(_mlstm_prep_kernel, tpb),
        grid=(R // ROW_TILE, nb),
        in_specs=[pl.BlockSpec((ROW_TILE, tc), lambda i, c: (i, c)), qprev, qnext,
                  pl.BlockSpec((ROW_TILE, tc), lambda i, c: (i, nb + c)), kprev, knext,
                  pl.BlockSpec((3, tc), lambda i, c: (0, c)),
                  pl.BlockSpec((3, tc), lambda i, c: (0, nb + c)),
                  pl.BlockSpec((1, tc), lambda i, c: (0, c)),
                  pl.BlockSpec((1, tc), lambda i, c: (0, nb + c))],
        out_specs=[pl.BlockSpec((ROW_TILE, tc), lambda i, c: (i, c)),
                   pl.BlockSpec((1, tc, ROW_TILE), lambda i, c: (i // tpb, c, i % tpb))],
        out_shape=[jax.ShapeDtypeStruct((R, ML_QK_WIDTH), bf16),
                   jax.ShapeDtypeStruct((B, ML_QK_WIDTH, T), bf16)],
        compiler_params=_params("arbitrary", "arbitrary"),
        name="mlstm_prep",
    )(p, p, p, p, p, p, conv_w, conv_w, conv_b.reshape(1, -1), conv_b.reshape(1, -1))


def _log_sigmoid(x):
    return jnp.minimum(x, 0.0) - jnp.log(1.0 + jnp.exp(-jnp.abs(x)))


def _mlstm_gates_kernel(g_ref, b_ref, gc_out, gr_out):
    pre = g_ref[...] + b_ref[...]
    lane = lax.broadcasted_iota(jnp.int32, pre.shape, 1)
    is_forget = (lane // ML_HEADS) % 2 == 1
    gc = jnp.where(is_forget, _log_sigmoid(pre), pre)
    gc_out[...] = gc
    gr_out[0] = gc.T


def mlstm_gates(gates, gate_b, B, T):
    R = gates.shape[0]
    tpb = T // ROW_TILE
    gb = jnp.pad(gate_b, (0, LANE - gate_b.shape[0])).reshape(1, LANE)
    return pl.pallas_call(
        _mlstm_gates_kernel,
        grid=(R // ROW_TILE,),
        in_specs=[pl.BlockSpec((ROW_TILE, LANE), lambda i: (i, 0)),
                  pl.BlockSpec((1, LANE), lambda i: (0, 0))],
        out_specs=[pl.BlockSpec((ROW_TILE, LANE), lambda i: (i, 0)),
                   pl.BlockSpec((1, LANE, ROW_TILE), lambda i: (i // tpb, 0, i % tpb))],
        out_shape=[jax.ShapeDtypeStruct((R, LANE), f32), jax.ShapeDtypeStruct((B, LANE, T), f32)],
        compiler_params=_params("arbitrary"),
        name="mlstm_gates",
    )(gates, gb)


def _mlstm_scan_kernel(reverse, q_ref, kt_ref, v_ref, gc_ref, gr_ref, o_ref, ct_ref, m_ref):
    Lc = q_ref.shape[0]
    H = ML_HEADS
    i_off = 2 * H if reverse else 0
    f_off = i_off + H

    @pl.when(pl.program_id(1) == 0)
    def _():
        ct_ref[...] = jnp.zeros_like(ct_ref)
        m_ref[...] = jnp.zeros_like(m_ref)

    t_idx = lax.broadcasted_iota(jnp.int32, (Lc, Lc), 0)
    s_idx = lax.broadcasted_iota(jnp.int32, (Lc, Lc), 1)
    causal = (s_idx >= t_idx) if reverse else (s_idx <= t_idx)
    tri = causal.astype(f32)
    gc = gc_ref[...]
    gr = gr_ref[0]
    b_col_all = jnp.dot(tri, gc[:, f_off:f_off + H], preferred_element_type=f32, precision=HIGHEST)
    b_row_all = lax.dot_general(gr[f_off:f_off + H, :], tri, (((1,), (1,)), ((), ())),
                                preferred_element_type=f32, precision=HIGHEST)
    end = 0 if reverse else Lc - 1
    ones_col = (lax.broadcasted_iota(jnp.int32, (Lc, LANE), 1) == 0).astype(f32)
    neg_inf = -jnp.inf

    for h in range(H):
        q = q_ref[:, h * ML_QK:(h + 1) * ML_QK]
        kt = kt_ref[0, h * ML_QK:(h + 1) * ML_QK, :]
        v = jnp.concatenate([v_ref[:, h * ML_V:(h + 1) * ML_V], ones_col], axis=1)
        i_col = gc[:, i_off + h:i_off + h + 1]
        i_row = gr[i_off + h:i_off + h + 1, :]
        b_col = b_col_all[:, h:h + 1]
        b_row = b_row_all[h:h + 1, :]
        b_end = b_row[:, end:end + 1]
        m_prev = m_ref[h, 0:1, 0:1]

        d = jnp.where(causal, b_col - b_row + i_row, neg_inf)
        inter = b_col + m_prev
        m_row = jnp.maximum(inter, jnp.max(d, axis=-1, keepdims=True))
        s = jnp.dot(q, kt, preferred_element_type=f32) * jnp.exp(d - m_row)
        w_prev = jnp.exp(inter - m_row)
        ct = ct_ref[h]
        tot = (jnp.dot(s.astype(bf16), v.astype(bf16), preferred_element_type=f32)
               + w_prev * jnp.dot(q, ct.astype(bf16), preferred_element_type=f32))
        den = tot[:, ML_V:ML_V + 1]
        o_ref[:, h * ML_V:(h + 1) * ML_V] = tot[:, :ML_V] / jnp.maximum(jnp.abs(den), jnp.exp(-m_row))

        g_col = b_end - b_col + i_col
        g_row = b_end - b_row + i_row
        m_new = jnp.maximum(b_end + m_prev, jnp.max(g_row, axis=-1, keepdims=True))
        a_col = jnp.exp(g_col - m_new)
        a_prev = jnp.exp(b_end + m_prev - m_new)
        ct_ref[h] = a_prev * ct + jnp.dot(kt, (v * a_col).astype(bf16), preferred_element_type=f32)
        m_ref[h] = jnp.broadcast_to(m_new, m_ref.shape[1:])


def mlstm_scan(q, kt, p, gc, gr, B, T, reverse):
    R = q.shape[0]
    tpb = T // ROW_TILE
    lat = tpb - 1

    def chunk(j):
        return jnp.where(j == 0, lat, (lat - j) if reverse else (j - 1))

    return pl.pallas_call(
        functools.partial(_mlstm_scan_kernel, reverse),
        grid=(B, tpb),
        in_specs=[pl.BlockSpec((ROW_TILE, ML_QK_WIDTH), lambda b, j: (b * tpb + chunk(j), 0)),
                  pl.BlockSpec((1, ML_QK_WIDTH, ROW_TILE), lambda b, j: (b, 0, chunk(j))),
                  pl.BlockSpec((ROW_TILE, ML_WIDTH), lambda b, j: (b * tpb + chunk(j), ODD_OFF_V // ML_WIDTH)),
                  pl.BlockSpec((ROW_TILE, LANE), lambda b, j: (b * tpb + chunk(j), 0)),
                  pl.BlockSpec((1, LANE, ROW_TILE), lambda b, j: (b, 0, chunk(j)))],
        out_specs=pl.BlockSpec((ROW_TILE, ML_WIDTH), lambda b, j: (b * tpb + chunk(j), 0)),
        out_shape=jax.ShapeDtypeStruct((R, ML_WIDTH), f32),
        scratch_shapes=[pltpu.VMEM((ML_HEADS, ML_QK, ML_V + LANE), f32),
                        pltpu.VMEM((ML_HEADS, 8, LANE), f32)],
        compiler_params=_params("arbitrary", "arbitrary"),
        name="mlstm_scan_bwd" if reverse else "mlstm_scan_fwd",
    )(q, kt, p, gc, gr)


def _odd_out_kernel(hf_ref, hb_ref, o_ref, z_ref, hn_ref, w_ref, gp_ref, m_ref, x_ref, out_ref):
    D = x_ref.shape[-1]
    hs = (hf_ref[...] + hb_ref[...]) * jax.nn.sigmoid(o_ref[...])
    parts = []
    for h in range(ML_HEADS):
        seg = hs[:, h * ML_V:(h + 1) * ML_V]
        parts.append(seg * lax.rsqrt(jnp.mean(seg * seg, axis=-1, keepdims=True) + NORM_EPS))
    hn = jnp.concatenate(parts, axis=1) * hn_ref[...] * _silu(z_ref[...])
    y = jnp.dot(hn.astype(bf16), w_ref[...], preferred_element_type=f32)
    out_ref[...] = _post_residual(x_ref[...], y, gp_ref[...], m_ref[0, :, 2 * D:3 * D])


def odd_out(hf, hb, p, head_norm, w_out, g_post, mods, x_all, B, n_lat, T):
    D = x_all.shape[1]
    tpb = T // ROW_TILE
    lat = n_lat // ROW_TILE
    row = lambda b, i: (b * tpb + i, 0)
    full = lambda b, i: (0, 0)
    return pl.pallas_call(
        _odd_out_kernel,
        grid=(B, lat),
        in_specs=[pl.BlockSpec((ROW_TILE, ML_WIDTH), row),
                  pl.BlockSpec((ROW_TILE, ML_WIDTH), row),
                  pl.BlockSpec((ROW_TILE, ML_WIDTH), lambda b, i: (b * tpb + i, ODD_OFF_O // ML_WIDTH)),
                  pl.BlockSpec((ROW_TILE, ML_WIDTH), lambda b, i: (b * tpb + i, ODD_OFF_Z // ML_WIDTH)),
                  pl.BlockSpec((1, ML_WIDTH), full),
                  pl.BlockSpec((ML_WIDTH, D), full),
                  pl.BlockSpec((1, D), full),
                  pl.BlockSpec((1, 1, 3 * D), lambda b, i: (b, 0, 0)),
                  pl.BlockSpec((ROW_TILE, D), row)],
        out_specs=pl.BlockSpec((ROW_TILE, D), lambda b, i: (b * lat + i, 0)),
        out_shape=jax.ShapeDtypeStruct((B * n_lat, D), f32),
        compiler_params=_params("arbitrary", "arbitrary"),
        name="odd_out",
    )(hf, hb, p, p, head_norm.reshape(1, ML_WIDTH), w_out.astype(bf16), g_post.reshape(1, D), mods, x_all)


def odd_layer_last(x_all, mods, B, n_lat, n_ctx, g_pre, g_post, w_in, conv_w, conv_b, gate_b, head_norm, w_out):
    T = n_lat + n_ctx
    tpb = T // ROW_TILE
    h = norm_mod(x_all, g_pre, mods, tpb, B)
    wb = w_in.astype(bf16)
    tm = _proj_row_tile(B * T)
    p = matmul(h, wb[:, :ODD_MAIN], tm=tm, tn=512, name="odd_in_proj")
    n_gates = w_in.shape[1] - ODD_MAIN
    gates = matmul(h, jnp.pad(wb[:, ODD_MAIN:], ((0, 0), (0, LANE - n_gates))), tm=tm, tn=LANE, name="odd_gate_proj")
    q, kt = mlstm_prep(p, conv_w, conv_b, B, T)
    gc, gr = mlstm_gates(gates, gate_b, B, T)
    hf = mlstm_scan(q, kt, p, gc, gr, B, T, False)
    hb = mlstm_scan(q, kt, p, gc, gr, B, T, True)
    return odd_out(hf, hb, p, head_norm, w_out, g_post, mods, x_all, B, n_lat, T)


def kernel(x, c, ctx, c_ctx, w_mod, b_mod, g_pre, g_post, e_w_in, e_conv_w, e_conv_b, e_filt_w1,
           e_filt_b1, e_filt_freq, e_filt_w2, e_filt_b2, e_filt_w3, e_hy_bias, e_q_norm, e_k_norm,
           e_w_out, o_w_in, o_conv_w, o_conv_b, o_gate_b, o_head_norm, o_w_out):
    B, n_lat, D = x.shape
    n_ctx = ctx.shape[1]
    T = n_lat + n_ctx
    depth = w_mod.shape[0]
    assert depth == 2 and B + 1 <= 8 and n_ctx == ROW_TILE and n_lat % ROW_TILE == 0
    cond = jnp.concatenate([c, c_ctx[None], jnp.zeros((8 - B - 1, D), f32)], axis=0)
    mods_all = adaln_all(cond, w_mod, b_mod)
    x_all = jnp.concatenate([x, ctx], axis=1).reshape(B * T, D)
    x_all = even_layer(x_all, mods_all[0].reshape(8, 1, 3 * D), B, n_lat, n_ctx, g_pre[0], g_post[0], e_w_in[0],
                       e_conv_w[0], e_conv_b[0], e_filt_w1[0], e_filt_b1[0], e_filt_freq[0], e_filt_w2[0],
                       e_filt_b2[0], e_filt_w3[0], e_hy_bias[0], e_q_norm[0], e_k_norm[0], e_w_out[0])
    out = odd_layer_last(x_all, mods_all[1].reshape(8, 1, 3 * D), B, n_lat, n_ctx, g_pre[1], g_post[1], o_w_in[0],
                         o_conv_w[0], o_conv_b[0], o_gate_b[0], o_head_norm[0], o_w_out[0])
    return out.reshape(B, n_lat, D)
```

```python
import functools
import math

import numpy as np
import jax
import jax.numpy as jnp
from jax import lax
from jax.experimental import pallas as pl
from jax.experimental.pallas import tpu as pltpu

f32 = jnp.float32
bf16 = jnp.bfloat16
HIGHEST = lax.Precision.HIGHEST

D_MODEL = 1024
GRID_W = 64
NORM_EPS = 1e-6

HY_WIDTH = 1024
HY_EMB = 33
HY_BANDS = (HY_EMB - 1) // 2
HY_HIDDEN = 64
HY_TARGET = 1e-2
HY_SHORT_DECAY_PCT = 0.3
HY_LONG_DECAY_PCT = 1.5

ATT_HEADS = 8
ATT_KV_HEADS = 2
ATT_GROUP = ATT_HEADS // ATT_KV_HEADS
HEAD_DIM = 128
ATT_WIDTH = ATT_HEADS * HEAD_DIM
ATT_KV_WIDTH = ATT_KV_HEADS * HEAD_DIM
ROPE_THETA = 10000.0
EVEN_OFF_XV = 0
EVEN_OFF_GHY = 3 * HY_WIDTH
EVEN_OFF_Q = EVEN_OFF_GHY + HY_WIDTH
EVEN_OFF_K = EVEN_OFF_Q + ATT_WIDTH
EVEN_OFF_V = EVEN_OFF_K + ATT_KV_WIDTH
EVEN_OFF_GATT = EVEN_OFF_V + ATT_KV_WIDTH
EVEN_IN = EVEN_OFF_GATT + ATT_WIDTH

ML_HEADS = 8
ML_QK = 128
ML_V = 256
ML_QK_WIDTH = ML_HEADS * ML_QK
ML_WIDTH = ML_HEADS * ML_V
ODD_OFF_Q = 0
ODD_OFF_K = ML_QK_WIDTH
ODD_OFF_V = 2 * ML_QK_WIDTH
ODD_OFF_O = ODD_OFF_V + ML_WIDTH
ODD_OFF_Z = ODD_OFF_O + ML_WIDTH
ODD_OFF_GATES = ODD_OFF_Z + ML_WIDTH
ODD_MAIN = ODD_OFF_GATES

ROW_TILE = 256
LANE = 128
VMEM_LIMIT_BYTES = 48 * 1024 * 1024

FFT_L1 = 64
FFT_L2 = 128


def _params(*sem):
    return pltpu.CompilerParams(dimension_semantics=sem, vmem_limit_bytes=VMEM_LIMIT_BYTES)


def _adaln_kernel(c_ref, w_ref, b_ref, o_ref):
    c = c_ref[...]
    s = c * jax.nn.sigmoid(c)
    o_ref[0] = jnp.dot(s, w_ref[0], preferred_element_type=f32, precision=HIGHEST) + b_ref[0]


def adaln_all(cond, w_mod, b_mod, *, tn=768):
    depth, D, N = w_mod.shape
    return pl.pallas_call(
        _adaln_kernel,
        grid=(depth, N // tn),
        in_specs=[
            pl.BlockSpec((8, D), lambda l, j: (0, 0)),
            pl.BlockSpec((1, D, tn), lambda l, j: (l, 0, j)),
            pl.BlockSpec((1, 1, tn), lambda l, j: (l, 0, j)),
        ],
        out_specs=pl.BlockSpec((1, 8, tn), lambda l, j: (l, 0, j)),
        out_shape=jax.ShapeDtypeStruct((depth, 8, N), f32),
        compiler_params=_params("arbitrary", "arbitrary"),
        name="adaln",
    )(cond, w_mod, b_mod.reshape(depth, 1, N))


def _mod_row(i, tiles_per_batch, ctx_row):
    lat_tiles = tiles_per_batch - 1
    return jnp.where(i % tiles_per_batch == lat_tiles, ctx_row, i // tiles_per_batch)


def _norm_mod_kernel(x_ref, g_ref, m_ref, o_ref):
    x = x_ref[...]
    D = x.shape[-1]
    y = x * lax.rsqrt(jnp.mean(x * x, axis=-1, keepdims=True) + NORM_EPS)
    shift = m_ref[0, :, 0:D]
    scale = m_ref[0, :, D:2 * D]
    o_ref[...] = (y * g_ref[...] * (1.0 + scale) + shift).astype(o_ref.dtype)


def norm_mod(x_all, g, mods, tiles_per_batch, ctx_row):
    R, D = x_all.shape
    return pl.pallas_call(
        _norm_mod_kernel,
        grid=(R // ROW_TILE,),
        in_specs=[
            pl.BlockSpec((ROW_TILE, D), lambda i: (i, 0)),
            pl.BlockSpec((1, D), lambda i: (0, 0)),
            pl.BlockSpec((1, 1, 3 * D), lambda i: (_mod_row(i, tiles_per_batch, ctx_row), 0, 0)),
        ],
        out_specs=pl.BlockSpec((ROW_TILE, D), lambda i: (i, 0)),
        out_shape=jax.ShapeDtypeStruct((R, D), bf16),
        compiler_params=_params("arbitrary"),
        name="norm_mod",
    )(x_all, g.reshape(1, D), mods)


def _matmul_kernel(a_ref, b_ref, o_ref):
    o_ref[...] = jnp.dot(a_ref[...], b_ref[...], preferred_element_type=f32).astype(o_ref.dtype)


def _proj_row_tile(rows):
    return next(t for t in (1024, 512, ROW_TILE) if rows % t == 0)


def matmul(a, b, *, tm, tn, out_dtype=f32, name="matmul"):
    M, K = a.shape
    N = b.shape[1]
    assert M % tm == 0 and N % tn == 0, (M, N, tm, tn)
    return pl.pallas_call(
        _matmul_kernel,
        grid=(M // tm, N // tn),
        in_specs=[pl.BlockSpec((tm, K), lambda i, j: (i, 0)),
                  pl.BlockSpec((K, tn), lambda i, j: (0, j))],
        out_specs=pl.BlockSpec((tm, tn), lambda i, j: (i, j)),
        out_shape=jax.ShapeDtypeStruct((M, N), out_dtype),
        compiler_params=_params("arbitrary", "arbitrary"),
        name=name,
    )(a, b)


def _conv3(x, prev_row, next_row, w, b):
    tm = x.shape[0]
    row = lax.broadcasted_iota(jnp.int32, x.shape, 0)
    xm = jnp.where(row == 0, prev_row, pltpu.roll(x, 1, 0))
    xp = jnp.where(row == tm - 1, next_row, pltpu.roll(x, tm - 1, 0))
    return w[0:1] * xm + w[1:2] * x + w[2:3] * xp + b


def _seq_edges(i, tiles_per_batch):
    r = i % tiles_per_batch
    lat_tiles = tiles_per_batch - 1
    first = jnp.logical_or(r == 0, r == lat_tiles)
    last = jnp.logical_or(r == lat_tiles - 1, r == lat_tiles)
    return first, last


HALO = 16


def _halo_rows(prev_ref, next_ref, first, last):
    prev_row = jnp.where(first, 0.0, prev_ref[...].astype(f32)[HALO - 1:HALO, :])
    next_row = jnp.where(last, 0.0, next_ref[...].astype(f32)[0:1, :])
    return prev_row, next_row


def _halo_specs(col_block, tc, n_rows):
    per = ROW_TILE // HALO
    n_blocks = n_rows // HALO
    prev = pl.BlockSpec((HALO, tc), lambda i, c: (jnp.maximum(i * per - 1, 0), col_block(c)))
    nxt = pl.BlockSpec((HALO, tc), lambda i, c: (jnp.minimum((i + 1) * per, n_blocks - 1), col_block(c)))
    return prev, nxt


def _hyena_pre_kernel(tiles_per_batch, x0_ref, x0p_ref, x0n_ref, x1_ref, x1p_ref, x1n_ref,
                      v_ref, vp_ref, vn_ref, w0_ref, w1_ref, w2_ref, b0_ref, b1_ref, b2_ref,
                      x0_out, g_out, gz_out):
    first, last = _seq_edges(pl.program_id(0), tiles_per_batch)

    def conv(x_ref, p_ref, n_ref, w_ref, b_ref):
        prev_row, next_row = _halo_rows(p_ref, n_ref, first, last)
        return _conv3(x_ref[...].astype(f32), prev_row, next_row, w_ref[...], b_ref[...])

    x0 = conv(x0_ref, x0p_ref, x0n_ref, w0_ref, b0_ref)
    x1 = conv(x1_ref, x1p_ref, x1n_ref, w1_ref, b1_ref)
    v = conv(v_ref, vp_ref, vn_ref, w2_ref, b2_ref)
    g = v * x1
    x0_out[...] = x0
    g_out[...] = g
    gz_out[...] = g.astype(gz_out.dtype)


def hyena_pre(p, conv_w, conv_b, tiles_per_batch, *, tc=512):
    R = p.shape[0]
    W = HY_WIDTH
    nb = W // tc
    specs = []
    for part in range(3):
        col = functools.partial(lambda c, part: part * nb + c, part=part)
        main = pl.BlockSpec((ROW_TILE, tc), functools.partial(lambda i, c, col: (i, col(c)), col=col))
        prev, nxt = _halo_specs(col, tc, R)
        specs += [main, prev, nxt]
    wspecs = [pl.BlockSpec((3, tc), functools.partial(lambda i, c, part: (0, part * nb + c), part=part))
              for part in range(3)]
    bspecs = [pl.BlockSpec((1, tc), functools.partial(lambda i, c, part: (0, part * nb + c), part=part))
              for part in range(3)]
    out_spec = pl.BlockSpec((ROW_TILE, tc), lambda i, c: (i, c))
    args = [p] * 9 + [conv_w] * 3 + [conv_b.reshape(1, -1)] * 3
    return pl.pallas_call(
        functools.partial(_hyena_pre_kernel, tiles_per_batch),
        grid=(R // ROW_TILE, nb),
        in_specs=specs + wspecs + bspecs,
        out_specs=[out_spec, out_spec, out_spec],
        out_shape=[jax.ShapeDtypeStruct((R, W), f32), jax.ShapeDtypeStruct((R, W), f32),
                   jax.ShapeDtypeStruct((R, W), bf16)],
        compiler_params=_params("arbitrary", "arbitrary"),
        name="hyena_pre",
    )(*args)


def _filter_kernel(n, rows, bands_ref, w1t_ref, w1c_ref, w1s_ref, b1_ref, fr_ref, w2_ref, b2_ref,
                   w3_ref, dl_ref, o_ref, nrm_ref):
    step = pl.program_id(0)
    j = step * rows + lax.broadcasted_iota(jnp.int32, (rows, 1), 0)
    d = jnp.where(j < n, j, 2 * n - j)
    valid = j != n
    df = d.astype(f32)
    t = df / float(n - 1)
    ang = (2.0 * math.pi / n) * df * bands_ref[...]
    fr = fr_ref[...]
    z1 = (t * w1t_ref[...]
          + jnp.dot(jnp.cos(ang), w1c_ref[...], preferred_element_type=f32, precision=HIGHEST)
          - jnp.dot(jnp.sin(ang), w1s_ref[...], preferred_element_type=f32, precision=HIGHEST)
          + b1_ref[...])
    hdn = jnp.sin(fr * z1)
    hdn = jnp.sin(fr * (jnp.dot(hdn, w2_ref[...], preferred_element_type=f32, precision=HIGHEST) + b2_ref[...]))
    h = jnp.dot(hdn.astype(bf16), w3_ref[0].astype(bf16), preferred_element_type=f32)
    h = h * jnp.exp(-t * jnp.abs(dl_ref[...]))
    h = jnp.where(valid, h, 0.0)
    o_ref[...] = h

    @pl.when(step == 0)
    def _():
        nrm_ref[...] = jnp.zeros_like(nrm_ref)

    nrm_ref[...] += jnp.sum(jnp.abs(h), axis=0, keepdims=True)


def hyena_filter(n, w1, b1, freq, w2, b2, w3, *, rows=256):
    W = HY_WIDTH
    Hd = HY_HIDDEN
    bands = jnp.linspace(1e-4, HY_BANDS - 1, HY_BANDS, dtype=f32).reshape(1, HY_BANDS)
    max_decay = math.log(HY_TARGET) / HY_SHORT_DECAY_PCT
    min_decay = math.log(HY_TARGET) / HY_LONG_DECAY_PCT
    deltas = jnp.linspace(min_decay, max_decay, W, dtype=f32).reshape(1, W)
    steps = 2 * n // rows
    half_steps = n // rows
    full = lambda s: (0, 0)
    w3r = w3.reshape(Hd, 2, W).transpose(1, 0, 2)
    return pl.pallas_call(
        functools.partial(_filter_kernel, n, rows),
        grid=(steps,),
        in_specs=[
            pl.BlockSpec((1, HY_BANDS), full),
            pl.BlockSpec((1, Hd), full),
            pl.BlockSpec((HY_BANDS, Hd), full),
            pl.BlockSpec((HY_BANDS, Hd), full),
            pl.BlockSpec((1, Hd), full),
            pl.BlockSpec((1, Hd), full),
            pl.BlockSpec((Hd, Hd), full),
            pl.BlockSpec((1, Hd), full),
            pl.BlockSpec((1, Hd, W), lambda s: (jnp.where(s * rows < n, 0, 1), 0, 0)),
            pl.BlockSpec((1, W), full),
        ],
        out_specs=[pl.BlockSpec((rows, W), lambda s: (s, 0)), pl.BlockSpec((1, W), full)],
        out_shape=[jax.ShapeDtypeStruct((2 * n, W), f32), jax.ShapeDtypeStruct((1, W), f32)],
        compiler_params=_params("arbitrary"),
        name="hyena_filter",
    )(bands, w1[0:1], w1[1:1 + HY_BANDS], w1[1 + HY_BANDS:], b1.reshape(1, Hd), freq.reshape(1, Hd),
      w2, b2.reshape(1, Hd), w3r, deltas)


@functools.lru_cache(maxsize=None)
def _fft_constants():
    L1, L2 = FFT_L1, FFT_L2
    L = L1 * L2
    k1 = np.arange(L1)
    nh = np.arange(L1 // 2)
    th = 2.0 * np.pi * np.outer(k1, nh) / L1
    m1r = np.zeros((2 * L1, L1 // 2))
    m1i = np.zeros((2 * L1, L1 // 2))
    m1r[0::2], m1r[1::2] = np.cos(th), -np.sin(th)
    m1i[0::2], m1i[1::2] = np.sin(th), np.cos(th)
    thf = 2.0 * np.pi * np.outer(k1, np.arange(L1)) / L1
    m1f = np.zeros((2 * L1, L1))
    m1f[0::2], m1f[1::2] = np.cos(thf), -np.sin(thf)
    n2 = np.arange(L2)
    k2 = np.arange(L2)
    m = (k1[:, None, None] * n2[None, None, :] + L1 * k2[None, :, None] * n2[None, None, :]) % L
    ph = 2.0 * np.pi * m / L
    gr, gi = np.cos(ph), -np.sin(ph)
    g = np.concatenate([np.concatenate([gr, -gi], axis=2), np.concatenate([gi, gr], axis=2)], axis=1)
    gt = np.transpose(g, (0, 2, 1))
    thi = 2.0 * np.pi * np.outer(nh, k1) / L1
    minv = np.zeros((2, L1 // 2, 2 * L1))
    minv[0][:, 0::2], minv[0][:, 1::2] = np.cos(thi), -np.sin(thi)
    minv[1][:, 0::2], minv[1][:, 1::2] = np.sin(thi), np.cos(thi)
    cast = lambda a: np.asarray(a, dtype=np.float32)
    return dict(m1r=cast(m1r), m1i=cast(m1i), m1f=cast(m1f), g=cast(g), gt=cast(gt), minv=cast(minv))


@functools.lru_cache(maxsize=None)
def _ctx_dft_constants(n):
    L = 2 * n
    k = np.arange(L)
    th = 2.0 * np.pi * np.outer(k, np.arange(n)) / L
    c, s = np.cos(th), np.sin(th)
    mc = np.concatenate([np.concatenate([c, s], axis=1), np.concatenate([-s, c], axis=1)], axis=0)
    thf = 2.0 * np.pi * np.outer(k, np.arange(L)) / L
    mf = np.concatenate([np.cos(thf), -np.sin(thf)], axis=0)
    ct, st = c.T, s.T
    minv = np.concatenate([np.concatenate([ct, -st], axis=1), np.concatenate([st, ct], axis=1)], axis=0)
    cast = lambda a: np.asarray(a, dtype=np.float32)
    return dict(mc=cast(mc), mf=cast(mf), minv=cast(minv))


def _bf16_constants(consts):
    return {k: jnp.asarray(v).astype(bf16) for k, v in consts.items()}


def _fft1_kernel(re_ref, im_ref, mr_ref, mi_ref, o_ref):
    o_ref[0] = (jnp.dot(mr_ref[...], re_ref[0], preferred_element_type=f32)
                + jnp.dot(mi_ref[...], im_ref[0], preferred_element_type=f32)).astype(o_ref.dtype)


def fft_step1(gz3, m1r, m1i, *, tcol=8192):
    B, _, ncol = gz3.shape
    half = FFT_L1 // 2
    return pl.pallas_call(
        _fft1_kernel,
        grid=(B // 2, ncol // tcol),
        in_specs=[pl.BlockSpec((1, half, tcol), lambda p, j: (2 * p, 0, j)),
                  pl.BlockSpec((1, half, tcol), lambda p, j: (2 * p + 1, 0, j)),
                  pl.BlockSpec((2 * FFT_L1, half), lambda p, j: (0, 0)),
                  pl.BlockSpec((2 * FFT_L1, half), lambda p, j: (0, 0))],
        out_specs=pl.BlockSpec((1, 2 * FFT_L1, tcol), lambda p, j: (p, 0, j)),
        out_shape=jax.ShapeDtypeStruct((B // 2, 2 * FFT_L1, ncol), bf16),
        compiler_params=_params("arbitrary", "arbitrary"),
        name="fft_step1",
    )(gz3, gz3, m1r, m1i)


def _fft1_filter_kernel(x_ref, m_ref, o_ref):
    o_ref[...] = jnp.dot(m_ref[...], x_ref[...].astype(bf16), preferred_element_type=f32).astype(o_ref.dtype)


def fft_step1_filter(ts2, m1f, *, tcol=8192):
    _, ncol = ts2.shape
    return pl.pallas_call(
        _fft1_filter_kernel,
        grid=(ncol // tcol,),
        in_specs=[pl.BlockSpec((FFT_L1, tcol), lambda j: (0, j)),
                  pl.BlockSpec((2 * FFT_L1, FFT_L1), lambda j: (0, 0))],
        out_specs=pl.BlockSpec((2 * FFT_L1, tcol), lambda j: (0, j)),
        out_shape=jax.ShapeDtypeStruct((2 * FFT_L1, ncol), bf16),
        compiler_params=_params("arbitrary"),
        name="fft_step1_filter",
    )(ts2, m1f)


def _cmul(yr, yi, hr, hi):
    return yr * hr - yi * hi, yr * hi + yi * hr


def _fft2_filter_kernel(kc, a_ref, g_ref, nrm_ref, o_ref):
    scale = 1.0 / (nrm_ref[...] * float(FFT_L1 * FFT_L2))
    for kk in range(kc):
        o_ref[kk] = jnp.dot(g_ref[kk], a_ref[kk], preferred_element_type=f32) * scale


def fft_step2_filter(af, g, nrm, *, kc=8, ct=256):
    L1, R2, W = af.shape
    return pl.pallas_call(
        functools.partial(_fft2_filter_kernel, kc),
        grid=(L1 // kc, W // ct),
        in_specs=[pl.BlockSpec((kc, R2, ct), lambda k, c: (k, 0, c)),
                  pl.BlockSpec((kc, R2, R2), lambda k, c: (k, 0, 0)),
                  pl.BlockSpec((1, ct), lambda k, c: (0, c))],
        out_specs=pl.BlockSpec((kc, R2, ct), lambda k, c: (k, 0, c)),
        out_shape=jax.ShapeDtypeStruct((L1, R2, W), f32),
        compiler_params=_params("arbitrary", "arbitrary"),
        name="fft_step2_filter",
    )(af, g, nrm)


def _fft2_kernel(kc, a_ref, g_ref, gt_ref, h_ref, o_ref):
    half = FFT_L2
    for kk in range(kc):
        y = jnp.dot(g_ref[kk], a_ref[0, kk], preferred_element_type=f32)
        pr, pi = _cmul(y[:half], y[half:], h_ref[kk, :half], h_ref[kk, half:])
        pcat = jnp.concatenate([pr, pi], axis=0).astype(bf16)
        o_ref[0, kk] = jnp.dot(gt_ref[kk], pcat, preferred_element_type=f32).astype(o_ref.dtype)


def fft_step2(a4, g, gt, hf, *, kc=8, ct=256):
    P, L1, R2, W = a4.shape
    return pl.pallas_call(
        functools.partial(_fft2_kernel, kc),
        grid=(L1 // kc, W // ct, P),
        in_specs=[pl.BlockSpec((1, kc, R2, ct), lambda k, c, p: (p, k, 0, c)),
                  pl.BlockSpec((kc, R2, R2), lambda k, c, p: (k, 0, 0)),
                  pl.BlockSpec((kc, R2, R2), lambda k, c, p: (k, 0, 0)),
                  pl.BlockSpec((kc, R2, ct), lambda k, c, p: (k, 0, c))],
        out_specs=pl.BlockSpec((1, kc, R2, ct), lambda k, c, p: (p, k, 0, c)),
        out_shape=jax.ShapeDtypeStruct((P, L1, R2, W), bf16),
        compiler_params=_params("arbitrary", "arbitrary", "arbitrary"),
        name="fft_step2",
    )(a4, g, gt, hf)


def _fft3_kernel(b_ref, m_ref, o_ref):
    o_ref[0] = jnp.dot(m_ref[0], b_ref[0], preferred_element_type=f32)


def fft_step3(b3, minv, rows_total, *, tcol=8192):
    P, _, ncol = b3.shape
    half = FFT_L1 // 2
    return pl.pallas_call(
        _fft3_kernel,
        grid=(2 * P, ncol // tcol),
        in_specs=[pl.BlockSpec((1, 2 * FFT_L1, tcol), lambda b, j: (b // 2, 0, j)),
                  pl.BlockSpec((1, half, 2 * FFT_L1), lambda b, j: (b % 2, 0, 0))],
        out_specs=pl.BlockSpec((1, half, tcol), lambda b, j: (b, 0, j)),
        out_shape=jax.ShapeDtypeStruct((2 * P, rows_total, ncol), f32),
        compiler_params=_params("arbitrary", "arbitrary"),
        name="fft_step3",
    )(b3, minv)


def _ctx_filter_kernel(n, ts_ref, m_ref, nrm_ref, o_ref):
    scale = 1.0 / (nrm_ref[...] * float(2 * n))
    o_ref[...] = jnp.dot(m_ref[...], ts_ref[...].astype(bf16), preferred_element_type=f32) * scale


def ctx_filter_spectrum(ts, mf, nrm, *, ct=256):
    L, W = ts.shape
    return pl.pallas_call(
        functools.partial(_ctx_filter_kernel, L // 2),
        grid=(W // ct,),
        in_specs=[pl.BlockSpec((L, ct), lambda c: (0, c)),
                  pl.BlockSpec((2 * L, L), lambda c: (0, 0)),
                  pl.BlockSpec((1, ct), lambda c: (0, c))],
        out_specs=pl.BlockSpec((2 * L, ct), lambda c: (0, c)),
        out_shape=jax.ShapeDtypeStruct((2 * L, W), f32),
        compiler_params=_params("arbitrary"),
        name="ctx_filter_spectrum",
    )(ts, mf, nrm)


def _ctx_conv_kernel(n, re_ref, im_ref, mc_ref, minv_ref, h_ref, y_in_ref, o_ref):
    del y_in_ref
    L = 2 * n
    z = jnp.concatenate([re_ref[0], im_ref[0]], axis=0)
    y = jnp.dot(mc_ref[...], z, preferred_element_type=f32)
    pr, pi = _cmul(y[:L], y[L:], h_ref[:L], h_ref[L:])
    pcat = jnp.concatenate([pr, pi], axis=0).astype(bf16)
    out = jnp.dot(minv_ref[...], pcat, preferred_element_type=f32)
    o_ref[0, 0] = out[:n]
    o_ref[0, 1] = out[n:]


def ctx_long_conv(gz, yconv, hfc, consts, n_lat, n_ctx, *, ct=256):
    B, T, W = gz.shape
    blk = n_lat // n_ctx
    L = 2 * n_ctx
    y4 = yconv.reshape(B // 2, 2, T, W)
    out = pl.pallas_call(
        functools.partial(_ctx_conv_kernel, n_ctx),
        grid=(B // 2, W // ct),
        in_specs=[pl.BlockSpec((1, n_ctx, ct), lambda p, c: (2 * p, blk, c)),
                  pl.BlockSpec((1, n_ctx, ct), lambda p, c: (2 * p + 1, blk, c)),
                  pl.BlockSpec((2 * L, L), lambda p, c: (0, 0)),
                  pl.BlockSpec((L, 2 * L), lambda p, c: (0, 0)),
                  pl.BlockSpec((2 * L, ct), lambda p, c: (0, c)),
                  pl.BlockSpec(memory_space=pl.ANY)],
        out_specs=pl.BlockSpec((1, 2, n_ctx, ct), lambda p, c: (p, 0, blk, c)),
        out_shape=jax.ShapeDtypeStruct(y4.shape, f32),
        input_output_aliases={5: 0},
        compiler_params=_params("arbitrary", "arbitrary"),
        name="ctx_long_conv",
    )(gz, gz, consts["mc"], consts["minv"], hfc, y4)
    return out.reshape(B, T, W)


def hyena_long_conv(gz, n_lat, n_ctx, f_w1, f_b1, f_freq, f_w2, f_b2, f_w3):
    B, T, W = gz.shape
    cst = _bf16_constants(_fft_constants())
    L1, L2 = FFT_L1, FFT_L2
    assert 2 * n_lat == L1 * L2 and T % L2 == 0 and B % 2 == 0
    ts, nrm = hyena_filter(n_lat, f_w1, f_b1, f_freq, f_w2, f_b2, f_w3)
    af = fft_step1_filter(ts.reshape(L1, L2 * W), cst["m1f"])
    hf = fft_step2_filter(af.reshape(L1, 2 * L2, W), cst["g"], nrm)
    a = fft_step1(gz.reshape(B, T // L2, L2 * W), cst["m1r"], cst["m1i"])
    bq = fft_step2(a.reshape(B // 2, L1, 2 * L2, W), cst["g"], cst["gt"], hf)
    y = fft_step3(bq.reshape(B // 2, 2 * L1, L2 * W), cst["minv"], T // L2).reshape(B, T, W)
    ccst = _bf16_constants(_ctx_dft_constants(n_ctx))
    ts_c, nrm_c = hyena_filter(n_ctx, f_w1, f_b1, f_freq, f_w2, f_b2, f_w3)
    hfc = ctx_filter_spectrum(ts_c, ccst["mf"], nrm_c)
    return ctx_long_conv(gz, y, hfc, ccst, n_lat, n_ctx)


@functools.lru_cache(maxsize=None)
def _rope_tables(n_lat, n_ctx):
    half = HEAD_DIM // 2
    nf = half // 2
    inv = ROPE_THETA ** (-np.arange(nf, dtype=np.float64) / nf)
    t = np.arange(n_lat)
    pos = np.stack([t // GRID_W, t % GRID_W], axis=1).astype(np.float64)
    ang = pos[:, :, None] * inv[None, None, :]
    cos = np.concatenate([np.cos(ang), np.cos(ang)], axis=2).reshape(n_lat, HEAD_DIM)
    sin = np.concatenate([-np.sin(ang), np.sin(ang)], axis=2).reshape(n_lat, HEAD_DIM)
    cos = np.concatenate([cos, np.ones((n_ctx, HEAD_DIM))], axis=0)
    sin = np.concatenate([sin, np.zeros((n_ctx, HEAD_DIM))], axis=0)
    return np.asarray(cos, np.float32), np.asarray(sin, np.float32)


def _norm_rope(x, w, cos, sin, lo_lane):
    y = x * lax.rsqrt(jnp.mean(x * x, axis=-1, keepdims=True) + NORM_EPS) * w
    nf = HEAD_DIM // 4
    partner = jnp.where(lo_lane, pltpu.roll(y, HEAD_DIM - nf, 1), pltpu.roll(y, nf, 1))
    return y * cos + partner * sin


def _qkv_prep_kernel(q_ref, k_ref, v_ref, cos_ref, sin_ref, qn_ref, kn_ref, q_out, kt_out, v_out):
    cos = cos_ref[...]
    sin = sin_ref[...]
    lane = lax.broadcasted_iota(jnp.int32, cos.shape, 1)
    lo_lane = (lane % (HEAD_DIM // 2)) < (HEAD_DIM // 4)
    scale = HEAD_DIM ** -0.5 * math.log2(math.e)
    for h in range(ATT_HEADS):
        sl = slice(h * HEAD_DIM, (h + 1) * HEAD_DIM)
        q_out[:, sl] = (_norm_rope(q_ref[:, sl].astype(f32), qn_ref[...], cos, sin, lo_lane) * scale).astype(q_out.dtype)
    ones_col = (lane == 0).astype(v_out.dtype)
    for h in range(ATT_KV_HEADS):
        sl = slice(h * HEAD_DIM, (h + 1) * HEAD_DIM)
        k = _norm_rope(k_ref[:, sl].astype(f32), kn_ref[...], cos, sin, lo_lane)
        kt_out[0, sl, :] = k.T.astype(kt_out.dtype)
        v_out[:, 2 * h * HEAD_DIM:(2 * h + 1) * HEAD_DIM] = v_ref[:, sl].astype(v_out.dtype)
        v_out[:, (2 * h + 1) * HEAD_DIM:(2 * h + 2) * HEAD_DIM] = ones_col


def qkv_prep(p, q_norm, k_norm, B, n_lat, n_ctx):
    R = p.shape[0]
    T = n_lat + n_ctx
    tpb = T // ROW_TILE
    cos, sin = _rope_tables(n_lat, n_ctx)
    return pl.pallas_call(
        _qkv_prep_kernel,
        grid=(R // ROW_TILE,),
        in_specs=[pl.BlockSpec((ROW_TILE, ATT_WIDTH), lambda i: (i, EVEN_OFF_Q // ATT_WIDTH)),
                  pl.BlockSpec((ROW_TILE, ATT_KV_WIDTH), lambda i: (i, EVEN_OFF_K // ATT_KV_WIDTH)),
                  pl.BlockSpec((ROW_TILE, ATT_KV_WIDTH), lambda i: (i, EVEN_OFF_V // ATT_KV_WIDTH)),
                  pl.BlockSpec((ROW_TILE, HEAD_DIM), lambda i: (i % tpb, 0)),
                  pl.BlockSpec((ROW_TILE, HEAD_DIM), lambda i: (i % tpb, 0)),
                  pl.BlockSpec((1, HEAD_DIM), lambda i: (0, 0)),
                  pl.BlockSpec((1, HEAD_DIM), lambda i: (0, 0))],
        out_specs=[pl.BlockSpec((ROW_TILE, ATT_WIDTH), lambda i: (i, 0)),
                   pl.BlockSpec((1, ATT_KV_WIDTH, ROW_TILE), lambda i: (i // tpb, 0, i % tpb)),
                   pl.BlockSpec((ROW_TILE, 2 * ATT_KV_WIDTH), lambda i: (i, 0))],
        out_shape=[jax.ShapeDtypeStruct((R, ATT_WIDTH), bf16),
                   jax.ShapeDtypeStruct((B, ATT_KV_WIDTH, T), bf16),
                   jax.ShapeDtypeStruct((R, 2 * ATT_KV_WIDTH), bf16)],
        compiler_params=_params("arbitrary"),
        name="qkv_prep",
    )(p, p, p, jnp.asarray(cos), jnp.asarray(sin), q_norm.reshape(1, HEAD_DIM), k_norm.reshape(1, HEAD_DIM))


ATT_CHUNK = 256


def _attention_kernel(n_lat, q_ref, kt_ref, v_ref, o_ref, qs_ref, sa_ref, sb_ref, pa_ref, pb_ref, os_ref):
    tq = q_ref.shape[0]
    n_chunks = ATT_GROUP * tq // ATT_CHUNK
    for h in range(ATT_GROUP):
        qs_ref[h * tq:(h + 1) * tq, :] = q_ref[:, h * HEAD_DIM:(h + 1) * HEAD_DIM]

    def rows(c):
        return pl.ds(pl.multiple_of(c * ATT_CHUNK, ATT_CHUNK), ATT_CHUNK)

    def finish(o):
        for h in range(ATT_GROUP):
            o_ref[:, h * HEAD_DIM:(h + 1) * HEAD_DIM] = os_ref[h * tq:(h + 1) * tq, :].astype(o_ref.dtype)

    s_refs = (sa_ref, sb_ref)
    p_refs = (pa_ref, pb_ref)

    def run(k_lo):
        def scores(c, slot):
            s_refs[slot][:, k_lo:] = jnp.dot(qs_ref[rows(c), :], kt_ref[0, :, k_lo:], preferred_element_type=f32)

        def exponentials(slot):
            s = s_refs[slot][:, k_lo:]
            m = jnp.max(s, axis=-1, keepdims=True)
            p_refs[slot][:, k_lo:] = jnp.exp2(s - m).astype(bf16)

        def weighted_values(c, slot):
            r = jnp.dot(p_refs[slot][:, k_lo:], v_ref[0, k_lo:, :], preferred_element_type=f32)
            os_ref[rows(c), :] = r[:, :HEAD_DIM] / r[:, HEAD_DIM:HEAD_DIM + 1]

        def stage(c, slot):
            scores(c + 1, 1 - slot)
            exponentials(slot)
            weighted_values(c - 1, 1 - slot)

        scores(0, 0)
        exponentials(0)
        scores(1, 1)

        def body(i, carry):
            stage(2 * i + 1, 1)
            stage(2 * i + 2, 0)
            return carry

        lax.fori_loop(0, (n_chunks - 2) // 2, body, 0)
        exponentials((n_chunks - 1) % 2)
        weighted_values(n_chunks - 2, n_chunks % 2)
        weighted_values(n_chunks - 1, (n_chunks - 1) % 2)
        finish(None)

    @pl.when(pl.program_id(2) < pl.num_programs(2) - 1)
    def _():
        run(0)

    @pl.when(pl.program_id(2) == pl.num_programs(2) - 1)
    def _():
        run(n_lat)


def attention(q, kt, v, B, n_lat, n_ctx):
    R = q.shape[0]
    T = n_lat + n_ctx
    tq = ROW_TILE
    assert n_ctx == tq
    qt = T // tq
    gw = ATT_GROUP * HEAD_DIM
    return pl.pallas_call(
        functools.partial(_attention_kernel, n_lat),
        grid=(B, ATT_KV_HEADS, qt),
        in_specs=[pl.BlockSpec((tq, gw), lambda b, g, i: (b * qt + i, g)),
                  pl.BlockSpec((1, HEAD_DIM, T), lambda b, g, i: (b, g, 0)),
                  pl.BlockSpec((1, T, 2 * HEAD_DIM), lambda b, g, i: (b, 0, g))],
        out_specs=pl.BlockSpec((tq, gw), lambda b, g, i: (b * qt + i, g)),
        out_shape=jax.ShapeDtypeStruct((R, ATT_WIDTH), bf16),
        scratch_shapes=[pltpu.VMEM((ATT_GROUP * tq, HEAD_DIM), bf16),
                        pltpu.VMEM((ATT_CHUNK, T), f32), pltpu.VMEM((ATT_CHUNK, T), f32),
                        pltpu.VMEM((ATT_CHUNK, T), bf16), pltpu.VMEM((ATT_CHUNK, T), bf16),
                        pltpu.VMEM((ATT_GROUP * tq, HEAD_DIM), f32)],
        compiler_params=_params("arbitrary", "arbitrary", "arbitrary"),
        name="attention",
    )(q, kt, v.reshape(B, T, 2 * ATT_KV_WIDTH))


def _post_residual(x, y, g_post, gate):
    yn = y * lax.rsqrt(jnp.mean(y * y, axis=-1, keepdims=True) + NORM_EPS) * g_post
    return x + gate * yn


def _silu(x):
    return x * jax.nn.sigmoid(x)


def _even_out_kernel(x0_ref, g_ref, yc_ref, ghy_ref, att_ref, ga0_ref, ga1_ref, bias_ref, w_ref,
                     gp_ref, m_ref, x_ref, o_ref):
    D = x_ref.shape[-1]
    g = g_ref[...]
    hy = x0_ref[...] * (yc_ref[...] + g * bias_ref[...]) * _silu(ghy_ref[...].astype(f32))
    g_att = jnp.concatenate([ga0_ref[...], ga1_ref[...]], axis=1).astype(f32)
    at = att_ref[...].astype(f32) * _silu(g_att)
    lhs = jnp.concatenate([hy, at], axis=1).astype(bf16)
    y = jnp.dot(lhs, w_ref[...], preferred_element_type=f32)
    o_ref[...] = _post_residual(x_ref[...], y, gp_ref[...], m_ref[0, :, 2 * D:3 * D])


def even_out(x0, g, yconv, p, att, hy_bias, w_out, g_post, mods, x_all, tiles_per_batch, ctx_row):
    R, D = x_all.shape
    W = HY_WIDTH
    hw = ATT_WIDTH // 2
    row = lambda i: (i, 0)
    full = lambda i: (0, 0)
    return pl.pallas_call(
        _even_out_kernel,
        grid=(R // ROW_TILE,),
        in_specs=[pl.BlockSpec((ROW_TILE, W), row),
                  pl.BlockSpec((ROW_TILE, W), row),
                  pl.BlockSpec((ROW_TILE, W), row),
                  pl.BlockSpec((ROW_TILE, W), lambda i: (i, EVEN_OFF_GHY // W)),
                  pl.BlockSpec((ROW_TILE, ATT_WIDTH), row),
                  pl.BlockSpec((ROW_TILE, hw), lambda i: (i, EVEN_OFF_GATT // hw)),
                  pl.BlockSpec((ROW_TILE, hw), lambda i: (i, EVEN_OFF_GATT // hw + 1)),
                  pl.BlockSpec((1, W), full),
                  pl.BlockSpec((W + ATT_WIDTH, D), full),
                  pl.BlockSpec((1, D), full),
                  pl.BlockSpec((1, 1, 3 * D), lambda i: (_mod_row(i, tiles_per_batch, ctx_row), 0, 0)),
                  pl.BlockSpec((ROW_TILE, D), row)],
        out_specs=pl.BlockSpec((ROW_TILE, D), row),
        out_shape=jax.ShapeDtypeStruct((R, D), f32),
        compiler_params=_params("arbitrary"),
        name="even_out",
    )(x0, g, yconv, p, att, p, p, hy_bias.reshape(1, W), w_out.astype(bf16), g_post.reshape(1, D), mods, x_all)


def even_layer(x_all, mods, B, n_lat, n_ctx, g_pre, g_post, w_in, conv_w, conv_b, f_w1, f_b1, f_freq,
               f_w2, f_b2, f_w3, hy_bias, q_norm, k_norm, w_out):
    T = n_lat + n_ctx
    tpb = T // ROW_TILE
    h = norm_mod(x_all, g_pre, mods, tpb, B)
    p = matmul(h, w_in.astype(bf16), tm=_proj_row_tile(B * T), tn=512, out_dtype=bf16, name="even_in_proj")
    x0, g, gz = hyena_pre(p, conv_w, conv_b, tpb)
    yconv = hyena_long_conv(gz.reshape(B, T, HY_WIDTH), n_lat, n_ctx, f_w1, f_b1, f_freq, f_w2, f_b2, f_w3)
    q, kt, v = qkv_prep(p, q_norm, k_norm, B, n_lat, n_ctx)
    att = attention(q, kt, v, B, n_lat, n_ctx)
    return even_out(x0, g, yconv.reshape(B * T, HY_WIDTH), p, att, hy_bias, w_out, g_post, mods, x_all, tpb, B)


def _mlstm_prep_kernel(tiles_per_batch, q_ref, qp_ref, qn_ref, k_ref, kp_ref, kn_ref,
                       wq_ref, wk_ref, bq_ref, bk_ref, q_out, kt_out):
    first, last = _seq_edges(pl.program_id(0), tiles_per_batch)
    prev_row, next_row = _halo_rows(qp_ref, qn_ref, first, last)
    q = _silu(_conv3(q_ref[...].astype(f32), prev_row, next_row, wq_ref[...], bq_ref[...]))
    q_out[...] = q.astype(q_out.dtype)
    prev_row, next_row = _halo_rows(kp_ref, kn_ref, first, last)
    k = _silu(_conv3(k_ref[...].astype(f32), prev_row, next_row, wk_ref[...], bk_ref[...])) * (ML_QK ** -0.5)
    for h in range(k.shape[1] // ML_QK):
        sl = slice(h * ML_QK, (h + 1) * ML_QK)
        kt_out[0, sl, :] = k[:, sl].T.astype(kt_out.dtype)


def mlstm_prep(p, conv_w, conv_b, B, T, *, tc=512):
    R = p.shape[0]
    tpb = T // ROW_TILE
    nb = ML_QK_WIDTH // tc
    qcol = lambda c: c
    kcol = lambda c: nb + c
    qprev, qnext = _halo_specs(qcol, tc, R)
    kprev, knext = _halo_specs(kcol, tc, R)
    return pl.pallas_call(
        functools.partial(_mlstm_prep_kernel, tpb),
        grid=(R // ROW_TILE, nb),
        in_specs=[pl.BlockSpec((ROW_TILE, tc), lambda i, c: (i, c)), qprev, qnext,
                  pl.BlockSpec((ROW_TILE, tc), lambda i, c: (i, nb + c)), kprev, knext,
                  pl.BlockSpec((3, tc), lambda i, c: (0, c)),
                  pl.BlockSpec((3, tc), lambda i, c: (0, nb + c)),
                  pl.BlockSpec((1, tc), lambda i, c: (0, c)),
                  pl.BlockSpec((1, tc), lambda i, c: (0, nb + c))],
        out_specs=[pl.BlockSpec((ROW_TILE, tc), lambda i, c: (i, c)),
                   pl.BlockSpec((1, tc, ROW_TILE), lambda i, c: (i // tpb, c, i % tpb))],
        out_shape=[jax.ShapeDtypeStruct((R, ML_QK_WIDTH), bf16),
                   jax.ShapeDtypeStruct((B, ML_QK_WIDTH, T), bf16)],
        compiler_params=_params("arbitrary", "arbitrary"),
        name="mlstm_prep",
    )(p, p, p, p, p, p, conv_w, conv_w, conv_b.reshape(1, -1), conv_b.reshape(1, -1))


def _log_sigmoid(x):
    return jnp.minimum(x, 0.0) - jnp.log(1.0 + jnp.exp(-jnp.abs(x)))


def _mlstm_gates_kernel(g_ref, b_ref, gc_out, gr_out):
    pre = g_ref[...] + b_ref[...]
    lane = lax.broadcasted_iota(jnp.int32, pre.shape, 1)
    is_forget = (lane // ML_HEADS) % 2 == 1
    gc = jnp.where(is_forget, _log_sigmoid(pre), pre)
    gc_out[...] = gc
    gr_out[0] = gc.T


def mlstm_gates(gates, gate_b, B, T):
    R = gates.shape[0]
    tpb = T // ROW_TILE
    gb = jnp.pad(gate_b, (0, LANE - gate_b.shape[0])).reshape(1, LANE)
    return pl.pallas_call(
        _mlstm_gates_kernel,
        grid=(R // ROW_TILE,),
        in_specs=[pl.BlockSpec((ROW_TILE, LANE), lambda i: (i, 0)),
                  pl.BlockSpec((1, LANE), lambda i: (0, 0))],
        out_specs=[pl.BlockSpec((ROW_TILE, LANE), lambda i: (i, 0)),
                   pl.BlockSpec((1, LANE, ROW_TILE), lambda i: (i // tpb, 0, i % tpb))],
        out_shape=[jax.ShapeDtypeStruct((R, LANE), f32), jax.ShapeDtypeStruct((B, LANE, T), f32)],
        compiler_params=_params("arbitrary"),
        name="mlstm_gates",
    )(gates, gb)


def _mlstm_scan_kernel(reverse, q_ref, kt_ref, v_ref, gc_ref, gr_ref, o_ref, ct_ref, m_ref):
    Lc = q_ref.shape[0]
    H = ML_HEADS
    i_off = 2 * H if reverse else 0
    f_off = i_off + H

    @pl.when(pl.program_id(1) == 0)
    def _():
        ct_ref[...] = jnp.zeros_like(ct_ref)
        m_ref[...] = jnp.zeros_like(m_ref)

    t_idx = lax.broadcasted_iota(jnp.int32, (Lc, Lc), 0)
    s_idx = lax.broadcasted_iota(jnp.int32, (Lc, Lc), 1)
    causal = (s_idx >= t_idx) if reverse else (s_idx <= t_idx)
    tri = causal.astype(f32)
    gc = gc_ref[...]
    gr = gr_ref[0]
    b_col_all = jnp.dot(tri, gc[:, f_off:f_off + H], preferred_element_type=f32, precision=HIGHEST)
    b_row_all = lax.dot_general(gr[f_off:f_off + H, :], tri, (((1,), (1,)), ((), ())),
                                preferred_element_type=f32, precision=HIGHEST)
    end = 0 if reverse else Lc - 1
    ones_col = (lax.broadcasted_iota(jnp.int32, (Lc, LANE), 1) == 0).astype(f32)
    neg_inf = -jnp.inf

    for h in range(H):
        q = q_ref[:, h * ML_QK:(h + 1) * ML_QK]
        kt = kt_ref[0, h * ML_QK:(h + 1) * ML_QK, :]
        v = jnp.concatenate([v_ref[:, h * ML_V:(h + 1) * ML_V].astype(f32), ones_col], axis=1)
        i_col = gc[:, i_off + h:i_off + h + 1]
        i_row = gr[i_off + h:i_off + h + 1, :]
        b_col = b_col_all[:, h:h + 1]
        b_row = b_row_all[h:h + 1, :]
        b_end = b_row[:, end:end + 1]
        m_prev = m_ref[h, 0:1, 0:1]

        d = jnp.where(causal, b_col - b_row + i_row, neg_inf)
        inter = b_col + m_prev
        m_row = jnp.maximum(inter, jnp.max(d, axis=-1, keepdims=True))
        s = jnp.dot(q, kt, preferred_element_type=f32) * jnp.exp(d - m_row)
        w_prev = jnp.exp(inter - m_row)
        ct = ct_ref[h]
        tot = (jnp.dot(s.astype(bf16), v.astype(bf16), preferred_element_type=f32)
               + w_prev * jnp.dot(q, ct.astype(bf16), preferred_element_type=f32))
        den = tot[:, ML_V:ML_V + 1]
        o_ref[:, h * ML_V:(h + 1) * ML_V] = tot[:, :ML_V] / jnp.maximum(jnp.abs(den), jnp.exp(-m_row))

        g_col = b_end - b_col + i_col
        g_row = b_end - b_row + i_row
        m_new = jnp.maximum(b_end + m_prev, jnp.max(g_row, axis=-1, keepdims=True))
        a_col = jnp.exp(g_col - m_new)
        a_prev = jnp.exp(b_end + m_prev - m_new)
        ct_ref[h] = a_prev * ct + jnp.dot(kt, (v * a_col).astype(bf16), preferred_element_type=f32)
        m_ref[h] = jnp.broadcast_to(m_new, m_ref.shape[1:])


def mlstm_scan(q, kt, p, gc, gr, B, T, reverse):
    R = q.shape[0]
    tpb = T // ROW_TILE
    lat = tpb - 1

    def chunk(j):
        return jnp.where(j == 0, lat, (lat - j) if reverse else (j - 1))

    return pl.pallas_call(
        functools.partial(_mlstm_scan_kernel, reverse),
        grid=(B, tpb),
        in_specs=[pl.BlockSpec((ROW_TILE, ML_QK_WIDTH), lambda b, j: (b * tpb + chunk(j), 0)),
                  pl.BlockSpec((1, ML_QK_WIDTH, ROW_TILE), lambda b, j: (b, 0, chunk(j))),
                  pl.BlockSpec((ROW_TILE, ML_WIDTH), lambda b, j: (b * tpb + chunk(j), ODD_OFF_V // ML_WIDTH)),
                  pl.BlockSpec((ROW_TILE, LANE), lambda b, j: (b * tpb + chunk(j), 0)),
                  pl.BlockSpec((1, LANE, ROW_TILE), lambda b, j: (b, 0, chunk(j)))],
        out_specs=pl.BlockSpec((ROW_TILE, ML_WIDTH), lambda b, j: (b * tpb + chunk(j), 0)),
        out_shape=jax.ShapeDtypeStruct((R, ML_WIDTH), f32),
        scratch_shapes=[pltpu.VMEM((ML_HEADS, ML_QK, ML_V + LANE), f32),
                        pltpu.VMEM((ML_HEADS, 8, LANE), f32)],
        compiler_params=_params("arbitrary", "arbitrary"),
        name="mlstm_scan_bwd" if reverse else "mlstm_scan_fwd",
    )(q, kt, p, gc, gr)


def _odd_out_kernel(hf_ref, hb_ref, o_ref, z_ref, hn_ref, w_ref, gp_ref, m_ref, x_ref, out_ref):
    D = x_ref.shape[-1]
    hs = (hf_ref[...] + hb_ref[...]) * jax.nn.sigmoid(o_ref[...].astype(f32))
    parts = []
    for h in range(ML_HEADS):
        seg = hs[:, h * ML_V:(h + 1) * ML_V]
        parts.append(seg * lax.rsqrt(jnp.mean(seg * seg, axis=-1, keepdims=True) + NORM_EPS))
    hn = jnp.concatenate(parts, axis=1) * hn_ref[...] * _silu(z_ref[...].astype(f32))
    y = jnp.dot(hn.astype(bf16), w_ref[...], preferred_element_type=f32)
    out_ref[...] = _post_residual(x_ref[...], y, gp_ref[...], m_ref[0, :, 2 * D:3 * D])


def odd_out(hf, hb, p, head_norm, w_out, g_post, mods, x_all, B, n_lat, T):
    D = x_all.shape[1]
    tpb = T // ROW_TILE
    lat = n_lat // ROW_TILE
    row = lambda b, i: (b * tpb + i, 0)
    full = lambda b, i: (0, 0)
    return pl.pallas_call(
        _odd_out_kernel,
        grid=(B, lat),
        in_specs=[pl.BlockSpec((ROW_TILE, ML_WIDTH), row),
                  pl.BlockSpec((ROW_TILE, ML_WIDTH), row),
                  pl.BlockSpec((ROW_TILE, ML_WIDTH), lambda b, i: (b * tpb + i, ODD_OFF_O // ML_WIDTH)),
                  pl.BlockSpec((ROW_TILE, ML_WIDTH), lambda b, i: (b * tpb + i, ODD_OFF_Z // ML_WIDTH)),
                  pl.BlockSpec((1, ML_WIDTH), full),
                  pl.BlockSpec((ML_WIDTH, D), full),
                  pl.BlockSpec((1, D), full),
                  pl.BlockSpec((1, 1, 3 * D), lambda b, i: (b, 0, 0)),
                  pl.BlockSpec((ROW_TILE, D), row)],
        out_specs=pl.BlockSpec((ROW_TILE, D), lambda b, i: (b * lat + i, 0)),
        out_shape=jax.ShapeDtypeStruct((B * n_lat, D), f32),
        compiler_params=_params("arbitrary", "arbitrary"),
        name="odd_out",
    )(hf, hb, p, p, head_norm.reshape(1, ML_WIDTH), w_out.astype(bf16), g_post.reshape(1, D), mods, x_all)


def odd_layer_last(x_all, mods, B, n_lat, n_ctx, g_pre, g_post, w_in, conv_w, conv_b, gate_b, head_norm, w_out):
    T = n_lat + n_ctx
    tpb = T // ROW_TILE
    h = norm_mod(x_all, g_pre, mods, tpb, B)
    wb = w_in.astype(bf16)
    tm = _proj_row_tile(B * T)
    p = matmul(h, wb[:, :ODD_MAIN], tm=tm, tn=512, out_dtype=bf16, name="odd_in_proj")
    n_gates = w_in.shape[1] - ODD_MAIN
    gates = matmul(h, jnp.pad(wb[:, ODD_MAIN:], ((0, 0), (0, LANE - n_gates))), tm=tm, tn=LANE, name="odd_gate_proj")
    q, kt = mlstm_prep(p, conv_w, conv_b, B, T)
    gc, gr = mlstm_gates(gates, gate_b, B, T)
    hf = mlstm_scan(q, kt, p, gc, gr, B, T, False)
    hb = mlstm_scan(q, kt, p, gc, gr, B, T, True)
    return odd_out(hf, hb, p, head_norm, w_out, g_post, mods, x_all, B, n_lat, T)


def kernel(x, c, ctx, c_ctx, w_mod, b_mod, g_pre, g_post, e_w_in, e_conv_w, e_conv_b, e_filt_w1,
           e_filt_b1, e_filt_freq, e_filt_w2, e_filt_b2, e_filt_w3, e_hy_bias, e_q_norm, e_k_norm,
           e_w_out, o_w_in, o_conv_w, o_conv_b, o_gate_b, o_head_norm, o_w_out):
    B, n_lat, D = x.shape
    n_ctx = ctx.shape[1]
    T = n_lat + n_ctx
    depth = w_mod.shape[0]
    assert depth == 2 and B + 1 <= 8 and n_ctx == ROW_TILE and n_lat % ROW_TILE == 0
    cond = jnp.concatenate([c, c_ctx[None], jnp.zeros((8 - B - 1, D), f32)], axis=0)
    mods_all = adaln_all(cond, w_mod, b_mod)
    x_all = jnp.concatenate([x, ctx], axis=1).reshape(B * T, D)
    x_all = even_layer(x_all, mods_all[0].reshape(8, 1, 3 * D), B, n_lat, n_ctx, g_pre[0], g_post[0], e_w_in[0],
                       e_conv_w[0], e_conv_b[0], e_filt_w1[0], e_filt_b1[0], e_filt_freq[0], e_filt_w2[0],
                       e_filt_b2[0], e_filt_w3[0], e_hy_bias[0], e_q_norm[0], e_k_norm[0], e_w_out[0])
    out = odd_layer_last(x_all, mods_all[1].reshape(8, 1, 3 * D), B, n_lat, n_ctx, g_pre[1], g_post[1], o_w_in[0],
                         o_conv_w[0], o_conv_b[0], o_gate_b[0], o_head_norm[0], o_w_out[0])
    return out.reshape(B, n_lat, D)
```

```python
import functools
import math

import numpy as np
import jax
import jax.numpy as jnp
from jax import lax
from jax.experimental import pallas as pl
from jax.experimental.pallas import tpu as pltpu

f32 = jnp.float32
bf16 = jnp.bfloat16
HIGHEST = lax.Precision.HIGHEST

D_MODEL = 1024
GRID_W = 64
NORM_EPS = 1e-6

HY_WIDTH = 1024
HY_EMB = 33
HY_BANDS = (HY_EMB - 1) // 2
HY_HIDDEN = 64
HY_TARGET = 1e-2
HY_SHORT_DECAY_PCT = 0.3
HY_LONG_DECAY_PCT = 1.5

ATT_HEADS = 8
ATT_KV_HEADS = 2
ATT_GROUP = ATT_HEADS // ATT_KV_HEADS
HEAD_DIM = 128
ATT_WIDTH = ATT_HEADS * HEAD_DIM
ATT_KV_WIDTH = ATT_KV_HEADS * HEAD_DIM
ROPE_THETA = 10000.0
EVEN_OFF_XV = 0
EVEN_OFF_GHY = 3 * HY_WIDTH
EVEN_OFF_Q = EVEN_OFF_GHY + HY_WIDTH
EVEN_OFF_K = EVEN_OFF_Q + ATT_WIDTH
EVEN_OFF_V = EVEN_OFF_K + ATT_KV_WIDTH
EVEN_OFF_GATT = EVEN_OFF_V + ATT_KV_WIDTH
EVEN_IN = EVEN_OFF_GATT + ATT_WIDTH

ML_HEADS = 8
ML_QK = 128
ML_V = 256
ML_QK_WIDTH = ML_HEADS * ML_QK
ML_WIDTH = ML_HEADS * ML_V
ODD_OFF_Q = 0
ODD_OFF_K = ML_QK_WIDTH
ODD_OFF_V = 2 * ML_QK_WIDTH
ODD_OFF_O = ODD_OFF_V + ML_WIDTH
ODD_OFF_Z = ODD_OFF_O + ML_WIDTH
ODD_OFF_GATES = ODD_OFF_Z + ML_WIDTH
ODD_MAIN = ODD_OFF_GATES

ROW_TILE = 256
LANE = 128
VMEM_LIMIT_BYTES = 48 * 1024 * 1024

FFT_L1 = 64
FFT_L2 = 128


def _params(*sem):
    return pltpu.CompilerParams(dimension_semantics=sem, vmem_limit_bytes=VMEM_LIMIT_BYTES)


def _adaln_kernel(c_ref, w_ref, b_ref, o_ref):
    c = c_ref[...]
    s = c * jax.nn.sigmoid(c)
    o_ref[0] = jnp.dot(s, w_ref[0], preferred_element_type=f32, precision=HIGHEST) + b_ref[0]


def adaln_all(cond, w_mod, b_mod, *, tn=768):
    depth, D, N = w_mod.shape
    return pl.pallas_call(
        _adaln_kernel,
        grid=(depth, N // tn),
        in_specs=[
            pl.BlockSpec((8, D), lambda l, j: (0, 0)),
            pl.BlockSpec((1, D, tn), lambda l, j: (l, 0, j)),
            pl.BlockSpec((1, 1, tn), lambda l, j: (l, 0, j)),
        ],
        out_specs=pl.BlockSpec((1, 8, tn), lambda l, j: (l, 0, j)),
        out_shape=jax.ShapeDtypeStruct((depth, 8, N), f32),
        compiler_params=_params("arbitrary", "arbitrary"),
        name="adaln",
    )(cond, w_mod, b_mod.reshape(depth, 1, N))


def _mod_row(i, tiles_per_batch, ctx_row):
    lat_tiles = tiles_per_batch - 1
    return jnp.where(i % tiles_per_batch == lat_tiles, ctx_row, i // tiles_per_batch)


def _norm_mod_kernel(x_ref, g_ref, m_ref, o_ref):
    x = x_ref[...]
    D = x.shape[-1]
    y = x * lax.rsqrt(jnp.mean(x * x, axis=-1, keepdims=True) + NORM_EPS)
    shift = m_ref[0, :, 0:D]
    scale = m_ref[0, :, D:2 * D]
    o_ref[...] = (y * g_ref[...] * (1.0 + scale) + shift).astype(o_ref.dtype)


def norm_mod(x_all, g, mods, tiles_per_batch, ctx_row):
    R, D = x_all.shape
    return pl.pallas_call(
        _norm_mod_kernel,
        grid=(R // ROW_TILE,),
        in_specs=[
            pl.BlockSpec((ROW_TILE, D), lambda i: (i, 0)),
            pl.BlockSpec((1, D), lambda i: (0, 0)),
            pl.BlockSpec((1, 1, 3 * D), lambda i: (_mod_row(i, tiles_per_batch, ctx_row), 0, 0)),
        ],
        out_specs=pl.BlockSpec((ROW_TILE, D), lambda i: (i, 0)),
        out_shape=jax.ShapeDtypeStruct((R, D), bf16),
        compiler_params=_params("arbitrary"),
        name="norm_mod",
    )(x_all, g.reshape(1, D), mods)


def _matmul_kernel(a_ref, b_ref, o_ref):
    o_ref[...] = jnp.dot(a_ref[...], b_ref[...], preferred_element_type=f32).astype(o_ref.dtype)


def _proj_row_tile(rows):
    return next(t for t in (512, ROW_TILE) if rows % t == 0)


def _proj_col_tile(cols, n_tiles=4):
    groups = cols // LANE
    return next(g for g in range(groups // n_tiles, 0, -1) if groups % g == 0) * LANE


def matmul(a, b, *, tm, tn, out_dtype=f32, name="matmul"):
    M, K = a.shape
    N = b.shape[1]
    assert M % tm == 0 and N % tn == 0, (M, N, tm, tn)
    return pl.pallas_call(
        _matmul_kernel,
        grid=(N // tn, M // tm),
        in_specs=[pl.BlockSpec((tm, K), lambda j, i: (i, 0)),
                  pl.BlockSpec((K, tn), lambda j, i: (0, j))],
        out_specs=pl.BlockSpec((tm, tn), lambda j, i: (i, j)),
        out_shape=jax.ShapeDtypeStruct((M, N), out_dtype),
        compiler_params=_params("arbitrary", "arbitrary"),
        name=name,
    )(a, b)


def _conv3(x, prev_row, next_row, w, b):
    tm = x.shape[0]
    row = lax.broadcasted_iota(jnp.int32, x.shape, 0)
    xm = jnp.where(row == 0, prev_row, pltpu.roll(x, 1, 0))
    xp = jnp.where(row == tm - 1, next_row, pltpu.roll(x, tm - 1, 0))
    return w[0:1] * xm + w[1:2] * x + w[2:3] * xp + b


def _seq_edges(i, tiles_per_batch):
    r = i % tiles_per_batch
    lat_tiles = tiles_per_batch - 1
    first = jnp.logical_or(r == 0, r == lat_tiles)
    last = jnp.logical_or(r == lat_tiles - 1, r == lat_tiles)
    return first, last


HALO = 16


def _halo_rows(prev_ref, next_ref, first, last):
    prev_row = jnp.where(first, 0.0, prev_ref[...].astype(f32)[HALO - 1:HALO, :])
    next_row = jnp.where(last, 0.0, next_ref[...].astype(f32)[0:1, :])
    return prev_row, next_row


def _halo_specs(col_block, tc, n_rows):
    per = ROW_TILE // HALO
    n_blocks = n_rows // HALO
    prev = pl.BlockSpec((HALO, tc), lambda i, c: (jnp.maximum(i * per - 1, 0), col_block(c)))
    nxt = pl.BlockSpec((HALO, tc), lambda i, c: (jnp.minimum((i + 1) * per, n_blocks - 1), col_block(c)))
    return prev, nxt


def _hyena_pre_kernel(tiles_per_batch, x0_ref, x0p_ref, x0n_ref, x1_ref, x1p_ref, x1n_ref,
                      v_ref, vp_ref, vn_ref, w0_ref, w1_ref, w2_ref, b0_ref, b1_ref, b2_ref,
                      x0_out, g_out):
    first, last = _seq_edges(pl.program_id(0), tiles_per_batch)

    def conv(x_ref, p_ref, n_ref, w_ref, b_ref):
        prev_row, next_row = _halo_rows(p_ref, n_ref, first, last)
        return _conv3(x_ref[...].astype(f32), prev_row, next_row, w_ref[...], b_ref[...])

    x0 = conv(x0_ref, x0p_ref, x0n_ref, w0_ref, b0_ref)
    x1 = conv(x1_ref, x1p_ref, x1n_ref, w1_ref, b1_ref)
    v = conv(v_ref, vp_ref, vn_ref, w2_ref, b2_ref)
    g = v * x1
    x0_out[...] = x0
    g_out[...] = g


def hyena_pre(p, conv_w, conv_b, tiles_per_batch, *, tc=512):
    R = p.shape[0]
    W = HY_WIDTH
    nb = W // tc
    specs = []
    for part in range(3):
        col = functools.partial(lambda c, part: part * nb + c, part=part)
        main = pl.BlockSpec((ROW_TILE, tc), functools.partial(lambda i, c, col: (i, col(c)), col=col))
        prev, nxt = _halo_specs(col, tc, R)
        specs += [main, prev, nxt]
    wspecs = [pl.BlockSpec((3, tc), functools.partial(lambda i, c, part: (0, part * nb + c), part=part))
              for part in range(3)]
    bspecs = [pl.BlockSpec((1, tc), functools.partial(lambda i, c, part: (0, part * nb + c), part=part))
              for part in range(3)]
    out_spec = pl.BlockSpec((ROW_TILE, tc), lambda i, c: (i, c))
    args = [p] * 9 + [conv_w] * 3 + [conv_b.reshape(1, -1)] * 3
    return pl.pallas_call(
        functools.partial(_hyena_pre_kernel, tiles_per_batch),
        grid=(R // ROW_TILE, nb),
        in_specs=specs + wspecs + bspecs,
        out_specs=[out_spec, out_spec],
        out_shape=[jax.ShapeDtypeStruct((R, W), f32), jax.ShapeDtypeStruct((R, W), f32)],
        compiler_params=_params("arbitrary", "arbitrary"),
        name="hyena_pre",
    )(*args)


def _filter_kernel(n, rows, bands_ref, w1t_ref, w1c_ref, w1s_ref, b1_ref, fr_ref, w2_ref, b2_ref,
                   w3_ref, dl_ref, o_ref, nrm_ref):
    step = pl.program_id(0)
    j = step * rows + lax.broadcasted_iota(jnp.int32, (rows, 1), 0)
    d = jnp.where(j < n, j, 2 * n - j)
    valid = j != n
    df = d.astype(f32)
    t = df / float(n - 1)
    ang = (2.0 * math.pi / n) * df * bands_ref[...]
    fr = fr_ref[...]
    z1 = (t * w1t_ref[...]
          + jnp.dot(jnp.cos(ang), w1c_ref[...], preferred_element_type=f32, precision=HIGHEST)
          - jnp.dot(jnp.sin(ang), w1s_ref[...], preferred_element_type=f32, precision=HIGHEST)
          + b1_ref[...])
    hdn = jnp.sin(fr * z1)
    hdn = jnp.sin(fr * (jnp.dot(hdn, w2_ref[...], preferred_element_type=f32, precision=HIGHEST) + b2_ref[...]))
    h = jnp.dot(hdn.astype(bf16), w3_ref[0].astype(bf16), preferred_element_type=f32)
    h = h * jnp.exp(-t * jnp.abs(dl_ref[...]))
    h = jnp.where(valid, h, 0.0)
    o_ref[...] = h

    @pl.when(step == 0)
    def _():
        nrm_ref[...] = jnp.zeros_like(nrm_ref)

    nrm_ref[...] += jnp.sum(jnp.abs(h), axis=0, keepdims=True)


def hyena_filter(n, w1, b1, freq, w2, b2, w3, *, rows=256):
    W = HY_WIDTH
    Hd = HY_HIDDEN
    bands = jnp.linspace(1e-4, HY_BANDS - 1, HY_BANDS, dtype=f32).reshape(1, HY_BANDS)
    max_decay = math.log(HY_TARGET) / HY_SHORT_DECAY_PCT
    min_decay = math.log(HY_TARGET) / HY_LONG_DECAY_PCT
    deltas = jnp.linspace(min_decay, max_decay, W, dtype=f32).reshape(1, W)
    steps = 2 * n // rows
    half_steps = n // rows
    full = lambda s: (0, 0)
    w3r = w3.reshape(Hd, 2, W).transpose(1, 0, 2)
    return pl.pallas_call(
        functools.partial(_filter_kernel, n, rows),
        grid=(steps,),
        in_specs=[
            pl.BlockSpec((1, HY_BANDS), full),
            pl.BlockSpec((1, Hd), full),
            pl.BlockSpec((HY_BANDS, Hd), full),
            pl.BlockSpec((HY_BANDS, Hd), full),
            pl.BlockSpec((1, Hd), full),
            pl.BlockSpec((1, Hd), full),
            pl.BlockSpec((Hd, Hd), full),
            pl.BlockSpec((1, Hd), full),
            pl.BlockSpec((1, Hd, W), lambda s: (jnp.where(s * rows < n, 0, 1), 0, 0)),
            pl.BlockSpec((1, W), full),
        ],
        out_specs=[pl.BlockSpec((rows, W), lambda s: (s, 0)), pl.BlockSpec((1, W), full)],
        out_shape=[jax.ShapeDtypeStruct((2 * n, W), f32), jax.ShapeDtypeStruct((1, W), f32)],
        compiler_params=_params("arbitrary"),
        name="hyena_filter",
    )(bands, w1[0:1], w1[1:1 + HY_BANDS], w1[1 + HY_BANDS:], b1.reshape(1, Hd), freq.reshape(1, Hd),
      w2, b2.reshape(1, Hd), w3r, deltas)


@functools.lru_cache(maxsize=None)
def _fft_constants():
    L1, L2 = FFT_L1, FFT_L2
    L = L1 * L2
    k1 = np.arange(L1)
    nh = np.arange(L1 // 2)
    th = 2.0 * np.pi * np.outer(k1, nh) / L1
    m1r = np.concatenate([np.cos(th), -np.sin(th)], axis=0)
    m1i = np.concatenate([np.sin(th), np.cos(th)], axis=0)
    thf = 2.0 * np.pi * np.outer(k1, np.arange(L1)) / L1
    m1f = np.concatenate([np.cos(thf), -np.sin(thf)], axis=0)
    n2 = np.arange(L2)
    k2 = np.arange(L2)
    m = (k1[:, None, None] * n2[None, None, :] + L1 * k2[None, :, None] * n2[None, None, :]) % L
    ph = 2.0 * np.pi * m / L
    gr, gi = np.cos(ph), -np.sin(ph)
    g = np.concatenate([np.concatenate([gr, -gi], axis=2), np.concatenate([gi, gr], axis=2)], axis=1)
    gt = np.transpose(g, (0, 2, 1))
    thi = 2.0 * np.pi * np.outer(nh, k1) / L1
    m3r = np.concatenate([np.cos(thi), np.sin(thi)], axis=0)
    m3i = np.concatenate([-np.sin(thi), np.cos(thi)], axis=0)
    cast = lambda a: np.asarray(a, dtype=np.float32)
    return dict(m1r=cast(m1r), m1i=cast(m1i), m1f=cast(m1f), g=cast(g), gt=cast(gt), m3r=cast(m3r), m3i=cast(m3i))


@functools.lru_cache(maxsize=None)
def _ctx_dft_constants(n):
    L = 2 * n
    k = np.arange(L)
    th = 2.0 * np.pi * np.outer(k, np.arange(n)) / L
    c, s = np.cos(th), np.sin(th)
    mc = np.concatenate([np.concatenate([c, s], axis=1), np.concatenate([-s, c], axis=1)], axis=0)
    thf = 2.0 * np.pi * np.outer(k, np.arange(L)) / L
    mf = np.concatenate([np.cos(thf), -np.sin(thf)], axis=0)
    ct, st = c.T, s.T
    minv = np.concatenate([np.concatenate([ct, -st], axis=1), np.concatenate([st, ct], axis=1)], axis=0)
    cast = lambda a: np.asarray(a, dtype=np.float32)
    return dict(mc=cast(mc), mf=cast(mf), minv=cast(minv))


def _bf16_constants(consts):
    return {k: jnp.asarray(v).astype(bf16) for k, v in consts.items()}


SUB = 8
N2C = 32


def _store_step1(o_ref, j, a):
    for ch in range(a.shape[0] // SUB):
        o_ref[0, ch, j] = a[ch * SUB:(ch + 1) * SUB]


def _fft1_kernel(re_ref, im_ref, mr_ref, mi_ref, o_ref):
    base = pl.program_id(2) * N2C
    for j in range(N2C):
        rows = pl.ds(base + j, FFT_L1 // 2, stride=FFT_L2)
        _store_step1(o_ref, j,
                     jnp.dot(mr_ref[...], re_ref[0, rows, :].astype(bf16), preferred_element_type=f32)
                     + jnp.dot(mi_ref[...], im_ref[0, rows, :].astype(bf16), preferred_element_type=f32))


def _step1_out(P, W, index):
    groups = 2 * FFT_L1 // SUB
    spec = pl.BlockSpec((1, groups, N2C, SUB, LANE), index)
    return spec, jax.ShapeDtypeStruct((P, groups, FFT_L2, SUB, W), f32)


def fft_step1(g3, m1r, m1i, n_lat):
    B, _, W = g3.shape
    half = FFT_L1 // 2
    out_spec, out_shape = _step1_out(B // 2, W, lambda p, c, k: (p, 0, k, 0, c))
    return pl.pallas_call(
        _fft1_kernel,
        grid=(B // 2, W // LANE, FFT_L2 // N2C),
        in_specs=[pl.BlockSpec((1, n_lat, LANE), lambda p, c, k: (2 * p, 0, c)),
                  pl.BlockSpec((1, n_lat, LANE), lambda p, c, k: (2 * p + 1, 0, c)),
                  pl.BlockSpec((2 * FFT_L1, half), lambda p, c, k: (0, 0)),
                  pl.BlockSpec((2 * FFT_L1, half), lambda p, c, k: (0, 0))],
        out_specs=out_spec,
        out_shape=out_shape,
        compiler_params=_params("arbitrary", "arbitrary", "arbitrary"),
        name="fft_step1",
    )(g3, g3, m1r, m1i)


def _fft1_filter_kernel(x_ref, m_ref, o_ref):
    base = pl.program_id(1) * N2C
    for j in range(N2C):
        rows = pl.ds(base + j, FFT_L1, stride=FFT_L2)
        _store_step1(o_ref, j, jnp.dot(m_ref[...], x_ref[rows, :].astype(bf16), preferred_element_type=f32))


def fft_step1_filter(ts, m1f):
    L, W = ts.shape
    out_spec, out_shape = _step1_out(1, W, lambda c, k: (0, 0, k, 0, c))
    return pl.pallas_call(
        _fft1_filter_kernel,
        grid=(W // LANE, FFT_L2 // N2C),
        in_specs=[pl.BlockSpec((L, LANE), lambda c, k: (0, c)),
                  pl.BlockSpec((2 * FFT_L1, FFT_L1), lambda c, k: (0, 0))],
        out_specs=out_spec,
        out_shape=out_shape,
        compiler_params=_params("arbitrary", "arbitrary"),
        name="fft_step1_filter",
    )(ts, m1f)


def _cmul(yr, yi, hr, hi):
    return yr * hr - yi * hi, yr * hi + yi * hr


def _step1_column(a_refs, j):
    col = lambda ref: ref[0, pl.ds(j, FFT_L2, stride=SUB), :]
    re = jnp.concatenate([col(a_refs[0]), col(a_refs[1])], axis=1)
    im = jnp.concatenate([col(a_refs[2]), col(a_refs[3])], axis=1)
    return jnp.concatenate([re, im], axis=0).astype(bf16)


def _fft2_filter_kernel(a0_ref, a1_ref, a2_ref, a3_ref, g_ref, nrm_ref, o_ref):
    scale = 1.0 / (nrm_ref[...] * float(FFT_L1 * FFT_L2))
    for j in range(SUB):
        a = _step1_column((a0_ref, a1_ref, a2_ref, a3_ref), j)
        o_ref[j] = jnp.dot(g_ref[j], a, preferred_element_type=f32) * scale


def _step1_specs(index):
    def spec(part, lane_half):
        return pl.BlockSpec((1, FFT_L2 * SUB, LANE),
                            lambda *g: (index(*g)[0], part * (FFT_L1 // SUB) + index(*g)[1],
                                        2 * index(*g)[2] + lane_half))
    return [spec(0, 0), spec(0, 1), spec(1, 0), spec(1, 1)]


def fft_step2_filter(af, g, nrm):
    W = af.shape[-1]
    ct = 2 * LANE
    L1, R2 = FFT_L1, 2 * FFT_L2
    af = af.reshape(1, -1, W)
    return pl.pallas_call(
        _fft2_filter_kernel,
        grid=(L1 // SUB, W // ct),
        in_specs=_step1_specs(lambda k, c: (0, k, c)) + [
            pl.BlockSpec((SUB, R2, R2), lambda k, c: (k, 0, 0)),
            pl.BlockSpec((1, ct), lambda k, c: (0, c))],
        out_specs=pl.BlockSpec((SUB, R2, ct), lambda k, c: (k, 0, c)),
        out_shape=jax.ShapeDtypeStruct((L1, R2, W), f32),
        compiler_params=_params("arbitrary", "arbitrary"),
        name="fft_step2_filter",
    )(af, af, af, af, g, nrm)


def _fft2_kernel(a0_ref, a1_ref, a2_ref, a3_ref, g_ref, gt_ref, h_ref, ore_ref, oim_ref):
    half = FFT_L2
    for j in range(SUB):
        a = _step1_column((a0_ref, a1_ref, a2_ref, a3_ref), j)
        y = jnp.dot(g_ref[j], a, preferred_element_type=f32)
        pr, pi = _cmul(y[:half], y[half:], h_ref[j, :half], h_ref[j, half:])
        pcat = jnp.concatenate([pr, pi], axis=0).astype(bf16)
        b = jnp.dot(gt_ref[j], pcat, preferred_element_type=f32)
        for ch in range(FFT_L2 // N2C):
            ore_ref[0, ch, j] = b[ch * N2C:(ch + 1) * N2C]
            oim_ref[0, ch, j] = b[half + ch * N2C:half + (ch + 1) * N2C]


def fft_step2(a, g, gt, hf):
    P, W = a.shape[0], a.shape[-1]
    ct = 2 * LANE
    L1, R2 = FFT_L1, 2 * FFT_L2
    a = a.reshape(P, -1, W)
    out = pl.BlockSpec((1, FFT_L2 // N2C, SUB, N2C, ct), lambda k, c, p: (p, 0, k, 0, c))
    shape = jax.ShapeDtypeStruct((P, FFT_L2 // N2C, L1, N2C, W), f32)
    return pl.pallas_call(
        _fft2_kernel,
        grid=(L1 // SUB, W // ct, P),
        in_specs=_step1_specs(lambda k, c, p: (p, k, c)) + [
            pl.BlockSpec((SUB, R2, R2), lambda k, c, p: (k, 0, 0)),
            pl.BlockSpec((SUB, R2, R2), lambda k, c, p: (k, 0, 0)),
            pl.BlockSpec((SUB, R2, ct), lambda k, c, p: (k, 0, c))],
        out_specs=[out, out],
        out_shape=[shape, shape],
        compiler_params=_params("arbitrary", "arbitrary", "arbitrary"),
        name="fft_step2",
    )(a, a, a, a, g, gt, hf)


def _fft3_kernel(bre_ref, bim_ref, mr_ref, mi_ref, o_ref):
    L1 = FFT_L1
    half = L1 // 2
    base = pl.program_id(2) * N2C
    for j in range(N2C):
        br = bre_ref[0, pl.ds(j, L1, stride=N2C), :].astype(bf16)
        bi = bim_ref[0, pl.ds(j, L1, stride=N2C), :].astype(bf16)
        z = (jnp.dot(mr_ref[...], br, preferred_element_type=f32)
             + jnp.dot(mi_ref[...], bi, preferred_element_type=f32))
        rows = pl.ds(base + j, half, stride=FFT_L2)
        o_ref[0, 0, rows, :] = z[:half]
        o_ref[0, 1, rows, :] = z[half:]


def fft_step3(bre, bim, m3r, m3i, rows_total, n_lat):
    P, W = bre.shape[0], bre.shape[-1]
    L1, L2 = FFT_L1, FFT_L2
    bre = bre.reshape(P, -1, W)
    bim = bim.reshape(P, -1, W)
    spec = pl.BlockSpec((1, L1 * N2C, LANE), lambda p, c, k: (p, k, c))
    return pl.pallas_call(
        _fft3_kernel,
        grid=(P, W // LANE, L2 // N2C),
        in_specs=[spec, spec,
                  pl.BlockSpec((L1, L1), lambda p, c, k: (0, 0)),
                  pl.BlockSpec((L1, L1), lambda p, c, k: (0, 0))],
        out_specs=pl.BlockSpec((1, 2, n_lat, LANE), lambda p, c, k: (p, 0, 0, c)),
        out_shape=jax.ShapeDtypeStruct((P, 2, rows_total, W), f32),
        compiler_params=_params("arbitrary", "arbitrary", "arbitrary"),
        name="fft_step3",
    )(bre, bim, m3r, m3i)


def _ctx_filter_kernel(n, ts_ref, m_ref, nrm_ref, o_ref):
    scale = 1.0 / (nrm_ref[...] * float(2 * n))
    o_ref[...] = jnp.dot(m_ref[...], ts_ref[...].astype(bf16), preferred_element_type=f32) * scale


def ctx_filter_spectrum(ts, mf, nrm, *, ct=256):
    L, W = ts.shape
    return pl.pallas_call(
        functools.partial(_ctx_filter_kernel, L // 2),
        grid=(W // ct,),
        in_specs=[pl.BlockSpec((L, ct), lambda c: (0, c)),
                  pl.BlockSpec((2 * L, L), lambda c: (0, 0)),
                  pl.BlockSpec((1, ct), lambda c: (0, c))],
        out_specs=pl.BlockSpec((2 * L, ct), lambda c: (0, c)),
        out_shape=jax.ShapeDtypeStruct((2 * L, W), f32),
        compiler_params=_params("arbitrary"),
        name="ctx_filter_spectrum",
    )(ts, mf, nrm)


def _ctx_conv_kernel(n, re_ref, im_ref, mc_ref, minv_ref, h_ref, y_in_ref, o_ref):
    del y_in_ref
    L = 2 * n
    z = jnp.concatenate([re_ref[0], im_ref[0]], axis=0).astype(bf16)
    y = jnp.dot(mc_ref[...], z, preferred_element_type=f32)
    pr, pi = _cmul(y[:L], y[L:], h_ref[:L], h_ref[L:])
    pcat = jnp.concatenate([pr, pi], axis=0).astype(bf16)
    out = jnp.dot(minv_ref[...], pcat, preferred_element_type=f32)
    o_ref[0, 0] = out[:n]
    o_ref[0, 1] = out[n:]


def ctx_long_conv(gz, y4, hfc, consts, n_lat, n_ctx, *, ct=256):
    B, T, W = gz.shape
    blk = n_lat // n_ctx
    L = 2 * n_ctx
    out = pl.pallas_call(
        functools.partial(_ctx_conv_kernel, n_ctx),
        grid=(B // 2, W // ct),
        in_specs=[pl.BlockSpec((1, n_ctx, ct), lambda p, c: (2 * p, blk, c)),
                  pl.BlockSpec((1, n_ctx, ct), lambda p, c: (2 * p + 1, blk, c)),
                  pl.BlockSpec((2 * L, L), lambda p, c: (0, 0)),
                  pl.BlockSpec((L, 2 * L), lambda p, c: (0, 0)),
                  pl.BlockSpec((2 * L, ct), lambda p, c: (0, c)),
                  pl.BlockSpec(memory_space=pl.ANY)],
        out_specs=pl.BlockSpec((1, 2, n_ctx, ct), lambda p, c: (p, 0, blk, c)),
        out_shape=jax.ShapeDtypeStruct(y4.shape, f32),
        input_output_aliases={5: 0},
        compiler_params=_params("arbitrary", "arbitrary"),
        name="ctx_long_conv",
    )(gz, gz, consts["mc"], consts["minv"], hfc, y4)
    return out.reshape(B, T, W)


def hyena_long_conv(gz, n_lat, n_ctx, f_w1, f_b1, f_freq, f_w2, f_b2, f_w3):
    B, T, W = gz.shape
    cst = _bf16_constants(_fft_constants())
    L1, L2 = FFT_L1, FFT_L2
    assert 2 * n_lat == L1 * L2 and T % L2 == 0 and B % 2 == 0
    ts, nrm = hyena_filter(n_lat, f_w1, f_b1, f_freq, f_w2, f_b2, f_w3)
    af = fft_step1_filter(ts, cst["m1f"])
    hf = fft_step2_filter(af, cst["g"], nrm)
    a = fft_step1(gz, cst["m1r"], cst["m1i"], n_lat)
    bre, bim = fft_step2(a, cst["g"], cst["gt"], hf)
    y = fft_step3(bre, bim, cst["m3r"], cst["m3i"], T, n_lat)
    ccst = _bf16_constants(_ctx_dft_constants(n_ctx))
    ts_c, nrm_c = hyena_filter(n_ctx, f_w1, f_b1, f_freq, f_w2, f_b2, f_w3)
    hfc = ctx_filter_spectrum(ts_c, ccst["mf"], nrm_c)
    return ctx_long_conv(gz, y, hfc, ccst, n_lat, n_ctx)


@functools.lru_cache(maxsize=None)
def _rope_tables(n_lat, n_ctx):
    half = HEAD_DIM // 2
    nf = half // 2
    inv = ROPE_THETA ** (-np.arange(nf, dtype=np.float64) / nf)
    t = np.arange(n_lat)
    pos = np.stack([t // GRID_W, t % GRID_W], axis=1).astype(np.float64)
    ang = pos[:, :, None] * inv[None, None, :]
    cos = np.concatenate([np.cos(ang), np.cos(ang)], axis=2).reshape(n_lat, HEAD_DIM)
    sin = np.concatenate([-np.sin(ang), np.sin(ang)], axis=2).reshape(n_lat, HEAD_DIM)
    cos = np.concatenate([cos, np.ones((n_ctx, HEAD_DIM))], axis=0)
    sin = np.concatenate([sin, np.zeros((n_ctx, HEAD_DIM))], axis=0)
    return np.asarray(cos, np.float32), np.asarray(sin, np.float32)


def _norm_rope(x, w, cos, sin, lo_lane):
    y = x * lax.rsqrt(jnp.mean(x * x, axis=-1, keepdims=True) + NORM_EPS) * w
    nf = HEAD_DIM // 4
    partner = jnp.where(lo_lane, pltpu.roll(y, HEAD_DIM - nf, 1), pltpu.roll(y, nf, 1))
    return y * cos + partner * sin


def _qkv_prep_kernel(q_ref, k_ref, v_ref, cos_ref, sin_ref, qn_ref, kn_ref, q_out, kt_out, v_out):
    cos = cos_ref[...]
    sin = sin_ref[...]
    lane = lax.broadcasted_iota(jnp.int32, cos.shape, 1)
    lo_lane = (lane % (HEAD_DIM // 2)) < (HEAD_DIM // 4)
    scale = HEAD_DIM ** -0.5 * math.log2(math.e)
    for h in range(ATT_HEADS):
        sl = slice(h * HEAD_DIM, (h + 1) * HEAD_DIM)
        q_out[:, sl] = (_norm_rope(q_ref[:, sl].astype(f32), qn_ref[...], cos, sin, lo_lane) * scale).astype(q_out.dtype)
    ones_col = (lane == 0).astype(v_out.dtype)
    for h in range(ATT_KV_HEADS):
        sl = slice(h * HEAD_DIM, (h + 1) * HEAD_DIM)
        k = _norm_rope(k_ref[:, sl].astype(f32), kn_ref[...], cos, sin, lo_lane)
        kt_out[0, sl, :] = k.T.astype(kt_out.dtype)
        v_out[:, 2 * h * HEAD_DIM:(2 * h + 1) * HEAD_DIM] = v_ref[:, sl].astype(v_out.dtype)
        v_out[:, (2 * h + 1) * HEAD_DIM:(2 * h + 2) * HEAD_DIM] = ones_col


def qkv_prep(p, q_norm, k_norm, B, n_lat, n_ctx):
    R = p.shape[0]
    T = n_lat + n_ctx
    tpb = T // ROW_TILE
    cos, sin = _rope_tables(n_lat, n_ctx)
    return pl.pallas_call(
        _qkv_prep_kernel,
        grid=(R // ROW_TILE,),
        in_specs=[pl.BlockSpec((ROW_TILE, ATT_WIDTH), lambda i: (i, EVEN_OFF_Q // ATT_WIDTH)),
                  pl.BlockSpec((ROW_TILE, ATT_KV_WIDTH), lambda i: (i, EVEN_OFF_K // ATT_KV_WIDTH)),
                  pl.BlockSpec((ROW_TILE, ATT_KV_WIDTH), lambda i: (i, EVEN_OFF_V // ATT_KV_WIDTH)),
                  pl.BlockSpec((ROW_TILE, HEAD_DIM), lambda i: (i % tpb, 0)),
                  pl.BlockSpec((ROW_TILE, HEAD_DIM), lambda i: (i % tpb, 0)),
                  pl.BlockSpec((1, HEAD_DIM), lambda i: (0, 0)),
                  pl.BlockSpec((1, HEAD_DIM), lambda i: (0, 0))],
        out_specs=[pl.BlockSpec((ROW_TILE, ATT_WIDTH), lambda i: (i, 0)),
                   pl.BlockSpec((1, ATT_KV_WIDTH, ROW_TILE), lambda i: (i // tpb, 0, i % tpb)),
                   pl.BlockSpec((ROW_TILE, 2 * ATT_KV_WIDTH), lambda i: (i, 0))],
        out_shape=[jax.ShapeDtypeStruct((R, ATT_WIDTH), bf16),
                   jax.ShapeDtypeStruct((B, ATT_KV_WIDTH, T), bf16),
                   jax.ShapeDtypeStruct((R, 2 * ATT_KV_WIDTH), bf16)],
        compiler_params=_params("arbitrary"),
        name="qkv_prep",
    )(p, p, p, jnp.asarray(cos), jnp.asarray(sin), q_norm.reshape(1, HEAD_DIM), k_norm.reshape(1, HEAD_DIM))


ATT_CHUNK = 256


def _attention_kernel(n_lat, q_ref, kt_ref, v_ref, o_ref, qs_ref, sa_ref, sb_ref, pa_ref, pb_ref, os_ref):
    tq = q_ref.shape[0]
    n_chunks = ATT_GROUP * tq // ATT_CHUNK
    for h in range(ATT_GROUP):
        qs_ref[h * tq:(h + 1) * tq, :] = q_ref[:, h * HEAD_DIM:(h + 1) * HEAD_DIM]

    def rows(c):
        return pl.ds(pl.multiple_of(c * ATT_CHUNK, ATT_CHUNK), ATT_CHUNK)

    def finish(o):
        for h in range(ATT_GROUP):
            o_ref[:, h * HEAD_DIM:(h + 1) * HEAD_DIM] = os_ref[h * tq:(h + 1) * tq, :].astype(o_ref.dtype)

    s_refs = (sa_ref, sb_ref)
    p_refs = (pa_ref, pb_ref)

    def run(k_lo):
        def scores(c, slot):
            s_refs[slot][:, k_lo:] = jnp.dot(qs_ref[rows(c), :], kt_ref[0, :, k_lo:], preferred_element_type=f32)

        def exponentials(slot):
            s = s_refs[slot][:, k_lo:]
            m = jnp.max(s, axis=-1, keepdims=True)
            p_refs[slot][:, k_lo:] = jnp.exp2(s - m).astype(bf16)

        def weighted_values(c, slot):
            r = jnp.dot(p_refs[slot][:, k_lo:], v_ref[0, k_lo:, :], preferred_element_type=f32)
            os_ref[rows(c), :] = r[:, :HEAD_DIM] / r[:, HEAD_DIM:HEAD_DIM + 1]

        def stage(c, slot):
            scores(c + 1, 1 - slot)
            exponentials(slot)
            weighted_values(c - 1, 1 - slot)

        scores(0, 0)
        exponentials(0)
        scores(1, 1)

        def body(i, carry):
            stage(2 * i + 1, 1)
            stage(2 * i + 2, 0)
            return carry

        lax.fori_loop(0, (n_chunks - 2) // 2, body, 0)
        exponentials((n_chunks - 1) % 2)
        weighted_values(n_chunks - 2, n_chunks % 2)
        weighted_values(n_chunks - 1, (n_chunks - 1) % 2)
        finish(None)

    @pl.when(pl.program_id(2) < pl.num_programs(2) - 1)
    def _():
        run(0)

    @pl.when(pl.program_id(2) == pl.num_programs(2) - 1)
    def _():
        run(n_lat)


def attention(q, kt, v, B, n_lat, n_ctx):
    R = q.shape[0]
    T = n_lat + n_ctx
    tq = ROW_TILE
    assert n_ctx == tq
    qt = T // tq
    gw = ATT_GROUP * HEAD_DIM
    return pl.pallas_call(
        functools.partial(_attention_kernel, n_lat),
        grid=(B, ATT_KV_HEADS, qt),
        in_specs=[pl.BlockSpec((tq, gw), lambda b, g, i: (b * qt + i, g)),
                  pl.BlockSpec((1, HEAD_DIM, T), lambda b, g, i: (b, g, 0)),
                  pl.BlockSpec((1, T, 2 * HEAD_DIM), lambda b, g, i: (b, 0, g))],
        out_specs=pl.BlockSpec((tq, gw), lambda b, g, i: (b * qt + i, g)),
        out_shape=jax.ShapeDtypeStruct((R, ATT_WIDTH), bf16),
        scratch_shapes=[pltpu.VMEM((ATT_GROUP * tq, HEAD_DIM), bf16),
                        pltpu.VMEM((ATT_CHUNK, T), f32), pltpu.VMEM((ATT_CHUNK, T), f32),
                        pltpu.VMEM((ATT_CHUNK, T), bf16), pltpu.VMEM((ATT_CHUNK, T), bf16),
                        pltpu.VMEM((ATT_GROUP * tq, HEAD_DIM), f32)],
        compiler_params=_params("arbitrary", "arbitrary", "arbitrary"),
        name="attention",
    )(q, kt, v.reshape(B, T, 2 * ATT_KV_WIDTH))


def _post_residual(x, y, g_post, gate):
    yn = y * lax.rsqrt(jnp.mean(y * y, axis=-1, keepdims=True) + NORM_EPS) * g_post
    return x + gate * yn


def _silu(x):
    return x * jax.nn.sigmoid(x)


def _even_out_kernel(x0_ref, g_ref, yc_ref, ghy_ref, att_ref, ga0_ref, ga1_ref, bias_ref, w_ref,
                     gp_ref, m_ref, x_ref, o_ref):
    D = x_ref.shape[-1]
    g = g_ref[...]
    hy = x0_ref[...] * (yc_ref[...] + g * bias_ref[...]) * _silu(ghy_ref[...].astype(f32))
    g_att = jnp.concatenate([ga0_ref[...], ga1_ref[...]], axis=1).astype(f32)
    at = att_ref[...].astype(f32) * _silu(g_att)
    lhs = jnp.concatenate([hy, at], axis=1).astype(bf16)
    y = jnp.dot(lhs, w_ref[...], preferred_element_type=f32)
    o_ref[...] = _post_residual(x_ref[...], y, gp_ref[...], m_ref[0, :, 2 * D:3 * D])


def even_out(x0, g, yconv, p, att, hy_bias, w_out, g_post, mods, x_all, tiles_per_batch, ctx_row):
    R, D = x_all.shape
    W = HY_WIDTH
    hw = ATT_WIDTH // 2
    row = lambda i: (i, 0)
    full = lambda i: (0, 0)
    return pl.pallas_call(
        _even_out_kernel,
        grid=(R // ROW_TILE,),
        in_specs=[pl.BlockSpec((ROW_TILE, W), row),
                  pl.BlockSpec((ROW_TILE, W), row),
                  pl.BlockSpec((ROW_TILE, W), row),
                  pl.BlockSpec((ROW_TILE, W), lambda i: (i, EVEN_OFF_GHY // W)),
                  pl.BlockSpec((ROW_TILE, ATT_WIDTH), row),
                  pl.BlockSpec((ROW_TILE, hw), lambda i: (i, EVEN_OFF_GATT // hw)),
                  pl.BlockSpec((ROW_TILE, hw), lambda i: (i, EVEN_OFF_GATT // hw + 1)),
                  pl.BlockSpec((1, W), full),
                  pl.BlockSpec((W + ATT_WIDTH, D), full),
                  pl.BlockSpec((1, D), full),
                  pl.BlockSpec((1, 1, 3 * D), lambda i: (_mod_row(i, tiles_per_batch, ctx_row), 0, 0)),
                  pl.BlockSpec((ROW_TILE, D), row)],
        out_specs=pl.BlockSpec((ROW_TILE, D), row),
        out_shape=jax.ShapeDtypeStruct((R, D), f32),
        compiler_params=_params("arbitrary"),
        name="even_out",
    )(x0, g, yconv, p, att, p, p, hy_bias.reshape(1, W), w_out.astype(bf16), g_post.reshape(1, D), mods, x_all)


def even_layer(x_all, mods, B, n_lat, n_ctx, g_pre, g_post, w_in, conv_w, conv_b, f_w1, f_b1, f_freq,
               f_w2, f_b2, f_w3, hy_bias, q_norm, k_norm, w_out):
    T = n_lat + n_ctx
    tpb = T // ROW_TILE
    h = norm_mod(x_all, g_pre, mods, tpb, B)
    p = matmul(h, w_in.astype(bf16), tm=_proj_row_tile(B * T), tn=_proj_col_tile(w_in.shape[1]), out_dtype=bf16,
               name="even_in_proj")
    x0, g = hyena_pre(p, conv_w, conv_b, tpb)
    yconv = hyena_long_conv(g.reshape(B, T, HY_WIDTH), n_lat, n_ctx, f_w1, f_b1, f_freq, f_w2, f_b2, f_w3)
    q, kt, v = qkv_prep(p, q_norm, k_norm, B, n_lat, n_ctx)
    att = attention(q, kt, v, B, n_lat, n_ctx)
    return even_out(x0, g, yconv.reshape(B * T, HY_WIDTH), p, att, hy_bias, w_out, g_post, mods, x_all, tpb, B)


def _mlstm_prep_kernel(tiles_per_batch, q_ref, qp_ref, qn_ref, k_ref, kp_ref, kn_ref,
                       wq_ref, wk_ref, bq_ref, bk_ref, q_out, kt_out):
    first, last = _seq_edges(pl.program_id(0), tiles_per_batch)
    prev_row, next_row = _halo_rows(qp_ref, qn_ref, first, last)
    q = _silu(_conv3(q_ref[...].astype(f32), prev_row, next_row, wq_ref[...], bq_ref[...]))
    q_out[...] = q.astype(q_out.dtype)
    prev_row, next_row = _halo_rows(kp_ref, kn_ref, first, last)
    k = _silu(_conv3(k_ref[...].astype(f32), prev_row, next_row, wk_ref[...], bk_ref[...])) * (ML_QK ** -0.5)
    for h in range(k.shape[1] // ML_QK):
        sl = slice(h * ML_QK, (h + 1) * ML_QK)
        kt_out[0, sl, :] = k[:, sl].T.astype(kt_out.dtype)


def mlstm_prep(p, conv_w, conv_b, B, T, *, tc=512):
    R = p.shape[0]
    tpb = T // ROW_TILE
    nb = ML_QK_WIDTH // tc
    qcol = lambda c: c
    kcol = lambda c: nb + c
    qprev, qnext = _halo_specs(qcol, tc, R)
    kprev, knext = _halo_specs(kcol, tc, R)
    return pl.pallas_call(
        functools.partial(_mlstm_prep_kernel, tpb),
        grid=(R // ROW_TILE, nb),
        in_specs=[pl.BlockSpec((ROW_TILE, tc), lambda i, c: (i, c)), qprev, qnext,
                  pl.BlockSpec((ROW_TILE, tc), lambda i, c: (i, nb + c)), kprev, knext,
                  pl.BlockSpec((3, tc), lambda i, c: (0, c)),
                  pl.BlockSpec((3, tc), lambda i, c: (0, nb + c)),
                  pl.BlockSpec((1, tc), lambda i, c: (0, c)),
                  pl.BlockSpec((1, tc), lambda i, c: (0, nb + c))],
        out_specs=[pl.BlockSpec((ROW_TILE, tc), lambda i, c: (i, c)),
                   pl.BlockSpec((1, tc, ROW_TILE), lambda i, c: (i // tpb, c, i % tpb))],
        out_shape=[jax.ShapeDtypeStruct((R, ML_QK_WIDTH), bf16),
                   jax.ShapeDtypeStruct((B, ML_QK_WIDTH, T), bf16)],
        compiler_params=_params("arbitrary", "arbitrary"),
        name="mlstm_prep",
    )(p, p, p, p, p, p, conv_w, conv_w, conv_b.reshape(1, -1), conv_b.reshape(1, -1))


def _log_sigmoid(x):
    return jnp.minimum(x, 0.0) - jnp.log(1.0 + jnp.exp(-jnp.abs(x)))


def _mlstm_gates_kernel(g_ref, b_ref, gc_out, gr_out):
    pre = g_ref[...] + b_ref[...]
    lane = lax.broadcasted_iota(jnp.int32, pre.shape, 1)
    is_forget = (lane // ML_HEADS) % 2 == 1
    gc = jnp.where(is_forget, _log_sigmoid(pre), pre)
    gc_out[...] = gc
    gr_out[0] = gc.T


def mlstm_gates(gates, gate_b, B, T):
    R = gates.shape[0]
    tpb = T // ROW_TILE
    gb = jnp.pad(gate_b, (0, LANE - gate_b.shape[0])).reshape(1, LANE)
    return pl.pallas_call(
        _mlstm_gates_kernel,
        grid=(R // ROW_TILE,),
        in_specs=[pl.BlockSpec((ROW_TILE, LANE), lambda i: (i, 0)),
                  pl.BlockSpec((1, LANE), lambda i: (0, 0))],
        out_specs=[pl.BlockSpec((ROW_TILE, LANE), lambda i: (i, 0)),
                   pl.BlockSpec((1, LANE, ROW_TILE), lambda i: (i // tpb, 0, i % tpb))],
        out_shape=[jax.ShapeDtypeStruct((R, LANE), f32), jax.ShapeDtypeStruct((B, LANE, T), f32)],
        compiler_params=_params("arbitrary"),
        name="mlstm_gates",
    )(gates, gb)


def _mlstm_chunk_setup(reverse, gc_ref, gr_ref):
    Lc = gc_ref.shape[0]
    H = ML_HEADS
    i_off = 2 * H if reverse else 0
    f_off = i_off + H
    t_idx = lax.broadcasted_iota(jnp.int32, (Lc, Lc), 0)
    s_idx = lax.broadcasted_iota(jnp.int32, (Lc, Lc), 1)
    causal = (s_idx >= t_idx) if reverse else (s_idx <= t_idx)
    tri = causal.astype(f32)
    gc = gc_ref[...]
    gr = gr_ref[0]
    b_col_all = jnp.dot(tri, gc[:, f_off:f_off + H], preferred_element_type=f32, precision=HIGHEST)
    b_row_all = lax.dot_general(gr[f_off:f_off + H, :], tri, (((1,), (1,)), ((), ())),
                                preferred_element_type=f32, precision=HIGHEST)
    return dict(causal=causal, i_col=gc[:, i_off:i_off + H], i_row=gr[i_off:i_off + H, :],
                b_col=b_col_all, b_row=b_row_all, end=0 if reverse else Lc - 1)


class _MlstmChain:
    def __init__(self, h, cs, q_ref, kt_ref, v_ref, o_ref, ct_ref, m_ref):
        self.h, self.cs = h, cs
        self.q_ref, self.kt_ref, self.v_ref, self.o_ref, self.ct_ref, self.m_ref = q_ref, kt_ref, v_ref, o_ref, ct_ref, m_ref

    def _q(self):
        return self.q_ref[:, self.h * ML_QK:(self.h + 1) * ML_QK]

    def _kt(self):
        return self.kt_ref[0, self.h * ML_QK:(self.h + 1) * ML_QK, :]

    def _v(self):
        Lc = self.q_ref.shape[0]
        ones_col = (lax.broadcasted_iota(jnp.int32, (Lc, LANE), 1) == 0).astype(bf16)
        return jnp.concatenate([self.v_ref[:, self.h * ML_V:(self.h + 1) * ML_V], ones_col], axis=1)

    def scores(self):
        self.qk = jnp.dot(self._q(), self._kt(), preferred_element_type=f32)

    def gates(self):
        h, cs = self.h, self.cs
        i_row = cs["i_row"][h:h + 1, :]
        b_col = cs["b_col"][:, h:h + 1]
        b_row = cs["b_row"][h:h + 1, :]
        b_end = b_row[:, cs["end"]:cs["end"] + 1]
        m_prev = self.m_ref[h, 0:1, 0:1]
        d = jnp.where(cs["causal"], b_col + (i_row - b_row), -jnp.inf)
        inter = b_col + m_prev
        self.m_row = jnp.maximum(inter, jnp.max(d, axis=-1, keepdims=True))
        self.s = (self.qk * jnp.exp(d - self.m_row)).astype(bf16)
        self.w_prev = jnp.exp(inter - self.m_row)
        g_col = b_end - b_col + cs["i_col"][:, h:h + 1]
        g_row = b_end - b_row + i_row
        self.m_new = jnp.maximum(b_end + m_prev, jnp.max(g_row, axis=-1, keepdims=True))
        self.a_col = jnp.exp(g_col - self.m_new)
        self.a_prev = jnp.exp(b_end + m_prev - self.m_new)

    def values(self):
        h = self.h
        self.ct = self.ct_ref[h]
        tot = (jnp.dot(self.s, self._v(), preferred_element_type=f32)
               + self.w_prev * jnp.dot(self._q(), self.ct.astype(bf16), preferred_element_type=f32))
        scale = 1.0 / jnp.maximum(jnp.abs(tot[:, ML_V:ML_V + 1]), jnp.exp(-self.m_row))
        self.o_ref[:, h * ML_V:(h + 1) * ML_V] = tot[:, :ML_V] * scale

    def update(self):
        h = self.h
        va = (self._v().astype(f32) * self.a_col).astype(bf16)
        self.ct_ref[h] = self.a_prev * self.ct + jnp.dot(self._kt(), va, preferred_element_type=f32)
        self.m_ref[h] = jnp.broadcast_to(self.m_new, self.m_ref.shape[1:])


def _mlstm_scan_kernel(qf_ref, ktf_ref, vf_ref, gcf_ref, grf_ref, qb_ref, ktb_ref, vb_ref, gcb_ref, grb_ref,
                       of_ref, ob_ref, ctf_ref, mf_ref, ctb_ref, mb_ref):
    @pl.when(pl.program_id(1) == 0)
    def _():
        for ref in (ctf_ref, mf_ref, ctb_ref, mb_ref):
            ref[...] = jnp.zeros_like(ref)

    fwd = _mlstm_chunk_setup(False, gcf_ref, grf_ref)
    bwd = _mlstm_chunk_setup(True, gcb_ref, grb_ref)
    chains = []
    for h in range(ML_HEADS):
        chains.append(_MlstmChain(h, fwd, qf_ref, ktf_ref, vf_ref, of_ref, ctf_ref, mf_ref))
        chains.append(_MlstmChain(h, bwd, qb_ref, ktb_ref, vb_ref, ob_ref, ctb_ref, mb_ref))
    stages = ("scores", "gates", "values", "update")
    for k in range(len(chains) + len(stages) - 1):
        for depth, stage in enumerate(stages):
            if 0 <= k - depth < len(chains):
                getattr(chains[k - depth], stage)()


def mlstm_scan(q, kt, p, gc, gr, B, T):
    R = q.shape[0]
    tpb = T // ROW_TILE
    lat = tpb - 1

    def specs(reverse):
        def chunk(j):
            return jnp.where(j == 0, lat, (lat - j) if reverse else (j - 1))
        ins = [pl.BlockSpec((ROW_TILE, ML_QK_WIDTH), lambda b, j: (b * tpb + chunk(j), 0)),
               pl.BlockSpec((1, ML_QK_WIDTH, ROW_TILE), lambda b, j: (b, 0, chunk(j))),
               pl.BlockSpec((ROW_TILE, ML_WIDTH), lambda b, j: (b * tpb + chunk(j), ODD_OFF_V // ML_WIDTH)),
               pl.BlockSpec((ROW_TILE, LANE), lambda b, j: (b * tpb + chunk(j), 0)),
               pl.BlockSpec((1, LANE, ROW_TILE), lambda b, j: (b, 0, chunk(j)))]
        out = pl.BlockSpec((ROW_TILE, ML_WIDTH), lambda b, j: (b * tpb + chunk(j), 0))
        return ins, out

    ins_f, out_f = specs(False)
    ins_b, out_b = specs(True)
    state = [pltpu.VMEM((ML_HEADS, ML_QK, ML_V + LANE), f32), pltpu.VMEM((ML_HEADS, 8, LANE), f32)]
    return pl.pallas_call(
        _mlstm_scan_kernel,
        grid=(B, tpb),
        in_specs=ins_f + ins_b,
        out_specs=[out_f, out_b],
        out_shape=[jax.ShapeDtypeStruct((R, ML_WIDTH), f32), jax.ShapeDtypeStruct((R, ML_WIDTH), f32)],
        scratch_shapes=state + state,
        compiler_params=_params("arbitrary", "arbitrary"),
        name="mlstm_scan",
    )(q, kt, p, gc, gr, q, kt, p, gc, gr)


def _odd_out_kernel(hf_ref, hb_ref, o_ref, z_ref, hn_ref, w_ref, gp_ref, m_ref, x_ref, out_ref):
    D = x_ref.shape[-1]
    hs = (hf_ref[...] + hb_ref[...]) * jax.nn.sigmoid(o_ref[...].astype(f32))
    parts = []
    for h in range(ML_HEADS):
        seg = hs[:, h * ML_V:(h + 1) * ML_V]
        parts.append(seg * lax.rsqrt(jnp.mean(seg * seg, axis=-1, keepdims=True) + NORM_EPS))
    hn = jnp.concatenate(parts, axis=1) * hn_ref[...] * _silu(z_ref[...].astype(f32))
    y = jnp.dot(hn.astype(bf16), w_ref[...], preferred_element_type=f32)
    out_ref[...] = _post_residual(x_ref[...], y, gp_ref[...], m_ref[0, :, 2 * D:3 * D])


def odd_out(hf, hb, p, head_norm, w_out, g_post, mods, x_all, B, n_lat, T):
    D = x_all.shape[1]
    tpb = T // ROW_TILE
    lat = n_lat // ROW_TILE
    row = lambda b, i: (b * tpb + i, 0)
    full = lambda b, i: (0, 0)
    return pl.pallas_call(
        _odd_out_kernel,
        grid=(B, lat),
        in_specs=[pl.BlockSpec((ROW_TILE, ML_WIDTH), row),
                  pl.BlockSpec((ROW_TILE, ML_WIDTH), row),
                  pl.BlockSpec((ROW_TILE, ML_WIDTH), lambda b, i: (b * tpb + i, ODD_OFF_O // ML_WIDTH)),
                  pl.BlockSpec((ROW_TILE, ML_WIDTH), lambda b, i: (b * tpb + i, ODD_OFF_Z // ML_WIDTH)),
                  pl.BlockSpec((1, ML_WIDTH), full),
                  pl.BlockSpec((ML_WIDTH, D), full),
                  pl.BlockSpec((1, D), full),
                  pl.BlockSpec((1, 1, 3 * D), lambda b, i: (b, 0, 0)),
                  pl.BlockSpec((ROW_TILE, D), row)],
        out_specs=pl.BlockSpec((ROW_TILE, D), lambda b, i: (b * lat + i, 0)),
        out_shape=jax.ShapeDtypeStruct((B * n_lat, D), f32),
        compiler_params=_params("arbitrary", "arbitrary"),
        name="odd_out",
    )(hf, hb, p, p, head_norm.reshape(1, ML_WIDTH), w_out.astype(bf16), g_post.reshape(1, D), mods, x_all)


def odd_layer_last(x_all, mods, B, n_lat, n_ctx, g_pre, g_post, w_in, conv_w, conv_b, gate_b, head_norm, w_out):
    T = n_lat + n_ctx
    tpb = T // ROW_TILE
    h = norm_mod(x_all, g_pre, mods, tpb, B)
    wb = w_in.astype(bf16)
    tm = _proj_row_tile(B * T)
    p = matmul(h, wb[:, :ODD_MAIN], tm=tm, tn=_proj_col_tile(ODD_MAIN), out_dtype=bf16, name="odd_in_proj")
    n_gates = w_in.shape[1] - ODD_MAIN
    gates = matmul(h, jnp.pad(wb[:, ODD_MAIN:], ((0, 0), (0, LANE - n_gates))), tm=tm, tn=LANE, name="odd_gate_proj")
    q, kt = mlstm_prep(p, conv_w, conv_b, B, T)
    gc, gr = mlstm_gates(gates, gate_b, B, T)
    hf, hb = mlstm_scan(q, kt, p, gc, gr, B, T)
    return odd_out(hf, hb, p, head_norm, w_out, g_post, mods, x_all, B, n_lat, T)


def kernel(x, c, ctx, c_ctx, w_mod, b_mod, g_pre, g_post, e_w_in, e_conv_w, e_conv_b, e_filt_w1,
           e_filt_b1, e_filt_freq, e_filt_w2, e_filt_b2, e_filt_w3, e_hy_bias, e_q_norm, e_k_norm,
           e_w_out, o_w_in, o_conv_w, o_conv_b, o_gate_b, o_head_norm, o_w_out):
    B, n_lat, D = x.shape
    n_ctx = ctx.shape[1]
    T = n_lat + n_ctx
    depth = w_mod.shape[0]
    assert depth == 2 and B + 1 <= 8 and n_ctx == ROW_TILE and n_lat % ROW_TILE == 0
    cond = jnp.concatenate([c, c_ctx[None], jnp.zeros((8 - B - 1, D), f32)], axis=0)
    mods_all = adaln_all(cond, w_mod, b_mod)
    x_all = jnp.concatenate([x, ctx], axis=1).reshape(B * T, D)
    x_all = even_layer(x_all, mods_all[0].reshape(8, 1, 3 * D), B, n_lat, n_ctx, g_pre[0], g_post[0], e_w_in[0],
                       e_conv_w[0], e_conv_b[0], e_filt_w1[0], e_filt_b1[0], e_filt_freq[0], e_filt_w2[0],
                       e_filt_b2[0], e_filt_w3[0], e_hy_bias[0], e_q_norm[0], e_k_norm[0], e_w_out[0])
    out = odd_layer_last(x_all, mods_all[1].reshape(8, 1, 3 * D), B, n_lat, n_ctx, g_pre[1], g_post[1], o_w_in[0],
                         o_conv_w[0], o_conv_b[0], o_gate_b[0], o_head_norm[0], o_w_out[0])
    return out.reshape(B, n_lat, D)
```

```python
import functools
import math

import numpy as np
import jax
import jax.numpy as jnp
from jax import lax
from jax.experimental import pallas as pl
from jax.experimental.pallas import tpu as pltpu

f32 = jnp.float32
bf16 = jnp.bfloat16
HIGHEST = lax.Precision.HIGHEST

D_MODEL = 1024
GRID_W = 64
NORM_EPS = 1e-6

HY_WIDTH = 1024
HY_EMB = 33
HY_BANDS = (HY_EMB - 1) // 2
HY_HIDDEN = 64
HY_TARGET = 1e-2
HY_SHORT_DECAY_PCT = 0.3
HY_LONG_DECAY_PCT = 1.5

ATT_HEADS = 8
ATT_KV_HEADS = 2
ATT_GROUP = ATT_HEADS // ATT_KV_HEADS
HEAD_DIM = 128
ATT_WIDTH = ATT_HEADS * HEAD_DIM
ATT_KV_WIDTH = ATT_KV_HEADS * HEAD_DIM
ROPE_THETA = 10000.0
EVEN_OFF_XV = 0
EVEN_OFF_GHY = 3 * HY_WIDTH
EVEN_OFF_Q = EVEN_OFF_GHY + HY_WIDTH
EVEN_OFF_K = EVEN_OFF_Q + ATT_WIDTH
EVEN_OFF_V = EVEN_OFF_K + ATT_KV_WIDTH
EVEN_OFF_GATT = EVEN_OFF_V + ATT_KV_WIDTH
EVEN_IN = EVEN_OFF_GATT + ATT_WIDTH

ML_HEADS = 8
ML_QK = 128
ML_V = 256
ML_QK_WIDTH = ML_HEADS * ML_QK
ML_WIDTH = ML_HEADS * ML_V
ODD_OFF_Q = 0
ODD_OFF_K = ML_QK_WIDTH
ODD_OFF_V = 2 * ML_QK_WIDTH
ODD_OFF_O = ODD_OFF_V + ML_WIDTH
ODD_OFF_Z = ODD_OFF_O + ML_WIDTH
ODD_OFF_GATES = ODD_OFF_Z + ML_WIDTH
ODD_MAIN = ODD_OFF_GATES

ROW_TILE = 256
ML_CHUNK = 256
LANE = 128
VMEM_LIMIT_BYTES = 48 * 1024 * 1024

FFT_L1 = 64
FFT_L2 = 128


def _params(*sem):
    return pltpu.CompilerParams(dimension_semantics=sem, vmem_limit_bytes=VMEM_LIMIT_BYTES)


def _adaln_kernel(c_ref, w_ref, b_ref, o_ref):
    c = c_ref[...]
    s = c * jax.nn.sigmoid(c)
    o_ref[0] = jnp.dot(s, w_ref[0], preferred_element_type=f32, precision=HIGHEST) + b_ref[0]


def adaln_all(cond, w_mod, b_mod, *, tn=768):
    depth, D, N = w_mod.shape
    return pl.pallas_call(
        _adaln_kernel,
        grid=(depth, N // tn),
        in_specs=[
            pl.BlockSpec((8, D), lambda l, j: (0, 0)),
            pl.BlockSpec((1, D, tn), lambda l, j: (l, 0, j)),
            pl.BlockSpec((1, 1, tn), lambda l, j: (l, 0, j)),
        ],
        out_specs=pl.BlockSpec((1, 8, tn), lambda l, j: (l, 0, j)),
        out_shape=jax.ShapeDtypeStruct((depth, 8, N), f32),
        compiler_params=_params("arbitrary", "arbitrary"),
        name="adaln",
    )(cond, w_mod, b_mod.reshape(depth, 1, N))


def _mod_row(i, tiles_per_batch, ctx_row):
    lat_tiles = tiles_per_batch - 1
    return jnp.where(i % tiles_per_batch == lat_tiles, ctx_row, i // tiles_per_batch)


def _norm_mod_rows(x, g, m):
    D = x.shape[-1]
    y = x * lax.rsqrt(jnp.mean(x * x, axis=-1, keepdims=True) + NORM_EPS)
    return y * g * (1.0 + m[:, D:2 * D]) + m[:, 0:D]


def _input_row_specs(tiles_per_batch, D):
    lat_tiles = tiles_per_batch - 1

    def lat_index(i):
        return ((i // tiles_per_batch) * lat_tiles + jnp.minimum(i % tiles_per_batch, lat_tiles - 1), 0)

    return (pl.BlockSpec((ROW_TILE, D), lat_index),
            pl.BlockSpec((ROW_TILE, D), lambda i: (i // tiles_per_batch, 0)))


def _input_rows(tiles_per_batch, lat_ref, ctx_ref):
    is_ctx = pl.program_id(0) % tiles_per_batch == tiles_per_batch - 1
    return jnp.where(is_ctx, ctx_ref[...], lat_ref[...])


def _norm_mod_kernel(tiles_per_batch, x_ref, c_ref, g_ref, m_ref, o_ref):
    x = _input_rows(tiles_per_batch, x_ref, c_ref)
    o_ref[...] = _norm_mod_rows(x, g_ref[...], m_ref[0]).astype(o_ref.dtype)


def norm_mod(x_lat, x_ctx, g, mods, tiles_per_batch, ctx_row):
    D = x_lat.shape[1]
    R = x_lat.shape[0] + x_ctx.shape[0]
    lat_spec, ctx_spec = _input_row_specs(tiles_per_batch, D)
    return pl.pallas_call(
        functools.partial(_norm_mod_kernel, tiles_per_batch),
        grid=(R // ROW_TILE,),
        in_specs=[
            lat_spec, ctx_spec,
            pl.BlockSpec((1, D), lambda i: (0, 0)),
            pl.BlockSpec((1, 1, 3 * D), lambda i: (_mod_row(i, tiles_per_batch, ctx_row), 0, 0)),
        ],
        out_specs=pl.BlockSpec((ROW_TILE, D), lambda i: (i, 0)),
        out_shape=jax.ShapeDtypeStruct((R, D), bf16),
        compiler_params=_params("arbitrary"),
        name="norm_mod",
    )(x_lat, x_ctx, g.reshape(1, D), mods)


def _matmul_kernel(a_ref, b_ref, o_ref):
    o_ref[...] = jnp.dot(a_ref[...], b_ref[...], preferred_element_type=f32).astype(o_ref.dtype)


def _proj_row_tile(rows):
    return next(t for t in (1024, 512, ROW_TILE) if rows % t == 0)


def _proj_col_tile(cols, n_tiles=4):
    groups = cols // LANE
    return next(g for g in range(groups // n_tiles, 0, -1) if groups % g == 0) * LANE


def matmul(a, b, *, tm, tn, n_cols=None, out_dtype=f32, name="matmul"):
    M, K = a.shape
    N = b.shape[1] if n_cols is None else n_cols
    assert M % tm == 0 and N % tn == 0, (M, N, tm, tn)
    return pl.pallas_call(
        _matmul_kernel,
        grid=(N // tn, M // tm),
        in_specs=[pl.BlockSpec((tm, K), lambda j, i: (i, 0)),
                  pl.BlockSpec((K, tn), lambda j, i: (0, j))],
        out_specs=pl.BlockSpec((tm, tn), lambda j, i: (i, j)),
        out_shape=jax.ShapeDtypeStruct((M, N), out_dtype),
        compiler_params=_params("arbitrary", "arbitrary"),
        name=name,
    )(a, b)


def _conv3(x, prev_row, next_row, w, b):
    tm = x.shape[0]
    row = lax.broadcasted_iota(jnp.int32, x.shape, 0)
    xm = jnp.where(row == 0, prev_row, pltpu.roll(x, 1, 0))
    xp = jnp.where(row == tm - 1, next_row, pltpu.roll(x, tm - 1, 0))
    return w[0:1] * xm + w[1:2] * x + w[2:3] * xp + b


def _seq_edges(i, tiles_per_batch):
    r = i % tiles_per_batch
    lat_tiles = tiles_per_batch - 1
    first = jnp.logical_or(r == 0, r == lat_tiles)
    last = jnp.logical_or(r == lat_tiles - 1, r == lat_tiles)
    return first, last


HALO = 16


def _halo_rows(prev_ref, next_ref, first, last):
    prev_row = jnp.where(first, 0.0, prev_ref[...].astype(f32)[HALO - 1:HALO, :])
    next_row = jnp.where(last, 0.0, next_ref[...].astype(f32)[0:1, :])
    return prev_row, next_row


def _halo_specs(col_block, tc, n_rows):
    per = ROW_TILE // HALO
    n_blocks = n_rows // HALO
    prev = pl.BlockSpec((HALO, tc), lambda i, c: (jnp.maximum(i * per - 1, 0), col_block(c)))
    nxt = pl.BlockSpec((HALO, tc), lambda i, c: (jnp.minimum((i + 1) * per, n_blocks - 1), col_block(c)))
    return prev, nxt


def _hyena_pre_kernel(tiles_per_batch, x0_ref, x0p_ref, x0n_ref, x1_ref, x1p_ref, x1n_ref,
                      v_ref, vp_ref, vn_ref, w0_ref, w1_ref, w2_ref, b0_ref, b1_ref, b2_ref,
                      x0_out, g_out):
    first, last = _seq_edges(pl.program_id(0), tiles_per_batch)

    def conv(x_ref, p_ref, n_ref, w_ref, b_ref):
        prev_row, next_row = _halo_rows(p_ref, n_ref, first, last)
        return _conv3(x_ref[...].astype(f32), prev_row, next_row, w_ref[...], b_ref[...])

    x0 = conv(x0_ref, x0p_ref, x0n_ref, w0_ref, b0_ref)
    x1 = conv(x1_ref, x1p_ref, x1n_ref, w1_ref, b1_ref)
    v = conv(v_ref, vp_ref, vn_ref, w2_ref, b2_ref)
    g = v * x1
    x0_out[...] = x0
    g_out[...] = g


def hyena_pre(p, conv_w, conv_b, tiles_per_batch, *, tc=512):
    R = p.shape[0]
    W = HY_WIDTH
    nb = W // tc
    specs = []
    for part in range(3):
        col = functools.partial(lambda c, part: part * nb + c, part=part)
        main = pl.BlockSpec((ROW_TILE, tc), functools.partial(lambda i, c, col: (i, col(c)), col=col))
        prev, nxt = _halo_specs(col, tc, R)
        specs += [main, prev, nxt]
    wspecs = [pl.BlockSpec((3, tc), functools.partial(lambda i, c, part: (0, part * nb + c), part=part))
              for part in range(3)]
    bspecs = [pl.BlockSpec((1, tc), functools.partial(lambda i, c, part: (0, part * nb + c), part=part))
              for part in range(3)]
    out_spec = pl.BlockSpec((ROW_TILE, tc), lambda i, c: (i, c))
    args = [p] * 9 + [conv_w] * 3 + [conv_b.reshape(1, -1)] * 3
    return pl.pallas_call(
        functools.partial(_hyena_pre_kernel, tiles_per_batch),
        grid=(R // ROW_TILE, nb),
        in_specs=specs + wspecs + bspecs,
        out_specs=[out_spec, out_spec],
        out_shape=[jax.ShapeDtypeStruct((R, W), f32), jax.ShapeDtypeStruct((R, W), f32)],
        compiler_params=_params("arbitrary", "arbitrary"),
        name="hyena_pre",
    )(*args)


def _filter_kernel(n, rows, bands_ref, w1t_ref, w1c_ref, w1s_ref, b1_ref, fr_ref, w2_ref, b2_ref,
                   w3_ref, dl_ref, o_ref, nrm_ref):
    step = pl.program_id(0)
    j = step * rows + lax.broadcasted_iota(jnp.int32, (rows, 1), 0)
    d = jnp.where(j < n, j, 2 * n - j)
    valid = j != n
    df = d.astype(f32)
    t = df / float(n - 1)
    ang = (2.0 * math.pi / n) * df * bands_ref[...]
    fr = fr_ref[...]
    z1 = (t * w1t_ref[...]
          + jnp.dot(jnp.cos(ang), w1c_ref[...], preferred_element_type=f32, precision=HIGHEST)
          - jnp.dot(jnp.sin(ang), w1s_ref[...], preferred_element_type=f32, precision=HIGHEST)
          + b1_ref[...])
    hdn = jnp.sin(fr * z1)
    hdn = jnp.sin(fr * (jnp.dot(hdn, w2_ref[...], preferred_element_type=f32, precision=HIGHEST) + b2_ref[...]))
    h = jnp.dot(hdn.astype(bf16), w3_ref[0].astype(bf16), preferred_element_type=f32)
    h = h * jnp.exp(-t * jnp.abs(dl_ref[...]))
    h = jnp.where(valid, h, 0.0)
    o_ref[...] = h

    @pl.when(step == 0)
    def _():
        nrm_ref[...] = jnp.zeros_like(nrm_ref)

    nrm_ref[...] += jnp.sum(jnp.abs(h), axis=0, keepdims=True)


def hyena_filter(n, w1, b1, freq, w2, b2, w3, *, rows=256):
    W = HY_WIDTH
    Hd = HY_HIDDEN
    bands = jnp.linspace(1e-4, HY_BANDS - 1, HY_BANDS, dtype=f32).reshape(1, HY_BANDS)
    max_decay = math.log(HY_TARGET) / HY_SHORT_DECAY_PCT
    min_decay = math.log(HY_TARGET) / HY_LONG_DECAY_PCT
    deltas = jnp.linspace(min_decay, max_decay, W, dtype=f32).reshape(1, W)
    steps = 2 * n // rows
    half_steps = n // rows
    full = lambda s: (0, 0)
    w3r = w3.reshape(Hd, 2, W).transpose(1, 0, 2)
    return pl.pallas_call(
        functools.partial(_filter_kernel, n, rows),
        grid=(steps,),
        in_specs=[
            pl.BlockSpec((1, HY_BANDS), full),
            pl.BlockSpec((1, Hd), full),
            pl.BlockSpec((HY_BANDS, Hd), full),
            pl.BlockSpec((HY_BANDS, Hd), full),
            pl.BlockSpec((1, Hd), full),
            pl.BlockSpec((1, Hd), full),
            pl.BlockSpec((Hd, Hd), full),
            pl.BlockSpec((1, Hd), full),
            pl.BlockSpec((1, Hd, W), lambda s: (jnp.where(s * rows < n, 0, 1), 0, 0)),
            pl.BlockSpec((1, W), full),
        ],
        out_specs=[pl.BlockSpec((rows, W), lambda s: (s, 0)), pl.BlockSpec((1, W), full)],
        out_shape=[jax.ShapeDtypeStruct((2 * n, W), f32), jax.ShapeDtypeStruct((1, W), f32)],
        compiler_params=_params("arbitrary"),
        name="hyena_filter",
    )(bands, w1[0:1], w1[1:1 + HY_BANDS], w1[1 + HY_BANDS:], b1.reshape(1, Hd), freq.reshape(1, Hd),
      w2, b2.reshape(1, Hd), w3r, deltas)


@functools.lru_cache(maxsize=None)
def _fft_constants():
    L1, L2 = FFT_L1, FFT_L2
    L = L1 * L2
    k1 = np.arange(L1)
    nh = np.arange(L1 // 2)
    th = 2.0 * np.pi * np.outer(k1, nh) / L1
    m1r = np.concatenate([np.cos(th), -np.sin(th)], axis=0)
    m1i = np.concatenate([np.sin(th), np.cos(th)], axis=0)
    thf = 2.0 * np.pi * np.outer(k1, np.arange(L1)) / L1
    m1f = np.concatenate([np.cos(thf), -np.sin(thf)], axis=0)
    n2 = np.arange(L2)
    k2 = np.arange(L2)
    m = (k1[:, None, None] * n2[None, None, :] + L1 * k2[None, :, None] * n2[None, None, :]) % L
    ph = 2.0 * np.pi * m / L
    gr, gi = np.cos(ph), -np.sin(ph)
    g = np.concatenate([np.concatenate([gr, -gi], axis=2), np.concatenate([gi, gr], axis=2)], axis=1)
    gt = np.transpose(g, (0, 2, 1))
    thi = 2.0 * np.pi * np.outer(nh, k1) / L1
    m3r = np.concatenate([np.cos(thi), np.sin(thi)], axis=0)
    m3i = np.concatenate([-np.sin(thi), np.cos(thi)], axis=0)
    cast = lambda a: np.asarray(a, dtype=np.float32)
    return dict(m1r=cast(m1r), m1i=cast(m1i), m1f=cast(m1f), g=cast(g), gt=cast(gt), m3r=cast(m3r), m3i=cast(m3i))


@functools.lru_cache(maxsize=None)
def _ctx_dft_constants(n):
    L = 2 * n
    k = np.arange(L)
    th = 2.0 * np.pi * np.outer(k, np.arange(n)) / L
    c, s = np.cos(th), np.sin(th)
    mc = np.concatenate([np.concatenate([c, s], axis=1), np.concatenate([-s, c], axis=1)], axis=0)
    thf = 2.0 * np.pi * np.outer(k, np.arange(L)) / L
    mf = np.concatenate([np.cos(thf), -np.sin(thf)], axis=0)
    ct, st = c.T, s.T
    minv = np.concatenate([np.concatenate([ct, -st], axis=1), np.concatenate([st, ct], axis=1)], axis=0)
    cast = lambda a: np.asarray(a, dtype=np.float32)
    return dict(mc=cast(mc), mf=cast(mf), minv=cast(minv))


def _bf16_constants(consts):
    return {k: jnp.asarray(v).astype(bf16) for k, v in consts.items()}


SUB = 8
N2C = 32


def _store_step1(o_ref, j, a):
    for ch in range(a.shape[0] // SUB):
        o_ref[0, ch, j] = a[ch * SUB:(ch + 1) * SUB]


def _fft1_kernel(re_ref, im_ref, mr_ref, mi_ref, o_ref):
    base = pl.program_id(2) * N2C
    for j in range(N2C):
        rows = pl.ds(base + j, FFT_L1 // 2, stride=FFT_L2)
        _store_step1(o_ref, j,
                     jnp.dot(mr_ref[...], re_ref[0, rows, :].astype(bf16), preferred_element_type=f32)
                     + jnp.dot(mi_ref[...], im_ref[0, rows, :].astype(bf16), preferred_element_type=f32))


def _step1_out(P, W, index):
    groups = 2 * FFT_L1 // SUB
    spec = pl.BlockSpec((1, groups, N2C, SUB, LANE), index)
    return spec, jax.ShapeDtypeStruct((P, groups, FFT_L2, SUB, W), f32)


def fft_step1(g3, m1r, m1i, n_lat):
    B, _, W = g3.shape
    half = FFT_L1 // 2
    out_spec, out_shape = _step1_out(B // 2, W, lambda p, c, k: (p, 0, k, 0, c))
    return pl.pallas_call(
        _fft1_kernel,
        grid=(B // 2, W // LANE, FFT_L2 // N2C),
        in_specs=[pl.BlockSpec((1, n_lat, LANE), lambda p, c, k: (2 * p, 0, c)),
                  pl.BlockSpec((1, n_lat, LANE), lambda p, c, k: (2 * p + 1, 0, c)),
                  pl.BlockSpec((2 * FFT_L1, half), lambda p, c, k: (0, 0)),
                  pl.BlockSpec((2 * FFT_L1, half), lambda p, c, k: (0, 0))],
        out_specs=out_spec,
        out_shape=out_shape,
        compiler_params=_params("arbitrary", "arbitrary", "arbitrary"),
        name="fft_step1",
    )(g3, g3, m1r, m1i)


def _fft1_filter_kernel(x_ref, m_ref, o_ref):
    base = pl.program_id(1) * N2C
    for j in range(N2C):
        rows = pl.ds(base + j, FFT_L1, stride=FFT_L2)
        _store_step1(o_ref, j, jnp.dot(m_ref[...], x_ref[rows, :].astype(bf16), preferred_element_type=f32))


def fft_step1_filter(ts, m1f):
    L, W = ts.shape
    out_spec, out_shape = _step1_out(1, W, lambda c, k: (0, 0, k, 0, c))
    return pl.pallas_call(
        _fft1_filter_kernel,
        grid=(W // LANE, FFT_L2 // N2C),
        in_specs=[pl.BlockSpec((L, LANE), lambda c, k: (0, c)),
                  pl.BlockSpec((2 * FFT_L1, FFT_L1), lambda c, k: (0, 0))],
        out_specs=out_spec,
        out_shape=out_shape,
        compiler_params=_params("arbitrary", "arbitrary"),
        name="fft_step1_filter",
    )(ts, m1f)


def _cmul(yr, yi, hr, hi):
    return yr * hr - yi * hi, yr * hi + yi * hr


def _step1_column(a_refs, j):
    col = lambda ref: ref[0, pl.ds(j, FFT_L2, stride=SUB), :]
    re = jnp.concatenate([col(a_refs[0]), col(a_refs[1])], axis=1)
    im = jnp.concatenate([col(a_refs[2]), col(a_refs[3])], axis=1)
    return jnp.concatenate([re, im], axis=0).astype(bf16)


def _fft2_filter_kernel(a0_ref, a1_ref, a2_ref, a3_ref, g_ref, nrm_ref, o_ref):
    scale = 1.0 / (nrm_ref[...] * float(FFT_L1 * FFT_L2))
    for j in range(SUB):
        a = _step1_column((a0_ref, a1_ref, a2_ref, a3_ref), j)
        o_ref[j] = jnp.dot(g_ref[j], a, preferred_element_type=f32) * scale


def _step1_specs(index):
    def spec(part, lane_half):
        return pl.BlockSpec((1, FFT_L2 * SUB, LANE),
                            lambda *g: (index(*g)[0], part * (FFT_L1 // SUB) + index(*g)[1],
                                        2 * index(*g)[2] + lane_half))
    return [spec(0, 0), spec(0, 1), spec(1, 0), spec(1, 1)]


def fft_step2_filter(af, g, nrm):
    W = af.shape[-1]
    ct = 2 * LANE
    L1, R2 = FFT_L1, 2 * FFT_L2
    af = af.reshape(1, -1, W)
    return pl.pallas_call(
        _fft2_filter_kernel,
        grid=(L1 // SUB, W // ct),
        in_specs=_step1_specs(lambda k, c: (0, k, c)) + [
            pl.BlockSpec((SUB, R2, R2), lambda k, c: (k, 0, 0)),
            pl.BlockSpec((1, ct), lambda k, c: (0, c))],
        out_specs=pl.BlockSpec((SUB, R2, ct), lambda k, c: (k, 0, c)),
        out_shape=jax.ShapeDtypeStruct((L1, R2, W), f32),
        compiler_params=_params("arbitrary", "arbitrary"),
        name="fft_step2_filter",
    )(af, af, af, af, g, nrm)


def _fft2_kernel(a0_ref, a1_ref, a2_ref, a3_ref, g_ref, gt_ref, h_ref, ore_ref, oim_ref):
    half = FFT_L2
    for j in range(SUB):
        a = _step1_column((a0_ref, a1_ref, a2_ref, a3_ref), j)
        y = jnp.dot(g_ref[j], a, preferred_element_type=f32)
        pr, pi = _cmul(y[:half], y[half:], h_ref[j, :half], h_ref[j, half:])
        pcat = jnp.concatenate([pr, pi], axis=0).astype(bf16)
        b = jnp.dot(gt_ref[j], pcat, preferred_element_type=f32)
        for ch in range(FFT_L2 // N2C):
            ore_ref[0, ch, j] = b[ch * N2C:(ch + 1) * N2C]
            oim_ref[0, ch, j] = b[half + ch * N2C:half + (ch + 1) * N2C]


def fft_step2(a, g, gt, hf):
    P, W = a.shape[0], a.shape[-1]
    ct = 2 * LANE
    L1, R2 = FFT_L1, 2 * FFT_L2
    a = a.reshape(P, -1, W)
    out = pl.BlockSpec((1, FFT_L2 // N2C, SUB, N2C, ct), lambda k, c, p: (p, 0, k, 0, c))
    shape = jax.ShapeDtypeStruct((P, FFT_L2 // N2C, L1, N2C, W), f32)
    return pl.pallas_call(
        _fft2_kernel,
        grid=(L1 // SUB, W // ct, P),
        in_specs=_step1_specs(lambda k, c, p: (p, k, c)) + [
            pl.BlockSpec((SUB, R2, R2), lambda k, c, p: (k, 0, 0)),
            pl.BlockSpec((SUB, R2, R2), lambda k, c, p: (k, 0, 0)),
            pl.BlockSpec((SUB, R2, ct), lambda k, c, p: (k, 0, c))],
        out_specs=[out, out],
        out_shape=[shape, shape],
        compiler_params=_params("arbitrary", "arbitrary", "arbitrary"),
        name="fft_step2",
    )(a, a, a, a, g, gt, hf)


def _fft3_kernel(bre_ref, bim_ref, mr_ref, mi_ref, o_ref):
    L1 = FFT_L1
    half = L1 // 2
    base = pl.program_id(2) * N2C
    for j in range(N2C):
        br = bre_ref[0, pl.ds(j, L1, stride=N2C), :].astype(bf16)
        bi = bim_ref[0, pl.ds(j, L1, stride=N2C), :].astype(bf16)
        z = (jnp.dot(mr_ref[...], br, preferred_element_type=f32)
             + jnp.dot(mi_ref[...], bi, preferred_element_type=f32))
        rows = pl.ds(base + j, half, stride=FFT_L2)
        o_ref[0, 0, rows, :] = z[:half]
        o_ref[0, 1, rows, :] = z[half:]


def fft_step3(bre, bim, m3r, m3i, rows_total, n_lat):
    P, W = bre.shape[0], bre.shape[-1]
    L1, L2 = FFT_L1, FFT_L2
    bre = bre.reshape(P, -1, W)
    bim = bim.reshape(P, -1, W)
    spec = pl.BlockSpec((1, L1 * N2C, LANE), lambda p, c, k: (p, k, c))
    return pl.pallas_call(
        _fft3_kernel,
        grid=(P, W // LANE, L2 // N2C),
        in_specs=[spec, spec,
                  pl.BlockSpec((L1, L1), lambda p, c, k: (0, 0)),
                  pl.BlockSpec((L1, L1), lambda p, c, k: (0, 0))],
        out_specs=pl.BlockSpec((1, 2, n_lat, LANE), lambda p, c, k: (p, 0, 0, c)),
        out_shape=jax.ShapeDtypeStruct((P, 2, rows_total, W), f32),
        compiler_params=_params("arbitrary", "arbitrary", "arbitrary"),
        name="fft_step3",
    )(bre, bim, m3r, m3i)


def _ctx_filter_kernel(n, ts_ref, m_ref, nrm_ref, o_ref):
    scale = 1.0 / (nrm_ref[...] * float(2 * n))
    o_ref[...] = jnp.dot(m_ref[...], ts_ref[...].astype(bf16), preferred_element_type=f32) * scale


def ctx_filter_spectrum(ts, mf, nrm, *, ct=256):
    L, W = ts.shape
    return pl.pallas_call(
        functools.partial(_ctx_filter_kernel, L // 2),
        grid=(W // ct,),
        in_specs=[pl.BlockSpec((L, ct), lambda c: (0, c)),
                  pl.BlockSpec((2 * L, L), lambda c: (0, 0)),
                  pl.BlockSpec((1, ct), lambda c: (0, c))],
        out_specs=pl.BlockSpec((2 * L, ct), lambda c: (0, c)),
        out_shape=jax.ShapeDtypeStruct((2 * L, W), f32),
        compiler_params=_params("arbitrary"),
        name="ctx_filter_spectrum",
    )(ts, mf, nrm)


def _ctx_conv_kernel(n, re_ref, im_ref, mc_ref, minv_ref, h_ref, y_in_ref, o_ref):
    del y_in_ref
    L = 2 * n
    z = jnp.concatenate([re_ref[0], im_ref[0]], axis=0).astype(bf16)
    y = jnp.dot(mc_ref[...], z, preferred_element_type=f32)
    pr, pi = _cmul(y[:L], y[L:], h_ref[:L], h_ref[L:])
    pcat = jnp.concatenate([pr, pi], axis=0).astype(bf16)
    out = jnp.dot(minv_ref[...], pcat, preferred_element_type=f32)
    o_ref[0, 0] = out[:n]
    o_ref[0, 1] = out[n:]


def ctx_long_conv(gz, y4, hfc, consts, n_lat, n_ctx, *, ct=256):
    B, T, W = gz.shape
    blk = n_lat // n_ctx
    L = 2 * n_ctx
    out = pl.pallas_call(
        functools.partial(_ctx_conv_kernel, n_ctx),
        grid=(B // 2, W // ct),
        in_specs=[pl.BlockSpec((1, n_ctx, ct), lambda p, c: (2 * p, blk, c)),
                  pl.BlockSpec((1, n_ctx, ct), lambda p, c: (2 * p + 1, blk, c)),
                  pl.BlockSpec((2 * L, L), lambda p, c: (0, 0)),
                  pl.BlockSpec((L, 2 * L), lambda p, c: (0, 0)),
                  pl.BlockSpec((2 * L, ct), lambda p, c: (0, c)),
                  pl.BlockSpec(memory_space=pl.ANY)],
        out_specs=pl.BlockSpec((1, 2, n_ctx, ct), lambda p, c: (p, 0, blk, c)),
        out_shape=jax.ShapeDtypeStruct(y4.shape, f32),
        input_output_aliases={5: 0},
        compiler_params=_params("arbitrary", "arbitrary"),
        name="ctx_long_conv",
    )(gz, gz, consts["mc"], consts["minv"], hfc, y4)
    return out.reshape(B, T, W)


def hyena_long_conv(gz, n_lat, n_ctx, f_w1, f_b1, f_freq, f_w2, f_b2, f_w3):
    B, T, W = gz.shape
    cst = _bf16_constants(_fft_constants())
    L1, L2 = FFT_L1, FFT_L2
    assert 2 * n_lat == L1 * L2 and T % L2 == 0 and B % 2 == 0
    ts, nrm = hyena_filter(n_lat, f_w1, f_b1, f_freq, f_w2, f_b2, f_w3)
    af = fft_step1_filter(ts, cst["m1f"])
    hf = fft_step2_filter(af, cst["g"], nrm)
    a = fft_step1(gz, cst["m1r"], cst["m1i"], n_lat)
    bre, bim = fft_step2(a, cst["g"], cst["gt"], hf)
    y = fft_step3(bre, bim, cst["m3r"], cst["m3i"], T, n_lat)
    ccst = _bf16_constants(_ctx_dft_constants(n_ctx))
    ts_c, nrm_c = hyena_filter(n_ctx, f_w1, f_b1, f_freq, f_w2, f_b2, f_w3)
    hfc = ctx_filter_spectrum(ts_c, ccst["mf"], nrm_c)
    return ctx_long_conv(gz, y, hfc, ccst, n_lat, n_ctx)


@functools.lru_cache(maxsize=None)
def _rope_tables(n_lat, n_ctx):
    half = HEAD_DIM // 2
    nf = half // 2
    inv = ROPE_THETA ** (-np.arange(nf, dtype=np.float64) / nf)
    t = np.arange(n_lat)
    pos = np.stack([t // GRID_W, t % GRID_W], axis=1).astype(np.float64)
    ang = pos[:, :, None] * inv[None, None, :]
    cos = np.concatenate([np.cos(ang), np.cos(ang)], axis=2).reshape(n_lat, HEAD_DIM)
    sin = np.concatenate([-np.sin(ang), np.sin(ang)], axis=2).reshape(n_lat, HEAD_DIM)
    cos = np.concatenate([cos, np.ones((n_ctx, HEAD_DIM))], axis=0)
    sin = np.concatenate([sin, np.zeros((n_ctx, HEAD_DIM))], axis=0)
    return np.asarray(cos, np.float32), np.asarray(sin, np.float32)


def _norm_rope(x, w, cos, sin, lo_lane):
    y = x * lax.rsqrt(jnp.mean(x * x, axis=-1, keepdims=True) + NORM_EPS) * w
    nf = HEAD_DIM // 4
    partner = jnp.where(lo_lane, pltpu.roll(y, HEAD_DIM - nf, 1), pltpu.roll(y, nf, 1))
    return y * cos + partner * sin


def _qkv_prep_kernel(q_ref, k_ref, v_ref, cos_ref, sin_ref, qn_ref, kn_ref, q_out, kt_out, v_out):
    cos = cos_ref[...]
    sin = sin_ref[...]
    lane = lax.broadcasted_iota(jnp.int32, cos.shape, 1)
    lo_lane = (lane % (HEAD_DIM // 2)) < (HEAD_DIM // 4)
    scale = HEAD_DIM ** -0.5 * math.log2(math.e)
    for h in range(ATT_HEADS):
        sl = slice(h * HEAD_DIM, (h + 1) * HEAD_DIM)
        q_out[:, sl] = (_norm_rope(q_ref[:, sl].astype(f32), qn_ref[...], cos, sin, lo_lane) * scale).astype(q_out.dtype)
    ones_col = (lane == 0).astype(v_out.dtype)
    for h in range(ATT_KV_HEADS):
        sl = slice(h * HEAD_DIM, (h + 1) * HEAD_DIM)
        k = _norm_rope(k_ref[:, sl].astype(f32), kn_ref[...], cos, sin, lo_lane)
        kt_out[0, sl, :] = k.T.astype(kt_out.dtype)
        v_out[:, 2 * h * HEAD_DIM:(2 * h + 1) * HEAD_DIM] = v_ref[:, sl].astype(v_out.dtype)
        v_out[:, (2 * h + 1) * HEAD_DIM:(2 * h + 2) * HEAD_DIM] = ones_col


def qkv_prep(p, q_norm, k_norm, B, n_lat, n_ctx):
    R = p.shape[0]
    T = n_lat + n_ctx
    tpb = T // ROW_TILE
    cos, sin = _rope_tables(n_lat, n_ctx)
    return pl.pallas_call(
        _qkv_prep_kernel,
        grid=(R // ROW_TILE,),
        in_specs=[pl.BlockSpec((ROW_TILE, ATT_WIDTH), lambda i: (i, EVEN_OFF_Q // ATT_WIDTH)),
                  pl.BlockSpec((ROW_TILE, ATT_KV_WIDTH), lambda i: (i, EVEN_OFF_K // ATT_KV_WIDTH)),
                  pl.BlockSpec((ROW_TILE, ATT_KV_WIDTH), lambda i: (i, EVEN_OFF_V // ATT_KV_WIDTH)),
                  pl.BlockSpec((ROW_TILE, HEAD_DIM), lambda i: (i % tpb, 0)),
                  pl.BlockSpec((ROW_TILE, HEAD_DIM), lambda i: (i % tpb, 0)),
                  pl.BlockSpec((1, HEAD_DIM), lambda i: (0, 0)),
                  pl.BlockSpec((1, HEAD_DIM), lambda i: (0, 0))],
        out_specs=[pl.BlockSpec((ROW_TILE, ATT_WIDTH), lambda i: (i, 0)),
                   pl.BlockSpec((1, ATT_KV_WIDTH, ROW_TILE), lambda i: (i // tpb, 0, i % tpb)),
                   pl.BlockSpec((ROW_TILE, 2 * ATT_KV_WIDTH), lambda i: (i, 0))],
        out_shape=[jax.ShapeDtypeStruct((R, ATT_WIDTH), bf16),
                   jax.ShapeDtypeStruct((B, ATT_KV_WIDTH, T), bf16),
                   jax.ShapeDtypeStruct((R, 2 * ATT_KV_WIDTH), bf16)],
        compiler_params=_params("arbitrary"),
        name="qkv_prep",
    )(p, p, p, jnp.asarray(cos), jnp.asarray(sin), q_norm.reshape(1, HEAD_DIM), k_norm.reshape(1, HEAD_DIM))


ATT_CHUNK = 256


def _attention_kernel(n_lat, q_ref, kt_ref, v_ref, o_ref, qs_ref, sa_ref, sb_ref, pa_ref, pb_ref, os_ref):
    tq = q_ref.shape[0]
    n_chunks = ATT_GROUP * tq // ATT_CHUNK
    for h in range(ATT_GROUP):
        qs_ref[h * tq:(h + 1) * tq, :] = q_ref[:, h * HEAD_DIM:(h + 1) * HEAD_DIM]

    def rows(c):
        return pl.ds(pl.multiple_of(c * ATT_CHUNK, ATT_CHUNK), ATT_CHUNK)

    def finish(o):
        for h in range(ATT_GROUP):
            o_ref[:, h * HEAD_DIM:(h + 1) * HEAD_DIM] = os_ref[h * tq:(h + 1) * tq, :].astype(o_ref.dtype)

    s_refs = (sa_ref, sb_ref)
    p_refs = (pa_ref, pb_ref)

    def run(k_lo):
        def scores(c, slot):
            s_refs[slot][:, k_lo:] = jnp.dot(qs_ref[rows(c), :], kt_ref[0, :, k_lo:], preferred_element_type=f32)

        def exponentials(slot):
            s = s_refs[slot][:, k_lo:]
            m = jnp.max(s, axis=-1, keepdims=True)
            p_refs[slot][:, k_lo:] = jnp.exp2(s - m).astype(bf16)

        def weighted_values(c, slot):
            r = jnp.dot(p_refs[slot][:, k_lo:], v_ref[0, k_lo:, :], preferred_element_type=f32)
            os_ref[rows(c), :] = r[:, :HEAD_DIM] / r[:, HEAD_DIM:HEAD_DIM + 1]

        def stage(c, slot):
            scores(c + 1, 1 - slot)
            exponentials(slot)
            weighted_values(c - 1, 1 - slot)

        scores(0, 0)
        exponentials(0)
        scores(1, 1)

        def body(i, carry):
            stage(2 * i + 1, 1)
            stage(2 * i + 2, 0)
            return carry

        lax.fori_loop(0, (n_chunks - 2) // 2, body, 0)
        exponentials((n_chunks - 1) % 2)
        weighted_values(n_chunks - 2, n_chunks % 2)
        weighted_values(n_chunks - 1, (n_chunks - 1) % 2)
        finish(None)

    @pl.when(pl.program_id(2) < pl.num_programs(2) - 1)
    def _():
        run(0)

    @pl.when(pl.program_id(2) == pl.num_programs(2) - 1)
    def _():
        run(n_lat)


def attention(q, kt, v, B, n_lat, n_ctx):
    R = q.shape[0]
    T = n_lat + n_ctx
    tq = ROW_TILE
    assert n_ctx == tq
    qt = T // tq
    gw = ATT_GROUP * HEAD_DIM
    return pl.pallas_call(
        functools.partial(_attention_kernel, n_lat),
        grid=(B, ATT_KV_HEADS, qt),
        in_specs=[pl.BlockSpec((tq, gw), lambda b, g, i: (b * qt + i, g)),
                  pl.BlockSpec((1, HEAD_DIM, T), lambda b, g, i: (b, g, 0)),
                  pl.BlockSpec((1, T, 2 * HEAD_DIM), lambda b, g, i: (b, 0, g))],
        out_specs=pl.BlockSpec((tq, gw), lambda b, g, i: (b * qt + i, g)),
        out_shape=jax.ShapeDtypeStruct((R, ATT_WIDTH), bf16),
        scratch_shapes=[pltpu.VMEM((ATT_GROUP * tq, HEAD_DIM), bf16),
                        pltpu.VMEM((ATT_CHUNK, T), f32), pltpu.VMEM((ATT_CHUNK, T), f32),
                        pltpu.VMEM((ATT_CHUNK, T), bf16), pltpu.VMEM((ATT_CHUNK, T), bf16),
                        pltpu.VMEM((ATT_GROUP * tq, HEAD_DIM), f32)],
        compiler_params=_params("arbitrary", "arbitrary", "arbitrary"),
        name="attention",
    )(q, kt, v.reshape(B, T, 2 * ATT_KV_WIDTH))


def _post_residual(x, y, g_post, gate):
    yn = y * lax.rsqrt(jnp.mean(y * y, axis=-1, keepdims=True) + NORM_EPS) * g_post
    return x + gate * yn


def _silu(x):
    return x * jax.nn.sigmoid(x)


def _even_out_kernel(tiles_per_batch, x0_ref, g_ref, yc_ref, ghy_ref, att_ref, ga0_ref, ga1_ref, bias_ref, w_ref,
                     gp_ref, m_ref, x_ref, c_ref, gn_ref, mn_ref, o_ref, hn_ref):
    D = x_ref.shape[-1]
    g = g_ref[...]
    hy = x0_ref[...] * (yc_ref[...] + g * bias_ref[...]) * _silu(ghy_ref[...].astype(f32))
    g_att = jnp.concatenate([ga0_ref[...], ga1_ref[...]], axis=1).astype(f32)
    at = att_ref[...].astype(f32) * _silu(g_att)
    lhs = jnp.concatenate([hy, at], axis=1).astype(bf16)
    y = jnp.dot(lhs, w_ref[...], preferred_element_type=f32)
    x_new = _post_residual(_input_rows(tiles_per_batch, x_ref, c_ref), y, gp_ref[...], m_ref[0, :, 2 * D:3 * D])
    o_ref[...] = x_new
    hn_ref[...] = _norm_mod_rows(x_new, gn_ref[...], mn_ref[0]).astype(hn_ref.dtype)


def even_out(x0, g, yconv, p, att, hy_bias, w_out, g_post, mods, x_lat, x_ctx, next_g_pre, next_mods,
             tiles_per_batch, ctx_row):
    R = x0.shape[0]
    D = x_lat.shape[1]
    W = HY_WIDTH
    hw = ATT_WIDTH // 2
    row = lambda i: (i, 0)
    full = lambda i: (0, 0)
    lat_spec, ctx_spec = _input_row_specs(tiles_per_batch, D)
    return pl.pallas_call(
        functools.partial(_even_out_kernel, tiles_per_batch),
        grid=(R // ROW_TILE,),
        in_specs=[pl.BlockSpec((ROW_TILE, W), row),
                  pl.BlockSpec((ROW_TILE, W), row),
                  pl.BlockSpec((ROW_TILE, W), row),
                  pl.BlockSpec((ROW_TILE, W), lambda i: (i, EVEN_OFF_GHY // W)),
                  pl.BlockSpec((ROW_TILE, ATT_WIDTH), row),
                  pl.BlockSpec((ROW_TILE, hw), lambda i: (i, EVEN_OFF_GATT // hw)),
                  pl.BlockSpec((ROW_TILE, hw), lambda i: (i, EVEN_OFF_GATT // hw + 1)),
                  pl.BlockSpec((1, W), full),
                  pl.BlockSpec((W + ATT_WIDTH, D), full),
                  pl.BlockSpec((1, D), full),
                  pl.BlockSpec((1, 1, 3 * D), lambda i: (_mod_row(i, tiles_per_batch, ctx_row), 0, 0)),
                  lat_spec, ctx_spec,
                  pl.BlockSpec((1, D), full),
                  pl.BlockSpec((1, 1, 3 * D), lambda i: (_mod_row(i, tiles_per_batch, ctx_row), 0, 0))],
        out_specs=[pl.BlockSpec((ROW_TILE, D), row), pl.BlockSpec((ROW_TILE, D), row)],
        out_shape=[jax.ShapeDtypeStruct((R, D), f32), jax.ShapeDtypeStruct((R, D), bf16)],
        compiler_params=_params("arbitrary"),
        name="even_out",
    )(x0, g, yconv, p, att, p, p, hy_bias.reshape(1, W), w_out.astype(bf16), g_post.reshape(1, D), mods,
      x_lat, x_ctx, next_g_pre.reshape(1, D), next_mods)


def even_layer(x_lat, x_ctx, mods, next_g_pre, next_mods, B, n_lat, n_ctx, g_pre, g_post, w_in, conv_w, conv_b,
               f_w1, f_b1, f_freq, f_w2, f_b2, f_w3, hy_bias, q_norm, k_norm, w_out):
    T = n_lat + n_ctx
    tpb = T // ROW_TILE
    h = norm_mod(x_lat, x_ctx, g_pre, mods, tpb, B)
    p = matmul(h, w_in.astype(bf16), tm=_proj_row_tile(B * T), tn=_proj_col_tile(w_in.shape[1]), out_dtype=bf16,
               name="even_in_proj")
    x0, g = hyena_pre(p, conv_w, conv_b, tpb)
    yconv = hyena_long_conv(g.reshape(B, T, HY_WIDTH), n_lat, n_ctx, f_w1, f_b1, f_freq, f_w2, f_b2, f_w3)
    q, kt, v = qkv_prep(p, q_norm, k_norm, B, n_lat, n_ctx)
    att = attention(q, kt, v, B, n_lat, n_ctx)
    return even_out(x0, g, yconv.reshape(B * T, HY_WIDTH), p, att, hy_bias, w_out, g_post, mods, x_lat, x_ctx,
                    next_g_pre, next_mods, tpb, B)


def _mlstm_prep_kernel(tiles_per_batch, q_ref, qp_ref, qn_ref, k_ref, kp_ref, kn_ref,
                       wq_ref, wk_ref, bq_ref, bk_ref, q_out, kt_out):
    first, last = _seq_edges(pl.program_id(0), tiles_per_batch)
    prev_row, next_row = _halo_rows(qp_ref, qn_ref, first, last)
    q = _silu(_conv3(q_ref[...].astype(f32), prev_row, next_row, wq_ref[...], bq_ref[...]))
    q_out[...] = q.astype(q_out.dtype)
    prev_row, next_row = _halo_rows(kp_ref, kn_ref, first, last)
    k = _silu(_conv3(k_ref[...].astype(f32), prev_row, next_row, wk_ref[...], bk_ref[...])) * (ML_QK ** -0.5)
    for h in range(k.shape[1] // ML_QK):
        sl = slice(h * ML_QK, (h + 1) * ML_QK)
        kt_out[0, sl, :] = k[:, sl].T.astype(kt_out.dtype)


def mlstm_prep(p, conv_w, conv_b, B, T, *, tc=512):
    R = p.shape[0]
    tpb = T // ROW_TILE
    nb = ML_QK_WIDTH // tc
    qcol = lambda c: c
    kcol = lambda c: nb + c
    qprev, qnext = _halo_specs(qcol, tc, R)
    kprev, knext = _halo_specs(kcol, tc, R)
    return pl.pallas_call(
        functools.partial(_mlstm_prep_kernel, tpb),
        grid=(R // ROW_TILE, nb),
        in_specs=[pl.BlockSpec((ROW_TILE, tc), lambda i, c: (i, c)), qprev, qnext,
                  pl.BlockSpec((ROW_TILE, tc), lambda i, c: (i, nb + c)), kprev, knext,
                  pl.BlockSpec((3, tc), lambda i, c: (0, c)),
                  pl.BlockSpec((3, tc), lambda i, c: (0, nb + c)),
                  pl.BlockSpec((1, tc), lambda i, c: (0, c)),
                  pl.BlockSpec((1, tc), lambda i, c: (0, nb + c))],
        out_specs=[pl.BlockSpec((ROW_TILE, tc), lambda i, c: (i, c)),
                   pl.BlockSpec((1, tc, ROW_TILE), lambda i, c: (i // tpb, c, i % tpb))],
        out_shape=[jax.ShapeDtypeStruct((R, ML_QK_WIDTH), bf16),
                   jax.ShapeDtypeStruct((B, ML_QK_WIDTH, T), bf16)],
        compiler_params=_params("arbitrary", "arbitrary"),
        name="mlstm_prep",
    )(p, p, p, p, p, p, conv_w, conv_w, conv_b.reshape(1, -1), conv_b.reshape(1, -1))


def _log_sigmoid(x):
    return jnp.minimum(x, 0.0) - jnp.log(1.0 + jnp.exp(-jnp.abs(x)))


def _mlstm_gates_kernel(g_ref, b_ref, gc_out, gr_out):
    pre = g_ref[...] + b_ref[...]
    lane = lax.broadcasted_iota(jnp.int32, pre.shape, 1)
    is_forget = (lane // ML_HEADS) % 2 == 1
    gc = jnp.where(is_forget, _log_sigmoid(pre), pre)
    gc_out[...] = gc
    gr_out[0] = gc.T


def mlstm_gates(gates, gate_b, B, T):
    R = gates.shape[0]
    tpb = T // ROW_TILE
    gb = jnp.pad(gate_b, (0, LANE - gate_b.shape[0])).reshape(1, LANE)
    return pl.pallas_call(
        _mlstm_gates_kernel,
        grid=(R // ROW_TILE,),
        in_specs=[pl.BlockSpec((ROW_TILE, LANE), lambda i: (i, 0)),
                  pl.BlockSpec((1, LANE), lambda i: (0, 0))],
        out_specs=[pl.BlockSpec((ROW_TILE, LANE), lambda i: (i, 0)),
                   pl.BlockSpec((1, LANE, ROW_TILE), lambda i: (i // tpb, 0, i % tpb))],
        out_shape=[jax.ShapeDtypeStruct((R, LANE), f32), jax.ShapeDtypeStruct((B, LANE, T), f32)],
        compiler_params=_params("arbitrary"),
        name="mlstm_gates",
    )(gates, gb)


def _mlstm_chunk_setup(reverse, gc_ref, gr_ref):
    Lc = gc_ref.shape[0]
    H = ML_HEADS
    i_off = 2 * H if reverse else 0
    f_off = i_off + H
    t_idx = lax.broadcasted_iota(jnp.int32, (Lc, Lc), 0)
    s_idx = lax.broadcasted_iota(jnp.int32, (Lc, Lc), 1)
    causal = (s_idx >= t_idx) if reverse else (s_idx <= t_idx)
    tri = causal.astype(f32)
    gc = gc_ref[...]
    gr = gr_ref[0]
    b_col_all = jnp.dot(tri, gc[:, f_off:f_off + H], preferred_element_type=f32, precision=HIGHEST)
    b_row_all = lax.dot_general(gr[f_off:f_off + H, :], tri, (((1,), (1,)), ((), ())),
                                preferred_element_type=f32, precision=HIGHEST)
    return dict(causal=causal, i_col=gc[:, i_off:i_off + H], i_row=gr[i_off:i_off + H, :],
                b_col=b_col_all, b_row=b_row_all, end=0 if reverse else Lc - 1)


class _MlstmChain:
    def __init__(self, h, cs, q_ref, kt_ref, v_ref, o_ref, ct_ref, m_ref):
        self.h, self.cs = h, cs
        self.q_ref, self.kt_ref, self.v_ref, self.o_ref, self.ct_ref, self.m_ref = q_ref, kt_ref, v_ref, o_ref, ct_ref, m_ref

    def _q(self):
        return self.q_ref[:, self.h * ML_QK:(self.h + 1) * ML_QK]

    def _kt(self):
        return self.kt_ref[0, self.h * ML_QK:(self.h + 1) * ML_QK, :]

    def _v(self):
        Lc = self.q_ref.shape[0]
        ones_col = (lax.broadcasted_iota(jnp.int32, (Lc, LANE), 1) == 0).astype(bf16)
        return jnp.concatenate([self.v_ref[:, self.h * ML_V:(self.h + 1) * ML_V], ones_col], axis=1)

    def scores(self):
        self.qk = jnp.dot(self._q(), self._kt(), preferred_element_type=f32)

    def gates(self):
        h, cs = self.h, self.cs
        i_row = cs["i_row"][h:h + 1, :]
        b_col = cs["b_col"][:, h:h + 1]
        b_row = cs["b_row"][h:h + 1, :]
        b_end = b_row[:, cs["end"]:cs["end"] + 1]
        m_prev = self.m_ref[h, 0:1, 0:1]
        d = jnp.where(cs["causal"], b_col + (i_row - b_row), -jnp.inf)
        inter = b_col + m_prev
        self.m_row = jnp.maximum(inter, jnp.max(d, axis=-1, keepdims=True))
        self.s = (self.qk * jnp.exp(d - self.m_row)).astype(bf16)
        self.w_prev = jnp.exp(inter - self.m_row)
        g_col = b_end - b_col + cs["i_col"][:, h:h + 1]
        g_row = b_end - b_row + i_row
        self.m_new = jnp.maximum(b_end + m_prev, jnp.max(g_row, axis=-1, keepdims=True))
        self.a_col = jnp.exp(g_col - self.m_new)
        self.a_prev = jnp.exp(b_end + m_prev - self.m_new)

    def values(self):
        h = self.h
        self.ct = self.ct_ref[h]
        tot = (jnp.dot(self.s, self._v(), preferred_element_type=f32)
               + self.w_prev * jnp.dot(self._q(), self.ct.astype(bf16), preferred_element_type=f32))
        scale = 1.0 / jnp.maximum(jnp.abs(tot[:, ML_V:ML_V + 1]), jnp.exp(-self.m_row))
        self.o_ref[:, h * ML_V:(h + 1) * ML_V] = (tot[:, :ML_V] * scale).astype(self.o_ref.dtype)

    def update(self):
        h = self.h
        va = (self._v().astype(f32) * self.a_col).astype(bf16)
        self.ct_ref[h] = self.a_prev * self.ct + jnp.dot(self._kt(), va, preferred_element_type=f32)
        self.m_ref[h] = jnp.broadcast_to(self.m_new, self.m_ref.shape[1:])


def _mlstm_scan_kernel(qf_ref, ktf_ref, vf_ref, gcf_ref, grf_ref, qb_ref, ktb_ref, vb_ref, gcb_ref, grb_ref,
                       of_ref, ob_ref, ctf_ref, mf_ref, ctb_ref, mb_ref):
    @pl.when(pl.program_id(1) == 0)
    def _():
        for ref in (ctf_ref, mf_ref, ctb_ref, mb_ref):
            ref[...] = jnp.zeros_like(ref)

    fwd = _mlstm_chunk_setup(False, gcf_ref, grf_ref)
    bwd = _mlstm_chunk_setup(True, gcb_ref, grb_ref)
    chains = []
    for h in range(ML_HEADS):
        chains.append(_MlstmChain(h, fwd, qf_ref, ktf_ref, vf_ref, of_ref, ctf_ref, mf_ref))
        chains.append(_MlstmChain(h, bwd, qb_ref, ktb_ref, vb_ref, ob_ref, ctb_ref, mb_ref))
    stages = ("scores", "gates", "values", "update")
    for k in range(len(chains) + len(stages) - 1):
        for depth, stage in enumerate(stages):
            if 0 <= k - depth < len(chains):
                getattr(chains[k - depth], stage)()


def mlstm_scan(q, kt, p, gc, gr, B, n_lat, n_ctx):
    R = q.shape[0]
    Lc = ML_CHUNK
    tpb = (n_lat + n_ctx) // Lc
    lat = n_lat // Lc
    ctx = n_ctx // Lc

    def specs(reverse):
        def chunk(j):
            if reverse:
                return tpb - 1 - j
            return jnp.where(j < ctx, lat + j, j - ctx)
        ins = [pl.BlockSpec((Lc, ML_QK_WIDTH), lambda b, j: (b * tpb + chunk(j), 0)),
               pl.BlockSpec((1, ML_QK_WIDTH, Lc), lambda b, j: (b, 0, chunk(j))),
               pl.BlockSpec((Lc, ML_WIDTH), lambda b, j: (b * tpb + chunk(j), ODD_OFF_V // ML_WIDTH)),
               pl.BlockSpec((Lc, LANE), lambda b, j: (b * tpb + chunk(j), 0)),
               pl.BlockSpec((1, LANE, Lc), lambda b, j: (b, 0, chunk(j)))]
        out = pl.BlockSpec((Lc, ML_WIDTH), lambda b, j: (b * tpb + chunk(j), 0))
        return ins, out

    ins_f, out_f = specs(False)
    ins_b, out_b = specs(True)
    state = [pltpu.VMEM((ML_HEADS, ML_QK, ML_V + LANE), f32), pltpu.VMEM((ML_HEADS, 8, LANE), f32)]
    return pl.pallas_call(
        _mlstm_scan_kernel,
        grid=(B, tpb),
        in_specs=ins_f + ins_b,
        out_specs=[out_f, out_b],
        out_shape=[jax.ShapeDtypeStruct((R, ML_WIDTH), bf16), jax.ShapeDtypeStruct((R, ML_WIDTH), bf16)],
        scratch_shapes=state + state,
        compiler_params=_params("arbitrary", "arbitrary"),
        name="mlstm_scan",
    )(q, kt, p, gc, gr, q, kt, p, gc, gr)


def _odd_out_kernel(hf_ref, hb_ref, o_ref, z_ref, hn_ref, w_ref, gp_ref, m_ref, x_ref, out_ref):
    D = x_ref.shape[-1]
    hs = (hf_ref[...].astype(f32) + hb_ref[...].astype(f32)) * jax.nn.sigmoid(o_ref[...].astype(f32))
    parts = []
    for h in range(ML_HEADS):
        seg = hs[:, h * ML_V:(h + 1) * ML_V]
        parts.append(seg * lax.rsqrt(jnp.mean(seg * seg, axis=-1, keepdims=True) + NORM_EPS))
    hn = jnp.concatenate(parts, axis=1) * hn_ref[...] * _silu(z_ref[...].astype(f32))
    y = jnp.dot(hn.astype(bf16), w_ref[...], preferred_element_type=f32)
    out_ref[...] = _post_residual(x_ref[...], y, gp_ref[...], m_ref[0, :, 2 * D:3 * D])


def odd_out(hf, hb, p, head_norm, w_out, g_post, mods, x_all, B, n_lat, T):
    D = x_all.shape[1]
    tpb = T // ROW_TILE
    lat = n_lat // ROW_TILE
    row = lambda b, i: (b * tpb + i, 0)
    full = lambda b, i: (0, 0)
    return pl.pallas_call(
        _odd_out_kernel,
        grid=(B, lat),
        in_specs=[pl.BlockSpec((ROW_TILE, ML_WIDTH), row),
                  pl.BlockSpec((ROW_TILE, ML_WIDTH), row),
                  pl.BlockSpec((ROW_TILE, ML_WIDTH), lambda b, i: (b * tpb + i, ODD_OFF_O // ML_WIDTH)),
                  pl.BlockSpec((ROW_TILE, ML_WIDTH), lambda b, i: (b * tpb + i, ODD_OFF_Z // ML_WIDTH)),
                  pl.BlockSpec((1, ML_WIDTH), full),
                  pl.BlockSpec((ML_WIDTH, D), full),
                  pl.BlockSpec((1, D), full),
                  pl.BlockSpec((1, 1, 3 * D), lambda b, i: (b, 0, 0)),
                  pl.BlockSpec((ROW_TILE, D), row)],
        out_specs=pl.BlockSpec((ROW_TILE, D), lambda b, i: (b * lat + i, 0)),
        out_shape=jax.ShapeDtypeStruct((B * n_lat, D), f32),
        compiler_params=_params("arbitrary", "arbitrary"),
        name="odd_out",
    )(hf, hb, p, p, head_norm.reshape(1, ML_WIDTH), w_out.astype(bf16), g_post.reshape(1, D), mods, x_all)


def odd_layer_last(x_all, h, mods, B, n_lat, n_ctx, g_post, w_in, conv_w, conv_b, gate_b, head_norm, w_out):
    T = n_lat + n_ctx
    wb = w_in.astype(bf16)
    tm = _proj_row_tile(B * T)
    p = matmul(h, wb, tm=tm, tn=_proj_col_tile(ODD_MAIN), n_cols=ODD_MAIN, out_dtype=bf16, name="odd_in_proj")
    n_gates = w_in.shape[1] - ODD_MAIN
    gates = matmul(h, jnp.pad(wb[:, ODD_MAIN:], ((0, 0), (0, LANE - n_gates))), tm=tm, tn=LANE, name="odd_gate_proj")
    q, kt = mlstm_prep(p, conv_w, conv_b, B, T)
    gc, gr = mlstm_gates(gates, gate_b, B, T)
    hf, hb = mlstm_scan(q, kt, p, gc, gr, B, n_lat, n_ctx)
    return odd_out(hf, hb, p, head_norm, w_out, g_post, mods, x_all, B, n_lat, T)


def kernel(x, c, ctx, c_ctx, w_mod, b_mod, g_pre, g_post, e_w_in, e_conv_w, e_conv_b, e_filt_w1,
           e_filt_b1, e_filt_freq, e_filt_w2, e_filt_b2, e_filt_w3, e_hy_bias, e_q_norm, e_k_norm,
           e_w_out, o_w_in, o_conv_w, o_conv_b, o_gate_b, o_head_norm, o_w_out):
    B, n_lat, D = x.shape
    n_ctx = ctx.shape[1]
    T = n_lat + n_ctx
    depth = w_mod.shape[0]
    assert depth == 2 and B + 1 <= 8 and n_ctx == ROW_TILE and n_lat % ROW_TILE == 0
    cond = jnp.concatenate([c, c_ctx[None], jnp.zeros((8 - B - 1, D), f32)], axis=0)
    mods_all = adaln_all(cond, w_mod, b_mod)
    mods0 = mods_all[0].reshape(8, 1, 3 * D)
    mods1 = mods_all[1].reshape(8, 1, 3 * D)
    x_all, h1 = even_layer(x.reshape(B * n_lat, D), ctx.reshape(B * n_ctx, D), mods0, g_pre[1], mods1,
                           B, n_lat, n_ctx, g_pre[0], g_post[0], e_w_in[0],
                           e_conv_w[0], e_conv_b[0], e_filt_w1[0], e_filt_b1[0], e_filt_freq[0], e_filt_w2[0],
                           e_filt_b2[0], e_filt_w3[0], e_hy_bias[0], e_q_norm[0], e_k_norm[0], e_w_out[0])
    out = odd_layer_last(x_all, h1, mods1, B, n_lat, n_ctx, g_post[1], o_w_in[0],
                         o_conv_w[0], o_conv_b[0], o_gate_b[0], o_head_norm[0], o_w_out[0])
    return out.reshape(B, n_lat, D)
```

```python
import functools
import math

import numpy as np
import jax
import jax.numpy as jnp
from jax import lax
from jax.experimental import pallas as pl
from jax.experimental.pallas import tpu as pltpu

f32 = jnp.float32
bf16 = jnp.bfloat16
HIGHEST = lax.Precision.HIGHEST

D_MODEL = 1024
GRID_W = 64
NORM_EPS = 1e-6

HY_WIDTH = 1024
HY_EMB = 33
HY_BANDS = (HY_EMB - 1) // 2
HY_HIDDEN = 64
HY_TARGET = 1e-2
HY_SHORT_DECAY_PCT = 0.3
HY_LONG_DECAY_PCT = 1.5

ATT_HEADS = 8
ATT_KV_HEADS = 2
ATT_GROUP = ATT_HEADS // ATT_KV_HEADS
HEAD_DIM = 128
ATT_WIDTH = ATT_HEADS * HEAD_DIM
ATT_KV_WIDTH = ATT_KV_HEADS * HEAD_DIM
ROPE_THETA = 10000.0
EVEN_OFF_XV = 0
EVEN_OFF_GHY = 3 * HY_WIDTH
EVEN_OFF_Q = EVEN_OFF_GHY + HY_WIDTH
EVEN_OFF_K = EVEN_OFF_Q + ATT_WIDTH
EVEN_OFF_V = EVEN_OFF_K + ATT_KV_WIDTH
EVEN_OFF_GATT = EVEN_OFF_V + ATT_KV_WIDTH
EVEN_IN = EVEN_OFF_GATT + ATT_WIDTH

ML_HEADS = 8
ML_QK = 128
ML_V = 256
ML_QK_WIDTH = ML_HEADS * ML_QK
ML_WIDTH = ML_HEADS * ML_V
ODD_OFF_Q = 0
ODD_OFF_K = ML_QK_WIDTH
ODD_OFF_V = 2 * ML_QK_WIDTH
ODD_OFF_O = ODD_OFF_V + ML_WIDTH
ODD_OFF_Z = ODD_OFF_O + ML_WIDTH
ODD_OFF_GATES = ODD_OFF_Z + ML_WIDTH
ODD_MAIN = ODD_OFF_GATES

ROW_TILE = 256
ML_CHUNK = 256
LANE = 128
VMEM_LIMIT_BYTES = 48 * 1024 * 1024

FFT_L1 = 64
FFT_L2 = 128


def _params(*sem):
    return pltpu.CompilerParams(dimension_semantics=sem, vmem_limit_bytes=VMEM_LIMIT_BYTES)


def _adaln_kernel(c_ref, w_ref, b_ref, o_ref):
    c = c_ref[...]
    s = c * jax.nn.sigmoid(c)
    o_ref[0] = jnp.dot(s, w_ref[0], preferred_element_type=f32, precision=HIGHEST) + b_ref[0]


def adaln_all(cond, w_mod, b_mod, *, tn=768):
    depth, D, N = w_mod.shape
    return pl.pallas_call(
        _adaln_kernel,
        grid=(depth, N // tn),
        in_specs=[
            pl.BlockSpec((8, D), lambda l, j: (0, 0)),
            pl.BlockSpec((1, D, tn), lambda l, j: (l, 0, j)),
            pl.BlockSpec((1, 1, tn), lambda l, j: (l, 0, j)),
        ],
        out_specs=pl.BlockSpec((1, 8, tn), lambda l, j: (l, 0, j)),
        out_shape=jax.ShapeDtypeStruct((depth, 8, N), f32),
        compiler_params=_params("arbitrary", "arbitrary"),
        name="adaln",
    )(cond, w_mod, b_mod.reshape(depth, 1, N))


def _mod_row(i, tiles_per_batch, ctx_row):
    lat_tiles = tiles_per_batch - 1
    return jnp.where(i % tiles_per_batch == lat_tiles, ctx_row, i // tiles_per_batch)


def _norm_mod_rows(x, g, m):
    D = x.shape[-1]
    y = x * lax.rsqrt(jnp.mean(x * x, axis=-1, keepdims=True) + NORM_EPS)
    return y * g * (1.0 + m[:, D:2 * D]) + m[:, 0:D]


def _input_row_specs(tiles_per_batch, D):
    lat_tiles = tiles_per_batch - 1

    def lat_index(i):
        return ((i // tiles_per_batch) * lat_tiles + jnp.minimum(i % tiles_per_batch, lat_tiles - 1), 0)

    return (pl.BlockSpec((ROW_TILE, D), lat_index),
            pl.BlockSpec((ROW_TILE, D), lambda i: (i // tiles_per_batch, 0)))


def _input_rows(tiles_per_batch, lat_ref, ctx_ref):
    is_ctx = pl.program_id(0) % tiles_per_batch == tiles_per_batch - 1
    return jnp.where(is_ctx, ctx_ref[...], lat_ref[...])


def _norm_mod_kernel(tiles_per_batch, x_ref, c_ref, g_ref, m_ref, o_ref):
    x = _input_rows(tiles_per_batch, x_ref, c_ref)
    o_ref[...] = _norm_mod_rows(x, g_ref[...], m_ref[0]).astype(o_ref.dtype)


def norm_mod(x_lat, x_ctx, g, mods, tiles_per_batch, ctx_row):
    D = x_lat.shape[1]
    R = x_lat.shape[0] + x_ctx.shape[0]
    lat_spec, ctx_spec = _input_row_specs(tiles_per_batch, D)
    return pl.pallas_call(
        functools.partial(_norm_mod_kernel, tiles_per_batch),
        grid=(R // ROW_TILE,),
        in_specs=[
            lat_spec, ctx_spec,
            pl.BlockSpec((1, D), lambda i: (0, 0)),
            pl.BlockSpec((1, 1, 3 * D), lambda i: (_mod_row(i, tiles_per_batch, ctx_row), 0, 0)),
        ],
        out_specs=pl.BlockSpec((ROW_TILE, D), lambda i: (i, 0)),
        out_shape=jax.ShapeDtypeStruct((R, D), bf16),
        compiler_params=_params("arbitrary"),
        name="norm_mod",
    )(x_lat, x_ctx, g.reshape(1, D), mods)


def _matmul_kernel(a_ref, b_ref, o_ref):
    o_ref[...] = jnp.dot(a_ref[...], b_ref[...], preferred_element_type=f32).astype(o_ref.dtype)


def _proj_row_tile(rows):
    return next(t for t in (1024, 512, ROW_TILE) if rows % t == 0)


def _proj_col_tile(cols, n_tiles=4):
    groups = cols // LANE
    return next(g for g in range(groups // n_tiles, 0, -1) if groups % g == 0) * LANE


def matmul(a, b, *, tm, tn, n_cols=None, out_dtype=f32, name="matmul"):
    M, K = a.shape
    N = b.shape[1] if n_cols is None else n_cols
    assert M % tm == 0 and N % tn == 0, (M, N, tm, tn)
    return pl.pallas_call(
        _matmul_kernel,
        grid=(N // tn, M // tm),
        in_specs=[pl.BlockSpec((tm, K), lambda j, i: (i, 0)),
                  pl.BlockSpec((K, tn), lambda j, i: (0, j))],
        out_specs=pl.BlockSpec((tm, tn), lambda j, i: (i, j)),
        out_shape=jax.ShapeDtypeStruct((M, N), out_dtype),
        compiler_params=_params("arbitrary", "arbitrary"),
        name=name,
    )(a, b)


def _conv3(x, prev_row, next_row, w, b):
    tm = x.shape[0]
    row = lax.broadcasted_iota(jnp.int32, x.shape, 0)
    xm = jnp.where(row == 0, prev_row, pltpu.roll(x, 1, 0))
    xp = jnp.where(row == tm - 1, next_row, pltpu.roll(x, tm - 1, 0))
    return w[0:1] * xm + w[1:2] * x + w[2:3] * xp + b


def _seq_edges(i, tiles_per_batch):
    r = i % tiles_per_batch
    lat_tiles = tiles_per_batch - 1
    first = jnp.logical_or(r == 0, r == lat_tiles)
    last = jnp.logical_or(r == lat_tiles - 1, r == lat_tiles)
    return first, last


HALO = 16


def _halo_rows(prev_ref, next_ref, first, last):
    prev_row = jnp.where(first, 0.0, prev_ref[...].astype(f32)[HALO - 1:HALO, :])
    next_row = jnp.where(last, 0.0, next_ref[...].astype(f32)[0:1, :])
    return prev_row, next_row


def _halo_specs(col_block, tc, n_rows):
    per = ROW_TILE // HALO
    n_blocks = n_rows // HALO
    prev = pl.BlockSpec((HALO, tc), lambda i, c: (jnp.maximum(i * per - 1, 0), col_block(c)))
    nxt = pl.BlockSpec((HALO, tc), lambda i, c: (jnp.minimum((i + 1) * per, n_blocks - 1), col_block(c)))
    return prev, nxt


def _hyena_pre_kernel(tiles_per_batch, x0_ref, x0p_ref, x0n_ref, x1_ref, x1p_ref, x1n_ref,
                      v_ref, vp_ref, vn_ref, w0_ref, w1_ref, w2_ref, b0_ref, b1_ref, b2_ref,
                      x0_out, g_out):
    first, last = _seq_edges(pl.program_id(0), tiles_per_batch)

    def conv(x_ref, p_ref, n_ref, w_ref, b_ref):
        prev_row, next_row = _halo_rows(p_ref, n_ref, first, last)
        return _conv3(x_ref[...].astype(f32), prev_row, next_row, w_ref[...], b_ref[...])

    x0 = conv(x0_ref, x0p_ref, x0n_ref, w0_ref, b0_ref)
    x1 = conv(x1_ref, x1p_ref, x1n_ref, w1_ref, b1_ref)
    v = conv(v_ref, vp_ref, vn_ref, w2_ref, b2_ref)
    g = v * x1
    x0_out[...] = x0.astype(x0_out.dtype)
    g_out[...] = g


def hyena_pre(p, conv_w, conv_b, tiles_per_batch, *, tc=1024):
    R = p.shape[0]
    W = HY_WIDTH
    nb = W // tc
    specs = []
    for part in range(3):
        col = functools.partial(lambda c, part: part * nb + c, part=part)
        main = pl.BlockSpec((ROW_TILE, tc), functools.partial(lambda i, c, col: (i, col(c)), col=col))
        prev, nxt = _halo_specs(col, tc, R)
        specs += [main, prev, nxt]
    wspecs = [pl.BlockSpec((3, tc), functools.partial(lambda i, c, part: (0, part * nb + c), part=part))
              for part in range(3)]
    bspecs = [pl.BlockSpec((1, tc), functools.partial(lambda i, c, part: (0, part * nb + c), part=part))
              for part in range(3)]
    out_spec = pl.BlockSpec((ROW_TILE, tc), lambda i, c: (i, c))
    args = [p] * 9 + [conv_w] * 3 + [conv_b.reshape(1, -1)] * 3
    return pl.pallas_call(
        functools.partial(_hyena_pre_kernel, tiles_per_batch),
        grid=(R // ROW_TILE, nb),
        in_specs=specs + wspecs + bspecs,
        out_specs=[out_spec, out_spec],
        out_shape=[jax.ShapeDtypeStruct((R, W), bf16), jax.ShapeDtypeStruct((R, W), f32)],
        compiler_params=_params("arbitrary", "arbitrary"),
        name="hyena_pre",
    )(*args)


def _filter_kernel(n, rows, bands_ref, w1t_ref, w1c_ref, w1s_ref, b1_ref, fr_ref, w2_ref, b2_ref,
                   w3_ref, dl_ref, o_ref, nrm_ref):
    step = pl.program_id(0)
    j = step * rows + lax.broadcasted_iota(jnp.int32, (rows, 1), 0)
    d = jnp.where(j < n, j, 2 * n - j)
    valid = j != n
    df = d.astype(f32)
    t = df / float(n - 1)
    ang = (2.0 * math.pi / n) * df * bands_ref[...]
    fr = fr_ref[...]
    z1 = (t * w1t_ref[...]
          + jnp.dot(jnp.cos(ang), w1c_ref[...], preferred_element_type=f32, precision=HIGHEST)
          - jnp.dot(jnp.sin(ang), w1s_ref[...], preferred_element_type=f32, precision=HIGHEST)
          + b1_ref[...])
    hdn = jnp.sin(fr * z1)
    hdn = jnp.sin(fr * (jnp.dot(hdn, w2_ref[...], preferred_element_type=f32, precision=HIGHEST) + b2_ref[...]))
    h = jnp.dot(hdn.astype(bf16), w3_ref[0].astype(bf16), preferred_element_type=f32)
    h = h * jnp.exp(-t * jnp.abs(dl_ref[...]))
    h = jnp.where(valid, h, 0.0)
    o_ref[...] = h

    @pl.when(step == 0)
    def _():
        nrm_ref[...] = jnp.zeros_like(nrm_ref)

    nrm_ref[...] += jnp.sum(jnp.abs(h), axis=0, keepdims=True)


def hyena_filter(n, w1, b1, freq, w2, b2, w3, *, rows=256):
    W = HY_WIDTH
    Hd = HY_HIDDEN
    bands = jnp.linspace(1e-4, HY_BANDS - 1, HY_BANDS, dtype=f32).reshape(1, HY_BANDS)
    max_decay = math.log(HY_TARGET) / HY_SHORT_DECAY_PCT
    min_decay = math.log(HY_TARGET) / HY_LONG_DECAY_PCT
    deltas = jnp.linspace(min_decay, max_decay, W, dtype=f32).reshape(1, W)
    steps = 2 * n // rows
    half_steps = n // rows
    full = lambda s: (0, 0)
    w3r = w3.reshape(Hd, 2, W).transpose(1, 0, 2)
    return pl.pallas_call(
        functools.partial(_filter_kernel, n, rows),
        grid=(steps,),
        in_specs=[
            pl.BlockSpec((1, HY_BANDS), full),
            pl.BlockSpec((1, Hd), full),
            pl.BlockSpec((HY_BANDS, Hd), full),
            pl.BlockSpec((HY_BANDS, Hd), full),
            pl.BlockSpec((1, Hd), full),
            pl.BlockSpec((1, Hd), full),
            pl.BlockSpec((Hd, Hd), full),
            pl.BlockSpec((1, Hd), full),
            pl.BlockSpec((1, Hd, W), lambda s: (jnp.where(s * rows < n, 0, 1), 0, 0)),
            pl.BlockSpec((1, W), full),
        ],
        out_specs=[pl.BlockSpec((rows, W), lambda s: (s, 0)), pl.BlockSpec((1, W), full)],
        out_shape=[jax.ShapeDtypeStruct((2 * n, W), f32), jax.ShapeDtypeStruct((1, W), f32)],
        compiler_params=_params("arbitrary"),
        name="hyena_filter",
    )(bands, w1[0:1], w1[1:1 + HY_BANDS], w1[1 + HY_BANDS:], b1.reshape(1, Hd), freq.reshape(1, Hd),
      w2, b2.reshape(1, Hd), w3r, deltas)


@functools.lru_cache(maxsize=None)
def _fft_constants():
    L1, L2 = FFT_L1, FFT_L2
    L = L1 * L2
    k1 = np.arange(L1)
    nh = np.arange(L1 // 2)
    th = 2.0 * np.pi * np.outer(k1, nh) / L1
    m1r = np.concatenate([np.cos(th), -np.sin(th)], axis=0)
    m1i = np.concatenate([np.sin(th), np.cos(th)], axis=0)
    thf = 2.0 * np.pi * np.outer(k1, np.arange(L1)) / L1
    m1f = np.concatenate([np.cos(thf), -np.sin(thf)], axis=0)
    n2 = np.arange(L2)
    k2 = np.arange(L2)
    m = (k1[:, None, None] * n2[None, None, :] + L1 * k2[None, :, None] * n2[None, None, :]) % L
    ph = 2.0 * np.pi * m / L
    gr, gi = np.cos(ph), -np.sin(ph)
    g = np.concatenate([np.concatenate([gr, -gi], axis=2), np.concatenate([gi, gr], axis=2)], axis=1)
    gt = np.transpose(g, (0, 2, 1))
    thi = 2.0 * np.pi * np.outer(nh, k1) / L1
    m3r = np.concatenate([np.cos(thi), np.sin(thi)], axis=0)
    m3i = np.concatenate([-np.sin(thi), np.cos(thi)], axis=0)
    cast = lambda a: np.asarray(a, dtype=np.float32)
    return dict(m1r=cast(m1r), m1i=cast(m1i), m1f=cast(m1f), g=cast(g), gt=cast(gt), m3r=cast(m3r), m3i=cast(m3i))


@functools.lru_cache(maxsize=None)
def _ctx_dft_constants(n):
    L = 2 * n
    k = np.arange(L)
    th = 2.0 * np.pi * np.outer(k, np.arange(n)) / L
    c, s = np.cos(th), np.sin(th)
    mc = np.concatenate([np.concatenate([c, s], axis=1), np.concatenate([-s, c], axis=1)], axis=0)
    thf = 2.0 * np.pi * np.outer(k, np.arange(L)) / L
    mf = np.concatenate([np.cos(thf), -np.sin(thf)], axis=0)
    ct, st = c.T, s.T
    minv = np.concatenate([np.concatenate([ct, -st], axis=1), np.concatenate([st, ct], axis=1)], axis=0)
    cast = lambda a: np.asarray(a, dtype=np.float32)
    return dict(mc=cast(mc), mf=cast(mf), minv=cast(minv))


def _bf16_constants(consts):
    return {k: jnp.asarray(v).astype(bf16) for k, v in consts.items()}


SUB = 8
N2C = 32


def _store_step1(o_ref, j, a):
    for ch in range(a.shape[0] // SUB):
        o_ref[0, ch, j] = a[ch * SUB:(ch + 1) * SUB]


def _fft1_kernel(re_ref, im_ref, mr_ref, mi_ref, o_ref):
    base = pl.program_id(2) * N2C
    for j in range(N2C):
        rows = pl.ds(base + j, FFT_L1 // 2, stride=FFT_L2)
        _store_step1(o_ref, j,
                     jnp.dot(mr_ref[...], re_ref[0, rows, :].astype(bf16), preferred_element_type=f32)
                     + jnp.dot(mi_ref[...], im_ref[0, rows, :].astype(bf16), preferred_element_type=f32))


def _step1_out(P, W, index):
    groups = 2 * FFT_L1 // SUB
    spec = pl.BlockSpec((1, groups, N2C, SUB, LANE), index)
    return spec, jax.ShapeDtypeStruct((P, groups, FFT_L2, SUB, W), f32)


def fft_step1(g3, m1r, m1i, n_lat):
    B, _, W = g3.shape
    half = FFT_L1 // 2
    out_spec, out_shape = _step1_out(B // 2, W, lambda p, c, k: (p, 0, k, 0, c))
    return pl.pallas_call(
        _fft1_kernel,
        grid=(B // 2, W // LANE, FFT_L2 // N2C),
        in_specs=[pl.BlockSpec((1, n_lat, LANE), lambda p, c, k: (2 * p, 0, c)),
                  pl.BlockSpec((1, n_lat, LANE), lambda p, c, k: (2 * p + 1, 0, c)),
                  pl.BlockSpec((2 * FFT_L1, half), lambda p, c, k: (0, 0)),
                  pl.BlockSpec((2 * FFT_L1, half), lambda p, c, k: (0, 0))],
        out_specs=out_spec,
        out_shape=out_shape,
        compiler_params=_params("arbitrary", "arbitrary", "arbitrary"),
        name="fft_step1",
    )(g3, g3, m1r, m1i)


def _fft1_filter_kernel(x_ref, m_ref, o_ref):
    base = pl.program_id(1) * N2C
    for j in range(N2C):
        rows = pl.ds(base + j, FFT_L1, stride=FFT_L2)
        _store_step1(o_ref, j, jnp.dot(m_ref[...], x_ref[rows, :].astype(bf16), preferred_element_type=f32))


def fft_step1_filter(ts, m1f):
    L, W = ts.shape
    out_spec, out_shape = _step1_out(1, W, lambda c, k: (0, 0, k, 0, c))
    return pl.pallas_call(
        _fft1_filter_kernel,
        grid=(W // LANE, FFT_L2 // N2C),
        in_specs=[pl.BlockSpec((L, LANE), lambda c, k: (0, c)),
                  pl.BlockSpec((2 * FFT_L1, FFT_L1), lambda c, k: (0, 0))],
        out_specs=out_spec,
        out_shape=out_shape,
        compiler_params=_params("arbitrary", "arbitrary"),
        name="fft_step1_filter",
    )(ts, m1f)


def _cmul(yr, yi, hr, hi):
    return yr * hr - yi * hi, yr * hi + yi * hr


def _step1_column(a_refs, j):
    col = lambda ref: ref[0, pl.ds(j, FFT_L2, stride=SUB), :]
    re = jnp.concatenate([col(a_refs[0]), col(a_refs[1])], axis=1)
    im = jnp.concatenate([col(a_refs[2]), col(a_refs[3])], axis=1)
    return jnp.concatenate([re, im], axis=0).astype(bf16)


def _fft2_filter_kernel(a0_ref, a1_ref, a2_ref, a3_ref, g_ref, nrm_ref, o_ref):
    scale = 1.0 / (nrm_ref[...] * float(FFT_L1 * FFT_L2))
    for j in range(SUB):
        a = _step1_column((a0_ref, a1_ref, a2_ref, a3_ref), j)
        o_ref[j] = jnp.dot(g_ref[j], a, preferred_element_type=f32) * scale


def _step1_specs(index):
    def spec(part, lane_half):
        return pl.BlockSpec((1, FFT_L2 * SUB, LANE),
                            lambda *g: (index(*g)[0], part * (FFT_L1 // SUB) + index(*g)[1],
                                        2 * index(*g)[2] + lane_half))
    return [spec(0, 0), spec(0, 1), spec(1, 0), spec(1, 1)]


def fft_step2_filter(af, g, nrm):
    W = af.shape[-1]
    ct = 2 * LANE
    L1, R2 = FFT_L1, 2 * FFT_L2
    af = af.reshape(1, -1, W)
    return pl.pallas_call(
        _fft2_filter_kernel,
        grid=(L1 // SUB, W // ct),
        in_specs=_step1_specs(lambda k, c: (0, k, c)) + [
            pl.BlockSpec((SUB, R2, R2), lambda k, c: (k, 0, 0)),
            pl.BlockSpec((1, ct), lambda k, c: (0, c))],
        out_specs=pl.BlockSpec((SUB, R2, ct), lambda k, c: (k, 0, c)),
        out_shape=jax.ShapeDtypeStruct((L1, R2, W), f32),
        compiler_params=_params("arbitrary", "arbitrary"),
        name="fft_step2_filter",
    )(af, af, af, af, g, nrm)


def _fft2_kernel(a0_ref, a1_ref, a2_ref, a3_ref, g_ref, gt_ref, h_ref, ore_ref, oim_ref):
    half = FFT_L2
    for j in range(SUB):
        a = _step1_column((a0_ref, a1_ref, a2_ref, a3_ref), j)
        y = jnp.dot(g_ref[j], a, preferred_element_type=f32)
        pr, pi = _cmul(y[:half], y[half:], h_ref[j, :half], h_ref[j, half:])
        pcat = jnp.concatenate([pr, pi], axis=0).astype(bf16)
        b = jnp.dot(gt_ref[j], pcat, preferred_element_type=f32)
        for ch in range(FFT_L2 // N2C):
            ore_ref[0, ch, j] = b[ch * N2C:(ch + 1) * N2C]
            oim_ref[0, ch, j] = b[half + ch * N2C:half + (ch + 1) * N2C]


def fft_step2(a, g, gt, hf):
    P, W = a.shape[0], a.shape[-1]
    ct = 2 * LANE
    L1, R2 = FFT_L1, 2 * FFT_L2
    a = a.reshape(P, -1, W)
    out = pl.BlockSpec((1, FFT_L2 // N2C, SUB, N2C, ct), lambda k, c, p: (p, 0, k, 0, c))
    shape = jax.ShapeDtypeStruct((P, FFT_L2 // N2C, L1, N2C, W), f32)
    return pl.pallas_call(
        _fft2_kernel,
        grid=(L1 // SUB, W // ct, P),
        in_specs=_step1_specs(lambda k, c, p: (p, k, c)) + [
            pl.BlockSpec((SUB, R2, R2), lambda k, c, p: (k, 0, 0)),
            pl.BlockSpec((SUB, R2, R2), lambda k, c, p: (k, 0, 0)),
            pl.BlockSpec((SUB, R2, ct), lambda k, c, p: (k, 0, c))],
        out_specs=[out, out],
        out_shape=[shape, shape],
        compiler_params=_params("arbitrary", "arbitrary", "arbitrary"),
        name="fft_step2",
    )(a, a, a, a, g, gt, hf)


def _fft3_kernel(bre_ref, bim_ref, mr_ref, mi_ref, o_ref):
    L1 = FFT_L1
    half = L1 // 2
    base = pl.program_id(2) * N2C
    for j in range(N2C):
        br = bre_ref[0, pl.ds(j, L1, stride=N2C), :].astype(bf16)
        bi = bim_ref[0, pl.ds(j, L1, stride=N2C), :].astype(bf16)
        z = (jnp.dot(mr_ref[...], br, preferred_element_type=f32)
             + jnp.dot(mi_ref[...], bi, preferred_element_type=f32))
        rows = pl.ds(base + j, half, stride=FFT_L2)
        o_ref[0, 0, rows, :] = z[:half]
        o_ref[0, 1, rows, :] = z[half:]


def fft_step3(bre, bim, m3r, m3i, rows_total, n_lat):
    P, W = bre.shape[0], bre.shape[-1]
    L1, L2 = FFT_L1, FFT_L2
    bre = bre.reshape(P, -1, W)
    bim = bim.reshape(P, -1, W)
    spec = pl.BlockSpec((1, L1 * N2C, LANE), lambda p, c, k: (p, k, c))
    return pl.pallas_call(
        _fft3_kernel,
        grid=(P, W // LANE, L2 // N2C),
        in_specs=[spec, spec,
                  pl.BlockSpec((L1, L1), lambda p, c, k: (0, 0)),
                  pl.BlockSpec((L1, L1), lambda p, c, k: (0, 0))],
        out_specs=pl.BlockSpec((1, 2, n_lat, LANE), lambda p, c, k: (p, 0, 0, c)),
        out_shape=jax.ShapeDtypeStruct((P, 2, rows_total, W), f32),
        compiler_params=_params("arbitrary", "arbitrary", "arbitrary"),
        name="fft_step3",
    )(bre, bim, m3r, m3i)


def _ctx_filter_kernel(n, ts_ref, m_ref, nrm_ref, o_ref):
    scale = 1.0 / (nrm_ref[...] * float(2 * n))
    o_ref[...] = jnp.dot(m_ref[...], ts_ref[...].astype(bf16), preferred_element_type=f32) * scale


def ctx_filter_spectrum(ts, mf, nrm, *, ct=256):
    L, W = ts.shape
    return pl.pallas_call(
        functools.partial(_ctx_filter_kernel, L // 2),
        grid=(W // ct,),
        in_specs=[pl.BlockSpec((L, ct), lambda c: (0, c)),
                  pl.BlockSpec((2 * L, L), lambda c: (0, 0)),
                  pl.BlockSpec((1, ct), lambda c: (0, c))],
        out_specs=pl.BlockSpec((2 * L, ct), lambda c: (0, c)),
        out_shape=jax.ShapeDtypeStruct((2 * L, W), f32),
        compiler_params=_params("arbitrary"),
        name="ctx_filter_spectrum",
    )(ts, mf, nrm)


def _ctx_conv_kernel(n, re_ref, im_ref, mc_ref, minv_ref, h_ref, y_in_ref, o_ref):
    del y_in_ref
    L = 2 * n
    z = jnp.concatenate([re_ref[0], im_ref[0]], axis=0).astype(bf16)
    y = jnp.dot(mc_ref[...], z, preferred_element_type=f32)
    pr, pi = _cmul(y[:L], y[L:], h_ref[:L], h_ref[L:])
    pcat = jnp.concatenate([pr, pi], axis=0).astype(bf16)
    out = jnp.dot(minv_ref[...], pcat, preferred_element_type=f32)
    o_ref[0, 0] = out[:n]
    o_ref[0, 1] = out[n:]


def ctx_long_conv(gz, y4, hfc, consts, n_lat, n_ctx, *, ct=256):
    B, T, W = gz.shape
    blk = n_lat // n_ctx
    L = 2 * n_ctx
    out = pl.pallas_call(
        functools.partial(_ctx_conv_kernel, n_ctx),
        grid=(B // 2, W // ct),
        in_specs=[pl.BlockSpec((1, n_ctx, ct), lambda p, c: (2 * p, blk, c)),
                  pl.BlockSpec((1, n_ctx, ct), lambda p, c: (2 * p + 1, blk, c)),
                  pl.BlockSpec((2 * L, L), lambda p, c: (0, 0)),
                  pl.BlockSpec((L, 2 * L), lambda p, c: (0, 0)),
                  pl.BlockSpec((2 * L, ct), lambda p, c: (0, c)),
                  pl.BlockSpec(memory_space=pl.ANY)],
        out_specs=pl.BlockSpec((1, 2, n_ctx, ct), lambda p, c: (p, 0, blk, c)),
        out_shape=jax.ShapeDtypeStruct(y4.shape, f32),
        input_output_aliases={5: 0},
        compiler_params=_params("arbitrary", "arbitrary"),
        name="ctx_long_conv",
    )(gz, gz, consts["mc"], consts["minv"], hfc, y4)
    return out.reshape(B, T, W)


def hyena_long_conv(gz, n_lat, n_ctx, f_w1, f_b1, f_freq, f_w2, f_b2, f_w3):
    B, T, W = gz.shape
    cst = _bf16_constants(_fft_constants())
    L1, L2 = FFT_L1, FFT_L2
    assert 2 * n_lat == L1 * L2 and T % L2 == 0 and B % 2 == 0
    ts, nrm = hyena_filter(n_lat, f_w1, f_b1, f_freq, f_w2, f_b2, f_w3)
    af = fft_step1_filter(ts, cst["m1f"])
    hf = fft_step2_filter(af, cst["g"], nrm)
    a = fft_step1(gz, cst["m1r"], cst["m1i"], n_lat)
    bre, bim = fft_step2(a, cst["g"], cst["gt"], hf)
    y = fft_step3(bre, bim, cst["m3r"], cst["m3i"], T, n_lat)
    ccst = _bf16_constants(_ctx_dft_constants(n_ctx))
    ts_c, nrm_c = hyena_filter(n_ctx, f_w1, f_b1, f_freq, f_w2, f_b2, f_w3)
    hfc = ctx_filter_spectrum(ts_c, ccst["mf"], nrm_c)
    return ctx_long_conv(gz, y, hfc, ccst, n_lat, n_ctx)


@functools.lru_cache(maxsize=None)
def _rope_tables(n_lat, n_ctx):
    half = HEAD_DIM // 2
    nf = half // 2
    inv = ROPE_THETA ** (-np.arange(nf, dtype=np.float64) / nf)
    t = np.arange(n_lat)
    pos = np.stack([t // GRID_W, t % GRID_W], axis=1).astype(np.float64)
    ang = pos[:, :, None] * inv[None, None, :]
    cos = np.concatenate([np.cos(ang), np.cos(ang)], axis=2).reshape(n_lat, HEAD_DIM)
    sin = np.concatenate([-np.sin(ang), np.sin(ang)], axis=2).reshape(n_lat, HEAD_DIM)
    cos = np.concatenate([cos, np.ones((n_ctx, HEAD_DIM))], axis=0)
    sin = np.concatenate([sin, np.zeros((n_ctx, HEAD_DIM))], axis=0)
    return np.asarray(cos, np.float32), np.asarray(sin, np.float32)


def _norm_rope_heads(xs, w, cos, sin, lo_lane):
    nf = HEAD_DIM // 4
    ms = [jnp.mean(x * x, axis=-1, keepdims=True) for x in xs]
    ys = [x * lax.rsqrt(m + NORM_EPS) * w for x, m in zip(xs, ms)]
    ps = [jnp.where(lo_lane, pltpu.roll(y, HEAD_DIM - nf, 1), pltpu.roll(y, nf, 1)) for y in ys]
    return [y * cos + p * sin for y, p in zip(ys, ps)]


def _qkv_prep_kernel(q_ref, k_ref, v_ref, cos_ref, sin_ref, qn_ref, kn_ref, q_out, kt_out, v_out):
    cos = cos_ref[...]
    sin = sin_ref[...]
    lane = lax.broadcasted_iota(jnp.int32, cos.shape, 1)
    lo_lane = (lane % (HEAD_DIM // 2)) < (HEAD_DIM // 4)
    head = lambda ref, h: ref[:, h * HEAD_DIM:(h + 1) * HEAD_DIM].astype(f32)
    scale = HEAD_DIM ** -0.5 * math.log2(math.e)
    qs = _norm_rope_heads([head(q_ref, h) for h in range(ATT_HEADS)], qn_ref[...], cos, sin, lo_lane)
    for h, q in enumerate(qs):
        q_out[:, h * HEAD_DIM:(h + 1) * HEAD_DIM] = (q * scale).astype(q_out.dtype)
    ks = _norm_rope_heads([head(k_ref, h) for h in range(ATT_KV_HEADS)], kn_ref[...], cos, sin, lo_lane)
    ones_col = (lane == 0).astype(v_out.dtype)
    for h, k in enumerate(ks):
        sl = slice(h * HEAD_DIM, (h + 1) * HEAD_DIM)
        kt_out[0, sl, :] = k.T.astype(kt_out.dtype)
        v_out[:, 2 * h * HEAD_DIM:(2 * h + 1) * HEAD_DIM] = v_ref[:, sl]
        v_out[:, (2 * h + 1) * HEAD_DIM:(2 * h + 2) * HEAD_DIM] = ones_col


def qkv_prep(p, q_norm, k_norm, B, n_lat, n_ctx):
    R = p.shape[0]
    T = n_lat + n_ctx
    tpb = T // ROW_TILE
    cos, sin = _rope_tables(n_lat, n_ctx)
    return pl.pallas_call(
        _qkv_prep_kernel,
        grid=(R // ROW_TILE,),
        in_specs=[pl.BlockSpec((ROW_TILE, ATT_WIDTH), lambda i: (i, EVEN_OFF_Q // ATT_WIDTH)),
                  pl.BlockSpec((ROW_TILE, ATT_KV_WIDTH), lambda i: (i, EVEN_OFF_K // ATT_KV_WIDTH)),
                  pl.BlockSpec((ROW_TILE, ATT_KV_WIDTH), lambda i: (i, EVEN_OFF_V // ATT_KV_WIDTH)),
                  pl.BlockSpec((ROW_TILE, HEAD_DIM), lambda i: (i % tpb, 0)),
                  pl.BlockSpec((ROW_TILE, HEAD_DIM), lambda i: (i % tpb, 0)),
                  pl.BlockSpec((1, HEAD_DIM), lambda i: (0, 0)),
                  pl.BlockSpec((1, HEAD_DIM), lambda i: (0, 0))],
        out_specs=[pl.BlockSpec((ROW_TILE, ATT_WIDTH), lambda i: (i, 0)),
                   pl.BlockSpec((1, ATT_KV_WIDTH, ROW_TILE), lambda i: (i // tpb, 0, i % tpb)),
                   pl.BlockSpec((ROW_TILE, 2 * ATT_KV_WIDTH), lambda i: (i, 0))],
        out_shape=[jax.ShapeDtypeStruct((R, ATT_WIDTH), bf16),
                   jax.ShapeDtypeStruct((B, ATT_KV_WIDTH, T), bf16),
                   jax.ShapeDtypeStruct((R, 2 * ATT_KV_WIDTH), bf16)],
        compiler_params=_params("arbitrary"),
        name="qkv_prep",
    )(p, p, p, jnp.asarray(cos), jnp.asarray(sin), q_norm.reshape(1, HEAD_DIM), k_norm.reshape(1, HEAD_DIM))


ATT_CHUNK = 256


def _attention_kernel(n_lat, q_ref, kt_ref, v_ref, o_ref, qs_ref, sa_ref, sb_ref, pa_ref, pb_ref, os_ref):
    tq = q_ref.shape[0]
    n_chunks = ATT_GROUP * tq // ATT_CHUNK
    for h in range(ATT_GROUP):
        qs_ref[h * tq:(h + 1) * tq, :] = q_ref[:, h * HEAD_DIM:(h + 1) * HEAD_DIM]

    def rows(c):
        return pl.ds(pl.multiple_of(c * ATT_CHUNK, ATT_CHUNK), ATT_CHUNK)

    def finish(o):
        for h in range(ATT_GROUP):
            o_ref[:, h * HEAD_DIM:(h + 1) * HEAD_DIM] = os_ref[h * tq:(h + 1) * tq, :].astype(o_ref.dtype)

    s_refs = (sa_ref, sb_ref)
    p_refs = (pa_ref, pb_ref)

    def run(k_lo):
        def scores(c, slot):
            s_refs[slot][:, k_lo:] = jnp.dot(qs_ref[rows(c), :], kt_ref[0, :, k_lo:], preferred_element_type=f32)

        def exponentials(slot):
            s = s_refs[slot][:, k_lo:]
            m = jnp.max(s, axis=-1, keepdims=True)
            p_refs[slot][:, k_lo:] = jnp.exp2(s - m).astype(bf16)

        def weighted_values(c, slot):
            r = jnp.dot(p_refs[slot][:, k_lo:], v_ref[0, k_lo:, :], preferred_element_type=f32)
            os_ref[rows(c), :] = r[:, :HEAD_DIM] / r[:, HEAD_DIM:HEAD_DIM + 1]

        def stage(c, slot):
            scores(c + 1, 1 - slot)
            exponentials(slot)
            weighted_values(c - 1, 1 - slot)

        scores(0, 0)
        exponentials(0)
        scores(1, 1)

        def body(i, carry):
            stage(2 * i + 1, 1)
            stage(2 * i + 2, 0)
            return carry

        lax.fori_loop(0, (n_chunks - 2) // 2, body, 0)
        exponentials((n_chunks - 1) % 2)
        weighted_values(n_chunks - 2, n_chunks % 2)
        weighted_values(n_chunks - 1, (n_chunks - 1) % 2)
        finish(None)

    @pl.when(pl.program_id(2) < pl.num_programs(2) - 1)
    def _():
        run(0)

    @pl.when(pl.program_id(2) == pl.num_programs(2) - 1)
    def _():
        run(n_lat)


def attention(q, kt, v, B, n_lat, n_ctx):
    R = q.shape[0]
    T = n_lat + n_ctx
    tq = ROW_TILE
    assert n_ctx == tq
    qt = T // tq
    gw = ATT_GROUP * HEAD_DIM
    return pl.pallas_call(
        functools.partial(_attention_kernel, n_lat),
        grid=(B, ATT_KV_HEADS, qt),
        in_specs=[pl.BlockSpec((tq, gw), lambda b, g, i: (b * qt + i, g)),
                  pl.BlockSpec((1, HEAD_DIM, T), lambda b, g, i: (b, g, 0)),
                  pl.BlockSpec((1, T, 2 * HEAD_DIM), lambda b, g, i: (b, 0, g))],
        out_specs=pl.BlockSpec((tq, gw), lambda b, g, i: (b * qt + i, g)),
        out_shape=jax.ShapeDtypeStruct((R, ATT_WIDTH), bf16),
        scratch_shapes=[pltpu.VMEM((ATT_GROUP * tq, HEAD_DIM), bf16),
                        pltpu.VMEM((ATT_CHUNK, T), f32), pltpu.VMEM((ATT_CHUNK, T), f32),
                        pltpu.VMEM((ATT_CHUNK, T), bf16), pltpu.VMEM((ATT_CHUNK, T), bf16),
                        pltpu.VMEM((ATT_GROUP * tq, HEAD_DIM), f32)],
        compiler_params=_params("arbitrary", "arbitrary", "arbitrary"),
        name="attention",
    )(q, kt, v.reshape(B, T, 2 * ATT_KV_WIDTH))


def _post_residual(x, y, g_post, gate):
    yn = y * lax.rsqrt(jnp.mean(y * y, axis=-1, keepdims=True) + NORM_EPS) * g_post
    return x + gate * yn


def _silu(x):
    return x * jax.nn.sigmoid(x)


def _even_out_kernel(tiles_per_batch, x0_ref, g_ref, yc_ref, ghy_ref, att_ref, ga0_ref, ga1_ref, bias_ref, w_ref,
                     gp_ref, m_ref, x_ref, c_ref, gn_ref, mn_ref, o_ref, hn_ref):
    D = x_ref.shape[-1]
    g = g_ref[...]
    hy = x0_ref[...].astype(f32) * (yc_ref[...] + g * bias_ref[...]) * _silu(ghy_ref[...].astype(f32))
    g_att = jnp.concatenate([ga0_ref[...], ga1_ref[...]], axis=1).astype(f32)
    at = att_ref[...].astype(f32) * _silu(g_att)
    lhs = jnp.concatenate([hy, at], axis=1).astype(bf16)
    y = jnp.dot(lhs, w_ref[...], preferred_element_type=f32)
    x_new = _post_residual(_input_rows(tiles_per_batch, x_ref, c_ref), y, gp_ref[...], m_ref[0, :, 2 * D:3 * D])
    o_ref[...] = x_new
    hn_ref[...] = _norm_mod_rows(x_new, gn_ref[...], mn_ref[0]).astype(hn_ref.dtype)


def even_out(x0, g, yconv, p, att, hy_bias, w_out, g_post, mods, x_lat, x_ctx, next_g_pre, next_mods,
             tiles_per_batch, ctx_row):
    R = x0.shape[0]
    D = x_lat.shape[1]
    W = HY_WIDTH
    hw = ATT_WIDTH // 2
    row = lambda i: (i, 0)
    full = lambda i: (0, 0)
    lat_spec, ctx_spec = _input_row_specs(tiles_per_batch, D)
    return pl.pallas_call(
        functools.partial(_even_out_kernel, tiles_per_batch),
        grid=(R // ROW_TILE,),
        in_specs=[pl.BlockSpec((ROW_TILE, W), row),
                  pl.BlockSpec((ROW_TILE, W), row),
                  pl.BlockSpec((ROW_TILE, W), row),
                  pl.BlockSpec((ROW_TILE, W), lambda i: (i, EVEN_OFF_GHY // W)),
                  pl.BlockSpec((ROW_TILE, ATT_WIDTH), row),
                  pl.BlockSpec((ROW_TILE, hw), lambda i: (i, EVEN_OFF_GATT // hw)),
                  pl.BlockSpec((ROW_TILE, hw), lambda i: (i, EVEN_OFF_GATT // hw + 1)),
                  pl.BlockSpec((1, W), full),
                  pl.BlockSpec((W + ATT_WIDTH, D), full),
                  pl.BlockSpec((1, D), full),
                  pl.BlockSpec((1, 1, 3 * D), lambda i: (_mod_row(i, tiles_per_batch, ctx_row), 0, 0)),
                  lat_spec, ctx_spec,
                  pl.BlockSpec((1, D), full),
                  pl.BlockSpec((1, 1, 3 * D), lambda i: (_mod_row(i, tiles_per_batch, ctx_row), 0, 0))],
        out_specs=[pl.BlockSpec((ROW_TILE, D), row), pl.BlockSpec((ROW_TILE, D), row)],
        out_shape=[jax.ShapeDtypeStruct((R, D), f32), jax.ShapeDtypeStruct((R, D), bf16)],
        compiler_params=_params("arbitrary"),
        name="even_out",
    )(x0, g, yconv, p, att, p, p, hy_bias.reshape(1, W), w_out.astype(bf16), g_post.reshape(1, D), mods,
      x_lat, x_ctx, next_g_pre.reshape(1, D), next_mods)


def even_layer(x_lat, x_ctx, mods, next_g_pre, next_mods, B, n_lat, n_ctx, g_pre, g_post, w_in, conv_w, conv_b,
               f_w1, f_b1, f_freq, f_w2, f_b2, f_w3, hy_bias, q_norm, k_norm, w_out):
    T = n_lat + n_ctx
    tpb = T // ROW_TILE
    h = norm_mod(x_lat, x_ctx, g_pre, mods, tpb, B)
    p = matmul(h, w_in.astype(bf16), tm=_proj_row_tile(B * T), tn=_proj_col_tile(w_in.shape[1]), out_dtype=bf16,
               name="even_in_proj")
    x0, g = hyena_pre(p, conv_w, conv_b, tpb)
    yconv = hyena_long_conv(g.reshape(B, T, HY_WIDTH), n_lat, n_ctx, f_w1, f_b1, f_freq, f_w2, f_b2, f_w3)
    q, kt, v = qkv_prep(p, q_norm, k_norm, B, n_lat, n_ctx)
    att = attention(q, kt, v, B, n_lat, n_ctx)
    return even_out(x0, g, yconv.reshape(B * T, HY_WIDTH), p, att, hy_bias, w_out, g_post, mods, x_lat, x_ctx,
                    next_g_pre, next_mods, tpb, B)


def _mlstm_prep_kernel(tiles_per_batch, q_ref, qp_ref, qn_ref, k_ref, kp_ref, kn_ref,
                       wq_ref, wk_ref, bq_ref, bk_ref, q_out, kt_out):
    first, last = _seq_edges(pl.program_id(0), tiles_per_batch)
    prev_row, next_row = _halo_rows(qp_ref, qn_ref, first, last)
    q = _silu(_conv3(q_ref[...].astype(f32), prev_row, next_row, wq_ref[...], bq_ref[...]))
    q_out[...] = q.astype(q_out.dtype)
    prev_row, next_row = _halo_rows(kp_ref, kn_ref, first, last)
    k = _silu(_conv3(k_ref[...].astype(f32), prev_row, next_row, wk_ref[...], bk_ref[...])) * (ML_QK ** -0.5)
    for h in range(k.shape[1] // ML_QK):
        sl = slice(h * ML_QK, (h + 1) * ML_QK)
        kt_out[0, sl, :] = k[:, sl].T.astype(kt_out.dtype)


def mlstm_prep(p, conv_w, conv_b, B, T, *, tc=1024):
    R = p.shape[0]
    tpb = T // ROW_TILE
    nb = ML_QK_WIDTH // tc
    qcol = lambda c: c
    kcol = lambda c: nb + c
    qprev, qnext = _halo_specs(qcol, tc, R)
    kprev, knext = _halo_specs(kcol, tc, R)
    return pl.pallas_call(
        functools.partial(_mlstm_prep_kernel, tpb),
        grid=(R // ROW_TILE, nb),
        in_specs=[pl.BlockSpec((ROW_TILE, tc), lambda i, c: (i, c)), qprev, qnext,
                  pl.BlockSpec((ROW_TILE, tc), lambda i, c: (i, nb + c)), kprev, knext,
                  pl.BlockSpec((3, tc), lambda i, c: (0, c)),
                  pl.BlockSpec((3, tc), lambda i, c: (0, nb + c)),
                  pl.BlockSpec((1, tc), lambda i, c: (0, c)),
                  pl.BlockSpec((1, tc), lambda i, c: (0, nb + c))],
        out_specs=[pl.BlockSpec((ROW_TILE, tc), lambda i, c: (i, c)),
                   pl.BlockSpec((1, tc, ROW_TILE), lambda i, c: (i // tpb, c, i % tpb))],
        out_shape=[jax.ShapeDtypeStruct((R, ML_QK_WIDTH), bf16),
                   jax.ShapeDtypeStruct((B, ML_QK_WIDTH, T), bf16)],
        compiler_params=_params("arbitrary", "arbitrary"),
        name="mlstm_prep",
    )(p, p, p, p, p, p, conv_w, conv_w, conv_b.reshape(1, -1), conv_b.reshape(1, -1))


def _log_sigmoid(x):
    return jnp.minimum(x, 0.0) - jnp.log(1.0 + jnp.exp(-jnp.abs(x)))


def _mlstm_gates_kernel(g_ref, b_ref, gc_out, gr_out):
    pre = g_ref[...] + b_ref[...]
    lane = lax.broadcasted_iota(jnp.int32, pre.shape, 1)
    is_forget = (lane // ML_HEADS) % 2 == 1
    gc = jnp.where(is_forget, _log_sigmoid(pre), pre)
    gc_out[...] = gc
    gr_out[0] = gc.T


def mlstm_gates(gates, gate_b, B, T):
    R = gates.shape[0]
    tpb = T // ROW_TILE
    gb = jnp.pad(gate_b, (0, LANE - gate_b.shape[0])).reshape(1, LANE)
    return pl.pallas_call(
        _mlstm_gates_kernel,
        grid=(R // ROW_TILE,),
        in_specs=[pl.BlockSpec((ROW_TILE, LANE), lambda i: (i, 0)),
                  pl.BlockSpec((1, LANE), lambda i: (0, 0))],
        out_specs=[pl.BlockSpec((ROW_TILE, LANE), lambda i: (i, 0)),
                   pl.BlockSpec((1, LANE, ROW_TILE), lambda i: (i // tpb, 0, i % tpb))],
        out_shape=[jax.ShapeDtypeStruct((R, LANE), f32), jax.ShapeDtypeStruct((B, LANE, T), f32)],
        compiler_params=_params("arbitrary"),
        name="mlstm_gates",
    )(gates, gb)


def _mlstm_chunk_setup(reverse, gc_ref, gr_ref, m_ref):
    Lc = gc_ref.shape[0]
    H = ML_HEADS
    i_off = 2 * H if reverse else 0
    f_off = i_off + H
    t_idx = lax.broadcasted_iota(jnp.int32, (Lc, Lc), 0)
    s_idx = lax.broadcasted_iota(jnp.int32, (Lc, Lc), 1)
    causal = (s_idx >= t_idx) if reverse else (s_idx <= t_idx)
    tri = causal.astype(f32)
    gc = gc_ref[...]
    gr = gr_ref[0]
    b_col_all = jnp.dot(tri, gc[:, f_off:f_off + H], preferred_element_type=f32, precision=HIGHEST)
    b_row_all = lax.dot_general(gr[f_off:f_off + H, :], tri, (((1,), (1,)), ((), ())),
                                preferred_element_type=f32, precision=HIGHEST)
    i_col_all = gc[:, i_off:i_off + H]
    end = 0 if reverse else Lc - 1
    b_end = b_col_all[end:end + 1, :]
    m_prev = m_ref[0:1, 0:H]
    g_col = b_end - b_col_all + i_col_all
    m_new = jnp.maximum(b_end + m_prev, jnp.max(g_col, axis=0, keepdims=True))
    a_col = jnp.exp(g_col - m_new)
    a_prev = jnp.exp(b_end + m_prev - m_new)
    m_ref[:, 0:H] = jnp.broadcast_to(m_new, (m_ref.shape[0], H))
    return dict(causal=causal, i_row=gr[i_off:i_off + H, :], b_col=b_col_all, b_row=b_row_all,
                m_prev=m_prev, a_col=a_col, a_prev=a_prev)


class _MlstmChain:
    def __init__(self, h, cs, q_ref, kt_ref, v_ref, o_ref, ct_ref):
        self.h, self.cs = h, cs
        self.q_ref, self.kt_ref, self.v_ref, self.o_ref, self.ct_ref = q_ref, kt_ref, v_ref, o_ref, ct_ref

    def _q(self):
        return self.q_ref[:, self.h * ML_QK:(self.h + 1) * ML_QK]

    def _kt(self):
        return self.kt_ref[0, self.h * ML_QK:(self.h + 1) * ML_QK, :]

    def _v(self):
        Lc = self.q_ref.shape[0]
        ones_col = (lax.broadcasted_iota(jnp.int32, (Lc, LANE), 1) == 0).astype(bf16)
        return jnp.concatenate([self.v_ref[:, self.h * ML_V:(self.h + 1) * ML_V], ones_col], axis=1)

    def scores(self):
        self.qk = jnp.dot(self._q(), self._kt(), preferred_element_type=f32)

    def gates(self):
        h, cs = self.h, self.cs
        i_row = cs["i_row"][h:h + 1, :]
        b_col = cs["b_col"][:, h:h + 1]
        b_row = cs["b_row"][h:h + 1, :]
        m_prev = cs["m_prev"][:, h:h + 1]
        d = jnp.where(cs["causal"], b_col + (i_row - b_row), -jnp.inf)
        inter = b_col + m_prev
        self.m_row = jnp.maximum(inter, jnp.max(d, axis=-1, keepdims=True))
        self.s = (self.qk * jnp.exp(d - self.m_row)).astype(bf16)
        self.w_prev = jnp.exp(inter - self.m_row)

    def values(self):
        h = self.h
        self.ct = self.ct_ref[h]
        tot = (jnp.dot(self.s, self._v(), preferred_element_type=f32)
               + self.w_prev * jnp.dot(self._q(), self.ct.astype(bf16), preferred_element_type=f32))
        scale = 1.0 / jnp.maximum(jnp.abs(tot[:, ML_V:ML_V + 1]), jnp.exp(-self.m_row))
        self.o_ref[:, h * ML_V:(h + 1) * ML_V] = (tot[:, :ML_V] * scale).astype(self.o_ref.dtype)

    def update(self):
        h = self.h
        va = (self._v().astype(f32) * self.cs["a_col"][:, h:h + 1]).astype(bf16)
        self.ct_ref[h] = (self.cs["a_prev"][:, h:h + 1] * self.ct
                          + jnp.dot(self._kt(), va, preferred_element_type=f32))


def _mlstm_scan_kernel(qf_ref, ktf_ref, vf_ref, gcf_ref, grf_ref, qb_ref, ktb_ref, vb_ref, gcb_ref, grb_ref,
                       of_ref, ob_ref, ctf_ref, mf_ref, ctb_ref, mb_ref):
    @pl.when(pl.program_id(1) == 0)
    def _():
        for ref in (ctf_ref, mf_ref, ctb_ref, mb_ref):
            ref[...] = jnp.zeros_like(ref)

    fwd = _mlstm_chunk_setup(False, gcf_ref, grf_ref, mf_ref)
    bwd = _mlstm_chunk_setup(True, gcb_ref, grb_ref, mb_ref)
    chains = []
    for h in range(ML_HEADS):
        chains.append(_MlstmChain(h, fwd, qf_ref, ktf_ref, vf_ref, of_ref, ctf_ref))
        chains.append(_MlstmChain(h, bwd, qb_ref, ktb_ref, vb_ref, ob_ref, ctb_ref))
    stages = ("scores", "gates", "values", "update")
    for k in range(len(chains) + len(stages) - 1):
        for depth, stage in enumerate(stages):
            if 0 <= k - depth < len(chains):
                getattr(chains[k - depth], stage)()


def mlstm_scan(q, kt, p, gc, gr, B, n_lat, n_ctx):
    R = q.shape[0]
    Lc = ML_CHUNK
    tpb = (n_lat + n_ctx) // Lc
    lat = n_lat // Lc
    ctx = n_ctx // Lc

    def specs(reverse):
        def chunk(j):
            if reverse:
                return tpb - 1 - j
            return jnp.where(j < ctx, lat + j, j - ctx)
        ins = [pl.BlockSpec((Lc, ML_QK_WIDTH), lambda b, j: (b * tpb + chunk(j), 0)),
               pl.BlockSpec((1, ML_QK_WIDTH, Lc), lambda b, j: (b, 0, chunk(j))),
               pl.BlockSpec((Lc, ML_WIDTH), lambda b, j: (b * tpb + chunk(j), ODD_OFF_V // ML_WIDTH)),
               pl.BlockSpec((Lc, LANE), lambda b, j: (b * tpb + chunk(j), 0)),
               pl.BlockSpec((1, LANE, Lc), lambda b, j: (b, 0, chunk(j)))]
        out = pl.BlockSpec((Lc, ML_WIDTH), lambda b, j: (b * tpb + chunk(j), 0))
        return ins, out

    ins_f, out_f = specs(False)
    ins_b, out_b = specs(True)
    state = [pltpu.VMEM((ML_HEADS, ML_QK, ML_V + LANE), f32), pltpu.VMEM((SUB, LANE), f32)]
    return pl.pallas_call(
        _mlstm_scan_kernel,
        grid=(B, tpb),
        in_specs=ins_f + ins_b,
        out_specs=[out_f, out_b],
        out_shape=[jax.ShapeDtypeStruct((R, ML_WIDTH), bf16), jax.ShapeDtypeStruct((R, ML_WIDTH), bf16)],
        scratch_shapes=state + state,
        compiler_params=_params("arbitrary", "arbitrary"),
        name="mlstm_scan",
    )(q, kt, p, gc, gr, q, kt, p, gc, gr)


def _odd_out_kernel(hf_ref, hb_ref, o_ref, z_ref, hn_ref, w_ref, gp_ref, m_ref, x_ref, out_ref):
    D = x_ref.shape[-1]
    hs = (hf_ref[...].astype(f32) + hb_ref[...].astype(f32)) * jax.nn.sigmoid(o_ref[...].astype(f32))
    parts = []
    for h in range(ML_HEADS):
        seg = hs[:, h * ML_V:(h + 1) * ML_V]
        parts.append(seg * lax.rsqrt(jnp.mean(seg * seg, axis=-1, keepdims=True) + NORM_EPS))
    hn = jnp.concatenate(parts, axis=1) * hn_ref[...] * _silu(z_ref[...].astype(f32))
    y = jnp.dot(hn.astype(bf16), w_ref[...], preferred_element_type=f32)
    out_ref[...] = _post_residual(x_ref[...], y, gp_ref[...], m_ref[0, :, 2 * D:3 * D])


def odd_out(hf, hb, p, head_norm, w_out, g_post, mods, x_all, B, n_lat, T):
    D = x_all.shape[1]
    tpb = T // ROW_TILE
    lat = n_lat // ROW_TILE
    row = lambda b, i: (b * tpb + i, 0)
    full = lambda b, i: (0, 0)
    return pl.pallas_call(
        _odd_out_kernel,
        grid=(B, lat),
        in_specs=[pl.BlockSpec((ROW_TILE, ML_WIDTH), row),
                  pl.BlockSpec((ROW_TILE, ML_WIDTH), row),
                  pl.BlockSpec((ROW_TILE, ML_WIDTH), lambda b, i: (b * tpb + i, ODD_OFF_O // ML_WIDTH)),
                  pl.BlockSpec((ROW_TILE, ML_WIDTH), lambda b, i: (b * tpb + i, ODD_OFF_Z // ML_WIDTH)),
                  pl.BlockSpec((1, ML_WIDTH), full),
                  pl.BlockSpec((ML_WIDTH, D), full),
                  pl.BlockSpec((1, D), full),
                  pl.BlockSpec((1, 1, 3 * D), lambda b, i: (b, 0, 0)),
                  pl.BlockSpec((ROW_TILE, D), row)],
        out_specs=pl.BlockSpec((ROW_TILE, D), lambda b, i: (b * lat + i, 0)),
        out_shape=jax.ShapeDtypeStruct((B * n_lat, D), f32),
        compiler_params=_params("arbitrary", "arbitrary"),
        name="odd_out",
    )(hf, hb, p, p, head_norm.reshape(1, ML_WIDTH), w_out.astype(bf16), g_post.reshape(1, D), mods, x_all)


def odd_layer_last(x_all, h, mods, B, n_lat, n_ctx, g_post, w_in, conv_w, conv_b, gate_b, head_norm, w_out):
    T = n_lat + n_ctx
    wb = w_in.astype(bf16)
    tm = _proj_row_tile(B * T)
    p = matmul(h, wb, tm=tm, tn=_proj_col_tile(ODD_MAIN), n_cols=ODD_MAIN, out_dtype=bf16, name="odd_in_proj")
    n_gates = w_in.shape[1] - ODD_MAIN
    gates = matmul(h, jnp.pad(wb[:, ODD_MAIN:], ((0, 0), (0, LANE - n_gates))), tm=tm, tn=LANE, name="odd_gate_proj")
    q, kt = mlstm_prep(p, conv_w, conv_b, B, T)
    gc, gr = mlstm_gates(gates, gate_b, B, T)
    hf, hb = mlstm_scan(q, kt, p, gc, gr, B, n_lat, n_ctx)
    return odd_out(hf, hb, p, head_norm, w_out, g_post, mods, x_all, B, n_lat, T)


def kernel(x, c, ctx, c_ctx, w_mod, b_mod, g_pre, g_post, e_w_in, e_conv_w, e_conv_b, e_filt_w1,
           e_filt_b1, e_filt_freq, e_filt_w2, e_filt_b2, e_filt_w3, e_hy_bias, e_q_norm, e_k_norm,
           e_w_out, o_w_in, o_conv_w, o_conv_b, o_gate_b, o_head_norm, o_w_out):
    B, n_lat, D = x.shape
    n_ctx = ctx.shape[1]
    T = n_lat + n_ctx
    depth = w_mod.shape[0]
    assert depth == 2 and B + 1 <= 8 and n_ctx == ROW_TILE and n_lat % ROW_TILE == 0
    cond = jnp.concatenate([c, c_ctx[None], jnp.zeros((8 - B - 1, D), f32)], axis=0)
    mods_all = adaln_all(cond, w_mod, b_mod)
    mods0 = mods_all[0].reshape(8, 1, 3 * D)
    mods1 = mods_all[1].reshape(8, 1, 3 * D)
    x_all, h1 = even_layer(x.reshape(B * n_lat, D), ctx.reshape(B * n_ctx, D), mods0, g_pre[1], mods1,
                           B, n_lat, n_ctx, g_pre[0], g_post[0], e_w_in[0],
                           e_conv_w[0], e_conv_b[0], e_filt_w1[0], e_filt_b1[0], e_filt_freq[0], e_filt_w2[0],
                           e_filt_b2[0], e_filt_w3[0], e_hy_bias[0], e_q_norm[0], e_k_norm[0], e_w_out[0])
    out = odd_layer_last(x_all, h1, mods1, B, n_lat, n_ctx, g_post[1], o_w_in[0],
                         o_conv_w[0], o_conv_b[0], o_gate_b[0], o_head_norm[0], o_w_out[0])
    return out.reshape(B, n_lat, D)
```

```python
import functools
import math

import numpy as np
import jax
import jax.numpy as jnp
from jax import lax
from jax.experimental import pallas as pl
from jax.experimental.pallas import tpu as pltpu

f32 = jnp.float32
bf16 = jnp.bfloat16
HIGHEST = lax.Precision.HIGHEST

D_MODEL = 1024
GRID_W = 64
NORM_EPS = 1e-6

HY_WIDTH = 1024
HY_EMB = 33
HY_BANDS = (HY_EMB - 1) // 2
HY_HIDDEN = 64
HY_TARGET = 1e-2
HY_SHORT_DECAY_PCT = 0.3
HY_LONG_DECAY_PCT = 1.5

ATT_HEADS = 8
ATT_KV_HEADS = 2
ATT_GROUP = ATT_HEADS // ATT_KV_HEADS
HEAD_DIM = 128
ATT_WIDTH = ATT_HEADS * HEAD_DIM
ATT_KV_WIDTH = ATT_KV_HEADS * HEAD_DIM
ROPE_THETA = 10000.0
EVEN_OFF_XV = 0
EVEN_OFF_GHY = 3 * HY_WIDTH
EVEN_OFF_Q = EVEN_OFF_GHY + HY_WIDTH
EVEN_OFF_K = EVEN_OFF_Q + ATT_WIDTH
EVEN_OFF_V = EVEN_OFF_K + ATT_KV_WIDTH
EVEN_OFF_GATT = EVEN_OFF_V + ATT_KV_WIDTH
EVEN_IN = EVEN_OFF_GATT + ATT_WIDTH

ML_HEADS = 8
ML_QK = 128
ML_V = 256
ML_QK_WIDTH = ML_HEADS * ML_QK
ML_WIDTH = ML_HEADS * ML_V
ODD_OFF_Q = 0
ODD_OFF_K = ML_QK_WIDTH
ODD_OFF_V = 2 * ML_QK_WIDTH
ODD_OFF_O = ODD_OFF_V + ML_WIDTH
ODD_OFF_Z = ODD_OFF_O + ML_WIDTH
ODD_OFF_GATES = ODD_OFF_Z + ML_WIDTH
ODD_MAIN = ODD_OFF_GATES

ROW_TILE = 256
ML_CHUNK = 256
LANE = 128
VMEM_LIMIT_BYTES = 48 * 1024 * 1024

FFT_L1 = 64
FFT_L2 = 128


def _params(*sem):
    return pltpu.CompilerParams(dimension_semantics=sem, vmem_limit_bytes=VMEM_LIMIT_BYTES)


def _adaln_kernel(c_ref, w_ref, b_ref, o_ref):
    c = c_ref[...]
    s = c * jax.nn.sigmoid(c)
    o_ref[0] = jnp.dot(s, w_ref[0], preferred_element_type=f32, precision=HIGHEST) + b_ref[0]


def adaln_all(cond, w_mod, b_mod, *, tn=768):
    depth, D, N = w_mod.shape
    return pl.pallas_call(
        _adaln_kernel,
        grid=(depth, N // tn),
        in_specs=[
            pl.BlockSpec((8, D), lambda l, j: (0, 0)),
            pl.BlockSpec((1, D, tn), lambda l, j: (l, 0, j)),
            pl.BlockSpec((1, 1, tn), lambda l, j: (l, 0, j)),
        ],
        out_specs=pl.BlockSpec((1, 8, tn), lambda l, j: (l, 0, j)),
        out_shape=jax.ShapeDtypeStruct((depth, 8, N), f32),
        compiler_params=_params("arbitrary", "arbitrary"),
        name="adaln",
    )(cond, w_mod, b_mod.reshape(depth, 1, N))


def _mod_row(i, tiles_per_batch, ctx_row):
    lat_tiles = tiles_per_batch - 1
    return jnp.where(i % tiles_per_batch == lat_tiles, ctx_row, i // tiles_per_batch)


def _norm_mod_rows(x, g, m):
    D = x.shape[-1]
    y = x * lax.rsqrt(jnp.mean(x * x, axis=-1, keepdims=True) + NORM_EPS)
    return y * g * (1.0 + m[:, D:2 * D]) + m[:, 0:D]


def _input_row_specs(tiles_per_batch, D):
    lat_tiles = tiles_per_batch - 1

    def lat_index(i):
        return ((i // tiles_per_batch) * lat_tiles + jnp.minimum(i % tiles_per_batch, lat_tiles - 1), 0)

    return (pl.BlockSpec((ROW_TILE, D), lat_index),
            pl.BlockSpec((ROW_TILE, D), lambda i: (i // tiles_per_batch, 0)))


def _input_rows(tiles_per_batch, lat_ref, ctx_ref):
    is_ctx = pl.program_id(0) % tiles_per_batch == tiles_per_batch - 1
    return jnp.where(is_ctx, ctx_ref[...], lat_ref[...])


def _norm_mod_kernel(tiles_per_batch, x_ref, c_ref, g_ref, m_ref, o_ref):
    x = _input_rows(tiles_per_batch, x_ref, c_ref)
    o_ref[...] = _norm_mod_rows(x, g_ref[...], m_ref[0]).astype(o_ref.dtype)


def norm_mod(x_lat, x_ctx, g, mods, tiles_per_batch, ctx_row):
    D = x_lat.shape[1]
    R = x_lat.shape[0] + x_ctx.shape[0]
    lat_spec, ctx_spec = _input_row_specs(tiles_per_batch, D)
    return pl.pallas_call(
        functools.partial(_norm_mod_kernel, tiles_per_batch),
        grid=(R // ROW_TILE,),
        in_specs=[
            lat_spec, ctx_spec,
            pl.BlockSpec((1, D), lambda i: (0, 0)),
            pl.BlockSpec((1, 1, 3 * D), lambda i: (_mod_row(i, tiles_per_batch, ctx_row), 0, 0)),
        ],
        out_specs=pl.BlockSpec((ROW_TILE, D), lambda i: (i, 0)),
        out_shape=jax.ShapeDtypeStruct((R, D), bf16),
        compiler_params=_params("arbitrary"),
        name="norm_mod",
    )(x_lat, x_ctx, g.reshape(1, D), mods)


def _matmul_kernel(a_ref, b_ref, o_ref):
    o_ref[...] = jnp.dot(a_ref[...], b_ref[...], preferred_element_type=f32).astype(o_ref.dtype)


def _proj_row_tile(rows):
    return next(t for t in (1024, 512, ROW_TILE) if rows % t == 0)


def _proj_col_tile(cols, n_tiles=4):
    groups = cols // LANE
    return next(g for g in range(groups // n_tiles, 0, -1) if groups % g == 0) * LANE


def matmul(a, b, *, tm, tn, n_cols=None, out_dtype=f32, name="matmul"):
    M, K = a.shape
    N = b.shape[1] if n_cols is None else n_cols
    assert M % tm == 0 and N % tn == 0, (M, N, tm, tn)
    return pl.pallas_call(
        _matmul_kernel,
        grid=(N // tn, M // tm),
        in_specs=[pl.BlockSpec((tm, K), lambda j, i: (i, 0)),
                  pl.BlockSpec((K, tn), lambda j, i: (0, j))],
        out_specs=pl.BlockSpec((tm, tn), lambda j, i: (i, j)),
        out_shape=jax.ShapeDtypeStruct((M, N), out_dtype),
        compiler_params=_params("arbitrary", "arbitrary"),
        name=name,
    )(a, b)


def _conv3(x, prev_row, next_row, w, b):
    tm = x.shape[0]
    row = lax.broadcasted_iota(jnp.int32, x.shape, 0)
    xm = jnp.where(row == 0, prev_row, pltpu.roll(x, 1, 0))
    xp = jnp.where(row == tm - 1, next_row, pltpu.roll(x, tm - 1, 0))
    return w[0:1] * xm + w[1:2] * x + w[2:3] * xp + b


def _seq_edges(i, tiles_per_batch):
    r = i % tiles_per_batch
    lat_tiles = tiles_per_batch - 1
    first = jnp.logical_or(r == 0, r == lat_tiles)
    last = jnp.logical_or(r == lat_tiles - 1, r == lat_tiles)
    return first, last


HALO = 16


def _halo_rows(prev_ref, next_ref, first, last):
    prev_row = jnp.where(first, 0.0, prev_ref[...].astype(f32)[HALO - 1:HALO, :])
    next_row = jnp.where(last, 0.0, next_ref[...].astype(f32)[0:1, :])
    return prev_row, next_row


def _halo_specs(col_block, tc, n_rows):
    per = ROW_TILE // HALO
    n_blocks = n_rows // HALO
    prev = pl.BlockSpec((HALO, tc), lambda i, c: (jnp.maximum(i * per - 1, 0), col_block(c)))
    nxt = pl.BlockSpec((HALO, tc), lambda i, c: (jnp.minimum((i + 1) * per, n_blocks - 1), col_block(c)))
    return prev, nxt


def _hyena_pre_kernel(tiles_per_batch, x0_ref, x0p_ref, x0n_ref, x1_ref, x1p_ref, x1n_ref,
                      v_ref, vp_ref, vn_ref, w0_ref, w1_ref, w2_ref, b0_ref, b1_ref, b2_ref,
                      x0_out, g_out):
    first, last = _seq_edges(pl.program_id(0), tiles_per_batch)

    def conv(x_ref, p_ref, n_ref, w_ref, b_ref):
        prev_row, next_row = _halo_rows(p_ref, n_ref, first, last)
        return _conv3(x_ref[...].astype(f32), prev_row, next_row, w_ref[...], b_ref[...])

    x0 = conv(x0_ref, x0p_ref, x0n_ref, w0_ref, b0_ref)
    x1 = conv(x1_ref, x1p_ref, x1n_ref, w1_ref, b1_ref)
    v = conv(v_ref, vp_ref, vn_ref, w2_ref, b2_ref)
    g = v * x1
    x0_out[...] = x0.astype(x0_out.dtype)
    g_out[...] = g


def hyena_pre(p, conv_w, conv_b, tiles_per_batch, *, tc=1024):
    R = p.shape[0]
    W = HY_WIDTH
    nb = W // tc
    specs = []
    for part in range(3):
        col = functools.partial(lambda c, part: part * nb + c, part=part)
        main = pl.BlockSpec((ROW_TILE, tc), functools.partial(lambda i, c, col: (i, col(c)), col=col))
        prev, nxt = _halo_specs(col, tc, R)
        specs += [main, prev, nxt]
    wspecs = [pl.BlockSpec((3, tc), functools.partial(lambda i, c, part: (0, part * nb + c), part=part))
              for part in range(3)]
    bspecs = [pl.BlockSpec((1, tc), functools.partial(lambda i, c, part: (0, part * nb + c), part=part))
              for part in range(3)]
    out_spec = pl.BlockSpec((ROW_TILE, tc), lambda i, c: (i, c))
    args = [p] * 9 + [conv_w] * 3 + [conv_b.reshape(1, -1)] * 3
    return pl.pallas_call(
        functools.partial(_hyena_pre_kernel, tiles_per_batch),
        grid=(R // ROW_TILE, nb),
        in_specs=specs + wspecs + bspecs,
        out_specs=[out_spec, out_spec],
        out_shape=[jax.ShapeDtypeStruct((R, W), bf16), jax.ShapeDtypeStruct((R, W), f32)],
        compiler_params=_params("arbitrary", "arbitrary"),
        name="hyena_pre",
    )(*args)


def _filter_kernel(n, rows, bands_ref, w1t_ref, w1c_ref, w1s_ref, b1_ref, fr_ref, w2_ref, b2_ref,
                   w3_ref, dl_ref, o_ref, nrm_ref):
    step = pl.program_id(0)

    def offsets(shape, axis):
        j = step * rows + lax.broadcasted_iota(jnp.int32, shape, axis)
        return jnp.where(j < n, j, 2 * n - j).astype(f32), j != n

    d_row, _ = offsets((1, rows), 1)
    t_row = d_row / float(n - 1)
    ang = (2.0 * math.pi / n) * bands_ref[...] * d_row
    fr = fr_ref[...]
    z1 = (w1t_ref[...] * t_row
          + jnp.dot(w1c_ref[...], jnp.cos(ang), preferred_element_type=f32, precision=HIGHEST)
          - jnp.dot(w1s_ref[...], jnp.sin(ang), preferred_element_type=f32, precision=HIGHEST)
          + b1_ref[...])
    hdn = jnp.sin(fr * z1)
    hdn = jnp.sin(fr * (jnp.dot(w2_ref[...], hdn, preferred_element_type=f32, precision=HIGHEST) + b2_ref[...]))
    h = jnp.dot(hdn.T.astype(bf16), w3_ref[0].astype(bf16), preferred_element_type=f32)
    d_col, valid = offsets((rows, 1), 0)
    t = d_col / float(n - 1)
    h = h * jnp.exp(-t * jnp.abs(dl_ref[...]))
    h = jnp.where(valid, h, 0.0)
    o_ref[...] = h

    @pl.when(step == 0)
    def _():
        nrm_ref[...] = jnp.zeros_like(nrm_ref)

    nrm_ref[...] += jnp.sum(jnp.abs(h), axis=0, keepdims=True)


def hyena_filter(n, w1, b1, freq, w2, b2, w3, *, rows=256):
    W = HY_WIDTH
    Hd = HY_HIDDEN
    bands = jnp.linspace(1e-4, HY_BANDS - 1, HY_BANDS, dtype=f32).reshape(HY_BANDS, 1)
    max_decay = math.log(HY_TARGET) / HY_SHORT_DECAY_PCT
    min_decay = math.log(HY_TARGET) / HY_LONG_DECAY_PCT
    deltas = jnp.linspace(min_decay, max_decay, W, dtype=f32).reshape(1, W)
    steps = 2 * n // rows
    half_steps = n // rows
    full = lambda s: (0, 0)
    w3r = w3.reshape(Hd, 2, W).transpose(1, 0, 2)
    return pl.pallas_call(
        functools.partial(_filter_kernel, n, rows),
        grid=(steps,),
        in_specs=[
            pl.BlockSpec((HY_BANDS, 1), full),
            pl.BlockSpec((Hd, 1), full),
            pl.BlockSpec((Hd, HY_BANDS), full),
            pl.BlockSpec((Hd, HY_BANDS), full),
            pl.BlockSpec((Hd, 1), full),
            pl.BlockSpec((Hd, 1), full),
            pl.BlockSpec((Hd, Hd), full),
            pl.BlockSpec((Hd, 1), full),
            pl.BlockSpec((1, Hd, W), lambda s: (jnp.where(s * rows < n, 0, 1), 0, 0)),
            pl.BlockSpec((1, W), full),
        ],
        out_specs=[pl.BlockSpec((rows, W), lambda s: (s, 0)), pl.BlockSpec((1, W), full)],
        out_shape=[jax.ShapeDtypeStruct((2 * n, W), f32), jax.ShapeDtypeStruct((1, W), f32)],
        compiler_params=_params("arbitrary"),
        name="hyena_filter",
    )(bands, w1[0:1].T, w1[1:1 + HY_BANDS].T, w1[1 + HY_BANDS:].T, b1.reshape(Hd, 1), freq.reshape(Hd, 1),
      w2.T, b2.reshape(Hd, 1), w3r, deltas)


@functools.lru_cache(maxsize=None)
def _fft_constants():
    L1, L2 = FFT_L1, FFT_L2
    L = L1 * L2
    k1 = np.arange(L1)
    nh = np.arange(L1 // 2)
    th = 2.0 * np.pi * np.outer(k1, nh) / L1
    m1r = np.concatenate([np.cos(th), -np.sin(th)], axis=0)
    m1i = np.concatenate([np.sin(th), np.cos(th)], axis=0)
    thf = 2.0 * np.pi * np.outer(k1, np.arange(L1)) / L1
    m1f = np.concatenate([np.cos(thf), -np.sin(thf)], axis=0)
    n2 = np.arange(L2)
    k2 = np.arange(L2)
    m = (k1[:, None, None] * n2[None, None, :] + L1 * k2[None, :, None] * n2[None, None, :]) % L
    ph = 2.0 * np.pi * m / L
    gr, gi = np.cos(ph), -np.sin(ph)
    g = np.concatenate([np.concatenate([gr, -gi], axis=2), np.concatenate([gi, gr], axis=2)], axis=1)
    gt = np.transpose(g, (0, 2, 1))
    thi = 2.0 * np.pi * np.outer(nh, k1) / L1
    m3r = np.concatenate([np.cos(thi), np.sin(thi)], axis=0)
    m3i = np.concatenate([-np.sin(thi), np.cos(thi)], axis=0)
    cast = lambda a: np.asarray(a, dtype=np.float32)
    return dict(m1r=cast(m1r), m1i=cast(m1i), m1f=cast(m1f), g=cast(g), gt=cast(gt), m3r=cast(m3r), m3i=cast(m3i))


@functools.lru_cache(maxsize=None)
def _ctx_dft_constants(n):
    L = 2 * n
    k = np.arange(L)
    th = 2.0 * np.pi * np.outer(k, np.arange(n)) / L
    c, s = np.cos(th), np.sin(th)
    mc = np.concatenate([np.concatenate([c, s], axis=1), np.concatenate([-s, c], axis=1)], axis=0)
    thf = 2.0 * np.pi * np.outer(k, np.arange(L)) / L
    mf = np.concatenate([np.cos(thf), -np.sin(thf)], axis=0)
    ct, st = c.T, s.T
    minv = np.concatenate([np.concatenate([ct, -st], axis=1), np.concatenate([st, ct], axis=1)], axis=0)
    cast = lambda a: np.asarray(a, dtype=np.float32)
    return dict(mc=cast(mc), mf=cast(mf), minv=cast(minv))


def _bf16_constants(consts):
    return {k: jnp.asarray(v).astype(bf16) for k, v in consts.items()}


SUB = 8
N2C = 32


def _store_step1(o_ref, j, a):
    for ch in range(a.shape[0] // SUB):
        o_ref[0, ch, j] = a[ch * SUB:(ch + 1) * SUB]


def _fft1_kernel(re_ref, im_ref, mr_ref, mi_ref, o_ref):
    base = pl.program_id(2) * N2C
    for j in range(N2C):
        rows = pl.ds(base + j, FFT_L1 // 2, stride=FFT_L2)
        _store_step1(o_ref, j,
                     jnp.dot(mr_ref[...], re_ref[0, rows, :].astype(bf16), preferred_element_type=f32)
                     + jnp.dot(mi_ref[...], im_ref[0, rows, :].astype(bf16), preferred_element_type=f32))


def _step1_out(P, W, index):
    groups = 2 * FFT_L1 // SUB
    spec = pl.BlockSpec((1, groups, N2C, SUB, LANE), index)
    return spec, jax.ShapeDtypeStruct((P, groups, FFT_L2, SUB, W), f32)


def fft_step1(g3, m1r, m1i, n_lat):
    B, _, W = g3.shape
    half = FFT_L1 // 2
    out_spec, out_shape = _step1_out(B // 2, W, lambda p, c, k: (p, 0, k, 0, c))
    return pl.pallas_call(
        _fft1_kernel,
        grid=(B // 2, W // LANE, FFT_L2 // N2C),
        in_specs=[pl.BlockSpec((1, n_lat, LANE), lambda p, c, k: (2 * p, 0, c)),
                  pl.BlockSpec((1, n_lat, LANE), lambda p, c, k: (2 * p + 1, 0, c)),
                  pl.BlockSpec((2 * FFT_L1, half), lambda p, c, k: (0, 0)),
                  pl.BlockSpec((2 * FFT_L1, half), lambda p, c, k: (0, 0))],
        out_specs=out_spec,
        out_shape=out_shape,
        compiler_params=_params("arbitrary", "arbitrary", "arbitrary"),
        name="fft_step1",
    )(g3, g3, m1r, m1i)


def _fft1_filter_kernel(x_ref, m_ref, o_ref):
    base = pl.program_id(1) * N2C
    for j in range(N2C):
        rows = pl.ds(base + j, FFT_L1, stride=FFT_L2)
        _store_step1(o_ref, j, jnp.dot(m_ref[...], x_ref[rows, :].astype(bf16), preferred_element_type=f32))


def fft_step1_filter(ts, m1f):
    L, W = ts.shape
    out_spec, out_shape = _step1_out(1, W, lambda c, k: (0, 0, k, 0, c))
    return pl.pallas_call(
        _fft1_filter_kernel,
        grid=(W // LANE, FFT_L2 // N2C),
        in_specs=[pl.BlockSpec((L, LANE), lambda c, k: (0, c)),
                  pl.BlockSpec((2 * FFT_L1, FFT_L1), lambda c, k: (0, 0))],
        out_specs=out_spec,
        out_shape=out_shape,
        compiler_params=_params("arbitrary", "arbitrary"),
        name="fft_step1_filter",
    )(ts, m1f)


def _cmul(yr, yi, hr, hi):
    return yr * hr - yi * hi, yr * hi + yi * hr


def _step1_column(a_refs, j):
    col = lambda ref: ref[0, pl.ds(j, FFT_L2, stride=SUB), :]
    re = jnp.concatenate([col(a_refs[0]), col(a_refs[1])], axis=1)
    im = jnp.concatenate([col(a_refs[2]), col(a_refs[3])], axis=1)
    return jnp.concatenate([re, im], axis=0).astype(bf16)


def _fft2_filter_kernel(a0_ref, a1_ref, a2_ref, a3_ref, g_ref, nrm_ref, o_ref):
    scale = 1.0 / (nrm_ref[...] * float(FFT_L1 * FFT_L2))
    for j in range(SUB):
        a = _step1_column((a0_ref, a1_ref, a2_ref, a3_ref), j)
        o_ref[j] = jnp.dot(g_ref[j], a, preferred_element_type=f32) * scale


def _step1_specs(index):
    def spec(part, lane_half):
        return pl.BlockSpec((1, FFT_L2 * SUB, LANE),
                            lambda *g: (index(*g)[0], part * (FFT_L1 // SUB) + index(*g)[1],
                                        2 * index(*g)[2] + lane_half))
    return [spec(0, 0), spec(0, 1), spec(1, 0), spec(1, 1)]


def fft_step2_filter(af, g, nrm):
    W = af.shape[-1]
    ct = 2 * LANE
    L1, R2 = FFT_L1, 2 * FFT_L2
    af = af.reshape(1, -1, W)
    return pl.pallas_call(
        _fft2_filter_kernel,
        grid=(L1 // SUB, W // ct),
        in_specs=_step1_specs(lambda k, c: (0, k, c)) + [
            pl.BlockSpec((SUB, R2, R2), lambda k, c: (k, 0, 0)),
            pl.BlockSpec((1, ct), lambda k, c: (0, c))],
        out_specs=pl.BlockSpec((SUB, R2, ct), lambda k, c: (k, 0, c)),
        out_shape=jax.ShapeDtypeStruct((L1, R2, W), f32),
        compiler_params=_params("arbitrary", "arbitrary"),
        name="fft_step2_filter",
    )(af, af, af, af, g, nrm)


def _fft2_kernel(a0_ref, a1_ref, a2_ref, a3_ref, g_ref, gt_ref, h_ref, ore_ref, oim_ref):
    half = FFT_L2
    for j in range(SUB):
        a = _step1_column((a0_ref, a1_ref, a2_ref, a3_ref), j)
        y = jnp.dot(g_ref[j], a, preferred_element_type=f32)
        pr, pi = _cmul(y[:half], y[half:], h_ref[j, :half], h_ref[j, half:])
        pcat = jnp.concatenate([pr, pi], axis=0).astype(bf16)
        b = jnp.dot(gt_ref[j], pcat, preferred_element_type=f32)
        for ch in range(FFT_L2 // N2C):
            ore_ref[0, ch, j] = b[ch * N2C:(ch + 1) * N2C]
            oim_ref[0, ch, j] = b[half + ch * N2C:half + (ch + 1) * N2C]


def fft_step2(a, g, gt, hf):
    P, W = a.shape[0], a.shape[-1]
    ct = 2 * LANE
    L1, R2 = FFT_L1, 2 * FFT_L2
    a = a.reshape(P, -1, W)
    out = pl.BlockSpec((1, FFT_L2 // N2C, SUB, N2C, ct), lambda k, c, p: (p, 0, k, 0, c))
    shape = jax.ShapeDtypeStruct((P, FFT_L2 // N2C, L1, N2C, W), f32)
    return pl.pallas_call(
        _fft2_kernel,
        grid=(L1 // SUB, W // ct, P),
        in_specs=_step1_specs(lambda k, c, p: (p, k, c)) + [
            pl.BlockSpec((SUB, R2, R2), lambda k, c, p: (k, 0, 0)),
            pl.BlockSpec((SUB, R2, R2), lambda k, c, p: (k, 0, 0)),
            pl.BlockSpec((SUB, R2, ct), lambda k, c, p: (k, 0, c))],
        out_specs=[out, out],
        out_shape=[shape, shape],
        compiler_params=_params("arbitrary", "arbitrary", "arbitrary"),
        name="fft_step2",
    )(a, a, a, a, g, gt, hf)


def _fft3_kernel(bre_ref, bim_ref, mr_ref, mi_ref, o_ref):
    L1 = FFT_L1
    half = L1 // 2
    base = pl.program_id(2) * N2C
    for j in range(N2C):
        br = bre_ref[0, pl.ds(j, L1, stride=N2C), :].astype(bf16)
        bi = bim_ref[0, pl.ds(j, L1, stride=N2C), :].astype(bf16)
        z = (jnp.dot(mr_ref[...], br, preferred_element_type=f32)
             + jnp.dot(mi_ref[...], bi, preferred_element_type=f32))
        rows = pl.ds(base + j, half, stride=FFT_L2)
        o_ref[0, 0, rows, :] = z[:half]
        o_ref[0, 1, rows, :] = z[half:]


def fft_step3(bre, bim, m3r, m3i, rows_total, n_lat):
    P, W = bre.shape[0], bre.shape[-1]
    L1, L2 = FFT_L1, FFT_L2
    bre = bre.reshape(P, -1, W)
    bim = bim.reshape(P, -1, W)
    spec = pl.BlockSpec((1, L1 * N2C, LANE), lambda p, c, k: (p, k, c))
    return pl.pallas_call(
        _fft3_kernel,
        grid=(P, W // LANE, L2 // N2C),
        in_specs=[spec, spec,
                  pl.BlockSpec((L1, L1), lambda p, c, k: (0, 0)),
                  pl.BlockSpec((L1, L1), lambda p, c, k: (0, 0))],
        out_specs=pl.BlockSpec((1, 2, n_lat, LANE), lambda p, c, k: (p, 0, 0, c)),
        out_shape=jax.ShapeDtypeStruct((P, 2, rows_total, W), f32),
        compiler_params=_params("arbitrary", "arbitrary", "arbitrary"),
        name="fft_step3",
    )(bre, bim, m3r, m3i)


def _ctx_filter_kernel(n, ts_ref, m_ref, nrm_ref, o_ref):
    scale = 1.0 / (nrm_ref[...] * float(2 * n))
    o_ref[...] = jnp.dot(m_ref[...], ts_ref[...].astype(bf16), preferred_element_type=f32) * scale


def ctx_filter_spectrum(ts, mf, nrm, *, ct=256):
    L, W = ts.shape
    return pl.pallas_call(
        functools.partial(_ctx_filter_kernel, L // 2),
        grid=(W // ct,),
        in_specs=[pl.BlockSpec((L, ct), lambda c: (0, c)),
                  pl.BlockSpec((2 * L, L), lambda c: (0, 0)),
                  pl.BlockSpec((1, ct), lambda c: (0, c))],
        out_specs=pl.BlockSpec((2 * L, ct), lambda c: (0, c)),
        out_shape=jax.ShapeDtypeStruct((2 * L, W), f32),
        compiler_params=_params("arbitrary"),
        name="ctx_filter_spectrum",
    )(ts, mf, nrm)


def _ctx_conv_kernel(n, re_ref, im_ref, mc_ref, minv_ref, h_ref, y_in_ref, o_ref):
    del y_in_ref
    L = 2 * n
    z = jnp.concatenate([re_ref[0], im_ref[0]], axis=0).astype(bf16)
    y = jnp.dot(mc_ref[...], z, preferred_element_type=f32)
    pr, pi = _cmul(y[:L], y[L:], h_ref[:L], h_ref[L:])
    pcat = jnp.concatenate([pr, pi], axis=0).astype(bf16)
    out = jnp.dot(minv_ref[...], pcat, preferred_element_type=f32)
    o_ref[0, 0] = out[:n]
    o_ref[0, 1] = out[n:]


def ctx_long_conv(gz, y4, hfc, consts, n_lat, n_ctx, *, ct=256):
    B, T, W = gz.shape
    blk = n_lat // n_ctx
    L = 2 * n_ctx
    out = pl.pallas_call(
        functools.partial(_ctx_conv_kernel, n_ctx),
        grid=(B // 2, W // ct),
        in_specs=[pl.BlockSpec((1, n_ctx, ct), lambda p, c: (2 * p, blk, c)),
                  pl.BlockSpec((1, n_ctx, ct), lambda p, c: (2 * p + 1, blk, c)),
                  pl.BlockSpec((2 * L, L), lambda p, c: (0, 0)),
                  pl.BlockSpec((L, 2 * L), lambda p, c: (0, 0)),
                  pl.BlockSpec((2 * L, ct), lambda p, c: (0, c)),
                  pl.BlockSpec(memory_space=pl.ANY)],
        out_specs=pl.BlockSpec((1, 2, n_ctx, ct), lambda p, c: (p, 0, blk, c)),
        out_shape=jax.ShapeDtypeStruct(y4.shape, f32),
        input_output_aliases={5: 0},
        compiler_params=_params("arbitrary", "arbitrary"),
        name="ctx_long_conv",
    )(gz, gz, consts["mc"], consts["minv"], hfc, y4)
    return out.reshape(B, T, W)


def hyena_long_conv(gz, n_lat, n_ctx, f_w1, f_b1, f_freq, f_w2, f_b2, f_w3):
    B, T, W = gz.shape
    cst = _bf16_constants(_fft_constants())
    L1, L2 = FFT_L1, FFT_L2
    assert 2 * n_lat == L1 * L2 and T % L2 == 0 and B % 2 == 0
    ts, nrm = hyena_filter(n_lat, f_w1, f_b1, f_freq, f_w2, f_b2, f_w3)
    af = fft_step1_filter(ts, cst["m1f"])
    hf = fft_step2_filter(af, cst["g"], nrm)
    a = fft_step1(gz, cst["m1r"], cst["m1i"], n_lat)
    bre, bim = fft_step2(a, cst["g"], cst["gt"], hf)
    y = fft_step3(bre, bim, cst["m3r"], cst["m3i"], T, n_lat)
    ccst = _bf16_constants(_ctx_dft_constants(n_ctx))
    ts_c, nrm_c = hyena_filter(n_ctx, f_w1, f_b1, f_freq, f_w2, f_b2, f_w3)
    hfc = ctx_filter_spectrum(ts_c, ccst["mf"], nrm_c)
    return ctx_long_conv(gz, y, hfc, ccst, n_lat, n_ctx)


@functools.lru_cache(maxsize=None)
def _rope_tables(n_lat, n_ctx):
    half = HEAD_DIM // 2
    nf = half // 2
    inv = ROPE_THETA ** (-np.arange(nf, dtype=np.float64) / nf)
    t = np.arange(n_lat)
    pos = np.stack([t // GRID_W, t % GRID_W], axis=1).astype(np.float64)
    ang = pos[:, :, None] * inv[None, None, :]
    cos = np.concatenate([np.cos(ang), np.cos(ang)], axis=2).reshape(n_lat, HEAD_DIM)
    sin = np.concatenate([-np.sin(ang), np.sin(ang)], axis=2).reshape(n_lat, HEAD_DIM)
    cos = np.concatenate([cos, np.ones((n_ctx, HEAD_DIM))], axis=0)
    sin = np.concatenate([sin, np.zeros((n_ctx, HEAD_DIM))], axis=0)
    return np.asarray(cos, np.float32), np.asarray(sin, np.float32)


def _norm_rope_heads(xs, w, cos, sin, lo_lane):
    nf = HEAD_DIM // 4
    ms = [jnp.mean(x * x, axis=-1, keepdims=True) for x in xs]
    ys = [x * lax.rsqrt(m + NORM_EPS) * w for x, m in zip(xs, ms)]
    ps = [jnp.where(lo_lane, pltpu.roll(y, HEAD_DIM - nf, 1), pltpu.roll(y, nf, 1)) for y in ys]
    return [y * cos + p * sin for y, p in zip(ys, ps)]


def _qkv_prep_kernel(q_ref, k_ref, v_ref, cos_ref, sin_ref, qn_ref, kn_ref, q_out, kt_out, v_out):
    cos = cos_ref[...]
    sin = sin_ref[...]
    lane = lax.broadcasted_iota(jnp.int32, cos.shape, 1)
    lo_lane = (lane % (HEAD_DIM // 2)) < (HEAD_DIM // 4)
    head = lambda ref, h: ref[:, h * HEAD_DIM:(h + 1) * HEAD_DIM].astype(f32)
    scale = HEAD_DIM ** -0.5 * math.log2(math.e)
    qs = _norm_rope_heads([head(q_ref, h) for h in range(ATT_HEADS)], qn_ref[...], cos, sin, lo_lane)
    for h, q in enumerate(qs):
        q_out[:, h * HEAD_DIM:(h + 1) * HEAD_DIM] = (q * scale).astype(q_out.dtype)
    ks = _norm_rope_heads([head(k_ref, h) for h in range(ATT_KV_HEADS)], kn_ref[...], cos, sin, lo_lane)
    ones_col = (lane == 0).astype(v_out.dtype)
    for h, k in enumerate(ks):
        sl = slice(h * HEAD_DIM, (h + 1) * HEAD_DIM)
        kt_out[0, sl, :] = k.T.astype(kt_out.dtype)
        v_out[:, 2 * h * HEAD_DIM:(2 * h + 1) * HEAD_DIM] = v_ref[:, sl]
        v_out[:, (2 * h + 1) * HEAD_DIM:(2 * h + 2) * HEAD_DIM] = ones_col


def qkv_prep(p, q_norm, k_norm, B, n_lat, n_ctx):
    R = p.shape[0]
    T = n_lat + n_ctx
    tpb = T // ROW_TILE
    cos, sin = _rope_tables(n_lat, n_ctx)
    return pl.pallas_call(
        _qkv_prep_kernel,
        grid=(R // ROW_TILE,),
        in_specs=[pl.BlockSpec((ROW_TILE, ATT_WIDTH), lambda i: (i, EVEN_OFF_Q // ATT_WIDTH)),
                  pl.BlockSpec((ROW_TILE, ATT_KV_WIDTH), lambda i: (i, EVEN_OFF_K // ATT_KV_WIDTH)),
                  pl.BlockSpec((ROW_TILE, ATT_KV_WIDTH), lambda i: (i, EVEN_OFF_V // ATT_KV_WIDTH)),
                  pl.BlockSpec((ROW_TILE, HEAD_DIM), lambda i: (i % tpb, 0)),
                  pl.BlockSpec((ROW_TILE, HEAD_DIM), lambda i: (i % tpb, 0)),
                  pl.BlockSpec((1, HEAD_DIM), lambda i: (0, 0)),
                  pl.BlockSpec((1, HEAD_DIM), lambda i: (0, 0))],
        out_specs=[pl.BlockSpec((ROW_TILE, ATT_WIDTH), lambda i: (i, 0)),
                   pl.BlockSpec((1, ATT_KV_WIDTH, ROW_TILE), lambda i: (i // tpb, 0, i % tpb)),
                   pl.BlockSpec((ROW_TILE, 2 * ATT_KV_WIDTH), lambda i: (i, 0))],
        out_shape=[jax.ShapeDtypeStruct((R, ATT_WIDTH), bf16),
                   jax.ShapeDtypeStruct((B, ATT_KV_WIDTH, T), bf16),
                   jax.ShapeDtypeStruct((R, 2 * ATT_KV_WIDTH), bf16)],
        compiler_params=_params("arbitrary"),
        name="qkv_prep",
    )(p, p, p, jnp.asarray(cos), jnp.asarray(sin), q_norm.reshape(1, HEAD_DIM), k_norm.reshape(1, HEAD_DIM))


ATT_CHUNK = 256
ATT_Q_TILE = 512


def _attention_kernel(k_lo, q_ref, kt_ref, v_ref, *rest):
    o_ref, qs_ref, sa_ref, sb_ref, pa_ref, pb_ref, os_ref = rest[-7:]
    tq = q_ref.shape[1]
    n_chunks = ATT_GROUP * tq // ATT_CHUNK
    for h in range(ATT_GROUP):
        qs_ref[h * tq:(h + 1) * tq, :] = q_ref[0, :, h * HEAD_DIM:(h + 1) * HEAD_DIM]

    rows = lambda c: slice(c * ATT_CHUNK, (c + 1) * ATT_CHUNK)
    s_refs = (sa_ref, sb_ref)
    p_refs = (pa_ref, pb_ref)

    def scores(c):
        s_refs[c % 2][:, k_lo:] = jnp.dot(qs_ref[rows(c), :], kt_ref[0, :, k_lo:], preferred_element_type=f32)

    def exponentials(c):
        s = s_refs[c % 2][:, k_lo:]
        m = jnp.max(s, axis=-1, keepdims=True)
        p_refs[c % 2][:, k_lo:] = jnp.exp2(s - m).astype(bf16)

    def weighted_values(c):
        r = jnp.dot(p_refs[c % 2][:, k_lo:], v_ref[0, k_lo:, :], preferred_element_type=f32)
        os_ref[rows(c), :] = r[:, :HEAD_DIM] / r[:, HEAD_DIM:HEAD_DIM + 1]

    scores(0)
    for c in range(n_chunks):
        if c + 1 < n_chunks:
            scores(c + 1)
        exponentials(c)
        if c >= 1:
            weighted_values(c - 1)
    weighted_values(n_chunks - 1)
    for h in range(ATT_GROUP):
        o_ref[0, :, h * HEAD_DIM:(h + 1) * HEAD_DIM] = os_ref[h * tq:(h + 1) * tq, :].astype(o_ref.dtype)


def attention(q, kt, v, B, n_lat, n_ctx):
    R = q.shape[0]
    T = n_lat + n_ctx
    gw = ATT_GROUP * HEAD_DIM
    q3 = q.reshape(B, T, ATT_WIDTH)
    v3 = v.reshape(B, T, 2 * ATT_KV_WIDTH)
    out_shape = jax.ShapeDtypeStruct((B, T, ATT_WIDTH), bf16)

    def call(name, k_lo, tq, q_tiles, first_tile, extra_in, extra_specs, aliases):
        qo_spec = pl.BlockSpec((1, tq, gw), lambda b, g, i: (b, first_tile + i, g))
        return pl.pallas_call(
            functools.partial(_attention_kernel, k_lo),
            grid=(B, ATT_KV_HEADS, q_tiles),
            in_specs=[qo_spec,
                      pl.BlockSpec((1, HEAD_DIM, T), lambda b, g, i: (b, g, 0)),
                      pl.BlockSpec((1, T, 2 * HEAD_DIM), lambda b, g, i: (b, 0, g))] + extra_specs,
            out_specs=qo_spec,
            out_shape=out_shape,
            scratch_shapes=[pltpu.VMEM((ATT_GROUP * tq, HEAD_DIM), bf16),
                            pltpu.VMEM((ATT_CHUNK, T), f32), pltpu.VMEM((ATT_CHUNK, T), f32),
                            pltpu.VMEM((ATT_CHUNK, T), bf16), pltpu.VMEM((ATT_CHUNK, T), bf16),
                            pltpu.VMEM((ATT_GROUP * tq, HEAD_DIM), f32)],
            input_output_aliases=aliases,
            compiler_params=_params("arbitrary", "arbitrary", "arbitrary"),
            name=name,
        )(q3, kt, v3, *extra_in)

    att = call("attention", 0, ATT_Q_TILE, n_lat // ATT_Q_TILE, 0, [], [], {})
    att = call("attention_ctx", n_lat, n_ctx, 1, n_lat // n_ctx, [att], [pl.BlockSpec(memory_space=pl.ANY)], {3: 0})
    return att.reshape(R, ATT_WIDTH)


def _post_residual(x, y, g_post, gate):
    yn = y * lax.rsqrt(jnp.mean(y * y, axis=-1, keepdims=True) + NORM_EPS) * g_post
    return x + gate * yn


def _silu(x):
    return x * jax.nn.sigmoid(x)


def _even_out_kernel(tiles_per_batch, x0_ref, g_ref, yc_ref, ghy_ref, att_ref, ga0_ref, ga1_ref, bias_ref, w_ref,
                     gp_ref, m_ref, x_ref, c_ref, gn_ref, mn_ref, o_ref, hn_ref):
    D = x_ref.shape[-1]
    g = g_ref[...]
    hy = x0_ref[...].astype(f32) * (yc_ref[...] + g * bias_ref[...]) * _silu(ghy_ref[...].astype(f32))
    g_att = jnp.concatenate([ga0_ref[...], ga1_ref[...]], axis=1).astype(f32)
    at = att_ref[...].astype(f32) * _silu(g_att)
    lhs = jnp.concatenate([hy, at], axis=1).astype(bf16)
    y = jnp.dot(lhs, w_ref[...], preferred_element_type=f32)
    x_new = _post_residual(_input_rows(tiles_per_batch, x_ref, c_ref), y, gp_ref[...], m_ref[0, :, 2 * D:3 * D])
    o_ref[...] = x_new
    hn_ref[...] = _norm_mod_rows(x_new, gn_ref[...], mn_ref[0]).astype(hn_ref.dtype)


def even_out(x0, g, yconv, p, att, hy_bias, w_out, g_post, mods, x_lat, x_ctx, next_g_pre, next_mods,
             tiles_per_batch, ctx_row):
    R = x0.shape[0]
    D = x_lat.shape[1]
    W = HY_WIDTH
    hw = ATT_WIDTH // 2
    row = lambda i: (i, 0)
    full = lambda i: (0, 0)
    lat_spec, ctx_spec = _input_row_specs(tiles_per_batch, D)
    return pl.pallas_call(
        functools.partial(_even_out_kernel, tiles_per_batch),
        grid=(R // ROW_TILE,),
        in_specs=[pl.BlockSpec((ROW_TILE, W), row),
                  pl.BlockSpec((ROW_TILE, W), row),
                  pl.BlockSpec((ROW_TILE, W), row),
                  pl.BlockSpec((ROW_TILE, W), lambda i: (i, EVEN_OFF_GHY // W)),
                  pl.BlockSpec((ROW_TILE, ATT_WIDTH), row),
                  pl.BlockSpec((ROW_TILE, hw), lambda i: (i, EVEN_OFF_GATT // hw)),
                  pl.BlockSpec((ROW_TILE, hw), lambda i: (i, EVEN_OFF_GATT // hw + 1)),
                  pl.BlockSpec((1, W), full),
                  pl.BlockSpec((W + ATT_WIDTH, D), full),
                  pl.BlockSpec((1, D), full),
                  pl.BlockSpec((1, 1, 3 * D), lambda i: (_mod_row(i, tiles_per_batch, ctx_row), 0, 0)),
                  lat_spec, ctx_spec,
                  pl.BlockSpec((1, D), full),
                  pl.BlockSpec((1, 1, 3 * D), lambda i: (_mod_row(i, tiles_per_batch, ctx_row), 0, 0))],
        out_specs=[pl.BlockSpec((ROW_TILE, D), row), pl.BlockSpec((ROW_TILE, D), row)],
        out_shape=[jax.ShapeDtypeStruct((R, D), f32), jax.ShapeDtypeStruct((R, D), bf16)],
        compiler_params=_params("arbitrary"),
        name="even_out",
    )(x0, g, yconv, p, att, p, p, hy_bias.reshape(1, W), w_out.astype(bf16), g_post.reshape(1, D), mods,
      x_lat, x_ctx, next_g_pre.reshape(1, D), next_mods)


def even_layer(x_lat, x_ctx, mods, next_g_pre, next_mods, B, n_lat, n_ctx, g_pre, g_post, w_in, conv_w, conv_b,
               f_w1, f_b1, f_freq, f_w2, f_b2, f_w3, hy_bias, q_norm, k_norm, w_out):
    T = n_lat + n_ctx
    tpb = T // ROW_TILE
    h = norm_mod(x_lat, x_ctx, g_pre, mods, tpb, B)
    p = matmul(h, w_in.astype(bf16), tm=_proj_row_tile(B * T), tn=_proj_col_tile(w_in.shape[1]), out_dtype=bf16,
               name="even_in_proj")
    x0, g = hyena_pre(p, conv_w, conv_b, tpb)
    yconv = hyena_long_conv(g.reshape(B, T, HY_WIDTH), n_lat, n_ctx, f_w1, f_b1, f_freq, f_w2, f_b2, f_w3)
    q, kt, v = qkv_prep(p, q_norm, k_norm, B, n_lat, n_ctx)
    att = attention(q, kt, v, B, n_lat, n_ctx)
    return even_out(x0, g, yconv.reshape(B * T, HY_WIDTH), p, att, hy_bias, w_out, g_post, mods, x_lat, x_ctx,
                    next_g_pre, next_mods, tpb, B)


def _mlstm_prep_kernel(tiles_per_batch, q_ref, qp_ref, qn_ref, k_ref, kp_ref, kn_ref,
                       wq_ref, wk_ref, bq_ref, bk_ref, q_out, kt_out):
    first, last = _seq_edges(pl.program_id(0), tiles_per_batch)
    prev_row, next_row = _halo_rows(qp_ref, qn_ref, first, last)
    q = _silu(_conv3(q_ref[...].astype(f32), prev_row, next_row, wq_ref[...], bq_ref[...]))
    q_out[...] = q.astype(q_out.dtype)
    prev_row, next_row = _halo_rows(kp_ref, kn_ref, first, last)
    k = _silu(_conv3(k_ref[...].astype(f32), prev_row, next_row, wk_ref[...], bk_ref[...])) * (ML_QK ** -0.5)
    for h in range(k.shape[1] // ML_QK):
        sl = slice(h * ML_QK, (h + 1) * ML_QK)
        kt_out[0, sl, :] = k[:, sl].T.astype(kt_out.dtype)


def mlstm_prep(p, conv_w, conv_b, B, T, *, tc=1024):
    R = p.shape[0]
    tpb = T // ROW_TILE
    nb = ML_QK_WIDTH // tc
    qcol = lambda c: c
    kcol = lambda c: nb + c
    qprev, qnext = _halo_specs(qcol, tc, R)
    kprev, knext = _halo_specs(kcol, tc, R)
    return pl.pallas_call(
        functools.partial(_mlstm_prep_kernel, tpb),
        grid=(R // ROW_TILE, nb),
        in_specs=[pl.BlockSpec((ROW_TILE, tc), lambda i, c: (i, c)), qprev, qnext,
                  pl.BlockSpec((ROW_TILE, tc), lambda i, c: (i, nb + c)), kprev, knext,
                  pl.BlockSpec((3, tc), lambda i, c: (0, c)),
                  pl.BlockSpec((3, tc), lambda i, c: (0, nb + c)),
                  pl.BlockSpec((1, tc), lambda i, c: (0, c)),
                  pl.BlockSpec((1, tc), lambda i, c: (0, nb + c))],
        out_specs=[pl.BlockSpec((ROW_TILE, tc), lambda i, c: (i, c)),
                   pl.BlockSpec((1, tc, ROW_TILE), lambda i, c: (i // tpb, c, i % tpb))],
        out_shape=[jax.ShapeDtypeStruct((R, ML_QK_WIDTH), bf16),
                   jax.ShapeDtypeStruct((B, ML_QK_WIDTH, T), bf16)],
        compiler_params=_params("arbitrary", "arbitrary"),
        name="mlstm_prep",
    )(p, p, p, p, p, p, conv_w, conv_w, conv_b.reshape(1, -1), conv_b.reshape(1, -1))


def _log_sigmoid(x):
    return jnp.minimum(x, 0.0) - jnp.log(1.0 + jnp.exp(-jnp.abs(x)))


def _mlstm_gates_kernel(g_ref, b_ref, gc_out, gr_out):
    pre = g_ref[...] + b_ref[...]
    lane = lax.broadcasted_iota(jnp.int32, pre.shape, 1)
    is_forget = (lane // ML_HEADS) % 2 == 1
    gc = jnp.where(is_forget, _log_sigmoid(pre), pre)
    gc_out[...] = gc
    gr_out[0] = gc.T


def mlstm_gates(gates, gate_b, B, T):
    R = gates.shape[0]
    gb = jnp.pad(gate_b, (0, LANE - gate_b.shape[0])).reshape(1, LANE)
    return pl.pallas_call(
        _mlstm_gates_kernel,
        grid=(B,),
        in_specs=[pl.BlockSpec((T, LANE), lambda b: (b, 0)),
                  pl.BlockSpec((1, LANE), lambda b: (0, 0))],
        out_specs=[pl.BlockSpec((T, LANE), lambda b: (b, 0)),
                   pl.BlockSpec((1, LANE, T), lambda b: (b, 0, 0))],
        out_shape=[jax.ShapeDtypeStruct((R, LANE), f32), jax.ShapeDtypeStruct((B, LANE, T), f32)],
        compiler_params=_params("arbitrary"),
        name="mlstm_gates",
    )(gates, gb)


def _mlstm_chunk_setup(reverse, gc_ref, gr_ref, m_ref):
    Lc = gc_ref.shape[0]
    H = ML_HEADS
    i_off = 2 * H if reverse else 0
    f_off = i_off + H
    t_idx = lax.broadcasted_iota(jnp.int32, (Lc, Lc), 0)
    s_idx = lax.broadcasted_iota(jnp.int32, (Lc, Lc), 1)
    causal = (s_idx >= t_idx) if reverse else (s_idx <= t_idx)
    tri = causal.astype(f32)
    gc = gc_ref[...]
    gr = gr_ref[0]
    b_col_all = jnp.dot(tri, gc[:, f_off:f_off + H], preferred_element_type=f32, precision=HIGHEST)
    b_row_all = lax.dot_general(gr[f_off:f_off + H, :], tri, (((1,), (1,)), ((), ())),
                                preferred_element_type=f32, precision=HIGHEST)
    i_col_all = gc[:, i_off:i_off + H]
    end = 0 if reverse else Lc - 1
    b_end = b_col_all[end:end + 1, :]
    m_prev = m_ref[0:1, 0:H]
    g_col = b_end - b_col_all + i_col_all
    m_new = jnp.maximum(b_end + m_prev, jnp.max(g_col, axis=0, keepdims=True))
    a_col = jnp.exp(g_col - m_new)
    a_prev = jnp.exp(b_end + m_prev - m_new)
    m_ref[:, 0:H] = jnp.broadcast_to(m_new, (m_ref.shape[0], H))
    return dict(causal=causal, i_row=gr[i_off:i_off + H, :], b_col=b_col_all, b_row=b_row_all,
                m_prev=m_prev, a_col=a_col, a_prev=a_prev)


class _MlstmChain:
    def __init__(self, h, cs, q_ref, kt_ref, v_ref, o_ref, ct_ref):
        self.h, self.cs = h, cs
        self.q_ref, self.kt_ref, self.v_ref, self.o_ref, self.ct_ref = q_ref, kt_ref, v_ref, o_ref, ct_ref

    def _q(self):
        return self.q_ref[:, self.h * ML_QK:(self.h + 1) * ML_QK]

    def _kt(self):
        return self.kt_ref[0, self.h * ML_QK:(self.h + 1) * ML_QK, :]

    def _v(self):
        Lc = self.q_ref.shape[0]
        ones_col = (lax.broadcasted_iota(jnp.int32, (Lc, LANE), 1) == 0).astype(bf16)
        return jnp.concatenate([self.v_ref[:, self.h * ML_V:(self.h + 1) * ML_V], ones_col], axis=1)

    def scores(self):
        self.qk = jnp.dot(self._q(), self._kt(), preferred_element_type=f32)

    def gates(self):
        h, cs = self.h, self.cs
        i_row = cs["i_row"][h:h + 1, :]
        b_col = cs["b_col"][:, h:h + 1]
        b_row = cs["b_row"][h:h + 1, :]
        m_prev = cs["m_prev"][:, h:h + 1]
        d = jnp.where(cs["causal"], b_col + (i_row - b_row), -jnp.inf)
        inter = b_col + m_prev
        self.m_row = jnp.maximum(inter, jnp.max(d, axis=-1, keepdims=True))
        self.s = (self.qk * jnp.exp(d - self.m_row)).astype(bf16)
        self.w_prev = jnp.exp(inter - self.m_row)

    def values(self):
        h = self.h
        self.ct = self.ct_ref[h]
        tot = (jnp.dot(self.s, self._v(), preferred_element_type=f32)
               + self.w_prev * jnp.dot(self._q(), self.ct.astype(bf16), preferred_element_type=f32))
        scale = 1.0 / jnp.maximum(jnp.abs(tot[:, ML_V:ML_V + 1]), jnp.exp(-self.m_row))
        self.o_ref[:, h * ML_V:(h + 1) * ML_V] = (tot[:, :ML_V] * scale).astype(self.o_ref.dtype)

    def update(self):
        h = self.h
        va = (self._v().astype(f32) * self.cs["a_col"][:, h:h + 1]).astype(bf16)
        self.ct_ref[h] = (self.cs["a_prev"][:, h:h + 1] * self.ct
                          + jnp.dot(self._kt(), va, preferred_element_type=f32))


def _mlstm_scan_kernel(qf_ref, ktf_ref, vf_ref, gcf_ref, grf_ref, qb_ref, ktb_ref, vb_ref, gcb_ref, grb_ref,
                       of_ref, ob_ref, ctf_ref, mf_ref, ctb_ref, mb_ref):
    @pl.when(pl.program_id(1) == 0)
    def _():
        for ref in (ctf_ref, mf_ref, ctb_ref, mb_ref):
            ref[...] = jnp.zeros_like(ref)

    fwd = _mlstm_chunk_setup(False, gcf_ref, grf_ref, mf_ref)
    bwd = _mlstm_chunk_setup(True, gcb_ref, grb_ref, mb_ref)
    chains = []
    for h in range(ML_HEADS):
        chains.append(_MlstmChain(h, fwd, qf_ref, ktf_ref, vf_ref, of_ref, ctf_ref))
        chains.append(_MlstmChain(h, bwd, qb_ref, ktb_ref, vb_ref, ob_ref, ctb_ref))
    stages = ("scores", "gates", "values", "update")
    for k in range(len(chains) + len(stages) - 1):
        for depth, stage in enumerate(stages):
            if 0 <= k - depth < len(chains):
                getattr(chains[k - depth], stage)()


def mlstm_scan(q, kt, p, gc, gr, B, n_lat, n_ctx):
    R = q.shape[0]
    Lc = ML_CHUNK
    tpb = (n_lat + n_ctx) // Lc
    lat = n_lat // Lc
    ctx = n_ctx // Lc

    def specs(reverse):
        def chunk(j):
            if reverse:
                return tpb - 1 - j
            return jnp.where(j < ctx, lat + j, j - ctx)
        ins = [pl.BlockSpec((Lc, ML_QK_WIDTH), lambda b, j: (b * tpb + chunk(j), 0)),
               pl.BlockSpec((1, ML_QK_WIDTH, Lc), lambda b, j: (b, 0, chunk(j))),
               pl.BlockSpec((Lc, ML_WIDTH), lambda b, j: (b * tpb + chunk(j), ODD_OFF_V // ML_WIDTH)),
               pl.BlockSpec((Lc, LANE), lambda b, j: (b * tpb + chunk(j), 0)),
               pl.BlockSpec((1, LANE, Lc), lambda b, j: (b, 0, chunk(j)))]
        out = pl.BlockSpec((Lc, ML_WIDTH), lambda b, j: (b * tpb + chunk(j), 0))
        return ins, out

    ins_f, out_f = specs(False)
    ins_b, out_b = specs(True)
    state = [pltpu.VMEM((ML_HEADS, ML_QK, ML_V + LANE), f32), pltpu.VMEM((SUB, LANE), f32)]
    return pl.pallas_call(
        _mlstm_scan_kernel,
        grid=(B, tpb),
        in_specs=ins_f + ins_b,
        out_specs=[out_f, out_b],
        out_shape=[jax.ShapeDtypeStruct((R, ML_WIDTH), bf16), jax.ShapeDtypeStruct((R, ML_WIDTH), bf16)],
        scratch_shapes=state + state,
        compiler_params=_params("arbitrary", "arbitrary"),
        name="mlstm_scan",
    )(q, kt, p, gc, gr, q, kt, p, gc, gr)


def _odd_out_kernel(hf_ref, hb_ref, o_ref, z_ref, hn_ref, w_ref, gp_ref, m_ref, x_ref, out_ref):
    D = x_ref.shape[-1]
    hs = (hf_ref[0].astype(f32) + hb_ref[0].astype(f32)) * jax.nn.sigmoid(o_ref[0].astype(f32))
    segs = [hs[:, h * ML_V:(h + 1) * ML_V] for h in range(ML_HEADS)]
    ms = [jnp.mean(seg * seg, axis=-1, keepdims=True) for seg in segs]
    parts = [seg * lax.rsqrt(m + NORM_EPS) for seg, m in zip(segs, ms)]
    hn = jnp.concatenate(parts, axis=1) * hn_ref[...] * _silu(z_ref[0].astype(f32))
    y = jnp.dot(hn.astype(bf16), w_ref[...], preferred_element_type=f32)
    out_ref[0] = _post_residual(x_ref[0], y, gp_ref[...], m_ref[0, :, 2 * D:3 * D])


def odd_out(hf, hb, p, head_norm, w_out, g_post, mods, x_all, B, n_lat, T, *, tm=512):
    D = x_all.shape[1]
    view = lambda a: a.reshape(B, T, a.shape[1])
    row = lambda b, i: (b, i, 0)
    full = lambda b, i: (0, 0)
    return pl.pallas_call(
        _odd_out_kernel,
        grid=(B, n_lat // tm),
        in_specs=[pl.BlockSpec((1, tm, ML_WIDTH), row),
                  pl.BlockSpec((1, tm, ML_WIDTH), row),
                  pl.BlockSpec((1, tm, ML_WIDTH), lambda b, i: (b, i, ODD_OFF_O // ML_WIDTH)),
                  pl.BlockSpec((1, tm, ML_WIDTH), lambda b, i: (b, i, ODD_OFF_Z // ML_WIDTH)),
                  pl.BlockSpec((1, ML_WIDTH), full),
                  pl.BlockSpec((ML_WIDTH, D), full),
                  pl.BlockSpec((1, D), full),
                  pl.BlockSpec((1, 1, 3 * D), lambda b, i: (b, 0, 0)),
                  pl.BlockSpec((1, tm, D), row)],
        out_specs=pl.BlockSpec((1, tm, D), row),
        out_shape=jax.ShapeDtypeStruct((B, n_lat, D), f32),
        compiler_params=_params("arbitrary", "arbitrary"),
        name="odd_out",
    )(view(hf), view(hb), view(p), view(p), head_norm.reshape(1, ML_WIDTH), w_out.astype(bf16),
      g_post.reshape(1, D), mods, view(x_all))


def odd_layer_last(x_all, h, mods, B, n_lat, n_ctx, g_post, w_in, conv_w, conv_b, gate_b, head_norm, w_out):
    T = n_lat + n_ctx
    wb = w_in.astype(bf16)
    tm = _proj_row_tile(B * T)
    p = matmul(h, wb, tm=tm, tn=_proj_col_tile(ODD_MAIN), n_cols=ODD_MAIN, out_dtype=bf16, name="odd_in_proj")
    n_gates = w_in.shape[1] - ODD_MAIN
    gates = matmul(h, jnp.pad(wb[:, ODD_MAIN:], ((0, 0), (0, LANE - n_gates))), tm=tm, tn=LANE, name="odd_gate_proj")
    q, kt = mlstm_prep(p, conv_w, conv_b, B, T)
    gc, gr = mlstm_gates(gates, gate_b, B, T)
    hf, hb = mlstm_scan(q, kt, p, gc, gr, B, n_lat, n_ctx)
    return odd_out(hf, hb, p, head_norm, w_out, g_post, mods, x_all, B, n_lat, T)


def kernel(x, c, ctx, c_ctx, w_mod, b_mod, g_pre, g_post, e_w_in, e_conv_w, e_conv_b, e_filt_w1,
           e_filt_b1, e_filt_freq, e_filt_w2, e_filt_b2, e_filt_w3, e_hy_bias, e_q_norm, e_k_norm,
           e_w_out, o_w_in, o_conv_w, o_conv_b, o_gate_b, o_head_norm, o_w_out):
    B, n_lat, D = x.shape
    n_ctx = ctx.shape[1]
    T = n_lat + n_ctx
    depth = w_mod.shape[0]
    assert depth == 2 and B + 1 <= 8 and n_ctx == ROW_TILE and n_lat % ROW_TILE == 0
    cond = jnp.concatenate([c, c_ctx[None], jnp.zeros((8 - B - 1, D), f32)], axis=0)
    mods_all = adaln_all(cond, w_mod, b_mod)
    mods0 = mods_all[0].reshape(8, 1, 3 * D)
    mods1 = mods_all[1].reshape(8, 1, 3 * D)
    x_all, h1 = even_layer(x.reshape(B * n_lat, D), ctx.reshape(B * n_ctx, D), mods0, g_pre[1], mods1,
                           B, n_lat, n_ctx, g_pre[0], g_post[0], e_w_in[0],
                           e_conv_w[0], e_conv_b[0], e_filt_w1[0], e_filt_b1[0], e_filt_freq[0], e_filt_w2[0],
                           e_filt_b2[0], e_filt_w3[0], e_hy_bias[0], e_q_norm[0], e_k_norm[0], e_w_out[0])
    out = odd_layer_last(x_all, h1, mods1, B, n_lat, n_ctx, g_post[1], o_w_in[0],
                         o_conv_w[0], o_conv_b[0], o_gate_b[0], o_head_norm[0], o_w_out[0])
    return out.reshape(B, n_lat, D)
```

```python
import functools
import math

import numpy as np
import jax
import jax.numpy as jnp
from jax import lax
from jax.experimental import pallas as pl
from jax.experimental.pallas import tpu as pltpu

f32 = jnp.float32
bf16 = jnp.bfloat16
HIGHEST = lax.Precision.HIGHEST

D_MODEL = 1024
GRID_W = 64
NORM_EPS = 1e-6

HY_WIDTH = 1024
HY_EMB = 33
HY_BANDS = (HY_EMB - 1) // 2
HY_HIDDEN = 64
HY_TARGET = 1e-2
HY_SHORT_DECAY_PCT = 0.3
HY_LONG_DECAY_PCT = 1.5

ATT_HEADS = 8
ATT_KV_HEADS = 2
ATT_GROUP = ATT_HEADS // ATT_KV_HEADS
HEAD_DIM = 128
ATT_WIDTH = ATT_HEADS * HEAD_DIM
ATT_KV_WIDTH = ATT_KV_HEADS * HEAD_DIM
ROPE_THETA = 10000.0
EVEN_OFF_XV = 0
EVEN_OFF_GHY = 3 * HY_WIDTH
EVEN_OFF_Q = EVEN_OFF_GHY + HY_WIDTH
EVEN_OFF_K = EVEN_OFF_Q + ATT_WIDTH
EVEN_OFF_V = EVEN_OFF_K + ATT_KV_WIDTH
EVEN_OFF_GATT = EVEN_OFF_V + ATT_KV_WIDTH
EVEN_IN = EVEN_OFF_GATT + ATT_WIDTH

ML_HEADS = 8
ML_QK = 128
ML_V = 256
ML_QK_WIDTH = ML_HEADS * ML_QK
ML_WIDTH = ML_HEADS * ML_V
ODD_OFF_Q = 0
ODD_OFF_K = ML_QK_WIDTH
ODD_OFF_V = 2 * ML_QK_WIDTH
ODD_OFF_O = ODD_OFF_V + ML_WIDTH
ODD_OFF_Z = ODD_OFF_O + ML_WIDTH
ODD_OFF_GATES = ODD_OFF_Z + ML_WIDTH
ODD_MAIN = ODD_OFF_GATES

ROW_TILE = 256
ML_CHUNK = 256
LANE = 128
VMEM_LIMIT_BYTES = 48 * 1024 * 1024

FFT_L1 = 64
FFT_L2 = 128


def _params(*sem):
    return pltpu.CompilerParams(dimension_semantics=sem, vmem_limit_bytes=VMEM_LIMIT_BYTES)


def _adaln_kernel(c_ref, w_ref, b_ref, o_ref):
    c = c_ref[...]
    s = c * jax.nn.sigmoid(c)
    o_ref[0] = jnp.dot(s, w_ref[0], preferred_element_type=f32, precision=HIGHEST) + b_ref[0]


def adaln_all(cond, w_mod, b_mod, *, tn=768):
    depth, D, N = w_mod.shape
    return pl.pallas_call(
        _adaln_kernel,
        grid=(depth, N // tn),
        in_specs=[
            pl.BlockSpec((8, D), lambda l, j: (0, 0)),
            pl.BlockSpec((1, D, tn), lambda l, j: (l, 0, j)),
            pl.BlockSpec((1, 1, tn), lambda l, j: (l, 0, j)),
        ],
        out_specs=pl.BlockSpec((1, 8, tn), lambda l, j: (l, 0, j)),
        out_shape=jax.ShapeDtypeStruct((depth, 8, N), f32),
        compiler_params=_params("arbitrary", "arbitrary"),
        name="adaln",
    )(cond, w_mod, b_mod.reshape(depth, 1, N))


def _mod_row(i, tiles_per_batch, ctx_row):
    lat_tiles = tiles_per_batch - 1
    return jnp.where(i % tiles_per_batch == lat_tiles, ctx_row, i // tiles_per_batch)


def _norm_mod_rows(x, g, m):
    D = x.shape[-1]
    y = x * lax.rsqrt(jnp.mean(x * x, axis=-1, keepdims=True) + NORM_EPS)
    return y * g * (1.0 + m[:, D:2 * D]) + m[:, 0:D]


def _input_row_specs(tiles_per_batch, D):
    lat_tiles = tiles_per_batch - 1

    def lat_index(i):
        return ((i // tiles_per_batch) * lat_tiles + jnp.minimum(i % tiles_per_batch, lat_tiles - 1), 0)

    return (pl.BlockSpec((ROW_TILE, D), lat_index),
            pl.BlockSpec((ROW_TILE, D), lambda i: (i // tiles_per_batch, 0)))


def _input_rows(tiles_per_batch, lat_ref, ctx_ref):
    is_ctx = pl.program_id(0) % tiles_per_batch == tiles_per_batch - 1
    return jnp.where(is_ctx, ctx_ref[...], lat_ref[...])


def _norm_mod_kernel(tiles_per_batch, x_ref, c_ref, g_ref, m_ref, o_ref):
    x = _input_rows(tiles_per_batch, x_ref, c_ref)
    o_ref[...] = _norm_mod_rows(x, g_ref[...], m_ref[0]).astype(o_ref.dtype)


def norm_mod(x_lat, x_ctx, g, mods, tiles_per_batch, ctx_row):
    D = x_lat.shape[1]
    R = x_lat.shape[0] + x_ctx.shape[0]
    lat_spec, ctx_spec = _input_row_specs(tiles_per_batch, D)
    return pl.pallas_call(
        functools.partial(_norm_mod_kernel, tiles_per_batch),
        grid=(R // ROW_TILE,),
        in_specs=[
            lat_spec, ctx_spec,
            pl.BlockSpec((1, D), lambda i: (0, 0)),
            pl.BlockSpec((1, 1, 3 * D), lambda i: (_mod_row(i, tiles_per_batch, ctx_row), 0, 0)),
        ],
        out_specs=pl.BlockSpec((ROW_TILE, D), lambda i: (i, 0)),
        out_shape=jax.ShapeDtypeStruct((R, D), bf16),
        compiler_params=_params("arbitrary"),
        name="norm_mod",
    )(x_lat, x_ctx, g.reshape(1, D), mods)


def _matmul_kernel(a_ref, b_ref, o_ref, bq_ref):
    @pl.when(pl.program_id(1) == 0)
    def _():
        bq_ref[...] = b_ref[...].astype(bq_ref.dtype)

    o_ref[...] = jnp.dot(a_ref[...], bq_ref[...], preferred_element_type=f32).astype(o_ref.dtype)


def _proj_row_tile(rows):
    return next(t for t in (1024, 512, ROW_TILE) if rows % t == 0)


def _proj_col_tile(cols, n_tiles=4):
    groups = cols // LANE
    return next(g for g in range(groups // n_tiles, 0, -1) if groups % g == 0) * LANE


def matmul(a, b, *, tm, tn, n_cols=None, out_dtype=f32, name="matmul"):
    M, K = a.shape
    N = b.shape[1] if n_cols is None else n_cols
    assert M % tm == 0 and N % tn == 0, (M, N, tm, tn)
    return pl.pallas_call(
        _matmul_kernel,
        grid=(N // tn, M // tm),
        in_specs=[pl.BlockSpec((tm, K), lambda j, i: (i, 0)),
                  pl.BlockSpec((K, tn), lambda j, i: (0, j))],
        out_specs=pl.BlockSpec((tm, tn), lambda j, i: (i, j)),
        out_shape=jax.ShapeDtypeStruct((M, N), out_dtype),
        scratch_shapes=[pltpu.VMEM((K, tn), a.dtype)],
        compiler_params=_params("arbitrary", "arbitrary"),
        name=name,
    )(a, b)


def _conv3(x, prev_row, next_row, w, b):
    tm = x.shape[0]
    row = lax.broadcasted_iota(jnp.int32, x.shape, 0)
    xm = jnp.where(row == 0, prev_row, pltpu.roll(x, 1, 0))
    xp = jnp.where(row == tm - 1, next_row, pltpu.roll(x, tm - 1, 0))
    return w[0:1] * xm + w[1:2] * x + w[2:3] * xp + b


def _seq_edges(i, tiles_per_batch):
    r = i % tiles_per_batch
    lat_tiles = tiles_per_batch - 1
    first = jnp.logical_or(r == 0, r == lat_tiles)
    last = jnp.logical_or(r == lat_tiles - 1, r == lat_tiles)
    return first, last


HALO = 16


def _halo_rows(prev_ref, next_ref, first, last):
    prev_row = jnp.where(first, 0.0, prev_ref[...].astype(f32)[HALO - 1:HALO, :])
    next_row = jnp.where(last, 0.0, next_ref[...].astype(f32)[0:1, :])
    return prev_row, next_row


def _halo_specs(col_block, tc, n_rows):
    per = ROW_TILE // HALO
    n_blocks = n_rows // HALO
    prev = pl.BlockSpec((HALO, tc), lambda i, c: (jnp.maximum(i * per - 1, 0), col_block(c)))
    nxt = pl.BlockSpec((HALO, tc), lambda i, c: (jnp.minimum((i + 1) * per, n_blocks - 1), col_block(c)))
    return prev, nxt


def _hyena_pre_kernel(tiles_per_batch, x0_ref, x0p_ref, x0n_ref, x1_ref, x1p_ref, x1n_ref,
                      v_ref, vp_ref, vn_ref, w0_ref, w1_ref, w2_ref, b0_ref, b1_ref, b2_ref,
                      x0_out, g_out):
    first, last = _seq_edges(pl.program_id(0), tiles_per_batch)

    def conv(x_ref, p_ref, n_ref, w_ref, b_ref):
        prev_row, next_row = _halo_rows(p_ref, n_ref, first, last)
        return _conv3(x_ref[...].astype(f32), prev_row, next_row, w_ref[...], b_ref[...])

    x0 = conv(x0_ref, x0p_ref, x0n_ref, w0_ref, b0_ref)
    x1 = conv(x1_ref, x1p_ref, x1n_ref, w1_ref, b1_ref)
    v = conv(v_ref, vp_ref, vn_ref, w2_ref, b2_ref)
    g = v * x1
    x0_out[...] = x0.astype(x0_out.dtype)
    g_out[...] = g


def hyena_pre(p, conv_w, conv_b, tiles_per_batch, *, tc=1024):
    R = p.shape[0]
    W = HY_WIDTH
    nb = W // tc
    specs = []
    for part in range(3):
        col = functools.partial(lambda c, part: part * nb + c, part=part)
        main = pl.BlockSpec((ROW_TILE, tc), functools.partial(lambda i, c, col: (i, col(c)), col=col))
        prev, nxt = _halo_specs(col, tc, R)
        specs += [main, prev, nxt]
    wspecs = [pl.BlockSpec((3, tc), functools.partial(lambda i, c, part: (0, part * nb + c), part=part))
              for part in range(3)]
    bspecs = [pl.BlockSpec((1, tc), functools.partial(lambda i, c, part: (0, part * nb + c), part=part))
              for part in range(3)]
    out_spec = pl.BlockSpec((ROW_TILE, tc), lambda i, c: (i, c))
    args = [p] * 9 + [conv_w] * 3 + [conv_b.reshape(1, -1)] * 3
    return pl.pallas_call(
        functools.partial(_hyena_pre_kernel, tiles_per_batch),
        grid=(R // ROW_TILE, nb),
        in_specs=specs + wspecs + bspecs,
        out_specs=[out_spec, out_spec],
        out_shape=[jax.ShapeDtypeStruct((R, W), bf16), jax.ShapeDtypeStruct((R, W), f32)],
        compiler_params=_params("arbitrary", "arbitrary"),
        name="hyena_pre",
    )(*args)


def _filter_kernel(n, rows, bands_ref, w1t_ref, w1c_ref, w1s_ref, b1_ref, fr_ref, w2_ref, b2_ref,
                   w3_ref, dl_ref, o_ref, nrm_ref):
    step = pl.program_id(0)

    def offsets(shape, axis):
        j = step * rows + lax.broadcasted_iota(jnp.int32, shape, axis)
        return jnp.where(j < n, j, 2 * n - j).astype(f32), j != n

    d_row, _ = offsets((1, rows), 1)
    t_row = d_row / float(n - 1)
    ang = (2.0 * math.pi / n) * bands_ref[...] * d_row
    fr = fr_ref[...]
    z1 = (w1t_ref[...] * t_row
          + jnp.dot(w1c_ref[...], jnp.cos(ang), preferred_element_type=f32, precision=HIGHEST)
          - jnp.dot(w1s_ref[...], jnp.sin(ang), preferred_element_type=f32, precision=HIGHEST)
          + b1_ref[...])
    hdn = jnp.sin(fr * z1)
    hdn = jnp.sin(fr * (jnp.dot(w2_ref[...], hdn, preferred_element_type=f32, precision=HIGHEST) + b2_ref[...]))
    h = jnp.dot(hdn.T.astype(bf16), w3_ref[0].astype(bf16), preferred_element_type=f32)
    d_col, valid = offsets((rows, 1), 0)
    t = d_col / float(n - 1)
    h = h * jnp.exp(-t * jnp.abs(dl_ref[...]))
    h = jnp.where(valid, h, 0.0)
    o_ref[...] = h

    @pl.when(step == 0)
    def _():
        nrm_ref[...] = jnp.zeros_like(nrm_ref)

    nrm_ref[...] += jnp.sum(jnp.abs(h), axis=0, keepdims=True)


def hyena_filter(n, w1, b1, freq, w2, b2, w3, *, rows=256):
    W = HY_WIDTH
    Hd = HY_HIDDEN
    bands = jnp.linspace(1e-4, HY_BANDS - 1, HY_BANDS, dtype=f32).reshape(HY_BANDS, 1)
    max_decay = math.log(HY_TARGET) / HY_SHORT_DECAY_PCT
    min_decay = math.log(HY_TARGET) / HY_LONG_DECAY_PCT
    deltas = jnp.linspace(min_decay, max_decay, W, dtype=f32).reshape(1, W)
    steps = 2 * n // rows
    half_steps = n // rows
    full = lambda s: (0, 0)
    w3r = w3.reshape(Hd, 2, W).transpose(1, 0, 2)
    return pl.pallas_call(
        functools.partial(_filter_kernel, n, rows),
        grid=(steps,),
        in_specs=[
            pl.BlockSpec((HY_BANDS, 1), full),
            pl.BlockSpec((Hd, 1), full),
            pl.BlockSpec((Hd, HY_BANDS), full),
            pl.BlockSpec((Hd, HY_BANDS), full),
            pl.BlockSpec((Hd, 1), full),
            pl.BlockSpec((Hd, 1), full),
            pl.BlockSpec((Hd, Hd), full),
            pl.BlockSpec((Hd, 1), full),
            pl.BlockSpec((1, Hd, W), lambda s: (jnp.where(s * rows < n, 0, 1), 0, 0)),
            pl.BlockSpec((1, W), full),
        ],
        out_specs=[pl.BlockSpec((rows, W), lambda s: (s, 0)), pl.BlockSpec((1, W), full)],
        out_shape=[jax.ShapeDtypeStruct((2 * n, W), f32), jax.ShapeDtypeStruct((1, W), f32)],
        compiler_params=_params("arbitrary"),
        name="hyena_filter",
    )(bands, w1[0:1].T, w1[1:1 + HY_BANDS].T, w1[1 + HY_BANDS:].T, b1.reshape(Hd, 1), freq.reshape(Hd, 1),
      w2.T, b2.reshape(Hd, 1), w3r, deltas)


@functools.lru_cache(maxsize=None)
def _fft_constants():
    L1, L2 = FFT_L1, FFT_L2
    L = L1 * L2
    k1 = np.arange(L1)
    nh = np.arange(L1 // 2)
    th = 2.0 * np.pi * np.outer(k1, nh) / L1
    m1r = np.concatenate([np.cos(th), -np.sin(th)], axis=0)
    m1i = np.concatenate([np.sin(th), np.cos(th)], axis=0)
    thf = 2.0 * np.pi * np.outer(k1, np.arange(L1)) / L1
    m1f = np.concatenate([np.cos(thf), -np.sin(thf)], axis=0)
    n2 = np.arange(L2)
    k2 = np.arange(L2)
    m = (k1[:, None, None] * n2[None, None, :] + L1 * k2[None, :, None] * n2[None, None, :]) % L
    ph = 2.0 * np.pi * m / L
    gr, gi = np.cos(ph), -np.sin(ph)
    g = np.concatenate([np.concatenate([gr, -gi], axis=2), np.concatenate([gi, gr], axis=2)], axis=1)
    gt = np.transpose(g, (0, 2, 1))
    thi = 2.0 * np.pi * np.outer(nh, k1) / L1
    m3r = np.concatenate([np.cos(thi), np.sin(thi)], axis=0)
    m3i = np.concatenate([-np.sin(thi), np.cos(thi)], axis=0)
    cast = lambda a: np.asarray(a, dtype=np.float32)
    return dict(m1r=cast(m1r), m1i=cast(m1i), m1f=cast(m1f), g=cast(g), gt=cast(gt), m3r=cast(m3r), m3i=cast(m3i))


@functools.lru_cache(maxsize=None)
def _ctx_dft_constants(n):
    L = 2 * n
    k = np.arange(L)
    th = 2.0 * np.pi * np.outer(k, np.arange(n)) / L
    c, s = np.cos(th), np.sin(th)
    mc = np.concatenate([np.concatenate([c, s], axis=1), np.concatenate([-s, c], axis=1)], axis=0)
    thf = 2.0 * np.pi * np.outer(k, np.arange(L)) / L
    mf = np.concatenate([np.cos(thf), -np.sin(thf)], axis=0)
    ct, st = c.T, s.T
    minv = np.concatenate([np.concatenate([ct, -st], axis=1), np.concatenate([st, ct], axis=1)], axis=0)
    cast = lambda a: np.asarray(a, dtype=np.float32)
    return dict(mc=cast(mc), mf=cast(mf), minv=cast(minv))


def _bf16_constants(consts):
    return {k: jnp.asarray(v).astype(bf16) for k, v in consts.items()}


SUB = 8
N2C = 32


def _store_step1(o_ref, j, a):
    for ch in range(a.shape[0] // SUB):
        o_ref[0, ch, j] = a[ch * SUB:(ch + 1) * SUB]


def _fft1_kernel(re_ref, im_ref, mr_ref, mi_ref, o_ref):
    base = pl.program_id(2) * N2C
    for j in range(N2C):
        rows = pl.ds(base + j, FFT_L1 // 2, stride=FFT_L2)
        _store_step1(o_ref, j,
                     jnp.dot(mr_ref[...], re_ref[0, rows, :].astype(bf16), preferred_element_type=f32)
                     + jnp.dot(mi_ref[...], im_ref[0, rows, :].astype(bf16), preferred_element_type=f32))


def _step1_out(P, W, index):
    groups = 2 * FFT_L1 // SUB
    spec = pl.BlockSpec((1, groups, N2C, SUB, LANE), index)
    return spec, jax.ShapeDtypeStruct((P, groups, FFT_L2, SUB, W), f32)


def fft_step1(g3, m1r, m1i, n_lat):
    B, _, W = g3.shape
    half = FFT_L1 // 2
    out_spec, out_shape = _step1_out(B // 2, W, lambda p, c, k: (p, 0, k, 0, c))
    return pl.pallas_call(
        _fft1_kernel,
        grid=(B // 2, W // LANE, FFT_L2 // N2C),
        in_specs=[pl.BlockSpec((1, n_lat, LANE), lambda p, c, k: (2 * p, 0, c)),
                  pl.BlockSpec((1, n_lat, LANE), lambda p, c, k: (2 * p + 1, 0, c)),
                  pl.BlockSpec((2 * FFT_L1, half), lambda p, c, k: (0, 0)),
                  pl.BlockSpec((2 * FFT_L1, half), lambda p, c, k: (0, 0))],
        out_specs=out_spec,
        out_shape=out_shape,
        compiler_params=_params("arbitrary", "arbitrary", "arbitrary"),
        name="fft_step1",
    )(g3, g3, m1r, m1i)


def _fft1_filter_kernel(x_ref, m_ref, o_ref):
    base = pl.program_id(1) * N2C
    for j in range(N2C):
        rows = pl.ds(base + j, FFT_L1, stride=FFT_L2)
        _store_step1(o_ref, j, jnp.dot(m_ref[...], x_ref[rows, :].astype(bf16), preferred_element_type=f32))


def fft_step1_filter(ts, m1f):
    L, W = ts.shape
    out_spec, out_shape = _step1_out(1, W, lambda c, k: (0, 0, k, 0, c))
    return pl.pallas_call(
        _fft1_filter_kernel,
        grid=(W // LANE, FFT_L2 // N2C),
        in_specs=[pl.BlockSpec((L, LANE), lambda c, k: (0, c)),
                  pl.BlockSpec((2 * FFT_L1, FFT_L1), lambda c, k: (0, 0))],
        out_specs=out_spec,
        out_shape=out_shape,
        compiler_params=_params("arbitrary", "arbitrary"),
        name="fft_step1_filter",
    )(ts, m1f)


def _cmul(yr, yi, hr, hi):
    return yr * hr - yi * hi, yr * hi + yi * hr


def _step1_column(a_refs, j):
    col = lambda ref: ref[0, pl.ds(j, FFT_L2, stride=SUB), :]
    re = jnp.concatenate([col(a_refs[0]), col(a_refs[1])], axis=1)
    im = jnp.concatenate([col(a_refs[2]), col(a_refs[3])], axis=1)
    return jnp.concatenate([re, im], axis=0).astype(bf16)


def _fft2_filter_kernel(a0_ref, a1_ref, a2_ref, a3_ref, g_ref, nrm_ref, o_ref):
    scale = 1.0 / (nrm_ref[...] * float(FFT_L1 * FFT_L2))
    for j in range(SUB):
        a = _step1_column((a0_ref, a1_ref, a2_ref, a3_ref), j)
        o_ref[j] = jnp.dot(g_ref[j], a, preferred_element_type=f32) * scale


def _step1_specs(index):
    def spec(part, lane_half):
        return pl.BlockSpec((1, FFT_L2 * SUB, LANE),
                            lambda *g: (index(*g)[0], part * (FFT_L1 // SUB) + index(*g)[1],
                                        2 * index(*g)[2] + lane_half))
    return [spec(0, 0), spec(0, 1), spec(1, 0), spec(1, 1)]


def fft_step2_filter(af, g, nrm):
    W = af.shape[-1]
    ct = 2 * LANE
    L1, R2 = FFT_L1, 2 * FFT_L2
    af = af.reshape(1, -1, W)
    return pl.pallas_call(
        _fft2_filter_kernel,
        grid=(L1 // SUB, W // ct),
        in_specs=_step1_specs(lambda k, c: (0, k, c)) + [
            pl.BlockSpec((SUB, R2, R2), lambda k, c: (k, 0, 0)),
            pl.BlockSpec((1, ct), lambda k, c: (0, c))],
        out_specs=pl.BlockSpec((SUB, R2, ct), lambda k, c: (k, 0, c)),
        out_shape=jax.ShapeDtypeStruct((L1, R2, W), f32),
        compiler_params=_params("arbitrary", "arbitrary"),
        name="fft_step2_filter",
    )(af, af, af, af, g, nrm)


def _fft2_kernel(a0_ref, a1_ref, a2_ref, a3_ref, g_ref, gt_ref, h_ref, ore_ref, oim_ref):
    half = FFT_L2
    for j in range(SUB):
        a = _step1_column((a0_ref, a1_ref, a2_ref, a3_ref), j)
        y = jnp.dot(g_ref[j], a, preferred_element_type=f32)
        pr, pi = _cmul(y[:half], y[half:], h_ref[j, :half], h_ref[j, half:])
        pcat = jnp.concatenate([pr, pi], axis=0).astype(bf16)
        b = jnp.dot(gt_ref[j], pcat, preferred_element_type=f32)
        for ch in range(FFT_L2 // N2C):
            ore_ref[0, ch, j] = b[ch * N2C:(ch + 1) * N2C]
            oim_ref[0, ch, j] = b[half + ch * N2C:half + (ch + 1) * N2C]


def fft_step2(a, g, gt, hf):
    P, W = a.shape[0], a.shape[-1]
    ct = 2 * LANE
    L1, R2 = FFT_L1, 2 * FFT_L2
    a = a.reshape(P, -1, W)
    out = pl.BlockSpec((1, FFT_L2 // N2C, SUB, N2C, ct), lambda k, c, p: (p, 0, k, 0, c))
    shape = jax.ShapeDtypeStruct((P, FFT_L2 // N2C, L1, N2C, W), f32)
    return pl.pallas_call(
        _fft2_kernel,
        grid=(L1 // SUB, W // ct, P),
        in_specs=_step1_specs(lambda k, c, p: (p, k, c)) + [
            pl.BlockSpec((SUB, R2, R2), lambda k, c, p: (k, 0, 0)),
            pl.BlockSpec((SUB, R2, R2), lambda k, c, p: (k, 0, 0)),
            pl.BlockSpec((SUB, R2, ct), lambda k, c, p: (k, 0, c))],
        out_specs=[out, out],
        out_shape=[shape, shape],
        compiler_params=_params("arbitrary", "arbitrary", "arbitrary"),
        name="fft_step2",
    )(a, a, a, a, g, gt, hf)


def _fft3_kernel(bre_ref, bim_ref, mr_ref, mi_ref, o_ref):
    L1 = FFT_L1
    half = L1 // 2
    base = pl.program_id(2) * N2C
    for j in range(N2C):
        br = bre_ref[0, pl.ds(j, L1, stride=N2C), :].astype(bf16)
        bi = bim_ref[0, pl.ds(j, L1, stride=N2C), :].astype(bf16)
        z = (jnp.dot(mr_ref[...], br, preferred_element_type=f32)
             + jnp.dot(mi_ref[...], bi, preferred_element_type=f32))
        rows = pl.ds(base + j, half, stride=FFT_L2)
        o_ref[0, 0, rows, :] = z[:half]
        o_ref[0, 1, rows, :] = z[half:]


def fft_step3(bre, bim, m3r, m3i, rows_total, n_lat):
    P, W = bre.shape[0], bre.shape[-1]
    L1, L2 = FFT_L1, FFT_L2
    bre = bre.reshape(P, -1, W)
    bim = bim.reshape(P, -1, W)
    spec = pl.BlockSpec((1, L1 * N2C, LANE), lambda p, c, k: (p, k, c))
    return pl.pallas_call(
        _fft3_kernel,
        grid=(P, W // LANE, L2 // N2C),
        in_specs=[spec, spec,
                  pl.BlockSpec((L1, L1), lambda p, c, k: (0, 0)),
                  pl.BlockSpec((L1, L1), lambda p, c, k: (0, 0))],
        out_specs=pl.BlockSpec((1, 2, n_lat, LANE), lambda p, c, k: (p, 0, 0, c)),
        out_shape=jax.ShapeDtypeStruct((P, 2, rows_total, W), f32),
        compiler_params=_params("arbitrary", "arbitrary", "arbitrary"),
        name="fft_step3",
    )(bre, bim, m3r, m3i)


def _ctx_filter_kernel(n, ts_ref, m_ref, nrm_ref, o_ref):
    scale = 1.0 / (nrm_ref[...] * float(2 * n))
    o_ref[...] = jnp.dot(m_ref[...], ts_ref[...].astype(bf16), preferred_element_type=f32) * scale


def ctx_filter_spectrum(ts, mf, nrm, *, ct=256):
    L, W = ts.shape
    return pl.pallas_call(
        functools.partial(_ctx_filter_kernel, L // 2),
        grid=(W // ct,),
        in_specs=[pl.BlockSpec((L, ct), lambda c: (0, c)),
                  pl.BlockSpec((2 * L, L), lambda c: (0, 0)),
                  pl.BlockSpec((1, ct), lambda c: (0, c))],
        out_specs=pl.BlockSpec((2 * L, ct), lambda c: (0, c)),
        out_shape=jax.ShapeDtypeStruct((2 * L, W), f32),
        compiler_params=_params("arbitrary"),
        name="ctx_filter_spectrum",
    )(ts, mf, nrm)


def _ctx_conv_kernel(n, re_ref, im_ref, mc_ref, minv_ref, h_ref, y_in_ref, o_ref):
    del y_in_ref
    L = 2 * n
    z = jnp.concatenate([re_ref[0], im_ref[0]], axis=0).astype(bf16)
    y = jnp.dot(mc_ref[...], z, preferred_element_type=f32)
    pr, pi = _cmul(y[:L], y[L:], h_ref[:L], h_ref[L:])
    pcat = jnp.concatenate([pr, pi], axis=0).astype(bf16)
    out = jnp.dot(minv_ref[...], pcat, preferred_element_type=f32)
    o_ref[0, 0] = out[:n]
    o_ref[0, 1] = out[n:]


def ctx_long_conv(gz, y4, hfc, consts, n_lat, n_ctx, *, ct=256):
    B, T, W = gz.shape
    blk = n_lat // n_ctx
    L = 2 * n_ctx
    out = pl.pallas_call(
        functools.partial(_ctx_conv_kernel, n_ctx),
        grid=(B // 2, W // ct),
        in_specs=[pl.BlockSpec((1, n_ctx, ct), lambda p, c: (2 * p, blk, c)),
                  pl.BlockSpec((1, n_ctx, ct), lambda p, c: (2 * p + 1, blk, c)),
                  pl.BlockSpec((2 * L, L), lambda p, c: (0, 0)),
                  pl.BlockSpec((L, 2 * L), lambda p, c: (0, 0)),
                  pl.BlockSpec((2 * L, ct), lambda p, c: (0, c)),
                  pl.BlockSpec(memory_space=pl.ANY)],
        out_specs=pl.BlockSpec((1, 2, n_ctx, ct), lambda p, c: (p, 0, blk, c)),
        out_shape=jax.ShapeDtypeStruct(y4.shape, f32),
        input_output_aliases={5: 0},
        compiler_params=_params("arbitrary", "arbitrary"),
        name="ctx_long_conv",
    )(gz, gz, consts["mc"], consts["minv"], hfc, y4)
    return out.reshape(B, T, W)


def hyena_long_conv(gz, n_lat, n_ctx, f_w1, f_b1, f_freq, f_w2, f_b2, f_w3):
    B, T, W = gz.shape
    cst = _bf16_constants(_fft_constants())
    L1, L2 = FFT_L1, FFT_L2
    assert 2 * n_lat == L1 * L2 and T % L2 == 0 and B % 2 == 0
    ts, nrm = hyena_filter(n_lat, f_w1, f_b1, f_freq, f_w2, f_b2, f_w3)
    af = fft_step1_filter(ts, cst["m1f"])
    hf = fft_step2_filter(af, cst["g"], nrm)
    a = fft_step1(gz, cst["m1r"], cst["m1i"], n_lat)
    bre, bim = fft_step2(a, cst["g"], cst["gt"], hf)
    y = fft_step3(bre, bim, cst["m3r"], cst["m3i"], T, n_lat)
    ccst = _bf16_constants(_ctx_dft_constants(n_ctx))
    ts_c, nrm_c = hyena_filter(n_ctx, f_w1, f_b1, f_freq, f_w2, f_b2, f_w3)
    hfc = ctx_filter_spectrum(ts_c, ccst["mf"], nrm_c)
    return ctx_long_conv(gz, y, hfc, ccst, n_lat, n_ctx)


@functools.lru_cache(maxsize=None)
def _rope_tables(n_lat, n_ctx):
    half = HEAD_DIM // 2
    nf = half // 2
    inv = ROPE_THETA ** (-np.arange(nf, dtype=np.float64) / nf)
    t = np.arange(n_lat)
    pos = np.stack([t // GRID_W, t % GRID_W], axis=1).astype(np.float64)
    ang = pos[:, :, None] * inv[None, None, :]
    cos = np.concatenate([np.cos(ang), np.cos(ang)], axis=2).reshape(n_lat, HEAD_DIM)
    sin = np.concatenate([-np.sin(ang), np.sin(ang)], axis=2).reshape(n_lat, HEAD_DIM)
    cos = np.concatenate([cos, np.ones((n_ctx, HEAD_DIM))], axis=0)
    sin = np.concatenate([sin, np.zeros((n_ctx, HEAD_DIM))], axis=0)
    return np.asarray(cos, np.float32), np.asarray(sin, np.float32)


def _norm_rope_heads(xs, w, cos, sin, lo_lane):
    nf = HEAD_DIM // 4
    ms = [jnp.mean(x * x, axis=-1, keepdims=True) for x in xs]
    ys = [x * lax.rsqrt(m + NORM_EPS) * w for x, m in zip(xs, ms)]
    ps = [jnp.where(lo_lane, pltpu.roll(y, HEAD_DIM - nf, 1), pltpu.roll(y, nf, 1)) for y in ys]
    return [y * cos + p * sin for y, p in zip(ys, ps)]


def _qkv_prep_kernel(q_ref, k_ref, v_ref, cos_ref, sin_ref, qn_ref, kn_ref, q_out, kt_out, v_out):
    cos = cos_ref[...]
    sin = sin_ref[...]
    lane = lax.broadcasted_iota(jnp.int32, cos.shape, 1)
    lo_lane = (lane % (HEAD_DIM // 2)) < (HEAD_DIM // 4)
    head = lambda ref, h: ref[:, h * HEAD_DIM:(h + 1) * HEAD_DIM].astype(f32)
    scale = HEAD_DIM ** -0.5 * math.log2(math.e)
    qs = _norm_rope_heads([head(q_ref, h) for h in range(ATT_HEADS)], qn_ref[...], cos, sin, lo_lane)
    for h, q in enumerate(qs):
        q_out[:, h * HEAD_DIM:(h + 1) * HEAD_DIM] = (q * scale).astype(q_out.dtype)
    ks = _norm_rope_heads([head(k_ref, h) for h in range(ATT_KV_HEADS)], kn_ref[...], cos, sin, lo_lane)
    ones_col = (lane == 0).astype(v_out.dtype)
    for h, k in enumerate(ks):
        sl = slice(h * HEAD_DIM, (h + 1) * HEAD_DIM)
        kt_out[0, sl, :] = k.T.astype(kt_out.dtype)
        v_out[:, 2 * h * HEAD_DIM:(2 * h + 1) * HEAD_DIM] = v_ref[:, sl]
        v_out[:, (2 * h + 1) * HEAD_DIM:(2 * h + 2) * HEAD_DIM] = ones_col


def qkv_prep(p, q_norm, k_norm, B, n_lat, n_ctx):
    R = p.shape[0]
    T = n_lat + n_ctx
    tpb = T // ROW_TILE
    cos, sin = _rope_tables(n_lat, n_ctx)
    return pl.pallas_call(
        _qkv_prep_kernel,
        grid=(R // ROW_TILE,),
        in_specs=[pl.BlockSpec((ROW_TILE, ATT_WIDTH), lambda i: (i, EVEN_OFF_Q // ATT_WIDTH)),
                  pl.BlockSpec((ROW_TILE, ATT_KV_WIDTH), lambda i: (i, EVEN_OFF_K // ATT_KV_WIDTH)),
                  pl.BlockSpec((ROW_TILE, ATT_KV_WIDTH), lambda i: (i, EVEN_OFF_V // ATT_KV_WIDTH)),
                  pl.BlockSpec((ROW_TILE, HEAD_DIM), lambda i: (i % tpb, 0)),
                  pl.BlockSpec((ROW_TILE, HEAD_DIM), lambda i: (i % tpb, 0)),
                  pl.BlockSpec((1, HEAD_DIM), lambda i: (0, 0)),
                  pl.BlockSpec((1, HEAD_DIM), lambda i: (0, 0))],
        out_specs=[pl.BlockSpec((ROW_TILE, ATT_WIDTH), lambda i: (i, 0)),
                   pl.BlockSpec((1, ATT_KV_WIDTH, ROW_TILE), lambda i: (i // tpb, 0, i % tpb)),
                   pl.BlockSpec((ROW_TILE, 2 * ATT_KV_WIDTH), lambda i: (i, 0))],
        out_shape=[jax.ShapeDtypeStruct((R, ATT_WIDTH), bf16),
                   jax.ShapeDtypeStruct((B, ATT_KV_WIDTH, T), bf16),
                   jax.ShapeDtypeStruct((R, 2 * ATT_KV_WIDTH), bf16)],
        compiler_params=_params("arbitrary"),
        name="qkv_prep",
    )(p, p, p, jnp.asarray(cos), jnp.asarray(sin), q_norm.reshape(1, HEAD_DIM), k_norm.reshape(1, HEAD_DIM))


ATT_CHUNK = 256
ATT_Q_TILE = 512


def _attention_kernel(k_lo, q_ref, kt_ref, v_ref, *rest):
    o_ref, qs_ref, sa_ref, sb_ref, pa_ref, pb_ref, os_ref = rest[-7:]
    tq = q_ref.shape[1]
    n_chunks = ATT_GROUP * tq // ATT_CHUNK
    for h in range(ATT_GROUP):
        qs_ref[h * tq:(h + 1) * tq, :] = q_ref[0, :, h * HEAD_DIM:(h + 1) * HEAD_DIM]

    rows = lambda c: slice(c * ATT_CHUNK, (c + 1) * ATT_CHUNK)
    s_refs = (sa_ref, sb_ref)
    p_refs = (pa_ref, pb_ref)

    def scores(c):
        s_refs[c % 2][:, k_lo:] = jnp.dot(qs_ref[rows(c), :], kt_ref[0, :, k_lo:], preferred_element_type=f32)

    def exponentials(c):
        s = s_refs[c % 2][:, k_lo:]
        m = jnp.max(s, axis=-1, keepdims=True)
        p_refs[c % 2][:, k_lo:] = jnp.exp2(s - m).astype(bf16)

    def weighted_values(c):
        r = jnp.dot(p_refs[c % 2][:, k_lo:], v_ref[0, k_lo:, :], preferred_element_type=f32)
        os_ref[rows(c), :] = r[:, :HEAD_DIM] / r[:, HEAD_DIM:HEAD_DIM + 1]

    scores(0)
    for c in range(n_chunks):
        if c + 1 < n_chunks:
            scores(c + 1)
        exponentials(c)
        if c >= 1:
            weighted_values(c - 1)
    weighted_values(n_chunks - 1)
    for h in range(ATT_GROUP):
        o_ref[0, :, h * HEAD_DIM:(h + 1) * HEAD_DIM] = os_ref[h * tq:(h + 1) * tq, :].astype(o_ref.dtype)


def attention(q, kt, v, B, n_lat, n_ctx):
    R = q.shape[0]
    T = n_lat + n_ctx
    gw = ATT_GROUP * HEAD_DIM
    q3 = q.reshape(B, T, ATT_WIDTH)
    v3 = v.reshape(B, T, 2 * ATT_KV_WIDTH)
    out_shape = jax.ShapeDtypeStruct((B, T, ATT_WIDTH), bf16)

    def call(name, k_lo, tq, q_tiles, first_tile, extra_in, extra_specs, aliases):
        qo_spec = pl.BlockSpec((1, tq, gw), lambda b, g, i: (b, first_tile + i, g))
        return pl.pallas_call(
            functools.partial(_attention_kernel, k_lo),
            grid=(B, ATT_KV_HEADS, q_tiles),
            in_specs=[qo_spec,
                      pl.BlockSpec((1, HEAD_DIM, T), lambda b, g, i: (b, g, 0)),
                      pl.BlockSpec((1, T, 2 * HEAD_DIM), lambda b, g, i: (b, 0, g))] + extra_specs,
            out_specs=qo_spec,
            out_shape=out_shape,
            scratch_shapes=[pltpu.VMEM((ATT_GROUP * tq, HEAD_DIM), bf16),
                            pltpu.VMEM((ATT_CHUNK, T), f32), pltpu.VMEM((ATT_CHUNK, T), f32),
                            pltpu.VMEM((ATT_CHUNK, T), bf16), pltpu.VMEM((ATT_CHUNK, T), bf16),
                            pltpu.VMEM((ATT_GROUP * tq, HEAD_DIM), f32)],
            input_output_aliases=aliases,
            compiler_params=_params("arbitrary", "arbitrary", "arbitrary"),
            name=name,
        )(q3, kt, v3, *extra_in)

    att = call("attention", 0, ATT_Q_TILE, n_lat // ATT_Q_TILE, 0, [], [], {})
    att = call("attention_ctx", n_lat, n_ctx, 1, n_lat // n_ctx, [att], [pl.BlockSpec(memory_space=pl.ANY)], {3: 0})
    return att.reshape(R, ATT_WIDTH)


def _post_residual(x, y, g_post, gate):
    yn = y * lax.rsqrt(jnp.mean(y * y, axis=-1, keepdims=True) + NORM_EPS) * g_post
    return x + gate * yn


def _silu(x):
    return x * jax.nn.sigmoid(x)


def _even_out_kernel(tiles_per_batch, x0_ref, g_ref, yc_ref, ghy_ref, att_ref, ga0_ref, ga1_ref, bias_ref, w_ref,
                     gp_ref, m_ref, x_ref, c_ref, gn_ref, mn_ref, o_ref, hn_ref):
    D = x_ref.shape[-1]
    g = g_ref[...]
    hy = x0_ref[...].astype(f32) * (yc_ref[...] + g * bias_ref[...]) * _silu(ghy_ref[...].astype(f32))
    g_att = jnp.concatenate([ga0_ref[...], ga1_ref[...]], axis=1).astype(f32)
    at = att_ref[...].astype(f32) * _silu(g_att)
    lhs = jnp.concatenate([hy, at], axis=1).astype(bf16)
    y = jnp.dot(lhs, w_ref[...], preferred_element_type=f32)
    x_new = _post_residual(_input_rows(tiles_per_batch, x_ref, c_ref), y, gp_ref[...], m_ref[0, :, 2 * D:3 * D])
    o_ref[...] = x_new
    hn_ref[...] = _norm_mod_rows(x_new, gn_ref[...], mn_ref[0]).astype(hn_ref.dtype)


def even_out(x0, g, yconv, p, att, hy_bias, w_out, g_post, mods, x_lat, x_ctx, next_g_pre, next_mods,
             tiles_per_batch, ctx_row):
    R = x0.shape[0]
    D = x_lat.shape[1]
    W = HY_WIDTH
    hw = ATT_WIDTH // 2
    row = lambda i: (i, 0)
    full = lambda i: (0, 0)
    lat_spec, ctx_spec = _input_row_specs(tiles_per_batch, D)
    return pl.pallas_call(
        functools.partial(_even_out_kernel, tiles_per_batch),
        grid=(R // ROW_TILE,),
        in_specs=[pl.BlockSpec((ROW_TILE, W), row),
                  pl.BlockSpec((ROW_TILE, W), row),
                  pl.BlockSpec((ROW_TILE, W), row),
                  pl.BlockSpec((ROW_TILE, W), lambda i: (i, EVEN_OFF_GHY // W)),
                  pl.BlockSpec((ROW_TILE, ATT_WIDTH), row),
                  pl.BlockSpec((ROW_TILE, hw), lambda i: (i, EVEN_OFF_GATT // hw)),
                  pl.BlockSpec((ROW_TILE, hw), lambda i: (i, EVEN_OFF_GATT // hw + 1)),
                  pl.BlockSpec((1, W), full),
                  pl.BlockSpec((W + ATT_WIDTH, D), full),
                  pl.BlockSpec((1, D), full),
                  pl.BlockSpec((1, 1, 3 * D), lambda i: (_mod_row(i, tiles_per_batch, ctx_row), 0, 0)),
                  lat_spec, ctx_spec,
                  pl.BlockSpec((1, D), full),
                  pl.BlockSpec((1, 1, 3 * D), lambda i: (_mod_row(i, tiles_per_batch, ctx_row), 0, 0))],
        out_specs=[pl.BlockSpec((ROW_TILE, D), row), pl.BlockSpec((ROW_TILE, D), row)],
        out_shape=[jax.ShapeDtypeStruct((R, D), f32), jax.ShapeDtypeStruct((R, D), bf16)],
        compiler_params=_params("arbitrary"),
        name="even_out",
    )(x0, g, yconv, p, att, p, p, hy_bias.reshape(1, W), w_out.astype(bf16), g_post.reshape(1, D), mods,
      x_lat, x_ctx, next_g_pre.reshape(1, D), next_mods)


def even_layer(x_lat, x_ctx, mods, next_g_pre, next_mods, B, n_lat, n_ctx, g_pre, g_post, w_in, conv_w, conv_b,
               f_w1, f_b1, f_freq, f_w2, f_b2, f_w3, hy_bias, q_norm, k_norm, w_out):
    T = n_lat + n_ctx
    tpb = T // ROW_TILE
    h = norm_mod(x_lat, x_ctx, g_pre, mods, tpb, B)
    p = matmul(h, w_in, tm=_proj_row_tile(B * T), tn=_proj_col_tile(w_in.shape[1]), out_dtype=bf16,
               name="even_in_proj")
    x0, g = hyena_pre(p, conv_w, conv_b, tpb)
    yconv = hyena_long_conv(g.reshape(B, T, HY_WIDTH), n_lat, n_ctx, f_w1, f_b1, f_freq, f_w2, f_b2, f_w3)
    q, kt, v = qkv_prep(p, q_norm, k_norm, B, n_lat, n_ctx)
    att = attention(q, kt, v, B, n_lat, n_ctx)
    return even_out(x0, g, yconv.reshape(B * T, HY_WIDTH), p, att, hy_bias, w_out, g_post, mods, x_lat, x_ctx,
                    next_g_pre, next_mods, tpb, B)


def _mlstm_prep_kernel(tiles_per_batch, q_ref, qp_ref, qn_ref, k_ref, kp_ref, kn_ref,
                       wq_ref, wk_ref, bq_ref, bk_ref, q_out, kt_out):
    first, last = _seq_edges(pl.program_id(0), tiles_per_batch)
    prev_row, next_row = _halo_rows(qp_ref, qn_ref, first, last)
    q = _silu(_conv3(q_ref[...].astype(f32), prev_row, next_row, wq_ref[...], bq_ref[...]))
    q_out[...] = q.astype(q_out.dtype)
    prev_row, next_row = _halo_rows(kp_ref, kn_ref, first, last)
    k = _silu(_conv3(k_ref[...].astype(f32), prev_row, next_row, wk_ref[...], bk_ref[...])) * (ML_QK ** -0.5)
    for h in range(k.shape[1] // ML_QK):
        sl = slice(h * ML_QK, (h + 1) * ML_QK)
        kt_out[0, sl, :] = k[:, sl].T.astype(kt_out.dtype)


def mlstm_prep(p, conv_w, conv_b, B, T, *, tc=1024):
    R = p.shape[0]
    tpb = T // ROW_TILE
    nb = ML_QK_WIDTH // tc
    qcol = lambda c: c
    kcol = lambda c: nb + c
    qprev, qnext = _halo_specs(qcol, tc, R)
    kprev, knext = _halo_specs(kcol, tc, R)
    return pl.pallas_call(
        functools.partial(_mlstm_prep_kernel, tpb),
        grid=(R // ROW_TILE, nb),
        in_specs=[pl.BlockSpec((ROW_TILE, tc), lambda i, c: (i, c)), qprev, qnext,
                  pl.BlockSpec((ROW_TILE, tc), lambda i, c: (i, nb + c)), kprev, knext,
                  pl.BlockSpec((3, tc), lambda i, c: (0, c)),
                  pl.BlockSpec((3, tc), lambda i, c: (0, nb + c)),
                  pl.BlockSpec((1, tc), lambda i, c: (0, c)),
                  pl.BlockSpec((1, tc), lambda i, c: (0, nb + c))],
        out_specs=[pl.BlockSpec((ROW_TILE, tc), lambda i, c: (i, c)),
                   pl.BlockSpec((1, tc, ROW_TILE), lambda i, c: (i // tpb, c, i % tpb))],
        out_shape=[jax.ShapeDtypeStruct((R, ML_QK_WIDTH), bf16),
                   jax.ShapeDtypeStruct((B, ML_QK_WIDTH, T), bf16)],
        compiler_params=_params("arbitrary", "arbitrary"),
        name="mlstm_prep",
    )(p, p, p, p, p, p, conv_w, conv_w, conv_b.reshape(1, -1), conv_b.reshape(1, -1))


def _log_sigmoid(x):
    return jnp.minimum(x, 0.0) - jnp.log(1.0 + jnp.exp(-jnp.abs(x)))


def _mlstm_gates_kernel(g_ref, b_ref, gc_out, gr_out):
    pre = g_ref[...] + b_ref[...]
    lane = lax.broadcasted_iota(jnp.int32, pre.shape, 1)
    is_forget = (lane // ML_HEADS) % 2 == 1
    gc = jnp.where(is_forget, _log_sigmoid(pre), pre)
    gc_out[...] = gc
    gr_out[0] = gc.T


def mlstm_gates(gates, gate_b, B, T):
    R = gates.shape[0]
    gb = jnp.pad(gate_b, (0, LANE - gate_b.shape[0])).reshape(1, LANE)
    return pl.pallas_call(
        _mlstm_gates_kernel,
        grid=(B,),
        in_specs=[pl.BlockSpec((T, LANE), lambda b: (b, 0)),
                  pl.BlockSpec((1, LANE), lambda b: (0, 0))],
        out_specs=[pl.BlockSpec((T, LANE), lambda b: (b, 0)),
                   pl.BlockSpec((1, LANE, T), lambda b: (b, 0, 0))],
        out_shape=[jax.ShapeDtypeStruct((R, LANE), f32), jax.ShapeDtypeStruct((B, LANE, T), f32)],
        compiler_params=_params("arbitrary"),
        name="mlstm_gates",
    )(gates, gb)


def _mlstm_chunk_setup(reverse, gc_ref, gr_ref, m_ref, ms_ref):
    Lc = gc_ref.shape[0]
    H = ML_HEADS
    i_off = 2 * H if reverse else 0
    f_off = i_off + H
    t_idx = lax.broadcasted_iota(jnp.int32, (Lc, Lc), 0)
    s_idx = lax.broadcasted_iota(jnp.int32, (Lc, Lc), 1)
    causal = (s_idx >= t_idx) if reverse else (s_idx <= t_idx)
    tri = causal.astype(f32)
    gc = gc_ref[...]
    gr = gr_ref[0]
    b_col_all = jnp.dot(tri, gc[:, f_off:f_off + H], preferred_element_type=f32, precision=HIGHEST)
    b_row_all = lax.dot_general(gr[f_off:f_off + H, :], tri, (((1,), (1,)), ((), ())),
                                preferred_element_type=f32, precision=HIGHEST)
    i_col_all = gc[:, i_off:i_off + H]
    i_row_all = gr[i_off:i_off + H, :]
    end = 0 if reverse else Lc - 1
    b_end = b_col_all[end:end + 1, :]
    m_prev = m_ref[0:1, 0:H]
    g_col = b_end - b_col_all + i_col_all
    m_new = jnp.maximum(b_end + m_prev, jnp.max(g_col, axis=0, keepdims=True))
    a_prev = jnp.exp(b_end + m_prev - m_new)
    m_ref[:, 0:H] = jnp.broadcast_to(m_new, (m_ref.shape[0], H))
    b_end_s = b_row_all[:, end:end + 1]
    m_prev_s = ms_ref[0:H, 0:1]
    g_row = b_end_s - b_row_all + i_row_all
    m_new_s = jnp.maximum(b_end_s + m_prev_s, jnp.max(g_row, axis=1, keepdims=True))
    a_row = jnp.exp(g_row - m_new_s)
    ms_ref[0:H, :] = jnp.broadcast_to(m_new_s, (H, ms_ref.shape[1]))
    return dict(causal=causal, i_row=i_row_all, b_col=b_col_all, b_row=b_row_all,
                m_prev=m_prev, a_row=a_row, a_prev=a_prev)


class _MlstmChain:
    def __init__(self, h, cs, q_ref, kt_ref, v_ref, o_ref, ct_ref):
        self.h, self.cs = h, cs
        self.q_ref, self.kt_ref, self.v_ref, self.o_ref, self.ct_ref = q_ref, kt_ref, v_ref, o_ref, ct_ref

    def _q(self):
        return self.q_ref[:, self.h * ML_QK:(self.h + 1) * ML_QK]

    def _kt(self):
        return self.kt_ref[0, self.h * ML_QK:(self.h + 1) * ML_QK, :]

    def _v(self):
        Lc = self.q_ref.shape[0]
        ones_col = (lax.broadcasted_iota(jnp.int32, (Lc, LANE), 1) == 0).astype(bf16)
        return jnp.concatenate([self.v_ref[:, self.h * ML_V:(self.h + 1) * ML_V], ones_col], axis=1)

    def scores(self):
        self.qk = jnp.dot(self._q(), self._kt(), preferred_element_type=f32)

    def gates(self):
        h, cs = self.h, self.cs
        i_row = cs["i_row"][h:h + 1, :]
        b_col = cs["b_col"][:, h:h + 1]
        b_row = cs["b_row"][h:h + 1, :]
        m_prev = cs["m_prev"][:, h:h + 1]
        d = jnp.where(cs["causal"], b_col + (i_row - b_row), -jnp.inf)
        inter = b_col + m_prev
        self.m_row = jnp.maximum(inter, jnp.max(d, axis=-1, keepdims=True))
        self.s = (self.qk * jnp.exp(d - self.m_row)).astype(bf16)
        self.w_prev = jnp.exp(inter - self.m_row)

    def values(self):
        h = self.h
        self.ct = self.ct_ref[h]
        qw = (self._q().astype(f32) * self.w_prev).astype(bf16)
        tot = jnp.dot(jnp.concatenate([self.s, qw], axis=1),
                      jnp.concatenate([self._v(), self.ct.astype(bf16)], axis=0),
                      preferred_element_type=f32)
        scale = 1.0 / jnp.maximum(jnp.abs(tot[:, ML_V:ML_V + 1]), jnp.exp(-self.m_row))
        self.o_ref[:, h * ML_V:(h + 1) * ML_V] = (tot[:, :ML_V] * scale).astype(self.o_ref.dtype)

    def update(self):
        h = self.h
        kta = (self._kt().astype(f32) * self.cs["a_row"][h:h + 1, :]).astype(bf16)
        self.ct_ref[h] = (self.cs["a_prev"][:, h:h + 1] * self.ct
                          + jnp.dot(kta, self._v(), preferred_element_type=f32))


def _mlstm_scan_kernel(qf_ref, ktf_ref, vf_ref, gcf_ref, grf_ref, qb_ref, ktb_ref, vb_ref, gcb_ref, grb_ref,
                       of_ref, ob_ref, ctf_ref, mf_ref, msf_ref, ctb_ref, mb_ref, msb_ref):
    @pl.when(pl.program_id(1) == 0)
    def _():
        for ref in (ctf_ref, mf_ref, msf_ref, ctb_ref, mb_ref, msb_ref):
            ref[...] = jnp.zeros_like(ref)

    fwd = _mlstm_chunk_setup(False, gcf_ref, grf_ref, mf_ref, msf_ref)
    bwd = _mlstm_chunk_setup(True, gcb_ref, grb_ref, mb_ref, msb_ref)
    chains = []
    for h in range(ML_HEADS):
        chains.append(_MlstmChain(h, fwd, qf_ref, ktf_ref, vf_ref, of_ref, ctf_ref))
        chains.append(_MlstmChain(h, bwd, qb_ref, ktb_ref, vb_ref, ob_ref, ctb_ref))
    stages = ("scores", "gates", "values", "update")
    for k in range(len(chains) + len(stages) - 1):
        for depth, stage in enumerate(stages):
            if 0 <= k - depth < len(chains):
                getattr(chains[k - depth], stage)()


def mlstm_scan(q, kt, p, gc, gr, B, n_lat, n_ctx):
    R = q.shape[0]
    Lc = ML_CHUNK
    tpb = (n_lat + n_ctx) // Lc
    lat = n_lat // Lc
    ctx = n_ctx // Lc

    def specs(reverse):
        def chunk(j):
            if reverse:
                return tpb - 1 - j
            return jnp.where(j < ctx, lat + j, j - ctx)
        ins = [pl.BlockSpec((Lc, ML_QK_WIDTH), lambda b, j: (b * tpb + chunk(j), 0)),
               pl.BlockSpec((1, ML_QK_WIDTH, Lc), lambda b, j: (b, 0, chunk(j))),
               pl.BlockSpec((Lc, ML_WIDTH), lambda b, j: (b * tpb + chunk(j), ODD_OFF_V // ML_WIDTH)),
               pl.BlockSpec((Lc, LANE), lambda b, j: (b * tpb + chunk(j), 0)),
               pl.BlockSpec((1, LANE, Lc), lambda b, j: (b, 0, chunk(j)))]
        out = pl.BlockSpec((Lc, ML_WIDTH), lambda b, j: (b * tpb + chunk(j), 0))
        return ins, out

    ins_f, out_f = specs(False)
    ins_b, out_b = specs(True)
    state = [pltpu.VMEM((ML_HEADS, ML_QK, ML_V + LANE), f32), pltpu.VMEM((SUB, LANE), f32),
             pltpu.VMEM((SUB, LANE), f32)]
    return pl.pallas_call(
        _mlstm_scan_kernel,
        grid=(B, tpb),
        in_specs=ins_f + ins_b,
        out_specs=[out_f, out_b],
        out_shape=[jax.ShapeDtypeStruct((R, ML_WIDTH), bf16), jax.ShapeDtypeStruct((R, ML_WIDTH), bf16)],
        scratch_shapes=state + state,
        compiler_params=_params("arbitrary", "arbitrary"),
        name="mlstm_scan",
    )(q, kt, p, gc, gr, q, kt, p, gc, gr)


def _odd_out_kernel(hf_ref, hb_ref, o_ref, z_ref, hn_ref, w_ref, gp_ref, m_ref, x_ref, out_ref):
    D = x_ref.shape[-1]
    hs = (hf_ref[0].astype(f32) + hb_ref[0].astype(f32)) * jax.nn.sigmoid(o_ref[0].astype(f32))
    segs = [hs[:, h * ML_V:(h + 1) * ML_V] for h in range(ML_HEADS)]
    ms = [jnp.mean(seg * seg, axis=-1, keepdims=True) for seg in segs]
    parts = [seg * lax.rsqrt(m + NORM_EPS) for seg, m in zip(segs, ms)]
    hn = jnp.concatenate(parts, axis=1) * hn_ref[...] * _silu(z_ref[0].astype(f32))
    y = jnp.dot(hn.astype(bf16), w_ref[...], preferred_element_type=f32)
    out_ref[0] = _post_residual(x_ref[0], y, gp_ref[...], m_ref[0, :, 2 * D:3 * D])


def odd_out(hf, hb, p, head_norm, w_out, g_post, mods, x_all, B, n_lat, T, *, tm=512):
    D = x_all.shape[1]
    view = lambda a: a.reshape(B, T, a.shape[1])
    row = lambda b, i: (b, i, 0)
    full = lambda b, i: (0, 0)
    return pl.pallas_call(
        _odd_out_kernel,
        grid=(B, n_lat // tm),
        in_specs=[pl.BlockSpec((1, tm, ML_WIDTH), row),
                  pl.BlockSpec((1, tm, ML_WIDTH), row),
                  pl.BlockSpec((1, tm, ML_WIDTH), lambda b, i: (b, i, ODD_OFF_O // ML_WIDTH)),
                  pl.BlockSpec((1, tm, ML_WIDTH), lambda b, i: (b, i, ODD_OFF_Z // ML_WIDTH)),
                  pl.BlockSpec((1, ML_WIDTH), full),
                  pl.BlockSpec((ML_WIDTH, D), full),
                  pl.BlockSpec((1, D), full),
                  pl.BlockSpec((1, 1, 3 * D), lambda b, i: (b, 0, 0)),
                  pl.BlockSpec((1, tm, D), row)],
        out_specs=pl.BlockSpec((1, tm, D), row),
        out_shape=jax.ShapeDtypeStruct((B, n_lat, D), f32),
        compiler_params=_params("arbitrary", "arbitrary"),
        name="odd_out",
    )(view(hf), view(hb), view(p), view(p), head_norm.reshape(1, ML_WIDTH), w_out.astype(bf16),
      g_post.reshape(1, D), mods, view(x_all))


def odd_layer_last(x_all, h, mods, B, n_lat, n_ctx, g_post, w_in, conv_w, conv_b, gate_b, head_norm, w_out):
    T = n_lat + n_ctx
    tm = _proj_row_tile(B * T)
    p = matmul(h, w_in, tm=tm, tn=_proj_col_tile(ODD_MAIN), n_cols=ODD_MAIN, out_dtype=bf16, name="odd_in_proj")
    n_gates = w_in.shape[1] - ODD_MAIN
    gates = matmul(h, jnp.pad(w_in[:, ODD_MAIN:], ((0, 0), (0, LANE - n_gates))), tm=tm, tn=LANE, name="odd_gate_proj")
    q, kt = mlstm_prep(p, conv_w, conv_b, B, T)
    gc, gr = mlstm_gates(gates, gate_b, B, T)
    hf, hb = mlstm_scan(q, kt, p, gc, gr, B, n_lat, n_ctx)
    return odd_out(hf, hb, p, head_norm, w_out, g_post, mods, x_all, B, n_lat, T)


def kernel(x, c, ctx, c_ctx, w_mod, b_mod, g_pre, g_post, e_w_in, e_conv_w, e_conv_b, e_filt_w1,
           e_filt_b1, e_filt_freq, e_filt_w2, e_filt_b2, e_filt_w3, e_hy_bias, e_q_norm, e_k_norm,
           e_w_out, o_w_in, o_conv_w, o_conv_b, o_gate_b, o_head_norm, o_w_out):
    B, n_lat, D = x.shape
    n_ctx = ctx.shape[1]
    T = n_lat + n_ctx
    depth = w_mod.shape[0]
    assert depth == 2 and B + 1 <= 8 and n_ctx == ROW_TILE and n_lat % ROW_TILE == 0
    cond = jnp.concatenate([c, c_ctx[None], jnp.zeros((8 - B - 1, D), f32)], axis=0)
    mods_all = adaln_all(cond, w_mod, b_mod)
    mods0 = mods_all[0].reshape(8, 1, 3 * D)
    mods1 = mods_all[1].reshape(8, 1, 3 * D)
    x_all, h1 = even_layer(x.reshape(B * n_lat, D), ctx.reshape(B * n_ctx, D), mods0, g_pre[1], mods1,
                           B, n_lat, n_ctx, g_pre[0], g_post[0], e_w_in[0],
                           e_conv_w[0], e_conv_b[0], e_filt_w1[0], e_filt_b1[0], e_filt_freq[0], e_filt_w2[0],
                           e_filt_b2[0], e_filt_w3[0], e_hy_bias[0], e_q_norm[0], e_k_norm[0], e_w_out[0])
    out = odd_layer_last(x_all, h1, mods1, B, n_lat, n_ctx, g_post[1], o_w_in[0],
                         o_conv_w[0], o_conv_b[0], o_gate_b[0], o_head_norm[0], o_w_out[0])
    return out.reshape(B, n_lat, D)
```

```python
import functools
import math

import numpy as np
import jax
import jax.numpy as jnp
from jax import lax
from jax.experimental import pallas as pl
from jax.experimental.pallas import tpu as pltpu

f32 = jnp.float32
bf16 = jnp.bfloat16
HIGHEST = lax.Precision.HIGHEST

D_MODEL = 1024
GRID_W = 64
NORM_EPS = 1e-6

HY_WIDTH = 1024
HY_EMB = 33
HY_BANDS = (HY_EMB - 1) // 2
HY_HIDDEN = 64
HY_TARGET = 1e-2
HY_SHORT_DECAY_PCT = 0.3
HY_LONG_DECAY_PCT = 1.5

ATT_HEADS = 8
ATT_KV_HEADS = 2
ATT_GROUP = ATT_HEADS // ATT_KV_HEADS
HEAD_DIM = 128
ATT_WIDTH = ATT_HEADS * HEAD_DIM
ATT_KV_WIDTH = ATT_KV_HEADS * HEAD_DIM
ROPE_THETA = 10000.0
EVEN_OFF_XV = 0
EVEN_OFF_GHY = 3 * HY_WIDTH
EVEN_OFF_Q = EVEN_OFF_GHY + HY_WIDTH
EVEN_OFF_K = EVEN_OFF_Q + ATT_WIDTH
EVEN_OFF_V = EVEN_OFF_K + ATT_KV_WIDTH
EVEN_OFF_GATT = EVEN_OFF_V + ATT_KV_WIDTH
EVEN_IN = EVEN_OFF_GATT + ATT_WIDTH

ML_HEADS = 8
ML_QK = 128
ML_V = 256
ML_QK_WIDTH = ML_HEADS * ML_QK
ML_WIDTH = ML_HEADS * ML_V
ODD_OFF_Q = 0
ODD_OFF_K = ML_QK_WIDTH
ODD_OFF_V = 2 * ML_QK_WIDTH
ODD_OFF_O = ODD_OFF_V + ML_WIDTH
ODD_OFF_Z = ODD_OFF_O + ML_WIDTH
ODD_OFF_GATES = ODD_OFF_Z + ML_WIDTH
ODD_MAIN = ODD_OFF_GATES

ROW_TILE = 256
ML_CHUNK = 256
LANE = 128
VMEM_LIMIT_BYTES = 48 * 1024 * 1024

FFT_L1 = 64
FFT_L2 = 128


def _params(*sem):
    return pltpu.CompilerParams(dimension_semantics=sem, vmem_limit_bytes=VMEM_LIMIT_BYTES)


def _adaln_kernel(c_ref, w_ref, b_ref, o_ref):
    c = c_ref[...]
    s = c * jax.nn.sigmoid(c)
    o_ref[0] = jnp.dot(s, w_ref[0], preferred_element_type=f32, precision=HIGHEST) + b_ref[0]


def adaln_all(cond, w_mod, b_mod, *, tn=768):
    depth, D, N = w_mod.shape
    return pl.pallas_call(
        _adaln_kernel,
        grid=(depth, N // tn),
        in_specs=[
            pl.BlockSpec((8, D), lambda l, j: (0, 0)),
            pl.BlockSpec((1, D, tn), lambda l, j: (l, 0, j)),
            pl.BlockSpec((1, 1, tn), lambda l, j: (l, 0, j)),
        ],
        out_specs=pl.BlockSpec((1, 8, tn), lambda l, j: (l, 0, j)),
        out_shape=jax.ShapeDtypeStruct((depth, 8, N), f32),
        compiler_params=_params("arbitrary", "arbitrary"),
        name="adaln",
    )(cond, w_mod, b_mod.reshape(depth, 1, N))


def _mod_row(i, tiles_per_batch, ctx_row):
    lat_tiles = tiles_per_batch - 1
    return jnp.where(i % tiles_per_batch == lat_tiles, ctx_row, i // tiles_per_batch)


def _norm_mod_rows(x, g, m):
    D = x.shape[-1]
    y = x * lax.rsqrt(jnp.mean(x * x, axis=-1, keepdims=True) + NORM_EPS)
    return y * g * (1.0 + m[:, D:2 * D]) + m[:, 0:D]


def _input_row_specs(tiles_per_batch, D):
    lat_tiles = tiles_per_batch - 1

    def lat_index(i):
        return ((i // tiles_per_batch) * lat_tiles + jnp.minimum(i % tiles_per_batch, lat_tiles - 1), 0)

    return (pl.BlockSpec((ROW_TILE, D), lat_index),
            pl.BlockSpec((ROW_TILE, D), lambda i: (i // tiles_per_batch, 0)))


def _input_rows(tiles_per_batch, lat_ref, ctx_ref):
    is_ctx = pl.program_id(0) % tiles_per_batch == tiles_per_batch - 1
    return jnp.where(is_ctx, ctx_ref[...], lat_ref[...])


def _norm_mod_kernel(tiles_per_batch, x_ref, c_ref, g_ref, m_ref, o_ref):
    x = _input_rows(tiles_per_batch, x_ref, c_ref)
    o_ref[...] = _norm_mod_rows(x, g_ref[...], m_ref[0]).astype(o_ref.dtype)


def norm_mod(x_lat, x_ctx, g, mods, tiles_per_batch, ctx_row):
    D = x_lat.shape[1]
    R = x_lat.shape[0] + x_ctx.shape[0]
    lat_spec, ctx_spec = _input_row_specs(tiles_per_batch, D)
    return pl.pallas_call(
        functools.partial(_norm_mod_kernel, tiles_per_batch),
        grid=(R // ROW_TILE,),
        in_specs=[
            lat_spec, ctx_spec,
            pl.BlockSpec((1, D), lambda i: (0, 0)),
            pl.BlockSpec((1, 1, 3 * D), lambda i: (_mod_row(i, tiles_per_batch, ctx_row), 0, 0)),
        ],
        out_specs=pl.BlockSpec((ROW_TILE, D), lambda i: (i, 0)),
        out_shape=jax.ShapeDtypeStruct((R, D), bf16),
        compiler_params=_params("arbitrary"),
        name="norm_mod",
    )(x_lat, x_ctx, g.reshape(1, D), mods)


def _matmul_kernel(a_ref, b_ref, o_ref, bq_ref):
    @pl.when(pl.program_id(1) == 0)
    def _():
        bq_ref[...] = b_ref[...].astype(bq_ref.dtype)

    o_ref[...] = jnp.dot(a_ref[...], bq_ref[...], preferred_element_type=f32).astype(o_ref.dtype)


def _proj_row_tile(rows):
    return next(t for t in (1024, 512, ROW_TILE) if rows % t == 0)


def _proj_col_tile(cols, n_tiles=4):
    groups = cols // LANE
    return next(g for g in range(groups // n_tiles, 0, -1) if groups % g == 0) * LANE


def matmul(a, b, *, tm, tn, n_cols=None, out_dtype=f32, name="matmul"):
    M, K = a.shape
    N = b.shape[-1] if n_cols is None else n_cols
    assert M % tm == 0 and N % tn == 0, (M, N, tm, tn)
    b_spec = (pl.BlockSpec((K, tn), lambda j, i: (0, j)) if b.ndim == 2 else
              pl.BlockSpec((None, K, tn), lambda j, i: (0, 0, j)))
    return pl.pallas_call(
        _matmul_kernel,
        grid=(N // tn, M // tm),
        in_specs=[pl.BlockSpec((tm, K), lambda j, i: (i, 0)), b_spec],
        out_specs=pl.BlockSpec((tm, tn), lambda j, i: (i, j)),
        out_shape=jax.ShapeDtypeStruct((M, N), out_dtype),
        scratch_shapes=[pltpu.VMEM((K, tn), a.dtype)],
        compiler_params=_params("arbitrary", "arbitrary"),
        name=name,
    )(a, b)


def _conv3(x, prev_row, next_row, w, b):
    tm = x.shape[0]
    row = lax.broadcasted_iota(jnp.int32, x.shape, 0)
    xm = jnp.where(row == 0, prev_row, pltpu.roll(x, 1, 0))
    xp = jnp.where(row == tm - 1, next_row, pltpu.roll(x, tm - 1, 0))
    return w[0:1] * xm + w[1:2] * x + w[2:3] * xp + b


def _seq_edges(i, tiles_per_batch):
    r = i % tiles_per_batch
    lat_tiles = tiles_per_batch - 1
    first = jnp.logical_or(r == 0, r == lat_tiles)
    last = jnp.logical_or(r == lat_tiles - 1, r == lat_tiles)
    return first, last


HALO = 16


def _halo_rows(prev_ref, next_ref, first, last):
    prev_row = jnp.where(first, 0.0, prev_ref[...].astype(f32)[HALO - 1:HALO, :])
    next_row = jnp.where(last, 0.0, next_ref[...].astype(f32)[0:1, :])
    return prev_row, next_row


def _halo_specs(col_block, tc, n_rows):
    per = ROW_TILE // HALO
    n_blocks = n_rows // HALO
    prev = pl.BlockSpec((HALO, tc), lambda i, c: (jnp.maximum(i * per - 1, 0), col_block(c)))
    nxt = pl.BlockSpec((HALO, tc), lambda i, c: (jnp.minimum((i + 1) * per, n_blocks - 1), col_block(c)))
    return prev, nxt


def _hyena_pre_kernel(tiles_per_batch, x0_ref, x0p_ref, x0n_ref, x1_ref, x1p_ref, x1n_ref,
                      v_ref, vp_ref, vn_ref, w0_ref, w1_ref, w2_ref, b0_ref, b1_ref, b2_ref,
                      x0_out, g_out):
    first, last = _seq_edges(pl.program_id(0), tiles_per_batch)

    def conv(x_ref, p_ref, n_ref, w_ref, b_ref):
        prev_row, next_row = _halo_rows(p_ref, n_ref, first, last)
        return _conv3(x_ref[...].astype(f32), prev_row, next_row, w_ref[...], b_ref[...])

    x0 = conv(x0_ref, x0p_ref, x0n_ref, w0_ref, b0_ref)
    x1 = conv(x1_ref, x1p_ref, x1n_ref, w1_ref, b1_ref)
    v = conv(v_ref, vp_ref, vn_ref, w2_ref, b2_ref)
    g = v * x1
    x0_out[...] = x0.astype(x0_out.dtype)
    g_out[...] = g


def hyena_pre(p, conv_w, conv_b, tiles_per_batch, *, tc=1024):
    R = p.shape[0]
    W = HY_WIDTH
    nb = W // tc
    specs = []
    for part in range(3):
        col = functools.partial(lambda c, part: part * nb + c, part=part)
        main = pl.BlockSpec((ROW_TILE, tc), functools.partial(lambda i, c, col: (i, col(c)), col=col))
        prev, nxt = _halo_specs(col, tc, R)
        specs += [main, prev, nxt]
    wspecs = [pl.BlockSpec((3, tc), functools.partial(lambda i, c, part: (0, part * nb + c), part=part))
              for part in range(3)]
    bspecs = [pl.BlockSpec((1, tc), functools.partial(lambda i, c, part: (0, part * nb + c), part=part))
              for part in range(3)]
    out_spec = pl.BlockSpec((ROW_TILE, tc), lambda i, c: (i, c))
    args = [p] * 9 + [conv_w] * 3 + [conv_b.reshape(1, -1)] * 3
    return pl.pallas_call(
        functools.partial(_hyena_pre_kernel, tiles_per_batch),
        grid=(R // ROW_TILE, nb),
        in_specs=specs + wspecs + bspecs,
        out_specs=[out_spec, out_spec],
        out_shape=[jax.ShapeDtypeStruct((R, W), bf16), jax.ShapeDtypeStruct((R, W), f32)],
        compiler_params=_params("arbitrary", "arbitrary"),
        name="hyena_pre",
    )(*args)


def _filter_kernel(n, rows, bands_ref, w1t_ref, w1c_ref, w1s_ref, b1_ref, fr_ref, w2_ref, b2_ref,
                   w3_ref, dl_ref, o_ref, nrm_ref):
    step = pl.program_id(0)

    def offsets(shape, axis):
        j = step * rows + lax.broadcasted_iota(jnp.int32, shape, axis)
        return jnp.where(j < n, j, 2 * n - j).astype(f32), j != n

    d_row, _ = offsets((1, rows), 1)
    t_row = d_row / float(n - 1)
    ang = (2.0 * math.pi / n) * bands_ref[...] * d_row
    fr = fr_ref[...]
    z1 = (w1t_ref[...] * t_row
          + jnp.dot(w1c_ref[...], jnp.cos(ang), preferred_element_type=f32, precision=HIGHEST)
          - jnp.dot(w1s_ref[...], jnp.sin(ang), preferred_element_type=f32, precision=HIGHEST)
          + b1_ref[...])
    hdn = jnp.sin(fr * z1)
    hdn = jnp.sin(fr * (jnp.dot(w2_ref[...], hdn, preferred_element_type=f32, precision=HIGHEST) + b2_ref[...]))
    h = jnp.dot(hdn.T.astype(bf16), w3_ref[0].astype(bf16), preferred_element_type=f32)
    d_col, valid = offsets((rows, 1), 0)
    t = d_col / float(n - 1)
    h = h * jnp.exp(-t * jnp.abs(dl_ref[...]))
    h = jnp.where(valid, h, 0.0)
    o_ref[...] = h

    @pl.when(step == 0)
    def _():
        nrm_ref[...] = jnp.zeros_like(nrm_ref)

    nrm_ref[...] += jnp.sum(jnp.abs(h), axis=0, keepdims=True)


def hyena_filter(n, w1, b1, freq, w2, b2, w3, *, rows=256):
    W = HY_WIDTH
    Hd = HY_HIDDEN
    bands = jnp.linspace(1e-4, HY_BANDS - 1, HY_BANDS, dtype=f32).reshape(HY_BANDS, 1)
    max_decay = math.log(HY_TARGET) / HY_SHORT_DECAY_PCT
    min_decay = math.log(HY_TARGET) / HY_LONG_DECAY_PCT
    deltas = jnp.linspace(min_decay, max_decay, W, dtype=f32).reshape(1, W)
    steps = 2 * n // rows
    half_steps = n // rows
    full = lambda s: (0, 0)
    w3r = w3.reshape(Hd, 2, W).transpose(1, 0, 2)
    return pl.pallas_call(
        functools.partial(_filter_kernel, n, rows),
        grid=(steps,),
        in_specs=[
            pl.BlockSpec((HY_BANDS, 1), full),
            pl.BlockSpec((Hd, 1), full),
            pl.BlockSpec((Hd, HY_BANDS), full),
            pl.BlockSpec((Hd, HY_BANDS), full),
            pl.BlockSpec((Hd, 1), full),
            pl.BlockSpec((Hd, 1), full),
            pl.BlockSpec((Hd, Hd), full),
            pl.BlockSpec((Hd, 1), full),
            pl.BlockSpec((1, Hd, W), lambda s: (jnp.where(s * rows < n, 0, 1), 0, 0)),
            pl.BlockSpec((1, W), full),
        ],
        out_specs=[pl.BlockSpec((rows, W), lambda s: (s, 0)), pl.BlockSpec((1, W), full)],
        out_shape=[jax.ShapeDtypeStruct((2 * n, W), f32), jax.ShapeDtypeStruct((1, W), f32)],
        compiler_params=_params("arbitrary"),
        name="hyena_filter",
    )(bands, w1[0:1].T, w1[1:1 + HY_BANDS].T, w1[1 + HY_BANDS:].T, b1.reshape(Hd, 1), freq.reshape(Hd, 1),
      w2.T, b2.reshape(Hd, 1), w3r, deltas)


@functools.lru_cache(maxsize=None)
def _fft_constants():
    L1, L2 = FFT_L1, FFT_L2
    L = L1 * L2
    k1 = np.arange(L1)
    nh = np.arange(L1 // 2)
    th = 2.0 * np.pi * np.outer(k1, nh) / L1
    m1r = np.concatenate([np.cos(th), -np.sin(th)], axis=0)
    m1i = np.concatenate([np.sin(th), np.cos(th)], axis=0)
    thf = 2.0 * np.pi * np.outer(k1, np.arange(L1)) / L1
    m1f = np.concatenate([np.cos(thf), -np.sin(thf)], axis=0)
    n2 = np.arange(L2)
    k2 = np.arange(L2)
    m = (k1[:, None, None] * n2[None, None, :] + L1 * k2[None, :, None] * n2[None, None, :]) % L
    ph = 2.0 * np.pi * m / L
    gr, gi = np.cos(ph), -np.sin(ph)
    g = np.concatenate([np.concatenate([gr, -gi], axis=2), np.concatenate([gi, gr], axis=2)], axis=1)
    gt = np.transpose(g, (0, 2, 1))
    thi = 2.0 * np.pi * np.outer(nh, k1) / L1
    m3r = np.concatenate([np.cos(thi), np.sin(thi)], axis=0)
    m3i = np.concatenate([-np.sin(thi), np.cos(thi)], axis=0)
    cast = lambda a: np.asarray(a, dtype=np.float32)
    return dict(m1r=cast(m1r), m1i=cast(m1i), m1f=cast(m1f), g=cast(g), gt=cast(gt), m3r=cast(m3r), m3i=cast(m3i))


@functools.lru_cache(maxsize=None)
def _ctx_dft_constants(n):
    L = 2 * n
    k = np.arange(L)
    th = 2.0 * np.pi * np.outer(k, np.arange(n)) / L
    c, s = np.cos(th), np.sin(th)
    mc = np.concatenate([np.concatenate([c, s], axis=1), np.concatenate([-s, c], axis=1)], axis=0)
    thf = 2.0 * np.pi * np.outer(k, np.arange(L)) / L
    mf = np.concatenate([np.cos(thf), -np.sin(thf)], axis=0)
    ct, st = c.T, s.T
    minv = np.concatenate([np.concatenate([ct, -st], axis=1), np.concatenate([st, ct], axis=1)], axis=0)
    cast = lambda a: np.asarray(a, dtype=np.float32)
    return dict(mc=cast(mc), mf=cast(mf), minv=cast(minv))


def _bf16_constants(consts):
    return {k: jnp.asarray(v).astype(bf16) for k, v in consts.items()}


SUB = 8
N2C = 32


def _store_step1(o_ref, j, a):
    for ch in range(a.shape[0] // SUB):
        o_ref[0, ch, j] = a[ch * SUB:(ch + 1) * SUB]


def _fft1_kernel(re_ref, im_ref, mr_ref, mi_ref, o_ref):
    base = pl.program_id(2) * N2C
    for j in range(N2C):
        rows = pl.ds(base + j, FFT_L1 // 2, stride=FFT_L2)
        _store_step1(o_ref, j,
                     jnp.dot(mr_ref[...], re_ref[0, rows, :].astype(bf16), preferred_element_type=f32)
                     + jnp.dot(mi_ref[...], im_ref[0, rows, :].astype(bf16), preferred_element_type=f32))


def _step1_out(P, W, index):
    groups = 2 * FFT_L1 // SUB
    spec = pl.BlockSpec((1, groups, N2C, SUB, LANE), index)
    return spec, jax.ShapeDtypeStruct((P, groups, FFT_L2, SUB, W), f32)


def fft_step1(g3, m1r, m1i, n_lat):
    B, _, W = g3.shape
    half = FFT_L1 // 2
    out_spec, out_shape = _step1_out(B // 2, W, lambda p, c, k: (p, 0, k, 0, c))
    return pl.pallas_call(
        _fft1_kernel,
        grid=(B // 2, W // LANE, FFT_L2 // N2C),
        in_specs=[pl.BlockSpec((1, n_lat, LANE), lambda p, c, k: (2 * p, 0, c)),
                  pl.BlockSpec((1, n_lat, LANE), lambda p, c, k: (2 * p + 1, 0, c)),
                  pl.BlockSpec((2 * FFT_L1, half), lambda p, c, k: (0, 0)),
                  pl.BlockSpec((2 * FFT_L1, half), lambda p, c, k: (0, 0))],
        out_specs=out_spec,
        out_shape=out_shape,
        compiler_params=_params("arbitrary", "arbitrary", "arbitrary"),
        name="fft_step1",
    )(g3, g3, m1r, m1i)


def _fft1_filter_kernel(x_ref, m_ref, o_ref):
    base = pl.program_id(1) * N2C
    for j in range(N2C):
        rows = pl.ds(base + j, FFT_L1, stride=FFT_L2)
        _store_step1(o_ref, j, jnp.dot(m_ref[...], x_ref[rows, :].astype(bf16), preferred_element_type=f32))


def fft_step1_filter(ts, m1f):
    L, W = ts.shape
    out_spec, out_shape = _step1_out(1, W, lambda c, k: (0, 0, k, 0, c))
    return pl.pallas_call(
        _fft1_filter_kernel,
        grid=(W // LANE, FFT_L2 // N2C),
        in_specs=[pl.BlockSpec((L, LANE), lambda c, k: (0, c)),
                  pl.BlockSpec((2 * FFT_L1, FFT_L1), lambda c, k: (0, 0))],
        out_specs=out_spec,
        out_shape=out_shape,
        compiler_params=_params("arbitrary", "arbitrary"),
        name="fft_step1_filter",
    )(ts, m1f)


def _cmul(yr, yi, hr, hi):
    return yr * hr - yi * hi, yr * hi + yi * hr


def _step1_column(a_refs, j):
    col = lambda ref: ref[0, pl.ds(j, FFT_L2, stride=SUB), :]
    re = jnp.concatenate([col(a_refs[0]), col(a_refs[1])], axis=1)
    im = jnp.concatenate([col(a_refs[2]), col(a_refs[3])], axis=1)
    return jnp.concatenate([re, im], axis=0).astype(bf16)


def _fft2_filter_kernel(a0_ref, a1_ref, a2_ref, a3_ref, g_ref, nrm_ref, o_ref):
    scale = 1.0 / (nrm_ref[...] * float(FFT_L1 * FFT_L2))
    for j in range(SUB):
        a = _step1_column((a0_ref, a1_ref, a2_ref, a3_ref), j)
        o_ref[j] = (jnp.dot(g_ref[j], a, preferred_element_type=f32) * scale).astype(o_ref.dtype)


def _step1_specs(index):
    def spec(part, lane_half):
        return pl.BlockSpec((1, FFT_L2 * SUB, LANE),
                            lambda *g: (index(*g)[0], part * (FFT_L1 // SUB) + index(*g)[1],
                                        2 * index(*g)[2] + lane_half))
    return [spec(0, 0), spec(0, 1), spec(1, 0), spec(1, 1)]


def fft_step2_filter(af, g, nrm):
    W = af.shape[-1]
    ct = 2 * LANE
    L1, R2 = FFT_L1, 2 * FFT_L2
    af = af.reshape(1, -1, W)
    return pl.pallas_call(
        _fft2_filter_kernel,
        grid=(L1 // SUB, W // ct),
        in_specs=_step1_specs(lambda k, c: (0, k, c)) + [
            pl.BlockSpec((SUB, R2, R2), lambda k, c: (k, 0, 0)),
            pl.BlockSpec((1, ct), lambda k, c: (0, c))],
        out_specs=pl.BlockSpec((SUB, R2, ct), lambda k, c: (k, 0, c)),
        out_shape=jax.ShapeDtypeStruct((L1, R2, W), bf16),
        compiler_params=_params("arbitrary", "arbitrary"),
        name="fft_step2_filter",
    )(af, af, af, af, g, nrm)


def _fft2_kernel(a0_ref, a1_ref, a2_ref, a3_ref, g_ref, gt_ref, h_ref, ore_ref, oim_ref):
    half = FFT_L2
    for j in range(SUB):
        a = _step1_column((a0_ref, a1_ref, a2_ref, a3_ref), j)
        y = jnp.dot(g_ref[j], a, preferred_element_type=f32)
        pr, pi = _cmul(y[:half], y[half:], h_ref[j, :half].astype(f32), h_ref[j, half:].astype(f32))
        pcat = jnp.concatenate([pr, pi], axis=0).astype(bf16)
        b = jnp.dot(gt_ref[j], pcat, preferred_element_type=f32)
        for ch in range(FFT_L2 // N2C):
            ore_ref[0, ch, j] = b[ch * N2C:(ch + 1) * N2C]
            oim_ref[0, ch, j] = b[half + ch * N2C:half + (ch + 1) * N2C]


def fft_step2(a, g, gt, hf):
    P, W = a.shape[0], a.shape[-1]
    ct = 2 * LANE
    L1, R2 = FFT_L1, 2 * FFT_L2
    a = a.reshape(P, -1, W)
    out = pl.BlockSpec((1, FFT_L2 // N2C, SUB, N2C, ct), lambda k, c, p: (p, 0, k, 0, c))
    shape = jax.ShapeDtypeStruct((P, FFT_L2 // N2C, L1, N2C, W), f32)
    return pl.pallas_call(
        _fft2_kernel,
        grid=(L1 // SUB, W // ct, P),
        in_specs=_step1_specs(lambda k, c, p: (p, k, c)) + [
            pl.BlockSpec((SUB, R2, R2), lambda k, c, p: (k, 0, 0)),
            pl.BlockSpec((SUB, R2, R2), lambda k, c, p: (k, 0, 0)),
            pl.BlockSpec((SUB, R2, ct), lambda k, c, p: (k, 0, c))],
        out_specs=[out, out],
        out_shape=[shape, shape],
        compiler_params=_params("arbitrary", "arbitrary", "arbitrary"),
        name="fft_step2",
    )(a, a, a, a, g, gt, hf)


def _fft3_kernel(bre_ref, bim_ref, mr_ref, mi_ref, o_ref):
    L1 = FFT_L1
    half = L1 // 2
    base = pl.program_id(2) * N2C
    for j in range(N2C):
        br = bre_ref[0, pl.ds(j, L1, stride=N2C), :].astype(bf16)
        bi = bim_ref[0, pl.ds(j, L1, stride=N2C), :].astype(bf16)
        z = (jnp.dot(mr_ref[...], br, preferred_element_type=f32)
             + jnp.dot(mi_ref[...], bi, preferred_element_type=f32))
        rows = pl.ds(base + j, half, stride=FFT_L2)
        o_ref[0, 0, rows, :] = z[:half]
        o_ref[0, 1, rows, :] = z[half:]


def fft_step3(bre, bim, m3r, m3i, rows_total, n_lat):
    P, W = bre.shape[0], bre.shape[-1]
    L1, L2 = FFT_L1, FFT_L2
    bre = bre.reshape(P, -1, W)
    bim = bim.reshape(P, -1, W)
    spec = pl.BlockSpec((1, L1 * N2C, LANE), lambda p, c, k: (p, k, c))
    return pl.pallas_call(
        _fft3_kernel,
        grid=(P, W // LANE, L2 // N2C),
        in_specs=[spec, spec,
                  pl.BlockSpec((L1, L1), lambda p, c, k: (0, 0)),
                  pl.BlockSpec((L1, L1), lambda p, c, k: (0, 0))],
        out_specs=pl.BlockSpec((1, 2, n_lat, LANE), lambda p, c, k: (p, 0, 0, c)),
        out_shape=jax.ShapeDtypeStruct((P, 2, rows_total, W), f32),
        compiler_params=_params("arbitrary", "arbitrary", "arbitrary"),
        name="fft_step3",
    )(bre, bim, m3r, m3i)


def _ctx_filter_kernel(n, ts_ref, m_ref, nrm_ref, o_ref):
    scale = 1.0 / (nrm_ref[...] * float(2 * n))
    o_ref[...] = jnp.dot(m_ref[...], ts_ref[...].astype(bf16), preferred_element_type=f32) * scale


def ctx_filter_spectrum(ts, mf, nrm, *, ct=256):
    L, W = ts.shape
    return pl.pallas_call(
        functools.partial(_ctx_filter_kernel, L // 2),
        grid=(W // ct,),
        in_specs=[pl.BlockSpec((L, ct), lambda c: (0, c)),
                  pl.BlockSpec((2 * L, L), lambda c: (0, 0)),
                  pl.BlockSpec((1, ct), lambda c: (0, c))],
        out_specs=pl.BlockSpec((2 * L, ct), lambda c: (0, c)),
        out_shape=jax.ShapeDtypeStruct((2 * L, W), f32),
        compiler_params=_params("arbitrary"),
        name="ctx_filter_spectrum",
    )(ts, mf, nrm)


def _ctx_conv_kernel(n, re_ref, im_ref, mc_ref, minv_ref, h_ref, y_in_ref, o_ref):
    del y_in_ref
    L = 2 * n
    z = jnp.concatenate([re_ref[0], im_ref[0]], axis=0).astype(bf16)
    y = jnp.dot(mc_ref[...], z, preferred_element_type=f32)
    pr, pi = _cmul(y[:L], y[L:], h_ref[:L], h_ref[L:])
    pcat = jnp.concatenate([pr, pi], axis=0).astype(bf16)
    out = jnp.dot(minv_ref[...], pcat, preferred_element_type=f32)
    o_ref[0, 0] = out[:n]
    o_ref[0, 1] = out[n:]


def ctx_long_conv(gz, y4, hfc, consts, n_lat, n_ctx, *, ct=256):
    B, T, W = gz.shape
    blk = n_lat // n_ctx
    L = 2 * n_ctx
    out = pl.pallas_call(
        functools.partial(_ctx_conv_kernel, n_ctx),
        grid=(B // 2, W // ct),
        in_specs=[pl.BlockSpec((1, n_ctx, ct), lambda p, c: (2 * p, blk, c)),
                  pl.BlockSpec((1, n_ctx, ct), lambda p, c: (2 * p + 1, blk, c)),
                  pl.BlockSpec((2 * L, L), lambda p, c: (0, 0)),
                  pl.BlockSpec((L, 2 * L), lambda p, c: (0, 0)),
                  pl.BlockSpec((2 * L, ct), lambda p, c: (0, c)),
                  pl.BlockSpec(memory_space=pl.ANY)],
        out_specs=pl.BlockSpec((1, 2, n_ctx, ct), lambda p, c: (p, 0, blk, c)),
        out_shape=jax.ShapeDtypeStruct(y4.shape, f32),
        input_output_aliases={5: 0},
        compiler_params=_params("arbitrary", "arbitrary"),
        name="ctx_long_conv",
    )(gz, gz, consts["mc"], consts["minv"], hfc, y4)
    return out.reshape(B, T, W)


def hyena_long_conv(gz, n_lat, n_ctx, f_w1, f_b1, f_freq, f_w2, f_b2, f_w3):
    B, T, W = gz.shape
    cst = _bf16_constants(_fft_constants())
    L1, L2 = FFT_L1, FFT_L2
    assert 2 * n_lat == L1 * L2 and T % L2 == 0 and B % 2 == 0
    ts, nrm = hyena_filter(n_lat, f_w1, f_b1, f_freq, f_w2, f_b2, f_w3)
    af = fft_step1_filter(ts, cst["m1f"])
    hf = fft_step2_filter(af, cst["g"], nrm)
    a = fft_step1(gz, cst["m1r"], cst["m1i"], n_lat)
    bre, bim = fft_step2(a, cst["g"], cst["gt"], hf)
    y = fft_step3(bre, bim, cst["m3r"], cst["m3i"], T, n_lat)
    ccst = _bf16_constants(_ctx_dft_constants(n_ctx))
    ts_c, nrm_c = hyena_filter(n_ctx, f_w1, f_b1, f_freq, f_w2, f_b2, f_w3)
    hfc = ctx_filter_spectrum(ts_c, ccst["mf"], nrm_c)
    return ctx_long_conv(gz, y, hfc, ccst, n_lat, n_ctx)


@functools.lru_cache(maxsize=None)
def _rope_tables(n_lat, n_ctx):
    half = HEAD_DIM // 2
    nf = half // 2
    inv = ROPE_THETA ** (-np.arange(nf, dtype=np.float64) / nf)
    t = np.arange(n_lat)
    pos = np.stack([t // GRID_W, t % GRID_W], axis=1).astype(np.float64)
    ang = pos[:, :, None] * inv[None, None, :]
    cos = np.concatenate([np.cos(ang), np.cos(ang)], axis=2).reshape(n_lat, HEAD_DIM)
    sin = np.concatenate([-np.sin(ang), np.sin(ang)], axis=2).reshape(n_lat, HEAD_DIM)
    cos = np.concatenate([cos, np.ones((n_ctx, HEAD_DIM))], axis=0)
    sin = np.concatenate([sin, np.zeros((n_ctx, HEAD_DIM))], axis=0)
    return np.asarray(cos, np.float32), np.asarray(sin, np.float32)


def _norm_rope_heads(xs, w, cos, sin, lo_lane):
    nf = HEAD_DIM // 4
    ms = [jnp.mean(x * x, axis=-1, keepdims=True) for x in xs]
    ys = [x * lax.rsqrt(m + NORM_EPS) * w for x, m in zip(xs, ms)]
    ps = [jnp.where(lo_lane, pltpu.roll(y, HEAD_DIM - nf, 1), pltpu.roll(y, nf, 1)) for y in ys]
    return [y * cos + p * sin for y, p in zip(ys, ps)]


def _qkv_prep_kernel(q_ref, k_ref, v_ref, cos_ref, sin_ref, qn_ref, kn_ref, q_out, kt_out, v_out):
    cos = cos_ref[...]
    sin = sin_ref[...]
    lane = lax.broadcasted_iota(jnp.int32, cos.shape, 1)
    lo_lane = (lane % (HEAD_DIM // 2)) < (HEAD_DIM // 4)
    head = lambda ref, h: ref[:, h * HEAD_DIM:(h + 1) * HEAD_DIM].astype(f32)
    scale = HEAD_DIM ** -0.5 * math.log2(math.e)
    qs = _norm_rope_heads([head(q_ref, h) for h in range(ATT_HEADS)], qn_ref[...], cos, sin, lo_lane)
    for h, q in enumerate(qs):
        q_out[:, h * HEAD_DIM:(h + 1) * HEAD_DIM] = (q * scale).astype(q_out.dtype)
    ks = _norm_rope_heads([head(k_ref, h) for h in range(ATT_KV_HEADS)], kn_ref[...], cos, sin, lo_lane)
    ones_col = (lane == 0).astype(v_out.dtype)
    for h, k in enumerate(ks):
        sl = slice(h * HEAD_DIM, (h + 1) * HEAD_DIM)
        kt_out[0, sl, :] = k.T.astype(kt_out.dtype)
        v_out[:, 2 * h * HEAD_DIM:(2 * h + 1) * HEAD_DIM] = v_ref[:, sl]
        v_out[:, (2 * h + 1) * HEAD_DIM:(2 * h + 2) * HEAD_DIM] = ones_col


def qkv_prep(p, q_norm, k_norm, B, n_lat, n_ctx):
    R = p.shape[0]
    T = n_lat + n_ctx
    tpb = T // ROW_TILE
    cos, sin = _rope_tables(n_lat, n_ctx)
    return pl.pallas_call(
        _qkv_prep_kernel,
        grid=(R // ROW_TILE,),
        in_specs=[pl.BlockSpec((ROW_TILE, ATT_WIDTH), lambda i: (i, EVEN_OFF_Q // ATT_WIDTH)),
                  pl.BlockSpec((ROW_TILE, ATT_KV_WIDTH), lambda i: (i, EVEN_OFF_K // ATT_KV_WIDTH)),
                  pl.BlockSpec((ROW_TILE, ATT_KV_WIDTH), lambda i: (i, EVEN_OFF_V // ATT_KV_WIDTH)),
                  pl.BlockSpec((ROW_TILE, HEAD_DIM), lambda i: (i % tpb, 0)),
                  pl.BlockSpec((ROW_TILE, HEAD_DIM), lambda i: (i % tpb, 0)),
                  pl.BlockSpec((1, HEAD_DIM), lambda i: (0, 0)),
                  pl.BlockSpec((1, HEAD_DIM), lambda i: (0, 0))],
        out_specs=[pl.BlockSpec((ROW_TILE, ATT_WIDTH), lambda i: (i, 0)),
                   pl.BlockSpec((1, ATT_KV_WIDTH, ROW_TILE), lambda i: (i // tpb, 0, i % tpb)),
                   pl.BlockSpec((ROW_TILE, 2 * ATT_KV_WIDTH), lambda i: (i, 0))],
        out_shape=[jax.ShapeDtypeStruct((R, ATT_WIDTH), bf16),
                   jax.ShapeDtypeStruct((B, ATT_KV_WIDTH, T), bf16),
                   jax.ShapeDtypeStruct((R, 2 * ATT_KV_WIDTH), bf16)],
        compiler_params=_params("arbitrary"),
        name="qkv_prep",
    )(p, p, p, jnp.asarray(cos), jnp.asarray(sin), q_norm.reshape(1, HEAD_DIM), k_norm.reshape(1, HEAD_DIM))


ATT_CHUNK = 256
ATT_Q_TILE = 512


def _attention_kernel(k_lo, q_ref, kt_ref, v_ref, *rest):
    o_ref, qs_ref, sa_ref, sb_ref, pa_ref, pb_ref, os_ref = rest[-7:]
    tq = q_ref.shape[1]
    n_chunks = ATT_GROUP * tq // ATT_CHUNK
    for h in range(ATT_GROUP):
        qs_ref[h * tq:(h + 1) * tq, :] = q_ref[0, :, h * HEAD_DIM:(h + 1) * HEAD_DIM]

    rows = lambda c: slice(c * ATT_CHUNK, (c + 1) * ATT_CHUNK)
    s_refs = (sa_ref, sb_ref)
    p_refs = (pa_ref, pb_ref)

    def scores(c):
        s_refs[c % 2][:, k_lo:] = jnp.dot(qs_ref[rows(c), :], kt_ref[0, :, k_lo:], preferred_element_type=f32)

    def exponentials(c):
        s = s_refs[c % 2][:, k_lo:]
        m = jnp.max(s, axis=-1, keepdims=True)
        p_refs[c % 2][:, k_lo:] = jnp.exp2(s - m).astype(bf16)

    def weighted_values(c):
        r = jnp.dot(p_refs[c % 2][:, k_lo:], v_ref[0, k_lo:, :], preferred_element_type=f32)
        os_ref[rows(c), :] = r[:, :HEAD_DIM] / r[:, HEAD_DIM:HEAD_DIM + 1]

    scores(0)
    for c in range(n_chunks):
        if c + 1 < n_chunks:
            scores(c + 1)
        exponentials(c)
        if c >= 1:
            weighted_values(c - 1)
    weighted_values(n_chunks - 1)
    for h in range(ATT_GROUP):
        o_ref[0, :, h * HEAD_DIM:(h + 1) * HEAD_DIM] = os_ref[h * tq:(h + 1) * tq, :].astype(o_ref.dtype)


def attention(q, kt, v, B, n_lat, n_ctx):
    R = q.shape[0]
    T = n_lat + n_ctx
    gw = ATT_GROUP * HEAD_DIM
    q3 = q.reshape(B, T, ATT_WIDTH)
    v3 = v.reshape(B, T, 2 * ATT_KV_WIDTH)
    out_shape = jax.ShapeDtypeStruct((B, T, ATT_WIDTH), bf16)

    def call(name, k_lo, tq, q_tiles, first_tile, extra_in, extra_specs, aliases):
        qo_spec = pl.BlockSpec((1, tq, gw), lambda b, g, i: (b, first_tile + i, g))
        return pl.pallas_call(
            functools.partial(_attention_kernel, k_lo),
            grid=(B, ATT_KV_HEADS, q_tiles),
            in_specs=[qo_spec,
                      pl.BlockSpec((1, HEAD_DIM, T), lambda b, g, i: (b, g, 0)),
                      pl.BlockSpec((1, T, 2 * HEAD_DIM), lambda b, g, i: (b, 0, g))] + extra_specs,
            out_specs=qo_spec,
            out_shape=out_shape,
            scratch_shapes=[pltpu.VMEM((ATT_GROUP * tq, HEAD_DIM), bf16),
                            pltpu.VMEM((ATT_CHUNK, T), f32), pltpu.VMEM((ATT_CHUNK, T), f32),
                            pltpu.VMEM((ATT_CHUNK, T), bf16), pltpu.VMEM((ATT_CHUNK, T), bf16),
                            pltpu.VMEM((ATT_GROUP * tq, HEAD_DIM), f32)],
            input_output_aliases=aliases,
            compiler_params=_params("arbitrary", "arbitrary", "arbitrary"),
            name=name,
        )(q3, kt, v3, *extra_in)

    att = call("attention", 0, ATT_Q_TILE, n_lat // ATT_Q_TILE, 0, [], [], {})
    att = call("attention_ctx", n_lat, n_ctx, 1, n_lat // n_ctx, [att], [pl.BlockSpec(memory_space=pl.ANY)], {3: 0})
    return att.reshape(R, ATT_WIDTH)


def _post_residual(x, y, g_post, gate):
    yn = y * lax.rsqrt(jnp.mean(y * y, axis=-1, keepdims=True) + NORM_EPS) * g_post
    return x + gate * yn


def _silu(x):
    return x * jax.nn.sigmoid(x)


def _even_out_kernel(tiles_per_batch, x0_ref, g_ref, yc_ref, ghy_ref, att_ref, ga0_ref, ga1_ref, bias_ref, w_ref,
                     gp_ref, m_ref, x_ref, c_ref, gn_ref, mn_ref, o_ref, hn_ref):
    D = x_ref.shape[-1]
    g = g_ref[...]
    hy = x0_ref[...].astype(f32) * (yc_ref[...] + g * bias_ref[...]) * _silu(ghy_ref[...].astype(f32))
    g_att = jnp.concatenate([ga0_ref[...], ga1_ref[...]], axis=1).astype(f32)
    at = att_ref[...].astype(f32) * _silu(g_att)
    lhs = jnp.concatenate([hy, at], axis=1).astype(bf16)
    y = jnp.dot(lhs, w_ref[...], preferred_element_type=f32)
    x_new = _post_residual(_input_rows(tiles_per_batch, x_ref, c_ref), y, gp_ref[...], m_ref[0, :, 2 * D:3 * D])
    o_ref[...] = x_new
    hn_ref[...] = _norm_mod_rows(x_new, gn_ref[...], mn_ref[0]).astype(hn_ref.dtype)


def even_out(x0, g, yconv, p, att, hy_bias, w_out, g_post, mods, x_lat, x_ctx, next_g_pre, next_mods,
             tiles_per_batch, ctx_row):
    R = x0.shape[0]
    D = x_lat.shape[1]
    W = HY_WIDTH
    hw = ATT_WIDTH // 2
    row = lambda i: (i, 0)
    full = lambda i: (0, 0)
    lat_spec, ctx_spec = _input_row_specs(tiles_per_batch, D)
    return pl.pallas_call(
        functools.partial(_even_out_kernel, tiles_per_batch),
        grid=(R // ROW_TILE,),
        in_specs=[pl.BlockSpec((ROW_TILE, W), row),
                  pl.BlockSpec((ROW_TILE, W), row),
                  pl.BlockSpec((ROW_TILE, W), row),
                  pl.BlockSpec((ROW_TILE, W), lambda i: (i, EVEN_OFF_GHY // W)),
                  pl.BlockSpec((ROW_TILE, ATT_WIDTH), row),
                  pl.BlockSpec((ROW_TILE, hw), lambda i: (i, EVEN_OFF_GATT // hw)),
                  pl.BlockSpec((ROW_TILE, hw), lambda i: (i, EVEN_OFF_GATT // hw + 1)),
                  pl.BlockSpec((1, W), full),
                  pl.BlockSpec((W + ATT_WIDTH, D), full),
                  pl.BlockSpec((1, D), full),
                  pl.BlockSpec((1, 1, 3 * D), lambda i: (_mod_row(i, tiles_per_batch, ctx_row), 0, 0)),
                  lat_spec, ctx_spec,
                  pl.BlockSpec((1, D), full),
                  pl.BlockSpec((1, 1, 3 * D), lambda i: (_mod_row(i, tiles_per_batch, ctx_row), 0, 0))],
        out_specs=[pl.BlockSpec((ROW_TILE, D), row), pl.BlockSpec((ROW_TILE, D), row)],
        out_shape=[jax.ShapeDtypeStruct((R, D), f32), jax.ShapeDtypeStruct((R, D), bf16)],
        compiler_params=_params("arbitrary"),
        name="even_out",
    )(x0, g, yconv, p, att, p, p, hy_bias.reshape(1, W), w_out.astype(bf16), g_post.reshape(1, D), mods,
      x_lat, x_ctx, next_g_pre.reshape(1, D), next_mods)


def even_layer(x_lat, x_ctx, mods, next_g_pre, next_mods, B, n_lat, n_ctx, g_pre, g_post, w_in, conv_w, conv_b,
               f_w1, f_b1, f_freq, f_w2, f_b2, f_w3, hy_bias, q_norm, k_norm, w_out):
    T = n_lat + n_ctx
    tpb = T // ROW_TILE
    h = norm_mod(x_lat, x_ctx, g_pre, mods, tpb, B)
    p = matmul(h, w_in, tm=_proj_row_tile(B * T), tn=_proj_col_tile(w_in.shape[1]), out_dtype=bf16,
               name="even_in_proj")
    x0, g = hyena_pre(p, conv_w, conv_b, tpb)
    yconv = hyena_long_conv(g.reshape(B, T, HY_WIDTH), n_lat, n_ctx, f_w1, f_b1, f_freq, f_w2, f_b2, f_w3)
    q, kt, v = qkv_prep(p, q_norm, k_norm, B, n_lat, n_ctx)
    att = attention(q, kt, v, B, n_lat, n_ctx)
    return even_out(x0, g, yconv.reshape(B * T, HY_WIDTH), p, att, hy_bias, w_out, g_post, mods, x_lat, x_ctx,
                    next_g_pre, next_mods, tpb, B)


def _mlstm_prep_kernel(tiles_per_batch, q_ref, qp_ref, qn_ref, k_ref, kp_ref, kn_ref,
                       wq_ref, wk_ref, bq_ref, bk_ref, q_out, kt_out):
    first, last = _seq_edges(pl.program_id(0), tiles_per_batch)
    prev_row, next_row = _halo_rows(qp_ref, qn_ref, first, last)
    q = _silu(_conv3(q_ref[...].astype(f32), prev_row, next_row, wq_ref[...], bq_ref[...]))
    q_out[...] = q.astype(q_out.dtype)
    prev_row, next_row = _halo_rows(kp_ref, kn_ref, first, last)
    k = _silu(_conv3(k_ref[...].astype(f32), prev_row, next_row, wk_ref[...], bk_ref[...])) * (ML_QK ** -0.5)
    for h in range(k.shape[1] // ML_QK):
        sl = slice(h * ML_QK, (h + 1) * ML_QK)
        kt_out[0, sl, :] = k[:, sl].T.astype(kt_out.dtype)


def mlstm_prep(p, conv_w, conv_b, B, T, *, tc=1024):
    R = p.shape[0]
    tpb = T // ROW_TILE
    nb = ML_QK_WIDTH // tc
    qcol = lambda c: c
    kcol = lambda c: nb + c
    qprev, qnext = _halo_specs(qcol, tc, R)
    kprev, knext = _halo_specs(kcol, tc, R)
    return pl.pallas_call(
        functools.partial(_mlstm_prep_kernel, tpb),
        grid=(R // ROW_TILE, nb),
        in_specs=[pl.BlockSpec((ROW_TILE, tc), lambda i, c: (i, c)), qprev, qnext,
                  pl.BlockSpec((ROW_TILE, tc), lambda i, c: (i, nb + c)), kprev, knext,
                  pl.BlockSpec((3, tc), lambda i, c: (0, c)),
                  pl.BlockSpec((3, tc), lambda i, c: (0, nb + c)),
                  pl.BlockSpec((1, tc), lambda i, c: (0, c)),
                  pl.BlockSpec((1, tc), lambda i, c: (0, nb + c))],
        out_specs=[pl.BlockSpec((ROW_TILE, tc), lambda i, c: (i, c)),
                   pl.BlockSpec((1, tc, ROW_TILE), lambda i, c: (i // tpb, c, i % tpb))],
        out_shape=[jax.ShapeDtypeStruct((R, ML_QK_WIDTH), bf16),
                   jax.ShapeDtypeStruct((B, ML_QK_WIDTH, T), bf16)],
        compiler_params=_params("arbitrary", "arbitrary"),
        name="mlstm_prep",
    )(p, p, p, p, p, p, conv_w, conv_w, conv_b.reshape(1, -1), conv_b.reshape(1, -1))


def _log_sigmoid(x):
    return jnp.minimum(x, 0.0) - jnp.log(1.0 + jnp.exp(-jnp.abs(x)))


def _mlstm_gates_kernel(g_ref, b_ref, gc_out, gr_out):
    pre = g_ref[...] + b_ref[...]
    lane = lax.broadcasted_iota(jnp.int32, pre.shape, 1)
    is_forget = (lane // ML_HEADS) % 2 == 1
    gc = jnp.where(is_forget, _log_sigmoid(pre), pre)
    gc_out[...] = gc
    gr_out[0] = gc.T


def mlstm_gates(gates, gate_b, B, T):
    R = gates.shape[0]
    gb = jnp.pad(gate_b, (0, LANE - gate_b.shape[0])).reshape(1, LANE)
    return pl.pallas_call(
        _mlstm_gates_kernel,
        grid=(B,),
        in_specs=[pl.BlockSpec((T, LANE), lambda b: (b, 0)),
                  pl.BlockSpec((1, LANE), lambda b: (0, 0))],
        out_specs=[pl.BlockSpec((T, LANE), lambda b: (b, 0)),
                   pl.BlockSpec((1, LANE, T), lambda b: (b, 0, 0))],
        out_shape=[jax.ShapeDtypeStruct((R, LANE), f32), jax.ShapeDtypeStruct((B, LANE, T), f32)],
        compiler_params=_params("arbitrary"),
        name="mlstm_gates",
    )(gates, gb)


def _mlstm_chunk_setup(reverse, gc_ref, gr_ref, m_ref, ms_ref):
    Lc = gc_ref.shape[0]
    H = ML_HEADS
    i_off = 2 * H if reverse else 0
    f_off = i_off + H
    t_idx = lax.broadcasted_iota(jnp.int32, (Lc, Lc), 0)
    s_idx = lax.broadcasted_iota(jnp.int32, (Lc, Lc), 1)
    causal = (s_idx >= t_idx) if reverse else (s_idx <= t_idx)
    tri = causal.astype(f32)
    gc = gc_ref[...]
    gr = gr_ref[0]
    b_col_all = jnp.dot(tri, gc[:, f_off:f_off + H], preferred_element_type=f32, precision=HIGHEST)
    b_row_all = lax.dot_general(gr[f_off:f_off + H, :], tri, (((1,), (1,)), ((), ())),
                                preferred_element_type=f32, precision=HIGHEST)
    i_col_all = gc[:, i_off:i_off + H]
    i_row_all = gr[i_off:i_off + H, :]
    end = 0 if reverse else Lc - 1
    b_end = b_col_all[end:end + 1, :]
    m_prev = m_ref[0:1, 0:H]
    g_col = b_end - b_col_all + i_col_all
    m_new = jnp.maximum(b_end + m_prev, jnp.max(g_col, axis=0, keepdims=True))
    a_prev = jnp.exp(b_end + m_prev - m_new)
    m_ref[:, 0:H] = jnp.broadcast_to(m_new, (m_ref.shape[0], H))
    b_end_s = b_row_all[:, end:end + 1]
    m_prev_s = ms_ref[0:H, 0:1]
    g_row = b_end_s - b_row_all + i_row_all
    m_new_s = jnp.maximum(b_end_s + m_prev_s, jnp.max(g_row, axis=1, keepdims=True))
    a_row = jnp.exp(g_row - m_new_s)
    ms_ref[0:H, :] = jnp.broadcast_to(m_new_s, (H, ms_ref.shape[1]))
    return dict(causal=causal, i_row=i_row_all, b_col=b_col_all, b_row=b_row_all,
                m_prev=m_prev, a_row=a_row, a_prev=a_prev)


class _MlstmChain:
    def __init__(self, h, cs, q_ref, kt_ref, v_ref, o_ref, ct_ref):
        self.h, self.cs = h, cs
        self.q_ref, self.kt_ref, self.v_ref, self.o_ref, self.ct_ref = q_ref, kt_ref, v_ref, o_ref, ct_ref

    def _q(self):
        return self.q_ref[:, self.h * ML_QK:(self.h + 1) * ML_QK]

    def _kt(self):
        return self.kt_ref[0, self.h * ML_QK:(self.h + 1) * ML_QK, :]

    def _v(self):
        Lc = self.q_ref.shape[0]
        ones_col = (lax.broadcasted_iota(jnp.int32, (Lc, LANE), 1) == 0).astype(bf16)
        return jnp.concatenate([self.v_ref[:, self.h * ML_V:(self.h + 1) * ML_V], ones_col], axis=1)

    def scores(self):
        self.qk = jnp.dot(self._q(), self._kt(), preferred_element_type=f32)

    def gates(self):
        h, cs = self.h, self.cs
        i_row = cs["i_row"][h:h + 1, :]
        b_col = cs["b_col"][:, h:h + 1]
        b_row = cs["b_row"][h:h + 1, :]
        m_prev = cs["m_prev"][:, h:h + 1]
        d = jnp.where(cs["causal"], b_col + (i_row - b_row), -jnp.inf)
        inter = b_col + m_prev
        self.m_row = jnp.maximum(inter, jnp.max(d, axis=-1, keepdims=True))
        self.s = (self.qk * jnp.exp(d - self.m_row)).astype(bf16)
        self.w_prev = jnp.exp(inter - self.m_row)

    def values(self):
        h = self.h
        self.ct = self.ct_ref[h]
        qw = (self._q().astype(f32) * self.w_prev).astype(bf16)
        tot = jnp.dot(jnp.concatenate([self.s, qw], axis=1),
                      jnp.concatenate([self._v(), self.ct.astype(bf16)], axis=0),
                      preferred_element_type=f32)
        scale = 1.0 / jnp.maximum(jnp.abs(tot[:, ML_V:ML_V + 1]), jnp.exp(-self.m_row))
        self.o_ref[:, h * ML_V:(h + 1) * ML_V] = (tot[:, :ML_V] * scale).astype(self.o_ref.dtype)

    def update(self):
        h = self.h
        kta = (self._kt().astype(f32) * self.cs["a_row"][h:h + 1, :]).astype(bf16)
        self.ct_ref[h] = (self.cs["a_prev"][:, h:h + 1] * self.ct
                          + jnp.dot(kta, self._v(), preferred_element_type=f32))


def _mlstm_scan_kernel(qf_ref, ktf_ref, vf_ref, gcf_ref, grf_ref, qb_ref, ktb_ref, vb_ref, gcb_ref, grb_ref,
                       of_ref, ob_ref, ctf_ref, mf_ref, msf_ref, ctb_ref, mb_ref, msb_ref):
    @pl.when(pl.program_id(1) == 0)
    def _():
        for ref in (ctf_ref, mf_ref, msf_ref, ctb_ref, mb_ref, msb_ref):
            ref[...] = jnp.zeros_like(ref)

    fwd = _mlstm_chunk_setup(False, gcf_ref, grf_ref, mf_ref, msf_ref)
    bwd = _mlstm_chunk_setup(True, gcb_ref, grb_ref, mb_ref, msb_ref)
    chains = []
    for h in range(ML_HEADS):
        chains.append(_MlstmChain(h, fwd, qf_ref, ktf_ref, vf_ref, of_ref, ctf_ref))
        chains.append(_MlstmChain(h, bwd, qb_ref, ktb_ref, vb_ref, ob_ref, ctb_ref))
    stages = ("scores", "gates", "values", "update")
    for k in range(len(chains) + len(stages) - 1):
        for depth, stage in enumerate(stages):
            if 0 <= k - depth < len(chains):
                getattr(chains[k - depth], stage)()


def mlstm_scan(q, kt, p, gc, gr, B, n_lat, n_ctx):
    R = q.shape[0]
    Lc = ML_CHUNK
    tpb = (n_lat + n_ctx) // Lc
    lat = n_lat // Lc
    ctx = n_ctx // Lc

    def specs(reverse):
        def chunk(j):
            if reverse:
                return tpb - 1 - j
            return jnp.where(j < ctx, lat + j, j - ctx)
        ins = [pl.BlockSpec((Lc, ML_QK_WIDTH), lambda b, j: (b * tpb + chunk(j), 0)),
               pl.BlockSpec((1, ML_QK_WIDTH, Lc), lambda b, j: (b, 0, chunk(j))),
               pl.BlockSpec((Lc, ML_WIDTH), lambda b, j: (b * tpb + chunk(j), ODD_OFF_V // ML_WIDTH)),
               pl.BlockSpec((Lc, LANE), lambda b, j: (b * tpb + chunk(j), 0)),
               pl.BlockSpec((1, LANE, Lc), lambda b, j: (b, 0, chunk(j)))]
        out = pl.BlockSpec((Lc, ML_WIDTH), lambda b, j: (b * tpb + chunk(j), 0))
        return ins, out

    ins_f, out_f = specs(False)
    ins_b, out_b = specs(True)
    state = [pltpu.VMEM((ML_HEADS, ML_QK, ML_V + LANE), f32), pltpu.VMEM((SUB, LANE), f32),
             pltpu.VMEM((SUB, LANE), f32)]
    return pl.pallas_call(
        _mlstm_scan_kernel,
        grid=(B, tpb),
        in_specs=ins_f + ins_b,
        out_specs=[out_f, out_b],
        out_shape=[jax.ShapeDtypeStruct((R, ML_WIDTH), bf16), jax.ShapeDtypeStruct((R, ML_WIDTH), bf16)],
        scratch_shapes=state + state,
        compiler_params=_params("arbitrary", "arbitrary"),
        name="mlstm_scan",
    )(q, kt, p, gc, gr, q, kt, p, gc, gr)


def _odd_out_kernel(hf_ref, hb_ref, o_ref, z_ref, hn_ref, w_ref, gp_ref, m_ref, x_ref, out_ref):
    D = x_ref.shape[-1]
    hs = (hf_ref[0].astype(f32) + hb_ref[0].astype(f32)) * jax.nn.sigmoid(o_ref[0].astype(f32))
    segs = [hs[:, h * ML_V:(h + 1) * ML_V] for h in range(ML_HEADS)]
    ms = [jnp.mean(seg * seg, axis=-1, keepdims=True) for seg in segs]
    parts = [seg * lax.rsqrt(m + NORM_EPS) for seg, m in zip(segs, ms)]
    hn = jnp.concatenate(parts, axis=1) * hn_ref[...] * _silu(z_ref[0].astype(f32))
    y = jnp.dot(hn.astype(bf16), w_ref[...], preferred_element_type=f32)
    out_ref[0] = _post_residual(x_ref[0], y, gp_ref[...], m_ref[0, :, 2 * D:3 * D])


def odd_out(hf, hb, p, head_norm, w_out, g_post, mods, x_all, B, n_lat, T, *, tm=512):
    D = x_all.shape[1]
    view = lambda a: a.reshape(B, T, a.shape[1])
    row = lambda b, i: (b, i, 0)
    full = lambda b, i: (0, 0)
    return pl.pallas_call(
        _odd_out_kernel,
        grid=(B, n_lat // tm),
        in_specs=[pl.BlockSpec((1, tm, ML_WIDTH), row),
                  pl.BlockSpec((1, tm, ML_WIDTH), row),
                  pl.BlockSpec((1, tm, ML_WIDTH), lambda b, i: (b, i, ODD_OFF_O // ML_WIDTH)),
                  pl.BlockSpec((1, tm, ML_WIDTH), lambda b, i: (b, i, ODD_OFF_Z // ML_WIDTH)),
                  pl.BlockSpec((1, ML_WIDTH), full),
                  pl.BlockSpec((ML_WIDTH, D), full),
                  pl.BlockSpec((1, D), full),
                  pl.BlockSpec((1, 1, 3 * D), lambda b, i: (b, 0, 0)),
                  pl.BlockSpec((1, tm, D), row)],
        out_specs=pl.BlockSpec((1, tm, D), row),
        out_shape=jax.ShapeDtypeStruct((B, n_lat, D), f32),
        compiler_params=_params("arbitrary", "arbitrary"),
        name="odd_out",
    )(view(hf), view(hb), view(p), view(p), head_norm.reshape(1, ML_WIDTH), w_out.astype(bf16),
      g_post.reshape(1, D), mods, view(x_all))


def odd_layer_last(x_all, h, mods, B, n_lat, n_ctx, g_post, w_in, conv_w, conv_b, gate_b, head_norm, w_out):
    T = n_lat + n_ctx
    tm = _proj_row_tile(B * T)
    p = matmul(h, w_in, tm=tm, tn=_proj_col_tile(ODD_MAIN), n_cols=ODD_MAIN, out_dtype=bf16, name="odd_in_proj")
    n_gates = w_in.shape[-1] - ODD_MAIN
    w_gates = jnp.pad(w_in[0, :, ODD_MAIN:], ((0, 0), (0, LANE - n_gates)))
    gates = matmul(h, w_gates, tm=tm, tn=LANE, name="odd_gate_proj")
    q, kt = mlstm_prep(p, conv_w, conv_b, B, T)
    gc, gr = mlstm_gates(gates, gate_b, B, T)
    hf, hb = mlstm_scan(q, kt, p, gc, gr, B, n_lat, n_ctx)
    return odd_out(hf, hb, p, head_norm, w_out, g_post, mods, x_all, B, n_lat, T)


def kernel(x, c, ctx, c_ctx, w_mod, b_mod, g_pre, g_post, e_w_in, e_conv_w, e_conv_b, e_filt_w1,
           e_filt_b1, e_filt_freq, e_filt_w2, e_filt_b2, e_filt_w3, e_hy_bias, e_q_norm, e_k_norm,
           e_w_out, o_w_in, o_conv_w, o_conv_b, o_gate_b, o_head_norm, o_w_out):
    B, n_lat, D = x.shape
    n_ctx = ctx.shape[1]
    T = n_lat + n_ctx
    depth = w_mod.shape[0]
    assert depth == 2 and B + 1 <= 8 and n_ctx == ROW_TILE and n_lat % ROW_TILE == 0
    cond = jnp.concatenate([c, c_ctx[None], jnp.zeros((8 - B - 1, D), f32)], axis=0)
    mods_all = adaln_all(cond, w_mod, b_mod)
    mods0 = mods_all[0].reshape(8, 1, 3 * D)
    mods1 = mods_all[1].reshape(8, 1, 3 * D)
    x_all, h1 = even_layer(x.reshape(B * n_lat, D), ctx.reshape(B * n_ctx, D), mods0, g_pre[1], mods1,
                           B, n_lat, n_ctx, g_pre[0], g_post[0], e_w_in[0],
                           e_conv_w[0], e_conv_b[0], e_filt_w1[0], e_filt_b1[0], e_filt_freq[0], e_filt_w2[0],
                           e_filt_b2[0], e_filt_w3[0], e_hy_bias[0], e_q_norm[0], e_k_norm[0], e_w_out[0])
    out = odd_layer_last(x_all, h1, mods1, B, n_lat, n_ctx, g_post[1], o_w_in,
                         o_conv_w[0], o_conv_b[0], o_gate_b[0], o_head_norm[0], o_w_out[0])
    return out.reshape(B, n_lat, D)
```

```python
import functools
import math

import numpy as np
import jax
import jax.numpy as jnp
from jax import lax
from jax.experimental import pallas as pl
from jax.experimental.pallas import tpu as pltpu

f32 = jnp.float32
bf16 = jnp.bfloat16
HIGHEST = lax.Precision.HIGHEST

D_MODEL = 1024
GRID_W = 64
NORM_EPS = 1e-6

HY_WIDTH = 1024
HY_EMB = 33
HY_BANDS = (HY_EMB - 1) // 2
HY_HIDDEN = 64
HY_TARGET = 1e-2
HY_SHORT_DECAY_PCT = 0.3
HY_LONG_DECAY_PCT = 1.5

ATT_HEADS = 8
ATT_KV_HEADS = 2
ATT_GROUP = ATT_HEADS // ATT_KV_HEADS
HEAD_DIM = 128
ATT_WIDTH = ATT_HEADS * HEAD_DIM
ATT_KV_WIDTH = ATT_KV_HEADS * HEAD_DIM
ROPE_THETA = 10000.0
EVEN_OFF_XV = 0
EVEN_OFF_GHY = 3 * HY_WIDTH
EVEN_OFF_Q = EVEN_OFF_GHY + HY_WIDTH
EVEN_OFF_K = EVEN_OFF_Q + ATT_WIDTH
EVEN_OFF_V = EVEN_OFF_K + ATT_KV_WIDTH
EVEN_OFF_GATT = EVEN_OFF_V + ATT_KV_WIDTH
EVEN_IN = EVEN_OFF_GATT + ATT_WIDTH

ML_HEADS = 8
ML_QK = 128
ML_V = 256
ML_QK_WIDTH = ML_HEADS * ML_QK
ML_WIDTH = ML_HEADS * ML_V
ODD_OFF_Q = 0
ODD_OFF_K = ML_QK_WIDTH
ODD_OFF_V = 2 * ML_QK_WIDTH
ODD_OFF_O = ODD_OFF_V + ML_WIDTH
ODD_OFF_Z = ODD_OFF_O + ML_WIDTH
ODD_OFF_GATES = ODD_OFF_Z + ML_WIDTH
ODD_MAIN = ODD_OFF_GATES

ROW_TILE = 256
ML_CHUNK = 256
LANE = 128
VMEM_LIMIT_BYTES = 48 * 1024 * 1024

FFT_L1 = 64
FFT_L2 = 128


def _params(*sem):
    return pltpu.CompilerParams(dimension_semantics=sem, vmem_limit_bytes=VMEM_LIMIT_BYTES)


def _adaln_kernel(c_ref, w_ref, b_ref, o_ref):
    c = c_ref[...]
    s = c * jax.nn.sigmoid(c)
    o_ref[0] = jnp.dot(s, w_ref[0], preferred_element_type=f32, precision=HIGHEST) + b_ref[0]


def adaln_all(cond, w_mod, b_mod, *, tn=768):
    depth, D, N = w_mod.shape
    return pl.pallas_call(
        _adaln_kernel,
        grid=(depth, N // tn),
        in_specs=[
            pl.BlockSpec((8, D), lambda l, j: (0, 0)),
            pl.BlockSpec((1, D, tn), lambda l, j: (l, 0, j)),
            pl.BlockSpec((1, 1, tn), lambda l, j: (l, 0, j)),
        ],
        out_specs=pl.BlockSpec((1, 8, tn), lambda l, j: (l, 0, j)),
        out_shape=jax.ShapeDtypeStruct((depth, 8, N), f32),
        compiler_params=_params("arbitrary", "arbitrary"),
        name="adaln",
    )(cond, w_mod, b_mod.reshape(depth, 1, N))


def _mod_row(i, tiles_per_batch, ctx_row):
    lat_tiles = tiles_per_batch - 1
    return jnp.where(i % tiles_per_batch == lat_tiles, ctx_row, i // tiles_per_batch)


def _norm_mod_rows(x, g, m):
    D = x.shape[-1]
    y = x * lax.rsqrt(jnp.mean(x * x, axis=-1, keepdims=True) + NORM_EPS)
    return y * g * (1.0 + m[:, D:2 * D]) + m[:, 0:D]


def _input_row_specs(tiles_per_batch, D):
    lat_tiles = tiles_per_batch - 1

    def lat_index(i):
        return ((i // tiles_per_batch) * lat_tiles + jnp.minimum(i % tiles_per_batch, lat_tiles - 1), 0)

    return (pl.BlockSpec((ROW_TILE, D), lat_index),
            pl.BlockSpec((ROW_TILE, D), lambda i: (i // tiles_per_batch, 0)))


def _input_rows(tiles_per_batch, lat_ref, ctx_ref):
    is_ctx = pl.program_id(0) % tiles_per_batch == tiles_per_batch - 1
    return jnp.where(is_ctx, ctx_ref[...], lat_ref[...])


def _norm_mod_kernel(tiles_per_batch, x_ref, c_ref, g_ref, m_ref, o_ref):
    x = _input_rows(tiles_per_batch, x_ref, c_ref)
    o_ref[...] = _norm_mod_rows(x, g_ref[...], m_ref[0]).astype(o_ref.dtype)


def norm_mod(x_lat, x_ctx, g, mods, tiles_per_batch, ctx_row):
    D = x_lat.shape[1]
    R = x_lat.shape[0] + x_ctx.shape[0]
    lat_spec, ctx_spec = _input_row_specs(tiles_per_batch, D)
    return pl.pallas_call(
        functools.partial(_norm_mod_kernel, tiles_per_batch),
        grid=(R // ROW_TILE,),
        in_specs=[
            lat_spec, ctx_spec,
            pl.BlockSpec((1, D), lambda i: (0, 0)),
            pl.BlockSpec((1, 1, 3 * D), lambda i: (_mod_row(i, tiles_per_batch, ctx_row), 0, 0)),
        ],
        out_specs=pl.BlockSpec((ROW_TILE, D), lambda i: (i, 0)),
        out_shape=jax.ShapeDtypeStruct((R, D), bf16),
        compiler_params=_params("arbitrary"),
        name="norm_mod",
    )(x_lat, x_ctx, g.reshape(1, D), mods)


def _matmul_kernel(a_ref, b_ref, o_ref, bq_ref):
    @pl.when(pl.program_id(1) == 0)
    def _():
        bq_ref[...] = b_ref[...].astype(bq_ref.dtype)

    o_ref[...] = jnp.dot(a_ref[...], bq_ref[...], preferred_element_type=f32).astype(o_ref.dtype)


def _proj_row_tile(rows):
    return next(t for t in (1024, 512, ROW_TILE) if rows % t == 0)


def _proj_col_tile(cols, n_tiles=4):
    groups = cols // LANE
    return next(g for g in range(groups // n_tiles, 0, -1) if groups % g == 0) * LANE


def matmul(a, b, *, tm, tn, n_cols=None, out_dtype=f32, name="matmul"):
    M, K = a.shape
    N = b.shape[-1] if n_cols is None else n_cols
    assert M % tm == 0 and N % tn == 0, (M, N, tm, tn)
    b_spec = (pl.BlockSpec((K, tn), lambda j, i: (0, j)) if b.ndim == 2 else
              pl.BlockSpec((None, K, tn), lambda j, i: (0, 0, j)))
    return pl.pallas_call(
        _matmul_kernel,
        grid=(N // tn, M // tm),
        in_specs=[pl.BlockSpec((tm, K), lambda j, i: (i, 0)), b_spec],
        out_specs=pl.BlockSpec((tm, tn), lambda j, i: (i, j)),
        out_shape=jax.ShapeDtypeStruct((M, N), out_dtype),
        scratch_shapes=[pltpu.VMEM((K, tn), a.dtype)],
        compiler_params=_params("arbitrary", "arbitrary"),
        name=name,
    )(a, b)


def _conv3(x, prev_row, next_row, w, b):
    tm = x.shape[0]
    row = lax.broadcasted_iota(jnp.int32, x.shape, 0)
    xm = jnp.where(row == 0, prev_row, pltpu.roll(x, 1, 0))
    xp = jnp.where(row == tm - 1, next_row, pltpu.roll(x, tm - 1, 0))
    return w[0:1] * xm + w[1:2] * x + w[2:3] * xp + b


def _seq_edges(i, tiles_per_batch):
    r = i % tiles_per_batch
    lat_tiles = tiles_per_batch - 1
    first = jnp.logical_or(r == 0, r == lat_tiles)
    last = jnp.logical_or(r == lat_tiles - 1, r == lat_tiles)
    return first, last


HALO = 16


def _halo_rows(prev_ref, next_ref, first, last):
    prev_row = jnp.where(first, 0.0, prev_ref[...].astype(f32)[HALO - 1:HALO, :])
    next_row = jnp.where(last, 0.0, next_ref[...].astype(f32)[0:1, :])
    return prev_row, next_row


def _halo_specs(col_block, tc, n_rows):
    per = ROW_TILE // HALO
    n_blocks = n_rows // HALO
    prev = pl.BlockSpec((HALO, tc), lambda i, c: (jnp.maximum(i * per - 1, 0), col_block(c)))
    nxt = pl.BlockSpec((HALO, tc), lambda i, c: (jnp.minimum((i + 1) * per, n_blocks - 1), col_block(c)))
    return prev, nxt


def _hyena_pre_kernel(tiles_per_batch, x0_ref, x0p_ref, x0n_ref, x1_ref, x1p_ref, x1n_ref,
                      v_ref, vp_ref, vn_ref, w0_ref, w1_ref, w2_ref, b0_ref, b1_ref, b2_ref,
                      x0_out, g_out):
    first, last = _seq_edges(pl.program_id(0), tiles_per_batch)

    def conv(x_ref, p_ref, n_ref, w_ref, b_ref):
        prev_row, next_row = _halo_rows(p_ref, n_ref, first, last)
        return _conv3(x_ref[...].astype(f32), prev_row, next_row, w_ref[...], b_ref[...])

    x0 = conv(x0_ref, x0p_ref, x0n_ref, w0_ref, b0_ref)
    x1 = conv(x1_ref, x1p_ref, x1n_ref, w1_ref, b1_ref)
    v = conv(v_ref, vp_ref, vn_ref, w2_ref, b2_ref)
    g = v * x1
    x0_out[...] = x0.astype(x0_out.dtype)
    g_out[...] = g


def hyena_pre(p, conv_w, conv_b, tiles_per_batch, *, tc=1024):
    R = p.shape[0]
    W = HY_WIDTH
    nb = W // tc
    specs = []
    for part in range(3):
        col = functools.partial(lambda c, part: part * nb + c, part=part)
        main = pl.BlockSpec((ROW_TILE, tc), functools.partial(lambda i, c, col: (i, col(c)), col=col))
        prev, nxt = _halo_specs(col, tc, R)
        specs += [main, prev, nxt]
    wspecs = [pl.BlockSpec((3, tc), functools.partial(lambda i, c, part: (0, part * nb + c), part=part))
              for part in range(3)]
    bspecs = [pl.BlockSpec((1, tc), functools.partial(lambda i, c, part: (0, part * nb + c), part=part))
              for part in range(3)]
    out_spec = pl.BlockSpec((ROW_TILE, tc), lambda i, c: (i, c))
    args = [p] * 9 + [conv_w] * 3 + [conv_b.reshape(1, -1)] * 3
    return pl.pallas_call(
        functools.partial(_hyena_pre_kernel, tiles_per_batch),
        grid=(R // ROW_TILE, nb),
        in_specs=specs + wspecs + bspecs,
        out_specs=[out_spec, out_spec],
        out_shape=[jax.ShapeDtypeStruct((R, W), bf16), jax.ShapeDtypeStruct((R, W), f32)],
        compiler_params=_params("arbitrary", "arbitrary"),
        name="hyena_pre",
    )(*args)


def _filter_kernel(n, rows, bands_ref, w1t_ref, w1c_ref, w1s_ref, b1_ref, fr_ref, w2_ref, b2_ref,
                   w3_ref, dl_ref, o_ref, nrm_ref):
    step = pl.program_id(0)

    def offsets(shape, axis):
        j = step * rows + lax.broadcasted_iota(jnp.int32, shape, axis)
        return jnp.where(j < n, j, 2 * n - j).astype(f32), j != n

    d_row, _ = offsets((1, rows), 1)
    t_row = d_row / float(n - 1)
    ang = (2.0 * math.pi / n) * bands_ref[...] * d_row
    fr = fr_ref[...]
    z1 = (w1t_ref[...] * t_row
          + jnp.dot(w1c_ref[...], jnp.cos(ang), preferred_element_type=f32, precision=HIGHEST)
          - jnp.dot(w1s_ref[...], jnp.sin(ang), preferred_element_type=f32, precision=HIGHEST)
          + b1_ref[...])
    hdn = jnp.sin(fr * z1)
    hdn = jnp.sin(fr * (jnp.dot(w2_ref[...], hdn, preferred_element_type=f32, precision=HIGHEST) + b2_ref[...]))
    h = jnp.dot(hdn.T.astype(bf16), w3_ref[0].astype(bf16), preferred_element_type=f32)
    d_col, valid = offsets((rows, 1), 0)
    t = d_col / float(n - 1)
    h = h * jnp.exp(-t * jnp.abs(dl_ref[...]))
    h = jnp.where(valid, h, 0.0)
    o_ref[...] = h

    @pl.when(step == 0)
    def _():
        nrm_ref[...] = jnp.zeros_like(nrm_ref)

    nrm_ref[...] += jnp.sum(jnp.abs(h), axis=0, keepdims=True)


def hyena_filter(n, w1, b1, freq, w2, b2, w3, *, rows=256):
    W = HY_WIDTH
    Hd = HY_HIDDEN
    bands = jnp.linspace(1e-4, HY_BANDS - 1, HY_BANDS, dtype=f32).reshape(HY_BANDS, 1)
    max_decay = math.log(HY_TARGET) / HY_SHORT_DECAY_PCT
    min_decay = math.log(HY_TARGET) / HY_LONG_DECAY_PCT
    deltas = jnp.linspace(min_decay, max_decay, W, dtype=f32).reshape(1, W)
    steps = 2 * n // rows
    half_steps = n // rows
    full = lambda s: (0, 0)
    w3r = w3.reshape(Hd, 2, W).transpose(1, 0, 2)
    return pl.pallas_call(
        functools.partial(_filter_kernel, n, rows),
        grid=(steps,),
        in_specs=[
            pl.BlockSpec((HY_BANDS, 1), full),
            pl.BlockSpec((Hd, 1), full),
            pl.BlockSpec((Hd, HY_BANDS), full),
            pl.BlockSpec((Hd, HY_BANDS), full),
            pl.BlockSpec((Hd, 1), full),
            pl.BlockSpec((Hd, 1), full),
            pl.BlockSpec((Hd, Hd), full),
            pl.BlockSpec((Hd, 1), full),
            pl.BlockSpec((1, Hd, W), lambda s: (jnp.where(s * rows < n, 0, 1), 0, 0)),
            pl.BlockSpec((1, W), full),
        ],
        out_specs=[pl.BlockSpec((rows, W), lambda s: (s, 0)), pl.BlockSpec((1, W), full)],
        out_shape=[jax.ShapeDtypeStruct((2 * n, W), f32), jax.ShapeDtypeStruct((1, W), f32)],
        compiler_params=_params("arbitrary"),
        name="hyena_filter",
    )(bands, w1[0:1].T, w1[1:1 + HY_BANDS].T, w1[1 + HY_BANDS:].T, b1.reshape(Hd, 1), freq.reshape(Hd, 1),
      w2.T, b2.reshape(Hd, 1), w3r, deltas)


@functools.lru_cache(maxsize=None)
def _fft_constants():
    L1, L2 = FFT_L1, FFT_L2
    L = L1 * L2
    k1 = np.arange(L1)
    nh = np.arange(L1 // 2)
    th = 2.0 * np.pi * np.outer(k1, nh) / L1
    m1r = np.concatenate([np.cos(th), -np.sin(th)], axis=0)
    m1i = np.concatenate([np.sin(th), np.cos(th)], axis=0)
    thf = 2.0 * np.pi * np.outer(k1, np.arange(L1)) / L1
    m1f = np.concatenate([np.cos(thf), -np.sin(thf)], axis=0)
    n2 = np.arange(L2)
    k2 = np.arange(L2)
    m = (k1[:, None, None] * n2[None, None, :] + L1 * k2[None, :, None] * n2[None, None, :]) % L
    ph = 2.0 * np.pi * m / L
    gr, gi = np.cos(ph), -np.sin(ph)
    g = np.concatenate([np.concatenate([gr, -gi], axis=2), np.concatenate([gi, gr], axis=2)], axis=1)
    gt = np.transpose(g, (0, 2, 1))
    thi = 2.0 * np.pi * np.outer(nh, k1) / L1
    m3r = np.concatenate([np.cos(thi), np.sin(thi)], axis=0)
    m3i = np.concatenate([-np.sin(thi), np.cos(thi)], axis=0)
    cast = lambda a: np.asarray(a, dtype=np.float32)
    return dict(m1r=cast(m1r), m1i=cast(m1i), m1f=cast(m1f), g=cast(g), gt=cast(gt), m3r=cast(m3r), m3i=cast(m3i))


@functools.lru_cache(maxsize=None)
def _ctx_dft_constants(n):
    L = 2 * n
    k = np.arange(L)
    th = 2.0 * np.pi * np.outer(k, np.arange(n)) / L
    c, s = np.cos(th), np.sin(th)
    mc = np.concatenate([np.concatenate([c, s], axis=1), np.concatenate([-s, c], axis=1)], axis=0)
    thf = 2.0 * np.pi * np.outer(k, np.arange(L)) / L
    mf = np.concatenate([np.cos(thf), -np.sin(thf)], axis=0)
    ct, st = c.T, s.T
    minv = np.concatenate([np.concatenate([ct, -st], axis=1), np.concatenate([st, ct], axis=1)], axis=0)
    cast = lambda a: np.asarray(a, dtype=np.float32)
    return dict(mc=cast(mc), mf=cast(mf), minv=cast(minv))


def _bf16_constants(consts):
    return {k: jnp.asarray(v).astype(bf16) for k, v in consts.items()}


SUB = 8
N2C = 64


def _store_step1(o_ref, j, a):
    for ch in range(a.shape[0] // SUB):
        o_ref[0, ch, j] = a[ch * SUB:(ch + 1) * SUB]


def _fft1_kernel(re_ref, im_ref, mr_ref, mi_ref, o_ref):
    base = pl.program_id(2) * N2C
    for j in range(N2C):
        rows = pl.ds(base + j, FFT_L1 // 2, stride=FFT_L2)
        _store_step1(o_ref, j,
                     jnp.dot(mr_ref[...], re_ref[0, rows, :].astype(bf16), preferred_element_type=f32)
                     + jnp.dot(mi_ref[...], im_ref[0, rows, :].astype(bf16), preferred_element_type=f32))


def _step1_out(P, W, index):
    groups = 2 * FFT_L1 // SUB
    spec = pl.BlockSpec((1, groups, N2C, SUB, LANE), index)
    return spec, jax.ShapeDtypeStruct((P, groups, FFT_L2, SUB, W), f32)


def fft_step1(g3, m1r, m1i, n_lat):
    B, _, W = g3.shape
    half = FFT_L1 // 2
    out_spec, out_shape = _step1_out(B // 2, W, lambda p, c, k: (p, 0, k, 0, c))
    return pl.pallas_call(
        _fft1_kernel,
        grid=(B // 2, W // LANE, FFT_L2 // N2C),
        in_specs=[pl.BlockSpec((1, n_lat, LANE), lambda p, c, k: (2 * p, 0, c)),
                  pl.BlockSpec((1, n_lat, LANE), lambda p, c, k: (2 * p + 1, 0, c)),
                  pl.BlockSpec((2 * FFT_L1, half), lambda p, c, k: (0, 0)),
                  pl.BlockSpec((2 * FFT_L1, half), lambda p, c, k: (0, 0))],
        out_specs=out_spec,
        out_shape=out_shape,
        compiler_params=_params("arbitrary", "arbitrary", "arbitrary"),
        name="fft_step1",
    )(g3, g3, m1r, m1i)


def _fft1_filter_kernel(x_ref, m_ref, o_ref):
    base = pl.program_id(1) * N2C
    for j in range(N2C):
        rows = pl.ds(base + j, FFT_L1, stride=FFT_L2)
        _store_step1(o_ref, j, jnp.dot(m_ref[...], x_ref[rows, :].astype(bf16), preferred_element_type=f32))


def fft_step1_filter(ts, m1f):
    L, W = ts.shape
    out_spec, out_shape = _step1_out(1, W, lambda c, k: (0, 0, k, 0, c))
    return pl.pallas_call(
        _fft1_filter_kernel,
        grid=(W // LANE, FFT_L2 // N2C),
        in_specs=[pl.BlockSpec((L, LANE), lambda c, k: (0, c)),
                  pl.BlockSpec((2 * FFT_L1, FFT_L1), lambda c, k: (0, 0))],
        out_specs=out_spec,
        out_shape=out_shape,
        compiler_params=_params("arbitrary", "arbitrary"),
        name="fft_step1_filter",
    )(ts, m1f)


def _cmul(yr, yi, hr, hi):
    return yr * hr - yi * hi, yr * hi + yi * hr


def _step1_column(a_refs, j):
    col = lambda ref: ref[0, pl.ds(j, FFT_L2, stride=SUB), :]
    re = jnp.concatenate([col(a_refs[0]), col(a_refs[1])], axis=1)
    im = jnp.concatenate([col(a_refs[2]), col(a_refs[3])], axis=1)
    return jnp.concatenate([re, im], axis=0).astype(bf16)


def _fft2_filter_kernel(a0_ref, a1_ref, a2_ref, a3_ref, g_ref, nrm_ref, o_ref):
    scale = 1.0 / (nrm_ref[...] * float(FFT_L1 * FFT_L2))
    for j in range(SUB):
        a = _step1_column((a0_ref, a1_ref, a2_ref, a3_ref), j)
        o_ref[j] = (jnp.dot(g_ref[j], a, preferred_element_type=f32) * scale).astype(o_ref.dtype)


def _step1_specs(index):
    def spec(part, lane_half):
        return pl.BlockSpec((1, FFT_L2 * SUB, LANE),
                            lambda *g: (index(*g)[0], part * (FFT_L1 // SUB) + index(*g)[1],
                                        2 * index(*g)[2] + lane_half))
    return [spec(0, 0), spec(0, 1), spec(1, 0), spec(1, 1)]


def fft_step2_filter(af, g, nrm):
    W = af.shape[-1]
    ct = 2 * LANE
    L1, R2 = FFT_L1, 2 * FFT_L2
    af = af.reshape(1, -1, W)
    return pl.pallas_call(
        _fft2_filter_kernel,
        grid=(L1 // SUB, W // ct),
        in_specs=_step1_specs(lambda k, c: (0, k, c)) + [
            pl.BlockSpec((SUB, R2, R2), lambda k, c: (k, 0, 0)),
            pl.BlockSpec((1, ct), lambda k, c: (0, c))],
        out_specs=pl.BlockSpec((SUB, R2, ct), lambda k, c: (k, 0, c)),
        out_shape=jax.ShapeDtypeStruct((L1, R2, W), bf16),
        compiler_params=_params("arbitrary", "arbitrary"),
        name="fft_step2_filter",
    )(af, af, af, af, g, nrm)


def _fft2_kernel(a0_ref, a1_ref, a2_ref, a3_ref, g_ref, gt_ref, h_ref, ore_ref, oim_ref):
    half = FFT_L2
    for j in range(SUB):
        a = _step1_column((a0_ref, a1_ref, a2_ref, a3_ref), j)
        y = jnp.dot(g_ref[j], a, preferred_element_type=f32)
        pr, pi = _cmul(y[:half], y[half:], h_ref[j, :half].astype(f32), h_ref[j, half:].astype(f32))
        pcat = jnp.concatenate([pr, pi], axis=0).astype(bf16)
        b = jnp.dot(gt_ref[j], pcat, preferred_element_type=f32)
        for ch in range(FFT_L2 // N2C):
            ore_ref[0, ch, j] = b[ch * N2C:(ch + 1) * N2C]
            oim_ref[0, ch, j] = b[half + ch * N2C:half + (ch + 1) * N2C]


def fft_step2(a, g, gt, hf):
    P, W = a.shape[0], a.shape[-1]
    ct = 2 * LANE
    L1, R2 = FFT_L1, 2 * FFT_L2
    a = a.reshape(P, -1, W)
    out = pl.BlockSpec((1, FFT_L2 // N2C, SUB, N2C, ct), lambda k, c, p: (p, 0, k, 0, c))
    shape = jax.ShapeDtypeStruct((P, FFT_L2 // N2C, L1, N2C, W), f32)
    return pl.pallas_call(
        _fft2_kernel,
        grid=(L1 // SUB, W // ct, P),
        in_specs=_step1_specs(lambda k, c, p: (p, k, c)) + [
            pl.BlockSpec((SUB, R2, R2), lambda k, c, p: (k, 0, 0)),
            pl.BlockSpec((SUB, R2, R2), lambda k, c, p: (k, 0, 0)),
            pl.BlockSpec((SUB, R2, ct), lambda k, c, p: (k, 0, c))],
        out_specs=[out, out],
        out_shape=[shape, shape],
        compiler_params=_params("arbitrary", "arbitrary", "arbitrary"),
        name="fft_step2",
    )(a, a, a, a, g, gt, hf)


def _fft3_kernel(bre_ref, bim_ref, mr_ref, mi_ref, o_ref):
    L1 = FFT_L1
    half = L1 // 2
    base = pl.program_id(2) * N2C
    for j in range(N2C):
        br = bre_ref[0, pl.ds(j, L1, stride=N2C), :].astype(bf16)
        bi = bim_ref[0, pl.ds(j, L1, stride=N2C), :].astype(bf16)
        z = (jnp.dot(mr_ref[...], br, preferred_element_type=f32)
             + jnp.dot(mi_ref[...], bi, preferred_element_type=f32))
        rows = pl.ds(base + j, half, stride=FFT_L2)
        o_ref[0, 0, rows, :] = z[:half]
        o_ref[0, 1, rows, :] = z[half:]


def fft_step3(bre, bim, m3r, m3i, rows_total, n_lat):
    P, W = bre.shape[0], bre.shape[-1]
    L1, L2 = FFT_L1, FFT_L2
    bre = bre.reshape(P, -1, W)
    bim = bim.reshape(P, -1, W)
    spec = pl.BlockSpec((1, L1 * N2C, LANE), lambda p, c, k: (p, k, c))
    return pl.pallas_call(
        _fft3_kernel,
        grid=(P, W // LANE, L2 // N2C),
        in_specs=[spec, spec,
                  pl.BlockSpec((L1, L1), lambda p, c, k: (0, 0)),
                  pl.BlockSpec((L1, L1), lambda p, c, k: (0, 0))],
        out_specs=pl.BlockSpec((1, 2, n_lat, LANE), lambda p, c, k: (p, 0, 0, c)),
        out_shape=jax.ShapeDtypeStruct((P, 2, rows_total, W), f32),
        compiler_params=_params("arbitrary", "arbitrary", "arbitrary"),
        name="fft_step3",
    )(bre, bim, m3r, m3i)


def _ctx_filter_kernel(n, ts_ref, m_ref, nrm_ref, o_ref):
    scale = 1.0 / (nrm_ref[...] * float(2 * n))
    o_ref[...] = jnp.dot(m_ref[...], ts_ref[...].astype(bf16), preferred_element_type=f32) * scale


def ctx_filter_spectrum(ts, mf, nrm, *, ct=256):
    L, W = ts.shape
    return pl.pallas_call(
        functools.partial(_ctx_filter_kernel, L // 2),
        grid=(W // ct,),
        in_specs=[pl.BlockSpec((L, ct), lambda c: (0, c)),
                  pl.BlockSpec((2 * L, L), lambda c: (0, 0)),
                  pl.BlockSpec((1, ct), lambda c: (0, c))],
        out_specs=pl.BlockSpec((2 * L, ct), lambda c: (0, c)),
        out_shape=jax.ShapeDtypeStruct((2 * L, W), f32),
        compiler_params=_params("arbitrary"),
        name="ctx_filter_spectrum",
    )(ts, mf, nrm)


def _ctx_conv_kernel(n, re_ref, im_ref, mc_ref, minv_ref, h_ref, y_in_ref, o_ref):
    del y_in_ref
    L = 2 * n
    z = jnp.concatenate([re_ref[0], im_ref[0]], axis=0).astype(bf16)
    y = jnp.dot(mc_ref[...], z, preferred_element_type=f32)
    pr, pi = _cmul(y[:L], y[L:], h_ref[:L], h_ref[L:])
    pcat = jnp.concatenate([pr, pi], axis=0).astype(bf16)
    out = jnp.dot(minv_ref[...], pcat, preferred_element_type=f32)
    o_ref[0, 0] = out[:n]
    o_ref[0, 1] = out[n:]


def ctx_long_conv(gz, y4, hfc, consts, n_lat, n_ctx, *, ct=256):
    B, T, W = gz.shape
    blk = n_lat // n_ctx
    L = 2 * n_ctx
    out = pl.pallas_call(
        functools.partial(_ctx_conv_kernel, n_ctx),
        grid=(B // 2, W // ct),
        in_specs=[pl.BlockSpec((1, n_ctx, ct), lambda p, c: (2 * p, blk, c)),
                  pl.BlockSpec((1, n_ctx, ct), lambda p, c: (2 * p + 1, blk, c)),
                  pl.BlockSpec((2 * L, L), lambda p, c: (0, 0)),
                  pl.BlockSpec((L, 2 * L), lambda p, c: (0, 0)),
                  pl.BlockSpec((2 * L, ct), lambda p, c: (0, c)),
                  pl.BlockSpec(memory_space=pl.ANY)],
        out_specs=pl.BlockSpec((1, 2, n_ctx, ct), lambda p, c: (p, 0, blk, c)),
        out_shape=jax.ShapeDtypeStruct(y4.shape, f32),
        input_output_aliases={5: 0},
        compiler_params=_params("arbitrary", "arbitrary"),
        name="ctx_long_conv",
    )(gz, gz, consts["mc"], consts["minv"], hfc, y4)
    return out.reshape(B, T, W)


def hyena_long_conv(gz, n_lat, n_ctx, f_w1, f_b1, f_freq, f_w2, f_b2, f_w3):
    B, T, W = gz.shape
    cst = _bf16_constants(_fft_constants())
    L1, L2 = FFT_L1, FFT_L2
    assert 2 * n_lat == L1 * L2 and T % L2 == 0 and B % 2 == 0
    ts, nrm = hyena_filter(n_lat, f_w1, f_b1, f_freq, f_w2, f_b2, f_w3)
    af = fft_step1_filter(ts, cst["m1f"])
    hf = fft_step2_filter(af, cst["g"], nrm)
    a = fft_step1(gz, cst["m1r"], cst["m1i"], n_lat)
    bre, bim = fft_step2(a, cst["g"], cst["gt"], hf)
    y = fft_step3(bre, bim, cst["m3r"], cst["m3i"], T, n_lat)
    ccst = _bf16_constants(_ctx_dft_constants(n_ctx))
    ts_c, nrm_c = hyena_filter(n_ctx, f_w1, f_b1, f_freq, f_w2, f_b2, f_w3)
    hfc = ctx_filter_spectrum(ts_c, ccst["mf"], nrm_c)
    return ctx_long_conv(gz, y, hfc, ccst, n_lat, n_ctx)


@functools.lru_cache(maxsize=None)
def _rope_tables(n_lat, n_ctx):
    half = HEAD_DIM // 2
    nf = half // 2
    inv = ROPE_THETA ** (-np.arange(nf, dtype=np.float64) / nf)
    t = np.arange(n_lat)
    pos = np.stack([t // GRID_W, t % GRID_W], axis=1).astype(np.float64)
    ang = pos[:, :, None] * inv[None, None, :]
    cos = np.concatenate([np.cos(ang), np.cos(ang)], axis=2).reshape(n_lat, HEAD_DIM)
    sin = np.concatenate([-np.sin(ang), np.sin(ang)], axis=2).reshape(n_lat, HEAD_DIM)
    cos = np.concatenate([cos, np.ones((n_ctx, HEAD_DIM))], axis=0)
    sin = np.concatenate([sin, np.zeros((n_ctx, HEAD_DIM))], axis=0)
    return np.asarray(cos, np.float32), np.asarray(sin, np.float32)


def _norm_rope_heads(xs, w, cos, sin, lo_lane):
    nf = HEAD_DIM // 4
    ms = [jnp.mean(x * x, axis=-1, keepdims=True) for x in xs]
    ys = [x * lax.rsqrt(m + NORM_EPS) * w for x, m in zip(xs, ms)]
    ps = [jnp.where(lo_lane, pltpu.roll(y, HEAD_DIM - nf, 1), pltpu.roll(y, nf, 1)) for y in ys]
    return [y * cos + p * sin for y, p in zip(ys, ps)]


def _qkv_prep_kernel(q_ref, k_ref, v_ref, cos_ref, sin_ref, qn_ref, kn_ref, q_out, kt_out, v_out):
    cos = cos_ref[...]
    sin = sin_ref[...]
    lane = lax.broadcasted_iota(jnp.int32, cos.shape, 1)
    lo_lane = (lane % (HEAD_DIM // 2)) < (HEAD_DIM // 4)
    head = lambda ref, h: ref[:, h * HEAD_DIM:(h + 1) * HEAD_DIM].astype(f32)
    scale = HEAD_DIM ** -0.5 * math.log2(math.e)
    qs = _norm_rope_heads([head(q_ref, h) for h in range(ATT_HEADS)], qn_ref[...], cos, sin, lo_lane)
    for h, q in enumerate(qs):
        q_out[:, h * HEAD_DIM:(h + 1) * HEAD_DIM] = (q * scale).astype(q_out.dtype)
    ks = _norm_rope_heads([head(k_ref, h) for h in range(ATT_KV_HEADS)], kn_ref[...], cos, sin, lo_lane)
    ones_col = (lane == 0).astype(v_out.dtype)
    for h, k in enumerate(ks):
        sl = slice(h * HEAD_DIM, (h + 1) * HEAD_DIM)
        kt_out[0, sl, :] = k.T.astype(kt_out.dtype)
        v_out[:, 2 * h * HEAD_DIM:(2 * h + 1) * HEAD_DIM] = v_ref[:, sl]
        v_out[:, (2 * h + 1) * HEAD_DIM:(2 * h + 2) * HEAD_DIM] = ones_col


def qkv_prep(p, q_norm, k_norm, B, n_lat, n_ctx):
    R = p.shape[0]
    T = n_lat + n_ctx
    tpb = T // ROW_TILE
    cos, sin = _rope_tables(n_lat, n_ctx)
    return pl.pallas_call(
        _qkv_prep_kernel,
        grid=(R // ROW_TILE,),
        in_specs=[pl.BlockSpec((ROW_TILE, ATT_WIDTH), lambda i: (i, EVEN_OFF_Q // ATT_WIDTH)),
                  pl.BlockSpec((ROW_TILE, ATT_KV_WIDTH), lambda i: (i, EVEN_OFF_K // ATT_KV_WIDTH)),
                  pl.BlockSpec((ROW_TILE, ATT_KV_WIDTH), lambda i: (i, EVEN_OFF_V // ATT_KV_WIDTH)),
                  pl.BlockSpec((ROW_TILE, HEAD_DIM), lambda i: (i % tpb, 0)),
                  pl.BlockSpec((ROW_TILE, HEAD_DIM), lambda i: (i % tpb, 0)),
                  pl.BlockSpec((1, HEAD_DIM), lambda i: (0, 0)),
                  pl.BlockSpec((1, HEAD_DIM), lambda i: (0, 0))],
        out_specs=[pl.BlockSpec((ROW_TILE, ATT_WIDTH), lambda i: (i, 0)),
                   pl.BlockSpec((1, ATT_KV_WIDTH, ROW_TILE), lambda i: (i // tpb, 0, i % tpb)),
                   pl.BlockSpec((ROW_TILE, 2 * ATT_KV_WIDTH), lambda i: (i, 0))],
        out_shape=[jax.ShapeDtypeStruct((R, ATT_WIDTH), bf16),
                   jax.ShapeDtypeStruct((B, ATT_KV_WIDTH, T), bf16),
                   jax.ShapeDtypeStruct((R, 2 * ATT_KV_WIDTH), bf16)],
        compiler_params=_params("arbitrary"),
        name="qkv_prep",
    )(p, p, p, jnp.asarray(cos), jnp.asarray(sin), q_norm.reshape(1, HEAD_DIM), k_norm.reshape(1, HEAD_DIM))


ATT_CHUNK = 256
ATT_Q_TILE = 512


def _attention_kernel(k_lo, q_ref, kt_ref, v_ref, *rest):
    o_ref, qs_ref, sa_ref, sb_ref, pa_ref, pb_ref, os_ref = rest[-7:]
    tq = q_ref.shape[1]
    n_chunks = ATT_GROUP * tq // ATT_CHUNK
    for h in range(ATT_GROUP):
        qs_ref[h * tq:(h + 1) * tq, :] = q_ref[0, :, h * HEAD_DIM:(h + 1) * HEAD_DIM]

    rows = lambda c: slice(c * ATT_CHUNK, (c + 1) * ATT_CHUNK)
    s_refs = (sa_ref, sb_ref)
    p_refs = (pa_ref, pb_ref)

    def scores(c):
        s_refs[c % 2][:, k_lo:] = jnp.dot(qs_ref[rows(c), :], kt_ref[0, :, k_lo:], preferred_element_type=f32)

    def exponentials(c):
        s = s_refs[c % 2][:, k_lo:]
        m = jnp.max(s, axis=-1, keepdims=True)
        p_refs[c % 2][:, k_lo:] = jnp.exp2(s - m).astype(bf16)

    def weighted_values(c):
        r = jnp.dot(p_refs[c % 2][:, k_lo:], v_ref[0, k_lo:, :], preferred_element_type=f32)
        os_ref[rows(c), :] = r[:, :HEAD_DIM] / r[:, HEAD_DIM:HEAD_DIM + 1]

    scores(0)
    for c in range(n_chunks):
        if c + 1 < n_chunks:
            scores(c + 1)
        exponentials(c)
        if c >= 1:
            weighted_values(c - 1)
    weighted_values(n_chunks - 1)
    for h in range(ATT_GROUP):
        o_ref[0, :, h * HEAD_DIM:(h + 1) * HEAD_DIM] = os_ref[h * tq:(h + 1) * tq, :].astype(o_ref.dtype)


def attention(q, kt, v, B, n_lat, n_ctx):
    R = q.shape[0]
    T = n_lat + n_ctx
    gw = ATT_GROUP * HEAD_DIM
    q3 = q.reshape(B, T, ATT_WIDTH)
    v3 = v.reshape(B, T, 2 * ATT_KV_WIDTH)
    out_shape = jax.ShapeDtypeStruct((B, T, ATT_WIDTH), bf16)

    def call(name, k_lo, tq, q_tiles, first_tile, extra_in, extra_specs, aliases):
        qo_spec = pl.BlockSpec((1, tq, gw), lambda b, g, i: (b, first_tile + i, g))
        return pl.pallas_call(
            functools.partial(_attention_kernel, k_lo),
            grid=(B, ATT_KV_HEADS, q_tiles),
            in_specs=[qo_spec,
                      pl.BlockSpec((1, HEAD_DIM, T), lambda b, g, i: (b, g, 0)),
                      pl.BlockSpec((1, T, 2 * HEAD_DIM), lambda b, g, i: (b, 0, g))] + extra_specs,
            out_specs=qo_spec,
            out_shape=out_shape,
            scratch_shapes=[pltpu.VMEM((ATT_GROUP * tq, HEAD_DIM), bf16),
                            pltpu.VMEM((ATT_CHUNK, T), f32), pltpu.VMEM((ATT_CHUNK, T), f32),
                            pltpu.VMEM((ATT_CHUNK, T), bf16), pltpu.VMEM((ATT_CHUNK, T), bf16),
                            pltpu.VMEM((ATT_GROUP * tq, HEAD_DIM), f32)],
            input_output_aliases=aliases,
            compiler_params=_params("arbitrary", "arbitrary", "arbitrary"),
            name=name,
        )(q3, kt, v3, *extra_in)

    att = call("attention", 0, ATT_Q_TILE, n_lat // ATT_Q_TILE, 0, [], [], {})
    att = call("attention_ctx", n_lat, n_ctx, 1, n_lat // n_ctx, [att], [pl.BlockSpec(memory_space=pl.ANY)], {3: 0})
    return att.reshape(R, ATT_WIDTH)


def _post_residual(x, y, g_post, gate):
    yn = y * lax.rsqrt(jnp.mean(y * y, axis=-1, keepdims=True) + NORM_EPS) * g_post
    return x + gate * yn


def _silu(x):
    return x * jax.nn.sigmoid(x)


def _even_out_kernel(tiles_per_batch, x0_ref, g_ref, yc_ref, ghy_ref, att_ref, ga0_ref, ga1_ref, bias_ref, w_ref,
                     gp_ref, m_ref, x_ref, c_ref, gn_ref, mn_ref, o_ref, hn_ref):
    D = x_ref.shape[-1]
    g = g_ref[...]
    hy = x0_ref[...].astype(f32) * (yc_ref[...] + g * bias_ref[...]) * _silu(ghy_ref[...].astype(f32))
    g_att = jnp.concatenate([ga0_ref[...], ga1_ref[...]], axis=1).astype(f32)
    at = att_ref[...].astype(f32) * _silu(g_att)
    lhs = jnp.concatenate([hy, at], axis=1).astype(bf16)
    y = jnp.dot(lhs, w_ref[...], preferred_element_type=f32)
    x_new = _post_residual(_input_rows(tiles_per_batch, x_ref, c_ref), y, gp_ref[...], m_ref[0, :, 2 * D:3 * D])
    o_ref[...] = x_new
    hn_ref[...] = _norm_mod_rows(x_new, gn_ref[...], mn_ref[0]).astype(hn_ref.dtype)


def even_out(x0, g, yconv, p, att, hy_bias, w_out, g_post, mods, x_lat, x_ctx, next_g_pre, next_mods,
             tiles_per_batch, ctx_row):
    R = x0.shape[0]
    D = x_lat.shape[1]
    W = HY_WIDTH
    hw = ATT_WIDTH // 2
    row = lambda i: (i, 0)
    full = lambda i: (0, 0)
    lat_spec, ctx_spec = _input_row_specs(tiles_per_batch, D)
    return pl.pallas_call(
        functools.partial(_even_out_kernel, tiles_per_batch),
        grid=(R // ROW_TILE,),
        in_specs=[pl.BlockSpec((ROW_TILE, W), row),
                  pl.BlockSpec((ROW_TILE, W), row),
                  pl.BlockSpec((ROW_TILE, W), row),
                  pl.BlockSpec((ROW_TILE, W), lambda i: (i, EVEN_OFF_GHY // W)),
                  pl.BlockSpec((ROW_TILE, ATT_WIDTH), row),
                  pl.BlockSpec((ROW_TILE, hw), lambda i: (i, EVEN_OFF_GATT // hw)),
                  pl.BlockSpec((ROW_TILE, hw), lambda i: (i, EVEN_OFF_GATT // hw + 1)),
                  pl.BlockSpec((1, W), full),
                  pl.BlockSpec((W + ATT_WIDTH, D), full),
                  pl.BlockSpec((1, D), full),
                  pl.BlockSpec((1, 1, 3 * D), lambda i: (_mod_row(i, tiles_per_batch, ctx_row), 0, 0)),
                  lat_spec, ctx_spec,
                  pl.BlockSpec((1, D), full),
                  pl.BlockSpec((1, 1, 3 * D), lambda i: (_mod_row(i, tiles_per_batch, ctx_row), 0, 0))],
        out_specs=[pl.BlockSpec((ROW_TILE, D), row), pl.BlockSpec((ROW_TILE, D), row)],
        out_shape=[jax.ShapeDtypeStruct((R, D), f32), jax.ShapeDtypeStruct((R, D), bf16)],
        compiler_params=_params("arbitrary"),
        name="even_out",
    )(x0, g, yconv, p, att, p, p, hy_bias.reshape(1, W), w_out.astype(bf16), g_post.reshape(1, D), mods,
      x_lat, x_ctx, next_g_pre.reshape(1, D), next_mods)


def even_layer(x_lat, x_ctx, mods, next_g_pre, next_mods, B, n_lat, n_ctx, g_pre, g_post, w_in, conv_w, conv_b,
               f_w1, f_b1, f_freq, f_w2, f_b2, f_w3, hy_bias, q_norm, k_norm, w_out):
    T = n_lat + n_ctx
    tpb = T // ROW_TILE
    h = norm_mod(x_lat, x_ctx, g_pre, mods, tpb, B)
    p = matmul(h, w_in, tm=_proj_row_tile(B * T), tn=_proj_col_tile(w_in.shape[1]), out_dtype=bf16,
               name="even_in_proj")
    x0, g = hyena_pre(p, conv_w, conv_b, tpb)
    yconv = hyena_long_conv(g.reshape(B, T, HY_WIDTH), n_lat, n_ctx, f_w1, f_b1, f_freq, f_w2, f_b2, f_w3)
    q, kt, v = qkv_prep(p, q_norm, k_norm, B, n_lat, n_ctx)
    att = attention(q, kt, v, B, n_lat, n_ctx)
    return even_out(x0, g, yconv.reshape(B * T, HY_WIDTH), p, att, hy_bias, w_out, g_post, mods, x_lat, x_ctx,
                    next_g_pre, next_mods, tpb, B)


def _mlstm_prep_kernel(tiles_per_batch, q_ref, qp_ref, qn_ref, k_ref, kp_ref, kn_ref,
                       wq_ref, wk_ref, bq_ref, bk_ref, q_out, kt_out):
    first, last = _seq_edges(pl.program_id(0), tiles_per_batch)
    prev_row, next_row = _halo_rows(qp_ref, qn_ref, first, last)
    q = _silu(_conv3(q_ref[...].astype(f32), prev_row, next_row, wq_ref[...], bq_ref[...]))
    q_out[...] = q.astype(q_out.dtype)
    prev_row, next_row = _halo_rows(kp_ref, kn_ref, first, last)
    k = _silu(_conv3(k_ref[...].astype(f32), prev_row, next_row, wk_ref[...], bk_ref[...])) * (ML_QK ** -0.5)
    for h in range(k.shape[1] // ML_QK):
        sl = slice(h * ML_QK, (h + 1) * ML_QK)
        kt_out[0, sl, :] = k[:, sl].T.astype(kt_out.dtype)


def mlstm_prep(p, conv_w, conv_b, B, T, *, tc=1024):
    R = p.shape[0]
    tpb = T // ROW_TILE
    nb = ML_QK_WIDTH // tc
    qcol = lambda c: c
    kcol = lambda c: nb + c
    qprev, qnext = _halo_specs(qcol, tc, R)
    kprev, knext = _halo_specs(kcol, tc, R)
    return pl.pallas_call(
        functools.partial(_mlstm_prep_kernel, tpb),
        grid=(R // ROW_TILE, nb),
        in_specs=[pl.BlockSpec((ROW_TILE, tc), lambda i, c: (i, c)), qprev, qnext,
                  pl.BlockSpec((ROW_TILE, tc), lambda i, c: (i, nb + c)), kprev, knext,
                  pl.BlockSpec((3, tc), lambda i, c: (0, c)),
                  pl.BlockSpec((3, tc), lambda i, c: (0, nb + c)),
                  pl.BlockSpec((1, tc), lambda i, c: (0, c)),
                  pl.BlockSpec((1, tc), lambda i, c: (0, nb + c))],
        out_specs=[pl.BlockSpec((ROW_TILE, tc), lambda i, c: (i, c)),
                   pl.BlockSpec((1, tc, ROW_TILE), lambda i, c: (i // tpb, c, i % tpb))],
        out_shape=[jax.ShapeDtypeStruct((R, ML_QK_WIDTH), bf16),
                   jax.ShapeDtypeStruct((B, ML_QK_WIDTH, T), bf16)],
        compiler_params=_params("arbitrary", "arbitrary"),
        name="mlstm_prep",
    )(p, p, p, p, p, p, conv_w, conv_w, conv_b.reshape(1, -1), conv_b.reshape(1, -1))


def _log_sigmoid(x):
    return jnp.minimum(x, 0.0) - jnp.log(1.0 + jnp.exp(-jnp.abs(x)))


def _mlstm_gates_kernel(g_ref, b_ref, gc_out, gr_out):
    pre = g_ref[...] + b_ref[...]
    lane = lax.broadcasted_iota(jnp.int32, pre.shape, 1)
    is_forget = (lane // ML_HEADS) % 2 == 1
    gc = jnp.where(is_forget, _log_sigmoid(pre), pre)
    gc_out[...] = gc
    gr_out[0] = gc.T


def mlstm_gates(gates, gate_b, B, T):
    R = gates.shape[0]
    gb = jnp.pad(gate_b, (0, LANE - gate_b.shape[0])).reshape(1, LANE)
    return pl.pallas_call(
        _mlstm_gates_kernel,
        grid=(B,),
        in_specs=[pl.BlockSpec((T, LANE), lambda b: (b, 0)),
                  pl.BlockSpec((1, LANE), lambda b: (0, 0))],
        out_specs=[pl.BlockSpec((T, LANE), lambda b: (b, 0)),
                   pl.BlockSpec((1, LANE, T), lambda b: (b, 0, 0))],
        out_shape=[jax.ShapeDtypeStruct((R, LANE), f32), jax.ShapeDtypeStruct((B, LANE, T), f32)],
        compiler_params=_params("arbitrary"),
        name="mlstm_gates",
    )(gates, gb)


def _mlstm_chunk_setup(reverse, gc_ref, gr_ref, m_ref, ms_ref):
    Lc = gc_ref.shape[0]
    H = ML_HEADS
    i_off = 2 * H if reverse else 0
    f_off = i_off + H
    t_idx = lax.broadcasted_iota(jnp.int32, (Lc, Lc), 0)
    s_idx = lax.broadcasted_iota(jnp.int32, (Lc, Lc), 1)
    causal = (s_idx >= t_idx) if reverse else (s_idx <= t_idx)
    tri = causal.astype(f32)
    gc = gc_ref[...]
    gr = gr_ref[0]
    b_col_all = jnp.dot(tri, gc[:, f_off:f_off + H], preferred_element_type=f32, precision=HIGHEST)
    b_row_all = lax.dot_general(gr[f_off:f_off + H, :], tri, (((1,), (1,)), ((), ())),
                                preferred_element_type=f32, precision=HIGHEST)
    i_col_all = gc[:, i_off:i_off + H]
    i_row_all = gr[i_off:i_off + H, :]
    end = 0 if reverse else Lc - 1
    b_end = b_col_all[end:end + 1, :]
    m_prev = m_ref[0:1, 0:H]
    g_col = b_end - b_col_all + i_col_all
    m_new = jnp.maximum(b_end + m_prev, jnp.max(g_col, axis=0, keepdims=True))
    a_prev = jnp.exp(b_end + m_prev - m_new)
    m_ref[:, 0:H] = jnp.broadcast_to(m_new, (m_ref.shape[0], H))
    b_end_s = b_row_all[:, end:end + 1]
    m_prev_s = ms_ref[0:H, 0:1]
    g_row = b_end_s - b_row_all + i_row_all
    m_new_s = jnp.maximum(b_end_s + m_prev_s, jnp.max(g_row, axis=1, keepdims=True))
    a_row = jnp.exp(g_row - m_new_s)
    ms_ref[0:H, :] = jnp.broadcast_to(m_new_s, (H, ms_ref.shape[1]))
    return dict(causal=causal, i_row=i_row_all, b_col=b_col_all, b_row=b_row_all,
                m_prev=m_prev, a_row=a_row, a_prev=a_prev)


class _MlstmChain:
    def __init__(self, h, cs, q_ref, kt_ref, v_ref, o_ref, ct_ref):
        self.h, self.cs = h, cs
        self.q_ref, self.kt_ref, self.v_ref, self.o_ref, self.ct_ref = q_ref, kt_ref, v_ref, o_ref, ct_ref

    def _q(self):
        return self.q_ref[:, self.h * ML_QK:(self.h + 1) * ML_QK]

    def _kt(self):
        return self.kt_ref[0, self.h * ML_QK:(self.h + 1) * ML_QK, :]

    def _v(self):
        Lc = self.q_ref.shape[0]
        ones_col = (lax.broadcasted_iota(jnp.int32, (Lc, LANE), 1) == 0).astype(bf16)
        return jnp.concatenate([self.v_ref[:, self.h * ML_V:(self.h + 1) * ML_V], ones_col], axis=1)

    def scores(self):
        self.qk = jnp.dot(self._q(), self._kt(), preferred_element_type=f32)

    def gates(self):
        h, cs = self.h, self.cs
        i_row = cs["i_row"][h:h + 1, :]
        b_col = cs["b_col"][:, h:h + 1]
        b_row = cs["b_row"][h:h + 1, :]
        m_prev = cs["m_prev"][:, h:h + 1]
        d = jnp.where(cs["causal"], b_col + (i_row - b_row), -jnp.inf)
        inter = b_col + m_prev
        self.m_row = jnp.maximum(inter, jnp.max(d, axis=-1, keepdims=True))
        self.s = (self.qk * jnp.exp(d - self.m_row)).astype(bf16)
        self.w_prev = jnp.exp(inter - self.m_row)

    def values(self):
        h = self.h
        self.ct = self.ct_ref[h]
        qw = (self._q().astype(f32) * self.w_prev).astype(bf16)
        tot = jnp.dot(jnp.concatenate([self.s, qw], axis=1),
                      jnp.concatenate([self._v(), self.ct.astype(bf16)], axis=0),
                      preferred_element_type=f32)
        scale = 1.0 / jnp.maximum(jnp.abs(tot[:, ML_V:ML_V + 1]), jnp.exp(-self.m_row))
        self.o_ref[:, h * ML_V:(h + 1) * ML_V] = (tot[:, :ML_V] * scale).astype(self.o_ref.dtype)

    def update(self):
        h = self.h
        kta = (self._kt().astype(f32) * self.cs["a_row"][h:h + 1, :]).astype(bf16)
        self.ct_ref[h] = (self.cs["a_prev"][:, h:h + 1] * self.ct
                          + jnp.dot(kta, self._v(), preferred_element_type=f32))


def _mlstm_scan_kernel(qf_ref, ktf_ref, vf_ref, gcf_ref, grf_ref, qb_ref, ktb_ref, vb_ref, gcb_ref, grb_ref,
                       of_ref, ob_ref, ctf_ref, mf_ref, msf_ref, ctb_ref, mb_ref, msb_ref):
    @pl.when(pl.program_id(1) == 0)
    def _():
        for ref in (ctf_ref, mf_ref, msf_ref, ctb_ref, mb_ref, msb_ref):
            ref[...] = jnp.zeros_like(ref)

    fwd = _mlstm_chunk_setup(False, gcf_ref, grf_ref, mf_ref, msf_ref)
    bwd = _mlstm_chunk_setup(True, gcb_ref, grb_ref, mb_ref, msb_ref)
    chains = []
    for h in range(ML_HEADS):
        chains.append(_MlstmChain(h, fwd, qf_ref, ktf_ref, vf_ref, of_ref, ctf_ref))
        chains.append(_MlstmChain(h, bwd, qb_ref, ktb_ref, vb_ref, ob_ref, ctb_ref))
    stages = ("scores", "gates", "values", "update")
    for k in range(len(chains) + len(stages) - 1):
        for depth, stage in enumerate(stages):
            if 0 <= k - depth < len(chains):
                getattr(chains[k - depth], stage)()


def mlstm_scan(q, kt, p, gc, gr, B, n_lat, n_ctx):
    R = q.shape[0]
    Lc = ML_CHUNK
    tpb = (n_lat + n_ctx) // Lc
    lat = n_lat // Lc
    ctx = n_ctx // Lc

    def specs(reverse):
        def chunk(j):
            if reverse:
                return tpb - 1 - j
            return jnp.where(j < ctx, lat + j, j - ctx)
        ins = [pl.BlockSpec((Lc, ML_QK_WIDTH), lambda b, j: (b * tpb + chunk(j), 0)),
               pl.BlockSpec((1, ML_QK_WIDTH, Lc), lambda b, j: (b, 0, chunk(j))),
               pl.BlockSpec((Lc, ML_WIDTH), lambda b, j: (b * tpb + chunk(j), ODD_OFF_V // ML_WIDTH)),
               pl.BlockSpec((Lc, LANE), lambda b, j: (b * tpb + chunk(j), 0)),
               pl.BlockSpec((1, LANE, Lc), lambda b, j: (b, 0, chunk(j)))]
        out = pl.BlockSpec((Lc, ML_WIDTH), lambda b, j: (b * tpb + chunk(j), 0))
        return ins, out

    ins_f, out_f = specs(False)
    ins_b, out_b = specs(True)
    state = [pltpu.VMEM((ML_HEADS, ML_QK, ML_V + LANE), f32), pltpu.VMEM((SUB, LANE), f32),
             pltpu.VMEM((SUB, LANE), f32)]
    return pl.pallas_call(
        _mlstm_scan_kernel,
        grid=(B, tpb),
        in_specs=ins_f + ins_b,
        out_specs=[out_f, out_b],
        out_shape=[jax.ShapeDtypeStruct((R, ML_WIDTH), bf16), jax.ShapeDtypeStruct((R, ML_WIDTH), bf16)],
        scratch_shapes=state + state,
        compiler_params=_params("arbitrary", "arbitrary"),
        name="mlstm_scan",
    )(q, kt, p, gc, gr, q, kt, p, gc, gr)


def _odd_out_kernel(hf_ref, hb_ref, o_ref, z_ref, hn_ref, w_ref, gp_ref, m_ref, x_ref, out_ref):
    D = x_ref.shape[-1]
    hs = (hf_ref[0].astype(f32) + hb_ref[0].astype(f32)) * jax.nn.sigmoid(o_ref[0].astype(f32))
    segs = [hs[:, h * ML_V:(h + 1) * ML_V] for h in range(ML_HEADS)]
    ms = [jnp.mean(seg * seg, axis=-1, keepdims=True) for seg in segs]
    parts = [seg * lax.rsqrt(m + NORM_EPS) for seg, m in zip(segs, ms)]
    hn = jnp.concatenate(parts, axis=1) * hn_ref[...] * _silu(z_ref[0].astype(f32))
    y = jnp.dot(hn.astype(bf16), w_ref[...], preferred_element_type=f32)
    out_ref[0] = _post_residual(x_ref[0], y, gp_ref[...], m_ref[0, :, 2 * D:3 * D])


def odd_out(hf, hb, p, head_norm, w_out, g_post, mods, x_all, B, n_lat, T, *, tm=512):
    D = x_all.shape[1]
    view = lambda a: a.reshape(B, T, a.shape[1])
    row = lambda b, i: (b, i, 0)
    full = lambda b, i: (0, 0)
    return pl.pallas_call(
        _odd_out_kernel,
        grid=(B, n_lat // tm),
        in_specs=[pl.BlockSpec((1, tm, ML_WIDTH), row),
                  pl.BlockSpec((1, tm, ML_WIDTH), row),
                  pl.BlockSpec((1, tm, ML_WIDTH), lambda b, i: (b, i, ODD_OFF_O // ML_WIDTH)),
                  pl.BlockSpec((1, tm, ML_WIDTH), lambda b, i: (b, i, ODD_OFF_Z // ML_WIDTH)),
                  pl.BlockSpec((1, ML_WIDTH), full),
                  pl.BlockSpec((ML_WIDTH, D), full),
                  pl.BlockSpec((1, D), full),
                  pl.BlockSpec((1, 1, 3 * D), lambda b, i: (b, 0, 0)),
                  pl.BlockSpec((1, tm, D), row)],
        out_specs=pl.BlockSpec((1, tm, D), row),
        out_shape=jax.ShapeDtypeStruct((B, n_lat, D), f32),
        compiler_params=_params("arbitrary", "arbitrary"),
        name="odd_out",
    )(view(hf), view(hb), view(p), view(p), head_norm.reshape(1, ML_WIDTH), w_out.astype(bf16),
      g_post.reshape(1, D), mods, view(x_all))


def odd_layer_last(x_all, h, mods, B, n_lat, n_ctx, g_post, w_in, conv_w, conv_b, gate_b, head_norm, w_out):
    T = n_lat + n_ctx
    tm = _proj_row_tile(B * T)
    p = matmul(h, w_in, tm=tm, tn=_proj_col_tile(ODD_MAIN), n_cols=ODD_MAIN, out_dtype=bf16, name="odd_in_proj")
    n_gates = w_in.shape[-1] - ODD_MAIN
    w_gates = jnp.pad(w_in[0, :, ODD_MAIN:], ((0, 0), (0, LANE - n_gates)))
    gates = matmul(h, w_gates, tm=tm, tn=LANE, name="odd_gate_proj")
    q, kt = mlstm_prep(p, conv_w, conv_b, B, T)
    gc, gr = mlstm_gates(gates, gate_b, B, T)
    hf, hb = mlstm_scan(q, kt, p, gc, gr, B, n_lat, n_ctx)
    return odd_out(hf, hb, p, head_norm, w_out, g_post, mods, x_all, B, n_lat, T)


def kernel(x, c, ctx, c_ctx, w_mod, b_mod, g_pre, g_post, e_w_in, e_conv_w, e_conv_b, e_filt_w1,
           e_filt_b1, e_filt_freq, e_filt_w2, e_filt_b2, e_filt_w3, e_hy_bias, e_q_norm, e_k_norm,
           e_w_out, o_w_in, o_conv_w, o_conv_b, o_gate_b, o_head_norm, o_w_out):
    B, n_lat, D = x.shape
    n_ctx = ctx.shape[1]
    T = n_lat + n_ctx
    depth = w_mod.shape[0]
    assert depth == 2 and B + 1 <= 8 and n_ctx == ROW_TILE and n_lat % ROW_TILE == 0
    cond = jnp.concatenate([c, c_ctx[None], jnp.zeros((8 - B - 1, D), f32)], axis=0)
    mods_all = adaln_all(cond, w_mod, b_mod)
    mods0 = mods_all[0].reshape(8, 1, 3 * D)
    mods1 = mods_all[1].reshape(8, 1, 3 * D)
    x_all, h1 = even_layer(x.reshape(B * n_lat, D), ctx.reshape(B * n_ctx, D), mods0, g_pre[1], mods1,
                           B, n_lat, n_ctx, g_pre[0], g_post[0], e_w_in[0],
                           e_conv_w[0], e_conv_b[0], e_filt_w1[0], e_filt_b1[0], e_filt_freq[0], e_filt_w2[0],
                           e_filt_b2[0], e_filt_w3[0], e_hy_bias[0], e_q_norm[0], e_k_norm[0], e_w_out[0])
    out = odd_layer_last(x_all, h1, mods1, B, n_lat, n_ctx, g_post[1], o_w_in,
                         o_conv_w[0], o_conv_b[0], o_gate_b[0], o_head_norm[0], o_w_out[0])
    return out.reshape(B, n_lat, D)
```

```python
import functools
import math

import numpy as np
import jax
import jax.numpy as jnp
from jax import lax
from jax.experimental import pallas as pl
from jax.experimental.pallas import tpu as pltpu

f32 = jnp.float32
bf16 = jnp.bfloat16
HIGHEST = lax.Precision.HIGHEST

D_MODEL = 1024
GRID_W = 64
NORM_EPS = 1e-6

HY_WIDTH = 1024
HY_EMB = 33
HY_BANDS = (HY_EMB - 1) // 2
HY_HIDDEN = 64
HY_TARGET = 1e-2
HY_SHORT_DECAY_PCT = 0.3
HY_LONG_DECAY_PCT = 1.5

ATT_HEADS = 8
ATT_KV_HEADS = 2
ATT_GROUP = ATT_HEADS // ATT_KV_HEADS
HEAD_DIM = 128
ATT_WIDTH = ATT_HEADS * HEAD_DIM
ATT_KV_WIDTH = ATT_KV_HEADS * HEAD_DIM
ROPE_THETA = 10000.0
EVEN_OFF_XV = 0
EVEN_OFF_GHY = 3 * HY_WIDTH
EVEN_OFF_Q = EVEN_OFF_GHY + HY_WIDTH
EVEN_OFF_K = EVEN_OFF_Q + ATT_WIDTH
EVEN_OFF_V = EVEN_OFF_K + ATT_KV_WIDTH
EVEN_OFF_GATT = EVEN_OFF_V + ATT_KV_WIDTH
EVEN_IN = EVEN_OFF_GATT + ATT_WIDTH

ML_HEADS = 8
ML_QK = 128
ML_V = 256
ML_QK_WIDTH = ML_HEADS * ML_QK
ML_WIDTH = ML_HEADS * ML_V
ODD_OFF_Q = 0
ODD_OFF_K = ML_QK_WIDTH
ODD_OFF_V = 2 * ML_QK_WIDTH
ODD_OFF_O = ODD_OFF_V + ML_WIDTH
ODD_OFF_Z = ODD_OFF_O + ML_WIDTH
ODD_OFF_GATES = ODD_OFF_Z + ML_WIDTH
ODD_MAIN = ODD_OFF_GATES

ROW_TILE = 256
ML_CHUNK = 256
LANE = 128
VMEM_LIMIT_BYTES = 48 * 1024 * 1024

FFT_L1 = 64
FFT_L2 = 128


def _params(*sem):
    return pltpu.CompilerParams(dimension_semantics=sem, vmem_limit_bytes=VMEM_LIMIT_BYTES)


def _adaln_kernel(c_ref, w_ref, b_ref, o_ref):
    c = c_ref[...]
    s = c * jax.nn.sigmoid(c)
    o_ref[0] = jnp.dot(s, w_ref[0], preferred_element_type=f32, precision=HIGHEST) + b_ref[0]


def adaln_all(cond, w_mod, b_mod, *, tn=768):
    depth, D, N = w_mod.shape
    return pl.pallas_call(
        _adaln_kernel,
        grid=(depth, N // tn),
        in_specs=[
            pl.BlockSpec((8, D), lambda l, j: (0, 0)),
            pl.BlockSpec((1, D, tn), lambda l, j: (l, 0, j)),
            pl.BlockSpec((1, 1, tn), lambda l, j: (l, 0, j)),
        ],
        out_specs=pl.BlockSpec((1, 8, tn), lambda l, j: (l, 0, j)),
        out_shape=jax.ShapeDtypeStruct((depth, 8, N), f32),
        compiler_params=_params("arbitrary", "arbitrary"),
        name="adaln",
    )(cond, w_mod, b_mod.reshape(depth, 1, N))


def _mod_row(i, tiles_per_batch, ctx_row):
    lat_tiles = tiles_per_batch - 1
    return jnp.where(i % tiles_per_batch == lat_tiles, ctx_row, i // tiles_per_batch)


def _norm_mod_rows(x, g, m):
    D = x.shape[-1]
    y = x * lax.rsqrt(jnp.mean(x * x, axis=-1, keepdims=True) + NORM_EPS)
    return y * g * (1.0 + m[:, D:2 * D]) + m[:, 0:D]


def _input_row_specs(tiles_per_batch, D):
    lat_tiles = tiles_per_batch - 1

    def lat_index(i):
        return ((i // tiles_per_batch) * lat_tiles + jnp.minimum(i % tiles_per_batch, lat_tiles - 1), 0)

    return (pl.BlockSpec((ROW_TILE, D), lat_index),
            pl.BlockSpec((ROW_TILE, D), lambda i: (i // tiles_per_batch, 0)))


def _input_rows(tiles_per_batch, lat_ref, ctx_ref):
    is_ctx = pl.program_id(0) % tiles_per_batch == tiles_per_batch - 1
    return jnp.where(is_ctx, ctx_ref[...], lat_ref[...])


def _norm_mod_kernel(tiles_per_batch, x_ref, c_ref, g_ref, m_ref, o_ref):
    x = _input_rows(tiles_per_batch, x_ref, c_ref)
    o_ref[...] = _norm_mod_rows(x, g_ref[...], m_ref[0]).astype(o_ref.dtype)


def norm_mod(x_lat, x_ctx, g, mods, tiles_per_batch, ctx_row):
    D = x_lat.shape[1]
    R = x_lat.shape[0] + x_ctx.shape[0]
    lat_spec, ctx_spec = _input_row_specs(tiles_per_batch, D)
    return pl.pallas_call(
        functools.partial(_norm_mod_kernel, tiles_per_batch),
        grid=(R // ROW_TILE,),
        in_specs=[
            lat_spec, ctx_spec,
            pl.BlockSpec((1, D), lambda i: (0, 0)),
            pl.BlockSpec((1, 1, 3 * D), lambda i: (_mod_row(i, tiles_per_batch, ctx_row), 0, 0)),
        ],
        out_specs=pl.BlockSpec((ROW_TILE, D), lambda i: (i, 0)),
        out_shape=jax.ShapeDtypeStruct((R, D), bf16),
        compiler_params=_params("arbitrary"),
        name="norm_mod",
    )(x_lat, x_ctx, g.reshape(1, D), mods)


def _matmul_kernel(a_ref, b_ref, o_ref, bq_ref):
    @pl.when(pl.program_id(1) == 0)
    def _():
        bq_ref[...] = b_ref[...].astype(bq_ref.dtype)

    o_ref[...] = jnp.dot(a_ref[...], bq_ref[...], preferred_element_type=f32).astype(o_ref.dtype)


def _proj_row_tile(rows):
    return next(t for t in (1024, 512, ROW_TILE) if rows % t == 0)


def _proj_col_tile(cols, n_tiles=4):
    groups = cols // LANE
    return next(g for g in range(groups // n_tiles, 0, -1) if groups % g == 0) * LANE


def matmul(a, b, *, tm, tn, n_cols=None, out_dtype=f32, name="matmul"):
    M, K = a.shape
    N = b.shape[-1] if n_cols is None else n_cols
    assert M % tm == 0 and N % tn == 0, (M, N, tm, tn)
    b_spec = (pl.BlockSpec((K, tn), lambda j, i: (0, j)) if b.ndim == 2 else
              pl.BlockSpec((None, K, tn), lambda j, i: (0, 0, j)))
    return pl.pallas_call(
        _matmul_kernel,
        grid=(N // tn, M // tm),
        in_specs=[pl.BlockSpec((tm, K), lambda j, i: (i, 0)), b_spec],
        out_specs=pl.BlockSpec((tm, tn), lambda j, i: (i, j)),
        out_shape=jax.ShapeDtypeStruct((M, N), out_dtype),
        scratch_shapes=[pltpu.VMEM((K, tn), a.dtype)],
        compiler_params=_params("arbitrary", "arbitrary"),
        name=name,
    )(a, b)


def _conv3(x, prev_row, next_row, w, b):
    tm = x.shape[0]
    row = lax.broadcasted_iota(jnp.int32, x.shape, 0)
    xm = jnp.where(row == 0, prev_row, pltpu.roll(x, 1, 0))
    xp = jnp.where(row == tm - 1, next_row, pltpu.roll(x, tm - 1, 0))
    return w[0:1] * xm + w[1:2] * x + w[2:3] * xp + b


def _seq_edges(i, tiles_per_batch):
    r = i % tiles_per_batch
    lat_tiles = tiles_per_batch - 1
    first = jnp.logical_or(r == 0, r == lat_tiles)
    last = jnp.logical_or(r == lat_tiles - 1, r == lat_tiles)
    return first, last


HALO = 16


def _halo_rows(prev_ref, next_ref, first, last):
    prev_row = jnp.where(first, 0.0, prev_ref[...].astype(f32)[HALO - 1:HALO, :])
    next_row = jnp.where(last, 0.0, next_ref[...].astype(f32)[0:1, :])
    return prev_row, next_row


def _halo_specs(col_block, tc, n_rows):
    per = ROW_TILE // HALO
    n_blocks = n_rows // HALO
    prev = pl.BlockSpec((HALO, tc), lambda i, c: (jnp.maximum(i * per - 1, 0), col_block(c)))
    nxt = pl.BlockSpec((HALO, tc), lambda i, c: (jnp.minimum((i + 1) * per, n_blocks - 1), col_block(c)))
    return prev, nxt


def _hyena_pre_kernel(tiles_per_batch, x0_ref, x0p_ref, x0n_ref, x1_ref, x1p_ref, x1n_ref,
                      v_ref, vp_ref, vn_ref, w0_ref, w1_ref, w2_ref, b0_ref, b1_ref, b2_ref,
                      x0_out, g_out):
    first, last = _seq_edges(pl.program_id(0), tiles_per_batch)

    def conv(x_ref, p_ref, n_ref, w_ref, b_ref):
        prev_row, next_row = _halo_rows(p_ref, n_ref, first, last)
        return _conv3(x_ref[...].astype(f32), prev_row, next_row, w_ref[...], b_ref[...])

    x0 = conv(x0_ref, x0p_ref, x0n_ref, w0_ref, b0_ref)
    x1 = conv(x1_ref, x1p_ref, x1n_ref, w1_ref, b1_ref)
    v = conv(v_ref, vp_ref, vn_ref, w2_ref, b2_ref)
    g = v * x1
    x0_out[...] = x0.astype(x0_out.dtype)
    g_out[...] = g


def hyena_pre(p, conv_w, conv_b, tiles_per_batch, *, tc=1024):
    R = p.shape[0]
    W = HY_WIDTH
    nb = W // tc
    specs = []
    for part in range(3):
        col = functools.partial(lambda c, part: part * nb + c, part=part)
        main = pl.BlockSpec((ROW_TILE, tc), functools.partial(lambda i, c, col: (i, col(c)), col=col))
        prev, nxt = _halo_specs(col, tc, R)
        specs += [main, prev, nxt]
    wspecs = [pl.BlockSpec((3, tc), functools.partial(lambda i, c, part: (0, part * nb + c), part=part))
              for part in range(3)]
    bspecs = [pl.BlockSpec((1, tc), functools.partial(lambda i, c, part: (0, part * nb + c), part=part))
              for part in range(3)]
    out_spec = pl.BlockSpec((ROW_TILE, tc), lambda i, c: (i, c))
    args = [p] * 9 + [conv_w] * 3 + [conv_b.reshape(1, -1)] * 3
    return pl.pallas_call(
        functools.partial(_hyena_pre_kernel, tiles_per_batch),
        grid=(R // ROW_TILE, nb),
        in_specs=specs + wspecs + bspecs,
        out_specs=[out_spec, out_spec],
        out_shape=[jax.ShapeDtypeStruct((R, W), bf16), jax.ShapeDtypeStruct((R, W), f32)],
        compiler_params=_params("arbitrary", "arbitrary"),
        name="hyena_pre",
    )(*args)


def _filter_kernel(n, rows, bands_ref, w1t_ref, w1c_ref, w1s_ref, b1_ref, fr_ref, w2_ref, b2_ref,
                   w3_ref, dl_ref, o_ref, nrm_ref):
    step = pl.program_id(0)

    def offsets(shape, axis):
        j = step * rows + lax.broadcasted_iota(jnp.int32, shape, axis)
        return jnp.where(j < n, j, 2 * n - j).astype(f32), j != n

    d_row, _ = offsets((1, rows), 1)
    t_row = d_row / float(n - 1)
    ang = (2.0 * math.pi / n) * bands_ref[...] * d_row
    fr = fr_ref[...]
    z1 = (w1t_ref[...] * t_row
          + jnp.dot(w1c_ref[...], jnp.cos(ang), preferred_element_type=f32, precision=HIGHEST)
          - jnp.dot(w1s_ref[...], jnp.sin(ang), preferred_element_type=f32, precision=HIGHEST)
          + b1_ref[...])
    hdn = jnp.sin(fr * z1)
    hdn = jnp.sin(fr * (jnp.dot(w2_ref[...], hdn, preferred_element_type=f32, precision=HIGHEST) + b2_ref[...]))
    h = jnp.dot(hdn.T.astype(bf16), w3_ref[0].astype(bf16), preferred_element_type=f32)
    d_col, valid = offsets((rows, 1), 0)
    t = d_col / float(n - 1)
    h = h * jnp.exp(-t * jnp.abs(dl_ref[...]))
    h = jnp.where(valid, h, 0.0)
    o_ref[...] = h

    @pl.when(step == 0)
    def _():
        nrm_ref[...] = jnp.zeros_like(nrm_ref)

    nrm_ref[...] += jnp.sum(jnp.abs(h), axis=0, keepdims=True)


def hyena_filter(n, w1, b1, freq, w2, b2, w3, *, rows=256):
    W = HY_WIDTH
    Hd = HY_HIDDEN
    bands = jnp.linspace(1e-4, HY_BANDS - 1, HY_BANDS, dtype=f32).reshape(HY_BANDS, 1)
    max_decay = math.log(HY_TARGET) / HY_SHORT_DECAY_PCT
    min_decay = math.log(HY_TARGET) / HY_LONG_DECAY_PCT
    deltas = jnp.linspace(min_decay, max_decay, W, dtype=f32).reshape(1, W)
    steps = 2 * n // rows
    half_steps = n // rows
    full = lambda s: (0, 0)
    w3r = w3.reshape(Hd, 2, W).transpose(1, 0, 2)
    return pl.pallas_call(
        functools.partial(_filter_kernel, n, rows),
        grid=(steps,),
        in_specs=[
            pl.BlockSpec((HY_BANDS, 1), full),
            pl.BlockSpec((Hd, 1), full),
            pl.BlockSpec((Hd, HY_BANDS), full),
            pl.BlockSpec((Hd, HY_BANDS), full),
            pl.BlockSpec((Hd, 1), full),
            pl.BlockSpec((Hd, 1), full),
            pl.BlockSpec((Hd, Hd), full),
            pl.BlockSpec((Hd, 1), full),
            pl.BlockSpec((1, Hd, W), lambda s: (jnp.where(s * rows < n, 0, 1), 0, 0)),
            pl.BlockSpec((1, W), full),
        ],
        out_specs=[pl.BlockSpec((rows, W), lambda s: (s, 0)), pl.BlockSpec((1, W), full)],
        out_shape=[jax.ShapeDtypeStruct((2 * n, W), f32), jax.ShapeDtypeStruct((1, W), f32)],
        compiler_params=_params("arbitrary"),
        name="hyena_filter",
    )(bands, w1[0:1].T, w1[1:1 + HY_BANDS].T, w1[1 + HY_BANDS:].T, b1.reshape(Hd, 1), freq.reshape(Hd, 1),
      w2.T, b2.reshape(Hd, 1), w3r, deltas)


@functools.lru_cache(maxsize=None)
def _fft_constants():
    L1, L2 = FFT_L1, FFT_L2
    L = L1 * L2
    k1 = np.arange(L1)
    nh = np.arange(L1 // 2)
    th = 2.0 * np.pi * np.outer(k1, nh) / L1
    m1r = np.concatenate([np.cos(th), -np.sin(th)], axis=0)
    m1i = np.concatenate([np.sin(th), np.cos(th)], axis=0)
    thf = 2.0 * np.pi * np.outer(k1, np.arange(L1)) / L1
    m1f = np.concatenate([np.cos(thf), -np.sin(thf)], axis=0)
    n2 = np.arange(L2)
    k2 = np.arange(L2)
    m = (k1[:, None, None] * n2[None, None, :] + L1 * k2[None, :, None] * n2[None, None, :]) % L
    ph = 2.0 * np.pi * m / L
    gr, gi = np.cos(ph), -np.sin(ph)
    g = np.concatenate([np.concatenate([gr, -gi], axis=2), np.concatenate([gi, gr], axis=2)], axis=1)
    gt = np.transpose(g, (0, 2, 1))
    thi = 2.0 * np.pi * np.outer(nh, k1) / L1
    m3r = np.concatenate([np.cos(thi), np.sin(thi)], axis=0)
    m3i = np.concatenate([-np.sin(thi), np.cos(thi)], axis=0)
    cast = lambda a: np.asarray(a, dtype=np.float32)
    return dict(m1r=cast(m1r), m1i=cast(m1i), m1f=cast(m1f), g=cast(g), gt=cast(gt), m3r=cast(m3r), m3i=cast(m3i))


@functools.lru_cache(maxsize=None)
def _ctx_dft_constants(n):
    L = 2 * n
    k = np.arange(L)
    th = 2.0 * np.pi * np.outer(k, np.arange(n)) / L
    c, s = np.cos(th), np.sin(th)
    mc = np.concatenate([np.concatenate([c, s], axis=1), np.concatenate([-s, c], axis=1)], axis=0)
    thf = 2.0 * np.pi * np.outer(k, np.arange(L)) / L
    mf = np.concatenate([np.cos(thf), -np.sin(thf)], axis=0)
    ct, st = c.T, s.T
    minv = np.concatenate([np.concatenate([ct, -st], axis=1), np.concatenate([st, ct], axis=1)], axis=0)
    cast = lambda a: np.asarray(a, dtype=np.float32)
    return dict(mc=cast(mc), mf=cast(mf), minv=cast(minv))


def _bf16_constants(consts):
    return {k: jnp.asarray(v).astype(bf16) for k, v in consts.items()}


SUB = 8
N2C = 128


def _store_step1(o_ref, j, a):
    for ch in range(a.shape[0] // SUB):
        o_ref[0, ch, j] = a[ch * SUB:(ch + 1) * SUB]


def _fft1_kernel(re_ref, im_ref, mr_ref, mi_ref, o_ref):
    base = pl.program_id(2) * N2C
    for j in range(N2C):
        rows = pl.ds(base + j, FFT_L1 // 2, stride=FFT_L2)
        _store_step1(o_ref, j,
                     jnp.dot(mr_ref[...], re_ref[0, rows, :].astype(bf16), preferred_element_type=f32)
                     + jnp.dot(mi_ref[...], im_ref[0, rows, :].astype(bf16), preferred_element_type=f32))


def _step1_out(P, W, index):
    groups = 2 * FFT_L1 // SUB
    spec = pl.BlockSpec((1, groups, N2C, SUB, LANE), index)
    return spec, jax.ShapeDtypeStruct((P, groups, FFT_L2, SUB, W), f32)


def fft_step1(g3, m1r, m1i, n_lat):
    B, _, W = g3.shape
    half = FFT_L1 // 2
    out_spec, out_shape = _step1_out(B // 2, W, lambda p, c, k: (p, 0, k, 0, c))
    return pl.pallas_call(
        _fft1_kernel,
        grid=(B // 2, W // LANE, FFT_L2 // N2C),
        in_specs=[pl.BlockSpec((1, n_lat, LANE), lambda p, c, k: (2 * p, 0, c)),
                  pl.BlockSpec((1, n_lat, LANE), lambda p, c, k: (2 * p + 1, 0, c)),
                  pl.BlockSpec((2 * FFT_L1, half), lambda p, c, k: (0, 0)),
                  pl.BlockSpec((2 * FFT_L1, half), lambda p, c, k: (0, 0))],
        out_specs=out_spec,
        out_shape=out_shape,
        compiler_params=_params("arbitrary", "arbitrary", "arbitrary"),
        name="fft_step1",
    )(g3, g3, m1r, m1i)


def _fft1_filter_kernel(x_ref, m_ref, o_ref):
    base = pl.program_id(1) * N2C
    for j in range(N2C):
        rows = pl.ds(base + j, FFT_L1, stride=FFT_L2)
        _store_step1(o_ref, j, jnp.dot(m_ref[...], x_ref[rows, :].astype(bf16), preferred_element_type=f32))


def fft_step1_filter(ts, m1f):
    L, W = ts.shape
    out_spec, out_shape = _step1_out(1, W, lambda c, k: (0, 0, k, 0, c))
    return pl.pallas_call(
        _fft1_filter_kernel,
        grid=(W // LANE, FFT_L2 // N2C),
        in_specs=[pl.BlockSpec((L, LANE), lambda c, k: (0, c)),
                  pl.BlockSpec((2 * FFT_L1, FFT_L1), lambda c, k: (0, 0))],
        out_specs=out_spec,
        out_shape=out_shape,
        compiler_params=_params("arbitrary", "arbitrary"),
        name="fft_step1_filter",
    )(ts, m1f)


def _cmul(yr, yi, hr, hi):
    return yr * hr - yi * hi, yr * hi + yi * hr


def _step1_column(a_refs, j):
    col = lambda ref: ref[0, pl.ds(j, FFT_L2, stride=SUB), :]
    re = jnp.concatenate([col(a_refs[0]), col(a_refs[1])], axis=1)
    im = jnp.concatenate([col(a_refs[2]), col(a_refs[3])], axis=1)
    return jnp.concatenate([re, im], axis=0).astype(bf16)


def _fft2_filter_kernel(a0_ref, a1_ref, a2_ref, a3_ref, g_ref, nrm_ref, o_ref):
    scale = 1.0 / (nrm_ref[...] * float(FFT_L1 * FFT_L2))
    for j in range(SUB):
        a = _step1_column((a0_ref, a1_ref, a2_ref, a3_ref), j)
        o_ref[j] = (jnp.dot(g_ref[j], a, preferred_element_type=f32) * scale).astype(o_ref.dtype)


def _step1_specs(index):
    def spec(part, lane_half):
        return pl.BlockSpec((1, FFT_L2 * SUB, LANE),
                            lambda *g: (index(*g)[0], part * (FFT_L1 // SUB) + index(*g)[1],
                                        2 * index(*g)[2] + lane_half))
    return [spec(0, 0), spec(0, 1), spec(1, 0), spec(1, 1)]


def fft_step2_filter(af, g, nrm):
    W = af.shape[-1]
    ct = 2 * LANE
    L1, R2 = FFT_L1, 2 * FFT_L2
    af = af.reshape(1, -1, W)
    return pl.pallas_call(
        _fft2_filter_kernel,
        grid=(L1 // SUB, W // ct),
        in_specs=_step1_specs(lambda k, c: (0, k, c)) + [
            pl.BlockSpec((SUB, R2, R2), lambda k, c: (k, 0, 0)),
            pl.BlockSpec((1, ct), lambda k, c: (0, c))],
        out_specs=pl.BlockSpec((SUB, R2, ct), lambda k, c: (k, 0, c)),
        out_shape=jax.ShapeDtypeStruct((L1, R2, W), bf16),
        compiler_params=_params("arbitrary", "arbitrary"),
        name="fft_step2_filter",
    )(af, af, af, af, g, nrm)


def _fft2_kernel(a0_ref, a1_ref, a2_ref, a3_ref, g_ref, gt_ref, h_ref, ore_ref, oim_ref):
    half = FFT_L2
    for j in range(SUB):
        a = _step1_column((a0_ref, a1_ref, a2_ref, a3_ref), j)
        y = jnp.dot(g_ref[j], a, preferred_element_type=f32)
        pr, pi = _cmul(y[:half], y[half:], h_ref[j, :half].astype(f32), h_ref[j, half:].astype(f32))
        pcat = jnp.concatenate([pr, pi], axis=0).astype(bf16)
        b = jnp.dot(gt_ref[j], pcat, preferred_element_type=f32)
        for ch in range(FFT_L2 // N2C):
            ore_ref[0, ch, j] = b[ch * N2C:(ch + 1) * N2C]
            oim_ref[0, ch, j] = b[half + ch * N2C:half + (ch + 1) * N2C]


def fft_step2(a, g, gt, hf):
    P, W = a.shape[0], a.shape[-1]
    ct = 2 * LANE
    L1, R2 = FFT_L1, 2 * FFT_L2
    a = a.reshape(P, -1, W)
    out = pl.BlockSpec((1, FFT_L2 // N2C, SUB, N2C, ct), lambda k, c, p: (p, 0, k, 0, c))
    shape = jax.ShapeDtypeStruct((P, FFT_L2 // N2C, L1, N2C, W), f32)
    return pl.pallas_call(
        _fft2_kernel,
        grid=(L1 // SUB, W // ct, P),
        in_specs=_step1_specs(lambda k, c, p: (p, k, c)) + [
            pl.BlockSpec((SUB, R2, R2), lambda k, c, p: (k, 0, 0)),
            pl.BlockSpec((SUB, R2, R2), lambda k, c, p: (k, 0, 0)),
            pl.BlockSpec((SUB, R2, ct), lambda k, c, p: (k, 0, c))],
        out_specs=[out, out],
        out_shape=[shape, shape],
        compiler_params=_params("arbitrary", "arbitrary", "arbitrary"),
        name="fft_step2",
    )(a, a, a, a, g, gt, hf)


def _fft3_kernel(bre_ref, bim_ref, mr_ref, mi_ref, o_ref):
    L1 = FFT_L1
    half = L1 // 2
    base = pl.program_id(2) * N2C
    for j in range(N2C):
        br = bre_ref[0, pl.ds(j, L1, stride=N2C), :].astype(bf16)
        bi = bim_ref[0, pl.ds(j, L1, stride=N2C), :].astype(bf16)
        z = (jnp.dot(mr_ref[...], br, preferred_element_type=f32)
             + jnp.dot(mi_ref[...], bi, preferred_element_type=f32))
        rows = pl.ds(base + j, half, stride=FFT_L2)
        o_ref[0, 0, rows, :] = z[:half]
        o_ref[0, 1, rows, :] = z[half:]


def fft_step3(bre, bim, m3r, m3i, rows_total, n_lat):
    P, W = bre.shape[0], bre.shape[-1]
    L1, L2 = FFT_L1, FFT_L2
    bre = bre.reshape(P, -1, W)
    bim = bim.reshape(P, -1, W)
    spec = pl.BlockSpec((1, L1 * N2C, LANE), lambda p, c, k: (p, k, c))
    return pl.pallas_call(
        _fft3_kernel,
        grid=(P, W // LANE, L2 // N2C),
        in_specs=[spec, spec,
                  pl.BlockSpec((L1, L1), lambda p, c, k: (0, 0)),
                  pl.BlockSpec((L1, L1), lambda p, c, k: (0, 0))],
        out_specs=pl.BlockSpec((1, 2, n_lat, LANE), lambda p, c, k: (p, 0, 0, c)),
        out_shape=jax.ShapeDtypeStruct((P, 2, rows_total, W), f32),
        compiler_params=_params("arbitrary", "arbitrary", "arbitrary"),
        name="fft_step3",
    )(bre, bim, m3r, m3i)


def _ctx_filter_kernel(n, ts_ref, m_ref, nrm_ref, o_ref):
    scale = 1.0 / (nrm_ref[...] * float(2 * n))
    o_ref[...] = jnp.dot(m_ref[...], ts_ref[...].astype(bf16), preferred_element_type=f32) * scale


def ctx_filter_spectrum(ts, mf, nrm, *, ct=256):
    L, W = ts.shape
    return pl.pallas_call(
        functools.partial(_ctx_filter_kernel, L // 2),
        grid=(W // ct,),
        in_specs=[pl.BlockSpec((L, ct), lambda c: (0, c)),
                  pl.BlockSpec((2 * L, L), lambda c: (0, 0)),
                  pl.BlockSpec((1, ct), lambda c: (0, c))],
        out_specs=pl.BlockSpec((2 * L, ct), lambda c: (0, c)),
        out_shape=jax.ShapeDtypeStruct((2 * L, W), f32),
        compiler_params=_params("arbitrary"),
        name="ctx_filter_spectrum",
    )(ts, mf, nrm)


def _ctx_conv_kernel(n, re_ref, im_ref, mc_ref, minv_ref, h_ref, y_in_ref, o_ref):
    del y_in_ref
    L = 2 * n
    z = jnp.concatenate([re_ref[0], im_ref[0]], axis=0).astype(bf16)
    y = jnp.dot(mc_ref[...], z, preferred_element_type=f32)
    pr, pi = _cmul(y[:L], y[L:], h_ref[:L], h_ref[L:])
    pcat = jnp.concatenate([pr, pi], axis=0).astype(bf16)
    out = jnp.dot(minv_ref[...], pcat, preferred_element_type=f32)
    o_ref[0, 0] = out[:n]
    o_ref[0, 1] = out[n:]


def ctx_long_conv(gz, y4, hfc, consts, n_lat, n_ctx, *, ct=256):
    B, T, W = gz.shape
    blk = n_lat // n_ctx
    L = 2 * n_ctx
    out = pl.pallas_call(
        functools.partial(_ctx_conv_kernel, n_ctx),
        grid=(B // 2, W // ct),
        in_specs=[pl.BlockSpec((1, n_ctx, ct), lambda p, c: (2 * p, blk, c)),
                  pl.BlockSpec((1, n_ctx, ct), lambda p, c: (2 * p + 1, blk, c)),
                  pl.BlockSpec((2 * L, L), lambda p, c: (0, 0)),
                  pl.BlockSpec((L, 2 * L), lambda p, c: (0, 0)),
                  pl.BlockSpec((2 * L, ct), lambda p, c: (0, c)),
                  pl.BlockSpec(memory_space=pl.ANY)],
        out_specs=pl.BlockSpec((1, 2, n_ctx, ct), lambda p, c: (p, 0, blk, c)),
        out_shape=jax.ShapeDtypeStruct(y4.shape, f32),
        input_output_aliases={5: 0},
        compiler_params=_params("arbitrary", "arbitrary"),
        name="ctx_long_conv",
    )(gz, gz, consts["mc"], consts["minv"], hfc, y4)
    return out.reshape(B, T, W)


def hyena_long_conv(gz, n_lat, n_ctx, f_w1, f_b1, f_freq, f_w2, f_b2, f_w3):
    B, T, W = gz.shape
    cst = _bf16_constants(_fft_constants())
    L1, L2 = FFT_L1, FFT_L2
    assert 2 * n_lat == L1 * L2 and T % L2 == 0 and B % 2 == 0
    ts, nrm = hyena_filter(n_lat, f_w1, f_b1, f_freq, f_w2, f_b2, f_w3)
    af = fft_step1_filter(ts, cst["m1f"])
    hf = fft_step2_filter(af, cst["g"], nrm)
    a = fft_step1(gz, cst["m1r"], cst["m1i"], n_lat)
    bre, bim = fft_step2(a, cst["g"], cst["gt"], hf)
    y = fft_step3(bre, bim, cst["m3r"], cst["m3i"], T, n_lat)
    ccst = _bf16_constants(_ctx_dft_constants(n_ctx))
    ts_c, nrm_c = hyena_filter(n_ctx, f_w1, f_b1, f_freq, f_w2, f_b2, f_w3)
    hfc = ctx_filter_spectrum(ts_c, ccst["mf"], nrm_c)
    return ctx_long_conv(gz, y, hfc, ccst, n_lat, n_ctx)


@functools.lru_cache(maxsize=None)
def _rope_tables(n_lat, n_ctx):
    half = HEAD_DIM // 2
    nf = half // 2
    inv = ROPE_THETA ** (-np.arange(nf, dtype=np.float64) / nf)
    t = np.arange(n_lat)
    pos = np.stack([t // GRID_W, t % GRID_W], axis=1).astype(np.float64)
    ang = pos[:, :, None] * inv[None, None, :]
    cos = np.concatenate([np.cos(ang), np.cos(ang)], axis=2).reshape(n_lat, HEAD_DIM)
    sin = np.concatenate([-np.sin(ang), np.sin(ang)], axis=2).reshape(n_lat, HEAD_DIM)
    cos = np.concatenate([cos, np.ones((n_ctx, HEAD_DIM))], axis=0)
    sin = np.concatenate([sin, np.zeros((n_ctx, HEAD_DIM))], axis=0)
    return np.asarray(cos, np.float32), np.asarray(sin, np.float32)


def _norm_rope_heads(xs, w, cos, sin, lo_lane):
    nf = HEAD_DIM // 4
    ms = [jnp.mean(x * x, axis=-1, keepdims=True) for x in xs]
    ys = [x * lax.rsqrt(m + NORM_EPS) * w for x, m in zip(xs, ms)]
    ps = [jnp.where(lo_lane, pltpu.roll(y, HEAD_DIM - nf, 1), pltpu.roll(y, nf, 1)) for y in ys]
    return [y * cos + p * sin for y, p in zip(ys, ps)]


def _qkv_prep_kernel(q_ref, k_ref, v_ref, cos_ref, sin_ref, qn_ref, kn_ref, q_out, kt_out, v_out):
    cos = cos_ref[...]
    sin = sin_ref[...]
    lane = lax.broadcasted_iota(jnp.int32, cos.shape, 1)
    lo_lane = (lane % (HEAD_DIM // 2)) < (HEAD_DIM // 4)
    head = lambda ref, h: ref[:, h * HEAD_DIM:(h + 1) * HEAD_DIM].astype(f32)
    scale = HEAD_DIM ** -0.5 * math.log2(math.e)
    qs = _norm_rope_heads([head(q_ref, h) for h in range(ATT_HEADS)], qn_ref[...], cos, sin, lo_lane)
    for h, q in enumerate(qs):
        q_out[:, h * HEAD_DIM:(h + 1) * HEAD_DIM] = (q * scale).astype(q_out.dtype)
    ks = _norm_rope_heads([head(k_ref, h) for h in range(ATT_KV_HEADS)], kn_ref[...], cos, sin, lo_lane)
    ones_col = (lane == 0).astype(v_out.dtype)
    for h, k in enumerate(ks):
        sl = slice(h * HEAD_DIM, (h + 1) * HEAD_DIM)
        kt_out[0, sl, :] = k.T.astype(kt_out.dtype)
        v_out[:, 2 * h * HEAD_DIM:(2 * h + 1) * HEAD_DIM] = v_ref[:, sl]
        v_out[:, (2 * h + 1) * HEAD_DIM:(2 * h + 2) * HEAD_DIM] = ones_col


def qkv_prep(p, q_norm, k_norm, B, n_lat, n_ctx):
    R = p.shape[0]
    T = n_lat + n_ctx
    tpb = T // ROW_TILE
    cos, sin = _rope_tables(n_lat, n_ctx)
    return pl.pallas_call(
        _qkv_prep_kernel,
        grid=(R // ROW_TILE,),
        in_specs=[pl.BlockSpec((ROW_TILE, ATT_WIDTH), lambda i: (i, EVEN_OFF_Q // ATT_WIDTH)),
                  pl.BlockSpec((ROW_TILE, ATT_KV_WIDTH), lambda i: (i, EVEN_OFF_K // ATT_KV_WIDTH)),
                  pl.BlockSpec((ROW_TILE, ATT_KV_WIDTH), lambda i: (i, EVEN_OFF_V // ATT_KV_WIDTH)),
                  pl.BlockSpec((ROW_TILE, HEAD_DIM), lambda i: (i % tpb, 0)),
                  pl.BlockSpec((ROW_TILE, HEAD_DIM), lambda i: (i % tpb, 0)),
                  pl.BlockSpec((1, HEAD_DIM), lambda i: (0, 0)),
                  pl.BlockSpec((1, HEAD_DIM), lambda i: (0, 0))],
        out_specs=[pl.BlockSpec((ROW_TILE, ATT_WIDTH), lambda i: (i, 0)),
                   pl.BlockSpec((1, ATT_KV_WIDTH, ROW_TILE), lambda i: (i // tpb, 0, i % tpb)),
                   pl.BlockSpec((ROW_TILE, 2 * ATT_KV_WIDTH), lambda i: (i, 0))],
        out_shape=[jax.ShapeDtypeStruct((R, ATT_WIDTH), bf16),
                   jax.ShapeDtypeStruct((B, ATT_KV_WIDTH, T), bf16),
                   jax.ShapeDtypeStruct((R, 2 * ATT_KV_WIDTH), bf16)],
        compiler_params=_params("arbitrary"),
        name="qkv_prep",
    )(p, p, p, jnp.asarray(cos), jnp.asarray(sin), q_norm.reshape(1, HEAD_DIM), k_norm.reshape(1, HEAD_DIM))


ATT_CHUNK = 256
ATT_Q_TILE = 1024


def _attention_kernel(k_lo, q_ref, kt_ref, v_ref, *rest):
    o_ref, qs_ref, sa_ref, sb_ref, pa_ref, pb_ref, os_ref = rest[-7:]
    tq = q_ref.shape[1]
    n_chunks = ATT_GROUP * tq // ATT_CHUNK
    for h in range(ATT_GROUP):
        qs_ref[h * tq:(h + 1) * tq, :] = q_ref[0, :, h * HEAD_DIM:(h + 1) * HEAD_DIM]

    rows = lambda c: slice(c * ATT_CHUNK, (c + 1) * ATT_CHUNK)
    s_refs = (sa_ref, sb_ref)
    p_refs = (pa_ref, pb_ref)

    def scores(c):
        s_refs[c % 2][:, k_lo:] = jnp.dot(qs_ref[rows(c), :], kt_ref[0, :, k_lo:], preferred_element_type=f32)

    def exponentials(c):
        s = s_refs[c % 2][:, k_lo:]
        m = jnp.max(s, axis=-1, keepdims=True)
        p_refs[c % 2][:, k_lo:] = jnp.exp2(s - m).astype(bf16)

    def weighted_values(c):
        r = jnp.dot(p_refs[c % 2][:, k_lo:], v_ref[0, k_lo:, :], preferred_element_type=f32)
        os_ref[rows(c), :] = r[:, :HEAD_DIM] / r[:, HEAD_DIM:HEAD_DIM + 1]

    scores(0)
    for c in range(n_chunks):
        if c + 1 < n_chunks:
            scores(c + 1)
        exponentials(c)
        if c >= 1:
            weighted_values(c - 1)
    weighted_values(n_chunks - 1)
    for h in range(ATT_GROUP):
        o_ref[0, :, h * HEAD_DIM:(h + 1) * HEAD_DIM] = os_ref[h * tq:(h + 1) * tq, :].astype(o_ref.dtype)


def attention(q, kt, v, B, n_lat, n_ctx):
    R = q.shape[0]
    T = n_lat + n_ctx
    gw = ATT_GROUP * HEAD_DIM
    q3 = q.reshape(B, T, ATT_WIDTH)
    v3 = v.reshape(B, T, 2 * ATT_KV_WIDTH)
    out_shape = jax.ShapeDtypeStruct((B, T, ATT_WIDTH), bf16)

    def call(name, k_lo, tq, q_tiles, first_tile, extra_in, extra_specs, aliases):
        qo_spec = pl.BlockSpec((1, tq, gw), lambda b, g, i: (b, first_tile + i, g))
        return pl.pallas_call(
            functools.partial(_attention_kernel, k_lo),
            grid=(B, ATT_KV_HEADS, q_tiles),
            in_specs=[qo_spec,
                      pl.BlockSpec((1, HEAD_DIM, T), lambda b, g, i: (b, g, 0)),
                      pl.BlockSpec((1, T, 2 * HEAD_DIM), lambda b, g, i: (b, 0, g))] + extra_specs,
            out_specs=qo_spec,
            out_shape=out_shape,
            scratch_shapes=[pltpu.VMEM((ATT_GROUP * tq, HEAD_DIM), bf16),
                            pltpu.VMEM((ATT_CHUNK, T), f32), pltpu.VMEM((ATT_CHUNK, T), f32),
                            pltpu.VMEM((ATT_CHUNK, T), bf16), pltpu.VMEM((ATT_CHUNK, T), bf16),
                            pltpu.VMEM((ATT_GROUP * tq, HEAD_DIM), f32)],
            input_output_aliases=aliases,
            compiler_params=_params("arbitrary", "arbitrary", "arbitrary"),
            name=name,
        )(q3, kt, v3, *extra_in)

    att = call("attention", 0, ATT_Q_TILE, n_lat // ATT_Q_TILE, 0, [], [], {})
    att = call("attention_ctx", n_lat, n_ctx, 1, n_lat // n_ctx, [att], [pl.BlockSpec(memory_space=pl.ANY)], {3: 0})
    return att.reshape(R, ATT_WIDTH)


def _post_residual(x, y, g_post, gate):
    yn = y * lax.rsqrt(jnp.mean(y * y, axis=-1, keepdims=True) + NORM_EPS) * g_post
    return x + gate * yn


def _silu(x):
    return x * jax.nn.sigmoid(x)


def _even_out_kernel(tiles_per_batch, x0_ref, g_ref, yc_ref, ghy_ref, att_ref, ga0_ref, ga1_ref, bias_ref, w_ref,
                     gp_ref, m_ref, x_ref, c_ref, gn_ref, mn_ref, o_ref, hn_ref):
    D = x_ref.shape[-1]
    g = g_ref[...]
    hy = x0_ref[...].astype(f32) * (yc_ref[...] + g * bias_ref[...]) * _silu(ghy_ref[...].astype(f32))
    g_att = jnp.concatenate([ga0_ref[...], ga1_ref[...]], axis=1).astype(f32)
    at = att_ref[...].astype(f32) * _silu(g_att)
    lhs = jnp.concatenate([hy, at], axis=1).astype(bf16)
    y = jnp.dot(lhs, w_ref[...], preferred_element_type=f32)
    x_new = _post_residual(_input_rows(tiles_per_batch, x_ref, c_ref), y, gp_ref[...], m_ref[0, :, 2 * D:3 * D])
    o_ref[...] = x_new
    hn_ref[...] = _norm_mod_rows(x_new, gn_ref[...], mn_ref[0]).astype(hn_ref.dtype)


def even_out(x0, g, yconv, p, att, hy_bias, w_out, g_post, mods, x_lat, x_ctx, next_g_pre, next_mods,
             tiles_per_batch, ctx_row):
    R = x0.shape[0]
    D = x_lat.shape[1]
    W = HY_WIDTH
    hw = ATT_WIDTH // 2
    row = lambda i: (i, 0)
    full = lambda i: (0, 0)
    lat_spec, ctx_spec = _input_row_specs(tiles_per_batch, D)
    return pl.pallas_call(
        functools.partial(_even_out_kernel, tiles_per_batch),
        grid=(R // ROW_TILE,),
        in_specs=[pl.BlockSpec((ROW_TILE, W), row),
                  pl.BlockSpec((ROW_TILE, W), row),
                  pl.BlockSpec((ROW_TILE, W), row),
                  pl.BlockSpec((ROW_TILE, W), lambda i: (i, EVEN_OFF_GHY // W)),
                  pl.BlockSpec((ROW_TILE, ATT_WIDTH), row),
                  pl.BlockSpec((ROW_TILE, hw), lambda i: (i, EVEN_OFF_GATT // hw)),
                  pl.BlockSpec((ROW_TILE, hw), lambda i: (i, EVEN_OFF_GATT // hw + 1)),
                  pl.BlockSpec((1, W), full),
                  pl.BlockSpec((W + ATT_WIDTH, D), full),
                  pl.BlockSpec((1, D), full),
                  pl.BlockSpec((1, 1, 3 * D), lambda i: (_mod_row(i, tiles_per_batch, ctx_row), 0, 0)),
                  lat_spec, ctx_spec,
                  pl.BlockSpec((1, D), full),
                  pl.BlockSpec((1, 1, 3 * D), lambda i: (_mod_row(i, tiles_per_batch, ctx_row), 0, 0))],
        out_specs=[pl.BlockSpec((ROW_TILE, D), row), pl.BlockSpec((ROW_TILE, D), row)],
        out_shape=[jax.ShapeDtypeStruct((R, D), f32), jax.ShapeDtypeStruct((R, D), bf16)],
        compiler_params=_params("arbitrary"),
        name="even_out",
    )(x0, g, yconv, p, att, p, p, hy_bias.reshape(1, W), w_out.astype(bf16), g_post.reshape(1, D), mods,
      x_lat, x_ctx, next_g_pre.reshape(1, D), next_mods)


def even_layer(x_lat, x_ctx, mods, next_g_pre, next_mods, B, n_lat, n_ctx, g_pre, g_post, w_in, conv_w, conv_b,
               f_w1, f_b1, f_freq, f_w2, f_b2, f_w3, hy_bias, q_norm, k_norm, w_out):
    T = n_lat + n_ctx
    tpb = T // ROW_TILE
    h = norm_mod(x_lat, x_ctx, g_pre, mods, tpb, B)
    p = matmul(h, w_in, tm=_proj_row_tile(B * T), tn=_proj_col_tile(w_in.shape[1]), out_dtype=bf16,
               name="even_in_proj")
    x0, g = hyena_pre(p, conv_w, conv_b, tpb)
    yconv = hyena_long_conv(g.reshape(B, T, HY_WIDTH), n_lat, n_ctx, f_w1, f_b1, f_freq, f_w2, f_b2, f_w3)
    q, kt, v = qkv_prep(p, q_norm, k_norm, B, n_lat, n_ctx)
    att = attention(q, kt, v, B, n_lat, n_ctx)
    return even_out(x0, g, yconv.reshape(B * T, HY_WIDTH), p, att, hy_bias, w_out, g_post, mods, x_lat, x_ctx,
                    next_g_pre, next_mods, tpb, B)


def _mlstm_prep_kernel(tiles_per_batch, q_ref, qp_ref, qn_ref, k_ref, kp_ref, kn_ref,
                       wq_ref, wk_ref, bq_ref, bk_ref, q_out, kt_out):
    first, last = _seq_edges(pl.program_id(0), tiles_per_batch)
    prev_row, next_row = _halo_rows(qp_ref, qn_ref, first, last)
    q = _silu(_conv3(q_ref[...].astype(f32), prev_row, next_row, wq_ref[...], bq_ref[...]))
    q_out[...] = q.astype(q_out.dtype)
    prev_row, next_row = _halo_rows(kp_ref, kn_ref, first, last)
    k = _silu(_conv3(k_ref[...].astype(f32), prev_row, next_row, wk_ref[...], bk_ref[...])) * (ML_QK ** -0.5)
    for h in range(k.shape[1] // ML_QK):
        sl = slice(h * ML_QK, (h + 1) * ML_QK)
        kt_out[0, sl, :] = k[:, sl].T.astype(kt_out.dtype)


def mlstm_prep(p, conv_w, conv_b, B, T, *, tc=1024):
    R = p.shape[0]
    tpb = T // ROW_TILE
    nb = ML_QK_WIDTH // tc
    qcol = lambda c: c
    kcol = lambda c: nb + c
    qprev, qnext = _halo_specs(qcol, tc, R)
    kprev, knext = _halo_specs(kcol, tc, R)
    return pl.pallas_call(
        functools.partial(_mlstm_prep_kernel, tpb),
        grid=(R // ROW_TILE, nb),
        in_specs=[pl.BlockSpec((ROW_TILE, tc), lambda i, c: (i, c)), qprev, qnext,
                  pl.BlockSpec((ROW_TILE, tc), lambda i, c: (i, nb + c)), kprev, knext,
                  pl.BlockSpec((3, tc), lambda i, c: (0, c)),
                  pl.BlockSpec((3, tc), lambda i, c: (0, nb + c)),
                  pl.BlockSpec((1, tc), lambda i, c: (0, c)),
                  pl.BlockSpec((1, tc), lambda i, c: (0, nb + c))],
        out_specs=[pl.BlockSpec((ROW_TILE, tc), lambda i, c: (i, c)),
                   pl.BlockSpec((1, tc, ROW_TILE), lambda i, c: (i // tpb, c, i % tpb))],
        out_shape=[jax.ShapeDtypeStruct((R, ML_QK_WIDTH), bf16),
                   jax.ShapeDtypeStruct((B, ML_QK_WIDTH, T), bf16)],
        compiler_params=_params("arbitrary", "arbitrary"),
        name="mlstm_prep",
    )(p, p, p, p, p, p, conv_w, conv_w, conv_b.reshape(1, -1), conv_b.reshape(1, -1))


def _log_sigmoid(x):
    return jnp.minimum(x, 0.0) - jnp.log(1.0 + jnp.exp(-jnp.abs(x)))


def _mlstm_gates_kernel(g_ref, b_ref, gc_out, gr_out):
    pre = g_ref[...] + b_ref[...]
    lane = lax.broadcasted_iota(jnp.int32, pre.shape, 1)
    is_forget = (lane // ML_HEADS) % 2 == 1
    gc = jnp.where(is_forget, _log_sigmoid(pre), pre)
    gc_out[...] = gc
    gr_out[0] = gc.T


def mlstm_gates(gates, gate_b, B, T):
    R = gates.shape[0]
    gb = jnp.pad(gate_b, (0, LANE - gate_b.shape[0])).reshape(1, LANE)
    return pl.pallas_call(
        _mlstm_gates_kernel,
        grid=(B,),
        in_specs=[pl.BlockSpec((T, LANE), lambda b: (b, 0)),
                  pl.BlockSpec((1, LANE), lambda b: (0, 0))],
        out_specs=[pl.BlockSpec((T, LANE), lambda b: (b, 0)),
                   pl.BlockSpec((1, LANE, T), lambda b: (b, 0, 0))],
        out_shape=[jax.ShapeDtypeStruct((R, LANE), f32), jax.ShapeDtypeStruct((B, LANE, T), f32)],
        compiler_params=_params("arbitrary"),
        name="mlstm_gates",
    )(gates, gb)


def _mlstm_chunk_setup(reverse, gc_ref, gr_ref, m_ref, ms_ref):
    Lc = gc_ref.shape[0]
    H = ML_HEADS
    i_off = 2 * H if reverse else 0
    f_off = i_off + H
    t_idx = lax.broadcasted_iota(jnp.int32, (Lc, Lc), 0)
    s_idx = lax.broadcasted_iota(jnp.int32, (Lc, Lc), 1)
    causal = (s_idx >= t_idx) if reverse else (s_idx <= t_idx)
    tri = causal.astype(f32)
    gc = gc_ref[...]
    gr = gr_ref[0]
    b_col_all = jnp.dot(tri, gc[:, f_off:f_off + H], preferred_element_type=f32, precision=HIGHEST)
    b_row_all = lax.dot_general(gr[f_off:f_off + H, :], tri, (((1,), (1,)), ((), ())),
                                preferred_element_type=f32, precision=HIGHEST)
    i_col_all = gc[:, i_off:i_off + H]
    i_row_all = gr[i_off:i_off + H, :]
    end = 0 if reverse else Lc - 1
    b_end = b_col_all[end:end + 1, :]
    m_prev = m_ref[0:1, 0:H]
    g_col = b_end - b_col_all + i_col_all
    m_new = jnp.maximum(b_end + m_prev, jnp.max(g_col, axis=0, keepdims=True))
    a_prev = jnp.exp(b_end + m_prev - m_new)
    m_ref[:, 0:H] = jnp.broadcast_to(m_new, (m_ref.shape[0], H))
    b_end_s = b_row_all[:, end:end + 1]
    m_prev_s = ms_ref[0:H, 0:1]
    g_row = b_end_s - b_row_all + i_row_all
    m_new_s = jnp.maximum(b_end_s + m_prev_s, jnp.max(g_row, axis=1, keepdims=True))
    a_row = jnp.exp(g_row - m_new_s)
    ms_ref[0:H, :] = jnp.broadcast_to(m_new_s, (H, ms_ref.shape[1]))
    return dict(causal=causal, i_row=i_row_all, b_col=b_col_all, b_row=b_row_all,
                m_prev=m_prev, a_row=a_row, a_prev=a_prev)


class _MlstmChain:
    def __init__(self, h, cs, q_ref, kt_ref, v_ref, o_ref, ct_ref):
        self.h, self.cs = h, cs
        self.q_ref, self.kt_ref, self.v_ref, self.o_ref, self.ct_ref = q_ref, kt_ref, v_ref, o_ref, ct_ref

    def _q(self):
        return self.q_ref[:, self.h * ML_QK:(self.h + 1) * ML_QK]

    def _kt(self):
        return self.kt_ref[0, self.h * ML_QK:(self.h + 1) * ML_QK, :]

    def _v(self):
        Lc = self.q_ref.shape[0]
        ones_col = (lax.broadcasted_iota(jnp.int32, (Lc, LANE), 1) == 0).astype(bf16)
        return jnp.concatenate([self.v_ref[:, self.h * ML_V:(self.h + 1) * ML_V], ones_col], axis=1)

    def scores(self):
        self.qk = jnp.dot(self._q(), self._kt(), preferred_element_type=f32)

    def gates(self):
        h, cs = self.h, self.cs
        i_row = cs["i_row"][h:h + 1, :]
        b_col = cs["b_col"][:, h:h + 1]
        b_row = cs["b_row"][h:h + 1, :]
        m_prev = cs["m_prev"][:, h:h + 1]
        d = jnp.where(cs["causal"], b_col + (i_row - b_row), -jnp.inf)
        inter = b_col + m_prev
        self.m_row = jnp.maximum(inter, jnp.max(d, axis=-1, keepdims=True))
        self.s = (self.qk * jnp.exp(d - self.m_row)).astype(bf16)
        self.w_prev = jnp.exp(inter - self.m_row)

    def values(self):
        h = self.h
        self.ct = self.ct_ref[h]
        qw = (self._q().astype(f32) * self.w_prev).astype(bf16)
        tot = jnp.dot(jnp.concatenate([self.s, qw], axis=1),
                      jnp.concatenate([self._v(), self.ct.astype(bf16)], axis=0),
                      preferred_element_type=f32)
        scale = 1.0 / jnp.maximum(jnp.abs(tot[:, ML_V:ML_V + 1]), jnp.exp(-self.m_row))
        self.o_ref[:, h * ML_V:(h + 1) * ML_V] = (tot[:, :ML_V] * scale).astype(self.o_ref.dtype)

    def update(self):
        h = self.h
        kta = (self._kt().astype(f32) * self.cs["a_row"][h:h + 1, :]).astype(bf16)
        self.ct_ref[h] = (self.cs["a_prev"][:, h:h + 1] * self.ct
                          + jnp.dot(kta, self._v(), preferred_element_type=f32))


def _mlstm_scan_kernel(qf_ref, ktf_ref, vf_ref, gcf_ref, grf_ref, qb_ref, ktb_ref, vb_ref, gcb_ref, grb_ref,
                       of_ref, ob_ref, ctf_ref, mf_ref, msf_ref, ctb_ref, mb_ref, msb_ref):
    @pl.when(pl.program_id(1) == 0)
    def _():
        for ref in (ctf_ref, mf_ref, msf_ref, ctb_ref, mb_ref, msb_ref):
            ref[...] = jnp.zeros_like(ref)

    fwd = _mlstm_chunk_setup(False, gcf_ref, grf_ref, mf_ref, msf_ref)
    bwd = _mlstm_chunk_setup(True, gcb_ref, grb_ref, mb_ref, msb_ref)
    chains = []
    for h in range(ML_HEADS):
        chains.append(_MlstmChain(h, fwd, qf_ref, ktf_ref, vf_ref, of_ref, ctf_ref))
        chains.append(_MlstmChain(h, bwd, qb_ref, ktb_ref, vb_ref, ob_ref, ctb_ref))
    stages = ("scores", "gates", "values", "update")
    for k in range(len(chains) + len(stages) - 1):
        for depth, stage in enumerate(stages):
            if 0 <= k - depth < len(chains):
                getattr(chains[k - depth], stage)()


def mlstm_scan(q, kt, p, gc, gr, B, n_lat, n_ctx):
    R = q.shape[0]
    Lc = ML_CHUNK
    tpb = (n_lat + n_ctx) // Lc
    lat = n_lat // Lc
    ctx = n_ctx // Lc

    def specs(reverse):
        def chunk(j):
            if reverse:
                return tpb - 1 - j
            return jnp.where(j < ctx, lat + j, j - ctx)
        ins = [pl.BlockSpec((Lc, ML_QK_WIDTH), lambda b, j: (b * tpb + chunk(j), 0)),
               pl.BlockSpec((1, ML_QK_WIDTH, Lc), lambda b, j: (b, 0, chunk(j))),
               pl.BlockSpec((Lc, ML_WIDTH), lambda b, j: (b * tpb + chunk(j), ODD_OFF_V // ML_WIDTH)),
               pl.BlockSpec((Lc, LANE), lambda b, j: (b * tpb + chunk(j), 0)),
               pl.BlockSpec((1, LANE, Lc), lambda b, j: (b, 0, chunk(j)))]
        out = pl.BlockSpec((Lc, ML_WIDTH), lambda b, j: (b * tpb + chunk(j), 0))
        return ins, out

    ins_f, out_f = specs(False)
    ins_b, out_b = specs(True)
    state = [pltpu.VMEM((ML_HEADS, ML_QK, ML_V + LANE), f32), pltpu.VMEM((SUB, LANE), f32),
             pltpu.VMEM((SUB, LANE), f32)]
    return pl.pallas_call(
        _mlstm_scan_kernel,
        grid=(B, tpb),
        in_specs=ins_f + ins_b,
        out_specs=[out_f, out_b],
        out_shape=[jax.ShapeDtypeStruct((R, ML_WIDTH), bf16), jax.ShapeDtypeStruct((R, ML_WIDTH), bf16)],
        scratch_shapes=state + state,
        compiler_params=_params("arbitrary", "arbitrary"),
        name="mlstm_scan",
    )(q, kt, p, gc, gr, q, kt, p, gc, gr)


def _odd_out_kernel(hf_ref, hb_ref, o_ref, z_ref, hn_ref, w_ref, gp_ref, m_ref, x_ref, out_ref):
    D = x_ref.shape[-1]
    hs = (hf_ref[0].astype(f32) + hb_ref[0].astype(f32)) * jax.nn.sigmoid(o_ref[0].astype(f32))
    segs = [hs[:, h * ML_V:(h + 1) * ML_V] for h in range(ML_HEADS)]
    ms = [jnp.mean(seg * seg, axis=-1, keepdims=True) for seg in segs]
    parts = [seg * lax.rsqrt(m + NORM_EPS) for seg, m in zip(segs, ms)]
    hn = jnp.concatenate(parts, axis=1) * hn_ref[...] * _silu(z_ref[0].astype(f32))
    y = jnp.dot(hn.astype(bf16), w_ref[...], preferred_element_type=f32)
    out_ref[0] = _post_residual(x_ref[0], y, gp_ref[...], m_ref[0, :, 2 * D:3 * D])


def odd_out(hf, hb, p, head_norm, w_out, g_post, mods, x_all, B, n_lat, T, *, tm=512):
    D = x_all.shape[1]
    view = lambda a: a.reshape(B, T, a.shape[1])
    row = lambda b, i: (b, i, 0)
    full = lambda b, i: (0, 0)
    return pl.pallas_call(
        _odd_out_kernel,
        grid=(B, n_lat // tm),
        in_specs=[pl.BlockSpec((1, tm, ML_WIDTH), row),
                  pl.BlockSpec((1, tm, ML_WIDTH), row),
                  pl.BlockSpec((1, tm, ML_WIDTH), lambda b, i: (b, i, ODD_OFF_O // ML_WIDTH)),
                  pl.BlockSpec((1, tm, ML_WIDTH), lambda b, i: (b, i, ODD_OFF_Z // ML_WIDTH)),
                  pl.BlockSpec((1, ML_WIDTH), full),
                  pl.BlockSpec((ML_WIDTH, D), full),
                  pl.BlockSpec((1, D), full),
                  pl.BlockSpec((1, 1, 3 * D), lambda b, i: (b, 0, 0)),
                  pl.BlockSpec((1, tm, D), row)],
        out_specs=pl.BlockSpec((1, tm, D), row),
        out_shape=jax.ShapeDtypeStruct((B, n_lat, D), f32),
        compiler_params=_params("arbitrary", "arbitrary"),
        name="odd_out",
    )(view(hf), view(hb), view(p), view(p), head_norm.reshape(1, ML_WIDTH), w_out.astype(bf16),
      g_post.reshape(1, D), mods, view(x_all))


def odd_layer_last(x_all, h, mods, B, n_lat, n_ctx, g_post, w_in, conv_w, conv_b, gate_b, head_norm, w_out):
    T = n_lat + n_ctx
    tm = _proj_row_tile(B * T)
    p = matmul(h, w_in, tm=tm, tn=_proj_col_tile(ODD_MAIN), n_cols=ODD_MAIN, out_dtype=bf16, name="odd_in_proj")
    n_gates = w_in.shape[-1] - ODD_MAIN
    w_gates = jnp.pad(w_in[0, :, ODD_MAIN:], ((0, 0), (0, LANE - n_gates)))
    gates = matmul(h, w_gates, tm=tm, tn=LANE, name="odd_gate_proj")
    q, kt = mlstm_prep(p, conv_w, conv_b, B, T)
    gc, gr = mlstm_gates(gates, gate_b, B, T)
    hf, hb = mlstm_scan(q, kt, p, gc, gr, B, n_lat, n_ctx)
    return odd_out(hf, hb, p, head_norm, w_out, g_post, mods, x_all, B, n_lat, T)


def kernel(x, c, ctx, c_ctx, w_mod, b_mod, g_pre, g_post, e_w_in, e_conv_w, e_conv_b, e_filt_w1,
           e_filt_b1, e_filt_freq, e_filt_w2, e_filt_b2, e_filt_w3, e_hy_bias, e_q_norm, e_k_norm,
           e_w_out, o_w_in, o_conv_w, o_conv_b, o_gate_b, o_head_norm, o_w_out):
    B, n_lat, D = x.shape
    n_ctx = ctx.shape[1]
    T = n_lat + n_ctx
    depth = w_mod.shape[0]
    assert depth == 2 and B + 1 <= 8 and n_ctx == ROW_TILE and n_lat % ROW_TILE == 0
    cond = jnp.concatenate([c, c_ctx[None], jnp.zeros((8 - B - 1, D), f32)], axis=0)
    mods_all = adaln_all(cond, w_mod, b_mod)
    mods0 = mods_all[0].reshape(8, 1, 3 * D)
    mods1 = mods_all[1].reshape(8, 1, 3 * D)
    x_all, h1 = even_layer(x.reshape(B * n_lat, D), ctx.reshape(B * n_ctx, D), mods0, g_pre[1], mods1,
                           B, n_lat, n_ctx, g_pre[0], g_post[0], e_w_in[0],
                           e_conv_w[0], e_conv_b[0], e_filt_w1[0], e_filt_b1[0], e_filt_freq[0], e_filt_w2[0],
                           e_filt_b2[0], e_filt_w3[0], e_hy_bias[0], e_q_norm[0], e_k_norm[0], e_w_out[0])
    out = odd_layer_last(x_all, h1, mods1, B, n_lat, n_ctx, g_post[1], o_w_in,
                         o_conv_w[0], o_conv_b[0], o_gate_b[0], o_head_norm[0], o_w_out[0])
    return out.reshape(B, n_lat, D)
```

```python
import functools
import math

import numpy as np
import jax
import jax.numpy as jnp
from jax import lax
from jax.experimental import pallas as pl
from jax.experimental.pallas import tpu as pltpu

f32 = jnp.float32
bf16 = jnp.bfloat16
HIGHEST = lax.Precision.HIGHEST

D_MODEL = 1024
GRID_W = 64
NORM_EPS = 1e-6

HY_WIDTH = 1024
HY_EMB = 33
HY_BANDS = (HY_EMB - 1) // 2
HY_HIDDEN = 64
HY_TARGET = 1e-2
HY_SHORT_DECAY_PCT = 0.3
HY_LONG_DECAY_PCT = 1.5

ATT_HEADS = 8
ATT_KV_HEADS = 2
ATT_GROUP = ATT_HEADS // ATT_KV_HEADS
HEAD_DIM = 128
ATT_WIDTH = ATT_HEADS * HEAD_DIM
ATT_KV_WIDTH = ATT_KV_HEADS * HEAD_DIM
ROPE_THETA = 10000.0
EVEN_OFF_XV = 0
EVEN_OFF_GHY = 3 * HY_WIDTH
EVEN_OFF_Q = EVEN_OFF_GHY + HY_WIDTH
EVEN_OFF_K = EVEN_OFF_Q + ATT_WIDTH
EVEN_OFF_V = EVEN_OFF_K + ATT_KV_WIDTH
EVEN_OFF_GATT = EVEN_OFF_V + ATT_KV_WIDTH
EVEN_IN = EVEN_OFF_GATT + ATT_WIDTH

ML_HEADS = 8
ML_QK = 128
ML_V = 256
ML_QK_WIDTH = ML_HEADS * ML_QK
ML_WIDTH = ML_HEADS * ML_V
ODD_OFF_Q = 0
ODD_OFF_K = ML_QK_WIDTH
ODD_OFF_V = 2 * ML_QK_WIDTH
ODD_OFF_O = ODD_OFF_V + ML_WIDTH
ODD_OFF_Z = ODD_OFF_O + ML_WIDTH
ODD_OFF_GATES = ODD_OFF_Z + ML_WIDTH
ODD_MAIN = ODD_OFF_GATES

ROW_TILE = 256
ML_CHUNK = 256
LANE = 128
VMEM_LIMIT_BYTES = 48 * 1024 * 1024

FFT_L1 = 64
FFT_L2 = 128


def _params(*sem):
    return pltpu.CompilerParams(dimension_semantics=sem, vmem_limit_bytes=VMEM_LIMIT_BYTES)


def _adaln_kernel(c_ref, w_ref, b_ref, o_ref):
    c = c_ref[...]
    s = c * jax.nn.sigmoid(c)
    o_ref[0] = jnp.dot(s, w_ref[0], preferred_element_type=f32, precision=HIGHEST) + b_ref[0]


def adaln_all(cond, w_mod, b_mod, *, tn=768):
    depth, D, N = w_mod.shape
    return pl.pallas_call(
        _adaln_kernel,
        grid=(depth, N // tn),
        in_specs=[
            pl.BlockSpec((8, D), lambda l, j: (0, 0)),
            pl.BlockSpec((1, D, tn), lambda l, j: (l, 0, j)),
            pl.BlockSpec((1, 1, tn), lambda l, j: (l, 0, j)),
        ],
        out_specs=pl.BlockSpec((1, 8, tn), lambda l, j: (l, 0, j)),
        out_shape=jax.ShapeDtypeStruct((depth, 8, N), f32),
        compiler_params=_params("arbitrary", "arbitrary"),
        name="adaln",
    )(cond, w_mod, b_mod.reshape(depth, 1, N))


def _mod_row(i, tiles_per_batch, ctx_row):
    lat_tiles = tiles_per_batch - 1
    return jnp.where(i % tiles_per_batch == lat_tiles, ctx_row, i // tiles_per_batch)


def _norm_mod_rows(x, g, m):
    D = x.shape[-1]
    y = x * lax.rsqrt(jnp.mean(x * x, axis=-1, keepdims=True) + NORM_EPS)
    return y * g * (1.0 + m[:, D:2 * D]) + m[:, 0:D]


def _input_row_specs(tiles_per_batch, D):
    lat_tiles = tiles_per_batch - 1

    def lat_index(i):
        return ((i // tiles_per_batch) * lat_tiles + jnp.minimum(i % tiles_per_batch, lat_tiles - 1), 0)

    return (pl.BlockSpec((ROW_TILE, D), lat_index),
            pl.BlockSpec((ROW_TILE, D), lambda i: (i // tiles_per_batch, 0)))


def _input_rows(tiles_per_batch, lat_ref, ctx_ref):
    is_ctx = pl.program_id(0) % tiles_per_batch == tiles_per_batch - 1
    return jnp.where(is_ctx, ctx_ref[...], lat_ref[...])


def _norm_mod_kernel(tiles_per_batch, x_ref, c_ref, g_ref, m_ref, o_ref):
    x = _input_rows(tiles_per_batch, x_ref, c_ref)
    o_ref[...] = _norm_mod_rows(x, g_ref[...], m_ref[0]).astype(o_ref.dtype)


def norm_mod(x_lat, x_ctx, g, mods, tiles_per_batch, ctx_row):
    D = x_lat.shape[1]
    R = x_lat.shape[0] + x_ctx.shape[0]
    lat_spec, ctx_spec = _input_row_specs(tiles_per_batch, D)
    return pl.pallas_call(
        functools.partial(_norm_mod_kernel, tiles_per_batch),
        grid=(R // ROW_TILE,),
        in_specs=[
            lat_spec, ctx_spec,
            pl.BlockSpec((1, D), lambda i: (0, 0)),
            pl.BlockSpec((1, 1, 3 * D), lambda i: (_mod_row(i, tiles_per_batch, ctx_row), 0, 0)),
        ],
        out_specs=pl.BlockSpec((ROW_TILE, D), lambda i: (i, 0)),
        out_shape=jax.ShapeDtypeStruct((R, D), bf16),
        compiler_params=_params("arbitrary"),
        name="norm_mod",
    )(x_lat, x_ctx, g.reshape(1, D), mods)


def _matmul_kernel(a_ref, b_ref, o_ref, bq_ref):
    @pl.when(pl.program_id(1) == 0)
    def _():
        bq_ref[...] = b_ref[...].astype(bq_ref.dtype)

    o_ref[...] = jnp.dot(a_ref[...], bq_ref[...], preferred_element_type=f32).astype(o_ref.dtype)


def _matmul_side_kernel(a_ref, b_ref, s_ref, o_ref, so_ref, bq_ref):
    _matmul_kernel(a_ref, b_ref, o_ref, bq_ref)

    @pl.when(pl.program_id(0) == 0)
    def _():
        so_ref[...] = jnp.dot(a_ref[...], s_ref[...].astype(a_ref.dtype), preferred_element_type=f32)


def matmul_with_side(a, b, side, *, tm, tn, n_cols, out_dtype, name):
    M, K = a.shape
    last = M // tm - 1
    return pl.pallas_call(
        _matmul_side_kernel,
        grid=(n_cols // tn, M // tm),
        in_specs=[pl.BlockSpec((tm, K), lambda j, i: (i, 0)),
                  pl.BlockSpec((None, K, tn), lambda j, i: (0, 0, j)),
                  pl.BlockSpec((K, LANE), lambda j, i: (0, 0))],
        out_specs=[pl.BlockSpec((tm, tn), lambda j, i: (i, j)),
                   pl.BlockSpec((tm, LANE), lambda j, i: (jnp.where(j == 0, i, last), 0))],
        out_shape=[jax.ShapeDtypeStruct((M, n_cols), out_dtype), jax.ShapeDtypeStruct((M, LANE), f32)],
        scratch_shapes=[pltpu.VMEM((K, tn), a.dtype)],
        compiler_params=_params("arbitrary", "arbitrary"),
        name=name,
    )(a, b, side)


def _proj_row_tile(rows):
    return next(t for t in (1024, 512, ROW_TILE) if rows % t == 0)


def _proj_col_tile(cols, n_tiles=4):
    groups = cols // LANE
    return next(g for g in range(groups // n_tiles, 0, -1) if groups % g == 0) * LANE


def matmul(a, b, *, tm, tn, n_cols=None, out_dtype=f32, name="matmul"):
    M, K = a.shape
    N = b.shape[-1] if n_cols is None else n_cols
    assert M % tm == 0 and N % tn == 0, (M, N, tm, tn)
    b_spec = (pl.BlockSpec((K, tn), lambda j, i: (0, j)) if b.ndim == 2 else
              pl.BlockSpec((None, K, tn), lambda j, i: (0, 0, j)))
    return pl.pallas_call(
        _matmul_kernel,
        grid=(N // tn, M // tm),
        in_specs=[pl.BlockSpec((tm, K), lambda j, i: (i, 0)), b_spec],
        out_specs=pl.BlockSpec((tm, tn), lambda j, i: (i, j)),
        out_shape=jax.ShapeDtypeStruct((M, N), out_dtype),
        scratch_shapes=[pltpu.VMEM((K, tn), a.dtype)],
        compiler_params=_params("arbitrary", "arbitrary"),
        name=name,
    )(a, b)


def _conv3(x, prev_row, next_row, w, b):
    tm = x.shape[0]
    row = lax.broadcasted_iota(jnp.int32, x.shape, 0)
    xm = jnp.where(row == 0, prev_row, pltpu.roll(x, 1, 0))
    xp = jnp.where(row == tm - 1, next_row, pltpu.roll(x, tm - 1, 0))
    return w[0:1] * xm + w[1:2] * x + w[2:3] * xp + b


def _seq_edges(i, tiles_per_batch):
    r = i % tiles_per_batch
    lat_tiles = tiles_per_batch - 1
    first = jnp.logical_or(r == 0, r == lat_tiles)
    last = jnp.logical_or(r == lat_tiles - 1, r == lat_tiles)
    return first, last


HALO = 16


def _halo_rows(prev_ref, next_ref, first, last):
    prev_row = jnp.where(first, 0.0, prev_ref[...].astype(f32)[HALO - 1:HALO, :])
    next_row = jnp.where(last, 0.0, next_ref[...].astype(f32)[0:1, :])
    return prev_row, next_row


def _halo_specs(col_block, tc, n_rows):
    per = ROW_TILE // HALO
    n_blocks = n_rows // HALO
    prev = pl.BlockSpec((HALO, tc), lambda i, c: (jnp.maximum(i * per - 1, 0), col_block(c)))
    nxt = pl.BlockSpec((HALO, tc), lambda i, c: (jnp.minimum((i + 1) * per, n_blocks - 1), col_block(c)))
    return prev, nxt


def _hyena_pre_kernel(tiles_per_batch, x0_ref, x0p_ref, x0n_ref, x1_ref, x1p_ref, x1n_ref,
                      v_ref, vp_ref, vn_ref, w0_ref, w1_ref, w2_ref, b0_ref, b1_ref, b2_ref,
                      x0_out, g_out):
    first, last = _seq_edges(pl.program_id(0), tiles_per_batch)

    def conv(x_ref, p_ref, n_ref, w_ref, b_ref):
        prev_row, next_row = _halo_rows(p_ref, n_ref, first, last)
        return _conv3(x_ref[...].astype(f32), prev_row, next_row, w_ref[...], b_ref[...])

    x0 = conv(x0_ref, x0p_ref, x0n_ref, w0_ref, b0_ref)
    x1 = conv(x1_ref, x1p_ref, x1n_ref, w1_ref, b1_ref)
    v = conv(v_ref, vp_ref, vn_ref, w2_ref, b2_ref)
    g = v * x1
    x0_out[...] = x0.astype(x0_out.dtype)
    g_out[...] = g


def hyena_pre(p, conv_w, conv_b, tiles_per_batch, *, tc=1024):
    R = p.shape[0]
    W = HY_WIDTH
    nb = W // tc
    specs = []
    for part in range(3):
        col = functools.partial(lambda c, part: part * nb + c, part=part)
        main = pl.BlockSpec((ROW_TILE, tc), functools.partial(lambda i, c, col: (i, col(c)), col=col))
        prev, nxt = _halo_specs(col, tc, R)
        specs += [main, prev, nxt]
    wspecs = [pl.BlockSpec((3, tc), functools.partial(lambda i, c, part: (0, part * nb + c), part=part))
              for part in range(3)]
    bspecs = [pl.BlockSpec((1, tc), functools.partial(lambda i, c, part: (0, part * nb + c), part=part))
              for part in range(3)]
    out_spec = pl.BlockSpec((ROW_TILE, tc), lambda i, c: (i, c))
    args = [p] * 9 + [conv_w] * 3 + [conv_b.reshape(1, -1)] * 3
    return pl.pallas_call(
        functools.partial(_hyena_pre_kernel, tiles_per_batch),
        grid=(R // ROW_TILE, nb),
        in_specs=specs + wspecs + bspecs,
        out_specs=[out_spec, out_spec],
        out_shape=[jax.ShapeDtypeStruct((R, W), bf16), jax.ShapeDtypeStruct((R, W), f32)],
        compiler_params=_params("arbitrary", "arbitrary"),
        name="hyena_pre",
    )(*args)


def _filter_kernel(n, rows, bands_ref, w1t_ref, w1c_ref, w1s_ref, b1_ref, fr_ref, w2_ref, b2_ref,
                   w3_ref, dl_ref, o_ref, nrm_ref):
    step = pl.program_id(0)

    def offsets(shape, axis):
        j = step * rows + lax.broadcasted_iota(jnp.int32, shape, axis)
        return jnp.where(j < n, j, 2 * n - j).astype(f32), j != n

    d_row, _ = offsets((1, rows), 1)
    t_row = d_row / float(n - 1)
    ang = (2.0 * math.pi / n) * bands_ref[...] * d_row
    fr = fr_ref[...]
    z1 = (w1t_ref[...] * t_row
          + jnp.dot(w1c_ref[...], jnp.cos(ang), preferred_element_type=f32, precision=HIGHEST)
          - jnp.dot(w1s_ref[...], jnp.sin(ang), preferred_element_type=f32, precision=HIGHEST)
          + b1_ref[...])
    hdn = jnp.sin(fr * z1)
    hdn = jnp.sin(fr * (jnp.dot(w2_ref[...], hdn, preferred_element_type=f32, precision=HIGHEST) + b2_ref[...]))
    h = jnp.dot(hdn.T.astype(bf16), w3_ref[0].astype(bf16), preferred_element_type=f32)
    d_col, valid = offsets((rows, 1), 0)
    t = d_col / float(n - 1)
    h = h * jnp.exp(-t * jnp.abs(dl_ref[...]))
    h = jnp.where(valid, h, 0.0)
    o_ref[...] = h

    @pl.when(step == 0)
    def _():
        nrm_ref[...] = jnp.zeros_like(nrm_ref)

    nrm_ref[...] += jnp.sum(jnp.abs(h), axis=0, keepdims=True)


def hyena_filter(n, w1, b1, freq, w2, b2, w3, *, rows=256):
    W = HY_WIDTH
    Hd = HY_HIDDEN
    bands = jnp.linspace(1e-4, HY_BANDS - 1, HY_BANDS, dtype=f32).reshape(HY_BANDS, 1)
    max_decay = math.log(HY_TARGET) / HY_SHORT_DECAY_PCT
    min_decay = math.log(HY_TARGET) / HY_LONG_DECAY_PCT
    deltas = jnp.linspace(min_decay, max_decay, W, dtype=f32).reshape(1, W)
    steps = 2 * n // rows
    half_steps = n // rows
    full = lambda s: (0, 0)
    w3r = w3.reshape(Hd, 2, W).transpose(1, 0, 2)
    return pl.pallas_call(
        functools.partial(_filter_kernel, n, rows),
        grid=(steps,),
        in_specs=[
            pl.BlockSpec((HY_BANDS, 1), full),
            pl.BlockSpec((Hd, 1), full),
            pl.BlockSpec((Hd, HY_BANDS), full),
            pl.BlockSpec((Hd, HY_BANDS), full),
            pl.BlockSpec((Hd, 1), full),
            pl.BlockSpec((Hd, 1), full),
            pl.BlockSpec((Hd, Hd), full),
            pl.BlockSpec((Hd, 1), full),
            pl.BlockSpec((1, Hd, W), lambda s: (jnp.where(s * rows < n, 0, 1), 0, 0)),
            pl.BlockSpec((1, W), full),
        ],
        out_specs=[pl.BlockSpec((rows, W), lambda s: (s, 0)), pl.BlockSpec((1, W), full)],
        out_shape=[jax.ShapeDtypeStruct((2 * n, W), f32), jax.ShapeDtypeStruct((1, W), f32)],
        compiler_params=_params("arbitrary"),
        name="hyena_filter",
    )(bands, w1[0:1].T, w1[1:1 + HY_BANDS].T, w1[1 + HY_BANDS:].T, b1.reshape(Hd, 1), freq.reshape(Hd, 1),
      w2.T, b2.reshape(Hd, 1), w3r, deltas)


@functools.lru_cache(maxsize=None)
def _fft_constants():
    L1, L2 = FFT_L1, FFT_L2
    L = L1 * L2
    k1 = np.arange(L1)
    nh = np.arange(L1 // 2)
    th = 2.0 * np.pi * np.outer(k1, nh) / L1
    m1r = np.concatenate([np.cos(th), -np.sin(th)], axis=0)
    m1i = np.concatenate([np.sin(th), np.cos(th)], axis=0)
    thf = 2.0 * np.pi * np.outer(k1, np.arange(L1)) / L1
    m1f = np.concatenate([np.cos(thf), -np.sin(thf)], axis=0)
    n2 = np.arange(L2)
    k2 = np.arange(L2)
    m = (k1[:, None, None] * n2[None, None, :] + L1 * k2[None, :, None] * n2[None, None, :]) % L
    ph = 2.0 * np.pi * m / L
    gr, gi = np.cos(ph), -np.sin(ph)
    g = np.concatenate([np.concatenate([gr, -gi], axis=2), np.concatenate([gi, gr], axis=2)], axis=1)
    gt = np.transpose(g, (0, 2, 1))
    thi = 2.0 * np.pi * np.outer(nh, k1) / L1
    m3r = np.concatenate([np.cos(thi), np.sin(thi)], axis=0)
    m3i = np.concatenate([-np.sin(thi), np.cos(thi)], axis=0)
    cast = lambda a: np.asarray(a, dtype=np.float32)
    return dict(m1r=cast(m1r), m1i=cast(m1i), m1f=cast(m1f), g=cast(g), gt=cast(gt), m3r=cast(m3r), m3i=cast(m3i))


@functools.lru_cache(maxsize=None)
def _ctx_dft_constants(n):
    L = 2 * n
    k = np.arange(L)
    th = 2.0 * np.pi * np.outer(k, np.arange(n)) / L
    c, s = np.cos(th), np.sin(th)
    mc = np.concatenate([np.concatenate([c, s], axis=1), np.concatenate([-s, c], axis=1)], axis=0)
    thf = 2.0 * np.pi * np.outer(k, np.arange(L)) / L
    mf = np.concatenate([np.cos(thf), -np.sin(thf)], axis=0)
    ct, st = c.T, s.T
    minv = np.concatenate([np.concatenate([ct, -st], axis=1), np.concatenate([st, ct], axis=1)], axis=0)
    cast = lambda a: np.asarray(a, dtype=np.float32)
    return dict(mc=cast(mc), mf=cast(mf), minv=cast(minv))


def _bf16_constants(consts):
    return {k: jnp.asarray(v).astype(bf16) for k, v in consts.items()}


SUB = 8
N2C = 128


def _store_step1(o_ref, j, a):
    for ch in range(a.shape[0] // SUB):
        o_ref[0, ch, j] = a[ch * SUB:(ch + 1) * SUB]


def _fft1_kernel(re_ref, im_ref, mr_ref, mi_ref, o_ref):
    base = pl.program_id(2) * N2C
    for j in range(N2C):
        rows = pl.ds(base + j, FFT_L1 // 2, stride=FFT_L2)
        _store_step1(o_ref, j,
                     jnp.dot(mr_ref[...], re_ref[0, rows, :].astype(bf16), preferred_element_type=f32)
                     + jnp.dot(mi_ref[...], im_ref[0, rows, :].astype(bf16), preferred_element_type=f32))


def _step1_out(P, W, index):
    groups = 2 * FFT_L1 // SUB
    spec = pl.BlockSpec((1, groups, N2C, SUB, LANE), index)
    return spec, jax.ShapeDtypeStruct((P, groups, FFT_L2, SUB, W), f32)


def fft_step1(g3, m1r, m1i, n_lat):
    B, _, W = g3.shape
    half = FFT_L1 // 2
    out_spec, out_shape = _step1_out(B // 2, W, lambda p, c, k: (p, 0, k, 0, c))
    return pl.pallas_call(
        _fft1_kernel,
        grid=(B // 2, W // LANE, FFT_L2 // N2C),
        in_specs=[pl.BlockSpec((1, n_lat, LANE), lambda p, c, k: (2 * p, 0, c)),
                  pl.BlockSpec((1, n_lat, LANE), lambda p, c, k: (2 * p + 1, 0, c)),
                  pl.BlockSpec((2 * FFT_L1, half), lambda p, c, k: (0, 0)),
                  pl.BlockSpec((2 * FFT_L1, half), lambda p, c, k: (0, 0))],
        out_specs=out_spec,
        out_shape=out_shape,
        compiler_params=_params("arbitrary", "arbitrary", "arbitrary"),
        name="fft_step1",
    )(g3, g3, m1r, m1i)


def _fft1_filter_kernel(x_ref, m_ref, o_ref):
    base = pl.program_id(1) * N2C
    for j in range(N2C):
        rows = pl.ds(base + j, FFT_L1, stride=FFT_L2)
        _store_step1(o_ref, j, jnp.dot(m_ref[...], x_ref[rows, :].astype(bf16), preferred_element_type=f32))


def fft_step1_filter(ts, m1f):
    L, W = ts.shape
    out_spec, out_shape = _step1_out(1, W, lambda c, k: (0, 0, k, 0, c))
    return pl.pallas_call(
        _fft1_filter_kernel,
        grid=(W // LANE, FFT_L2 // N2C),
        in_specs=[pl.BlockSpec((L, LANE), lambda c, k: (0, c)),
                  pl.BlockSpec((2 * FFT_L1, FFT_L1), lambda c, k: (0, 0))],
        out_specs=out_spec,
        out_shape=out_shape,
        compiler_params=_params("arbitrary", "arbitrary"),
        name="fft_step1_filter",
    )(ts, m1f)


def _cmul(yr, yi, hr, hi):
    return yr * hr - yi * hi, yr * hi + yi * hr


def _step1_column(a_refs, j):
    col = lambda ref: ref[0, pl.ds(j, FFT_L2, stride=SUB), :]
    re = jnp.concatenate([col(a_refs[0]), col(a_refs[1])], axis=1)
    im = jnp.concatenate([col(a_refs[2]), col(a_refs[3])], axis=1)
    return jnp.concatenate([re, im], axis=0).astype(bf16)


def _fft2_filter_kernel(a0_ref, a1_ref, a2_ref, a3_ref, g_ref, nrm_ref, o_ref):
    scale = 1.0 / (nrm_ref[...] * float(FFT_L1 * FFT_L2))
    for j in range(SUB):
        a = _step1_column((a0_ref, a1_ref, a2_ref, a3_ref), j)
        o_ref[j] = (jnp.dot(g_ref[j], a, preferred_element_type=f32) * scale).astype(o_ref.dtype)


def _step1_specs(index):
    def spec(part, lane_half):
        return pl.BlockSpec((1, FFT_L2 * SUB, LANE),
                            lambda *g: (index(*g)[0], part * (FFT_L1 // SUB) + index(*g)[1],
                                        2 * index(*g)[2] + lane_half))
    return [spec(0, 0), spec(0, 1), spec(1, 0), spec(1, 1)]


def fft_step2_filter(af, g, nrm):
    W = af.shape[-1]
    ct = 2 * LANE
    L1, R2 = FFT_L1, 2 * FFT_L2
    af = af.reshape(1, -1, W)
    return pl.pallas_call(
        _fft2_filter_kernel,
        grid=(L1 // SUB, W // ct),
        in_specs=_step1_specs(lambda k, c: (0, k, c)) + [
            pl.BlockSpec((SUB, R2, R2), lambda k, c: (k, 0, 0)),
            pl.BlockSpec((1, ct), lambda k, c: (0, c))],
        out_specs=pl.BlockSpec((SUB, R2, ct), lambda k, c: (k, 0, c)),
        out_shape=jax.ShapeDtypeStruct((L1, R2, W), bf16),
        compiler_params=_params("arbitrary", "arbitrary"),
        name="fft_step2_filter",
    )(af, af, af, af, g, nrm)


def _fft2_kernel(a0_ref, a1_ref, a2_ref, a3_ref, g_ref, gt_ref, h_ref, ore_ref, oim_ref):
    half = FFT_L2
    for j in range(SUB):
        a = _step1_column((a0_ref, a1_ref, a2_ref, a3_ref), j)
        y = jnp.dot(g_ref[j], a, preferred_element_type=f32)
        pr, pi = _cmul(y[:half], y[half:], h_ref[j, :half].astype(f32), h_ref[j, half:].astype(f32))
        pcat = jnp.concatenate([pr, pi], axis=0).astype(bf16)
        b = jnp.dot(gt_ref[j], pcat, preferred_element_type=f32)
        for ch in range(FFT_L2 // N2C):
            ore_ref[0, ch, j] = b[ch * N2C:(ch + 1) * N2C]
            oim_ref[0, ch, j] = b[half + ch * N2C:half + (ch + 1) * N2C]


def fft_step2(a, g, gt, hf):
    P, W = a.shape[0], a.shape[-1]
    ct = 2 * LANE
    L1, R2 = FFT_L1, 2 * FFT_L2
    a = a.reshape(P, -1, W)
    out = pl.BlockSpec((1, FFT_L2 // N2C, SUB, N2C, ct), lambda k, c, p: (p, 0, k, 0, c))
    shape = jax.ShapeDtypeStruct((P, FFT_L2 // N2C, L1, N2C, W), f32)
    return pl.pallas_call(
        _fft2_kernel,
        grid=(L1 // SUB, W // ct, P),
        in_specs=_step1_specs(lambda k, c, p: (p, k, c)) + [
            pl.BlockSpec((SUB, R2, R2), lambda k, c, p: (k, 0, 0)),
            pl.BlockSpec((SUB, R2, R2), lambda k, c, p: (k, 0, 0)),
            pl.BlockSpec((SUB, R2, ct), lambda k, c, p: (k, 0, c))],
        out_specs=[out, out],
        out_shape=[shape, shape],
        compiler_params=_params("arbitrary", "arbitrary", "arbitrary"),
        name="fft_step2",
    )(a, a, a, a, g, gt, hf)


def _fft3_kernel(bre_ref, bim_ref, mr_ref, mi_ref, o_ref):
    L1 = FFT_L1
    half = L1 // 2
    base = pl.program_id(2) * N2C
    for j in range(N2C):
        br = bre_ref[0, pl.ds(j, L1, stride=N2C), :].astype(bf16)
        bi = bim_ref[0, pl.ds(j, L1, stride=N2C), :].astype(bf16)
        z = (jnp.dot(mr_ref[...], br, preferred_element_type=f32)
             + jnp.dot(mi_ref[...], bi, preferred_element_type=f32))
        rows = pl.ds(base + j, half, stride=FFT_L2)
        o_ref[0, 0, rows, :] = z[:half]
        o_ref[0, 1, rows, :] = z[half:]


def fft_step3(bre, bim, m3r, m3i, rows_total, n_lat):
    P, W = bre.shape[0], bre.shape[-1]
    L1, L2 = FFT_L1, FFT_L2
    bre = bre.reshape(P, -1, W)
    bim = bim.reshape(P, -1, W)
    spec = pl.BlockSpec((1, L1 * N2C, LANE), lambda p, c, k: (p, k, c))
    return pl.pallas_call(
        _fft3_kernel,
        grid=(P, W // LANE, L2 // N2C),
        in_specs=[spec, spec,
                  pl.BlockSpec((L1, L1), lambda p, c, k: (0, 0)),
                  pl.BlockSpec((L1, L1), lambda p, c, k: (0, 0))],
        out_specs=pl.BlockSpec((1, 2, n_lat, LANE), lambda p, c, k: (p, 0, 0, c)),
        out_shape=jax.ShapeDtypeStruct((P, 2, rows_total, W), f32),
        compiler_params=_params("arbitrary", "arbitrary", "arbitrary"),
        name="fft_step3",
    )(bre, bim, m3r, m3i)


def _ctx_filter_kernel(n, ts_ref, m_ref, nrm_ref, o_ref):
    scale = 1.0 / (nrm_ref[...] * float(2 * n))
    o_ref[...] = jnp.dot(m_ref[...], ts_ref[...].astype(bf16), preferred_element_type=f32) * scale


def ctx_filter_spectrum(ts, mf, nrm, *, ct=256):
    L, W = ts.shape
    return pl.pallas_call(
        functools.partial(_ctx_filter_kernel, L // 2),
        grid=(W // ct,),
        in_specs=[pl.BlockSpec((L, ct), lambda c: (0, c)),
                  pl.BlockSpec((2 * L, L), lambda c: (0, 0)),
                  pl.BlockSpec((1, ct), lambda c: (0, c))],
        out_specs=pl.BlockSpec((2 * L, ct), lambda c: (0, c)),
        out_shape=jax.ShapeDtypeStruct((2 * L, W), f32),
        compiler_params=_params("arbitrary"),
        name="ctx_filter_spectrum",
    )(ts, mf, nrm)


def _ctx_conv_kernel(n, re_ref, im_ref, mc_ref, minv_ref, h_ref, y_in_ref, o_ref):
    del y_in_ref
    L = 2 * n
    z = jnp.concatenate([re_ref[0], im_ref[0]], axis=0).astype(bf16)
    y = jnp.dot(mc_ref[...], z, preferred_element_type=f32)
    pr, pi = _cmul(y[:L], y[L:], h_ref[:L], h_ref[L:])
    pcat = jnp.concatenate([pr, pi], axis=0).astype(bf16)
    out = jnp.dot(minv_ref[...], pcat, preferred_element_type=f32)
    o_ref[0, 0] = out[:n]
    o_ref[0, 1] = out[n:]


def ctx_long_conv(gz, y4, hfc, consts, n_lat, n_ctx, *, ct=256):
    B, T, W = gz.shape
    blk = n_lat // n_ctx
    L = 2 * n_ctx
    out = pl.pallas_call(
        functools.partial(_ctx_conv_kernel, n_ctx),
        grid=(B // 2, W // ct),
        in_specs=[pl.BlockSpec((1, n_ctx, ct), lambda p, c: (2 * p, blk, c)),
                  pl.BlockSpec((1, n_ctx, ct), lambda p, c: (2 * p + 1, blk, c)),
                  pl.BlockSpec((2 * L, L), lambda p, c: (0, 0)),
                  pl.BlockSpec((L, 2 * L), lambda p, c: (0, 0)),
                  pl.BlockSpec((2 * L, ct), lambda p, c: (0, c)),
                  pl.BlockSpec(memory_space=pl.ANY)],
        out_specs=pl.BlockSpec((1, 2, n_ctx, ct), lambda p, c: (p, 0, blk, c)),
        out_shape=jax.ShapeDtypeStruct(y4.shape, f32),
        input_output_aliases={5: 0},
        compiler_params=_params("arbitrary", "arbitrary"),
        name="ctx_long_conv",
    )(gz, gz, consts["mc"], consts["minv"], hfc, y4)
    return out.reshape(B, T, W)


def hyena_long_conv(gz, n_lat, n_ctx, f_w1, f_b1, f_freq, f_w2, f_b2, f_w3):
    B, T, W = gz.shape
    cst = _bf16_constants(_fft_constants())
    L1, L2 = FFT_L1, FFT_L2
    assert 2 * n_lat == L1 * L2 and T % L2 == 0 and B % 2 == 0
    ts, nrm = hyena_filter(n_lat, f_w1, f_b1, f_freq, f_w2, f_b2, f_w3)
    af = fft_step1_filter(ts, cst["m1f"])
    hf = fft_step2_filter(af, cst["g"], nrm)
    a = fft_step1(gz, cst["m1r"], cst["m1i"], n_lat)
    bre, bim = fft_step2(a, cst["g"], cst["gt"], hf)
    y = fft_step3(bre, bim, cst["m3r"], cst["m3i"], T, n_lat)
    ccst = _bf16_constants(_ctx_dft_constants(n_ctx))
    ts_c, nrm_c = hyena_filter(n_ctx, f_w1, f_b1, f_freq, f_w2, f_b2, f_w3)
    hfc = ctx_filter_spectrum(ts_c, ccst["mf"], nrm_c)
    return ctx_long_conv(gz, y, hfc, ccst, n_lat, n_ctx)


@functools.lru_cache(maxsize=None)
def _rope_tables(n_lat, n_ctx):
    half = HEAD_DIM // 2
    nf = half // 2
    inv = ROPE_THETA ** (-np.arange(nf, dtype=np.float64) / nf)
    t = np.arange(n_lat)
    pos = np.stack([t // GRID_W, t % GRID_W], axis=1).astype(np.float64)
    ang = pos[:, :, None] * inv[None, None, :]
    cos = np.concatenate([np.cos(ang), np.cos(ang)], axis=2).reshape(n_lat, HEAD_DIM)
    sin = np.concatenate([-np.sin(ang), np.sin(ang)], axis=2).reshape(n_lat, HEAD_DIM)
    cos = np.concatenate([cos, np.ones((n_ctx, HEAD_DIM))], axis=0)
    sin = np.concatenate([sin, np.zeros((n_ctx, HEAD_DIM))], axis=0)
    return np.asarray(cos, np.float32), np.asarray(sin, np.float32)


def _norm_rope_heads(xs, w, cos, sin, lo_lane):
    nf = HEAD_DIM // 4
    ms = [jnp.mean(x * x, axis=-1, keepdims=True) for x in xs]
    ys = [x * lax.rsqrt(m + NORM_EPS) * w for x, m in zip(xs, ms)]
    ps = [jnp.where(lo_lane, pltpu.roll(y, HEAD_DIM - nf, 1), pltpu.roll(y, nf, 1)) for y in ys]
    return [y * cos + p * sin for y, p in zip(ys, ps)]


def _qkv_prep_kernel(q_ref, k_ref, v_ref, cos_ref, sin_ref, qn_ref, kn_ref, q_out, kt_out, v_out):
    cos = cos_ref[...]
    sin = sin_ref[...]
    lane = lax.broadcasted_iota(jnp.int32, cos.shape, 1)
    lo_lane = (lane % (HEAD_DIM // 2)) < (HEAD_DIM // 4)
    head = lambda ref, h: ref[:, h * HEAD_DIM:(h + 1) * HEAD_DIM].astype(f32)
    scale = HEAD_DIM ** -0.5 * math.log2(math.e)
    qs = _norm_rope_heads([head(q_ref, h) for h in range(ATT_HEADS)], qn_ref[...], cos, sin, lo_lane)
    for h, q in enumerate(qs):
        q_out[:, h * HEAD_DIM:(h + 1) * HEAD_DIM] = (q * scale).astype(q_out.dtype)
    ks = _norm_rope_heads([head(k_ref, h) for h in range(ATT_KV_HEADS)], kn_ref[...], cos, sin, lo_lane)
    ones_col = (lane == 0).astype(v_out.dtype)
    for h, k in enumerate(ks):
        sl = slice(h * HEAD_DIM, (h + 1) * HEAD_DIM)
        kt_out[0, sl, :] = k.T.astype(kt_out.dtype)
        v_out[:, 2 * h * HEAD_DIM:(2 * h + 1) * HEAD_DIM] = v_ref[:, sl]
        v_out[:, (2 * h + 1) * HEAD_DIM:(2 * h + 2) * HEAD_DIM] = ones_col


def qkv_prep(p, q_norm, k_norm, B, n_lat, n_ctx):
    R = p.shape[0]
    T = n_lat + n_ctx
    tpb = T // ROW_TILE
    cos, sin = _rope_tables(n_lat, n_ctx)
    return pl.pallas_call(
        _qkv_prep_kernel,
        grid=(R // ROW_TILE,),
        in_specs=[pl.BlockSpec((ROW_TILE, ATT_WIDTH), lambda i: (i, EVEN_OFF_Q // ATT_WIDTH)),
                  pl.BlockSpec((ROW_TILE, ATT_KV_WIDTH), lambda i: (i, EVEN_OFF_K // ATT_KV_WIDTH)),
                  pl.BlockSpec((ROW_TILE, ATT_KV_WIDTH), lambda i: (i, EVEN_OFF_V // ATT_KV_WIDTH)),
                  pl.BlockSpec((ROW_TILE, HEAD_DIM), lambda i: (i % tpb, 0)),
                  pl.BlockSpec((ROW_TILE, HEAD_DIM), lambda i: (i % tpb, 0)),
                  pl.BlockSpec((1, HEAD_DIM), lambda i: (0, 0)),
                  pl.BlockSpec((1, HEAD_DIM), lambda i: (0, 0))],
        out_specs=[pl.BlockSpec((ROW_TILE, ATT_WIDTH), lambda i: (i, 0)),
                   pl.BlockSpec((1, ATT_KV_WIDTH, ROW_TILE), lambda i: (i // tpb, 0, i % tpb)),
                   pl.BlockSpec((ROW_TILE, 2 * ATT_KV_WIDTH), lambda i: (i, 0))],
        out_shape=[jax.ShapeDtypeStruct((R, ATT_WIDTH), bf16),
                   jax.ShapeDtypeStruct((B, ATT_KV_WIDTH, T), bf16),
                   jax.ShapeDtypeStruct((R, 2 * ATT_KV_WIDTH), bf16)],
        compiler_params=_params("arbitrary"),
        name="qkv_prep",
    )(p, p, p, jnp.asarray(cos), jnp.asarray(sin), q_norm.reshape(1, HEAD_DIM), k_norm.reshape(1, HEAD_DIM))


ATT_CHUNK = 256
ATT_Q_TILE = 1024


def _attention_kernel(k_lo, q_ref, kt_ref, v_ref, *rest):
    o_ref, qs_ref, sa_ref, sb_ref, pa_ref, pb_ref, os_ref = rest[-7:]
    tq = q_ref.shape[1]
    n_chunks = ATT_GROUP * tq // ATT_CHUNK
    for h in range(ATT_GROUP):
        qs_ref[h * tq:(h + 1) * tq, :] = q_ref[0, :, h * HEAD_DIM:(h + 1) * HEAD_DIM]

    rows = lambda c: slice(c * ATT_CHUNK, (c + 1) * ATT_CHUNK)
    s_refs = (sa_ref, sb_ref)
    p_refs = (pa_ref, pb_ref)

    def scores(c):
        s_refs[c % 2][:, k_lo:] = jnp.dot(qs_ref[rows(c), :], kt_ref[0, :, k_lo:], preferred_element_type=f32)

    def exponentials(c):
        s = s_refs[c % 2][:, k_lo:]
        m = jnp.max(s, axis=-1, keepdims=True)
        p_refs[c % 2][:, k_lo:] = jnp.exp2(s - m).astype(bf16)

    def weighted_values(c):
        r = jnp.dot(p_refs[c % 2][:, k_lo:], v_ref[0, k_lo:, :], preferred_element_type=f32)
        os_ref[rows(c), :] = r[:, :HEAD_DIM] / r[:, HEAD_DIM:HEAD_DIM + 1]

    scores(0)
    for c in range(n_chunks):
        if c + 1 < n_chunks:
            scores(c + 1)
        exponentials(c)
        if c >= 1:
            weighted_values(c - 1)
    weighted_values(n_chunks - 1)
    for h in range(ATT_GROUP):
        o_ref[0, :, h * HEAD_DIM:(h + 1) * HEAD_DIM] = os_ref[h * tq:(h + 1) * tq, :].astype(o_ref.dtype)


def attention(q, kt, v, B, n_lat, n_ctx):
    R = q.shape[0]
    T = n_lat + n_ctx
    gw = ATT_GROUP * HEAD_DIM
    q3 = q.reshape(B, T, ATT_WIDTH)
    v3 = v.reshape(B, T, 2 * ATT_KV_WIDTH)
    out_shape = jax.ShapeDtypeStruct((B, T, ATT_WIDTH), bf16)

    def call(name, k_lo, tq, q_tiles, first_tile, extra_in, extra_specs, aliases):
        qo_spec = pl.BlockSpec((1, tq, gw), lambda b, g, i: (b, first_tile + i, g))
        return pl.pallas_call(
            functools.partial(_attention_kernel, k_lo),
            grid=(B, ATT_KV_HEADS, q_tiles),
            in_specs=[qo_spec,
                      pl.BlockSpec((1, HEAD_DIM, T), lambda b, g, i: (b, g, 0)),
                      pl.BlockSpec((1, T, 2 * HEAD_DIM), lambda b, g, i: (b, 0, g))] + extra_specs,
            out_specs=qo_spec,
            out_shape=out_shape,
            scratch_shapes=[pltpu.VMEM((ATT_GROUP * tq, HEAD_DIM), bf16),
                            pltpu.VMEM((ATT_CHUNK, T), f32), pltpu.VMEM((ATT_CHUNK, T), f32),
                            pltpu.VMEM((ATT_CHUNK, T), bf16), pltpu.VMEM((ATT_CHUNK, T), bf16),
                            pltpu.VMEM((ATT_GROUP * tq, HEAD_DIM), f32)],
            input_output_aliases=aliases,
            compiler_params=_params("arbitrary", "arbitrary", "arbitrary"),
            name=name,
        )(q3, kt, v3, *extra_in)

    att = call("attention", 0, ATT_Q_TILE, n_lat // ATT_Q_TILE, 0, [], [], {})
    att = call("attention_ctx", n_lat, n_ctx, 1, n_lat // n_ctx, [att], [pl.BlockSpec(memory_space=pl.ANY)], {3: 0})
    return att.reshape(R, ATT_WIDTH)


def _post_residual(x, y, g_post, gate):
    yn = y * lax.rsqrt(jnp.mean(y * y, axis=-1, keepdims=True) + NORM_EPS) * g_post
    return x + gate * yn


def _silu(x):
    return x * jax.nn.sigmoid(x)


def _even_out_kernel(tiles_per_batch, x0_ref, g_ref, yc_ref, ghy_ref, att_ref, ga0_ref, ga1_ref, bias_ref, w_ref,
                     gp_ref, m_ref, x_ref, c_ref, gn_ref, mn_ref, o_ref, hn_ref):
    D = x_ref.shape[-1]
    g = g_ref[...]
    hy = x0_ref[...].astype(f32) * (yc_ref[...] + g * bias_ref[...]) * _silu(ghy_ref[...].astype(f32))
    g_att = jnp.concatenate([ga0_ref[...], ga1_ref[...]], axis=1).astype(f32)
    at = att_ref[...].astype(f32) * _silu(g_att)
    lhs = jnp.concatenate([hy, at], axis=1).astype(bf16)
    y = jnp.dot(lhs, w_ref[...], preferred_element_type=f32)
    x_new = _post_residual(_input_rows(tiles_per_batch, x_ref, c_ref), y, gp_ref[...], m_ref[0, :, 2 * D:3 * D])
    o_ref[...] = x_new
    hn_ref[...] = _norm_mod_rows(x_new, gn_ref[...], mn_ref[0]).astype(hn_ref.dtype)


def even_out(x0, g, yconv, p, att, hy_bias, w_out, g_post, mods, x_lat, x_ctx, next_g_pre, next_mods,
             tiles_per_batch, ctx_row):
    R = x0.shape[0]
    D = x_lat.shape[1]
    W = HY_WIDTH
    hw = ATT_WIDTH // 2
    row = lambda i: (i, 0)
    full = lambda i: (0, 0)
    lat_spec, ctx_spec = _input_row_specs(tiles_per_batch, D)
    return pl.pallas_call(
        functools.partial(_even_out_kernel, tiles_per_batch),
        grid=(R // ROW_TILE,),
        in_specs=[pl.BlockSpec((ROW_TILE, W), row),
                  pl.BlockSpec((ROW_TILE, W), row),
                  pl.BlockSpec((ROW_TILE, W), row),
                  pl.BlockSpec((ROW_TILE, W), lambda i: (i, EVEN_OFF_GHY // W)),
                  pl.BlockSpec((ROW_TILE, ATT_WIDTH), row),
                  pl.BlockSpec((ROW_TILE, hw), lambda i: (i, EVEN_OFF_GATT // hw)),
                  pl.BlockSpec((ROW_TILE, hw), lambda i: (i, EVEN_OFF_GATT // hw + 1)),
                  pl.BlockSpec((1, W), full),
                  pl.BlockSpec((W + ATT_WIDTH, D), full),
                  pl.BlockSpec((1, D), full),
                  pl.BlockSpec((1, 1, 3 * D), lambda i: (_mod_row(i, tiles_per_batch, ctx_row), 0, 0)),
                  lat_spec, ctx_spec,
                  pl.BlockSpec((1, D), full),
                  pl.BlockSpec((1, 1, 3 * D), lambda i: (_mod_row(i, tiles_per_batch, ctx_row), 0, 0))],
        out_specs=[pl.BlockSpec((ROW_TILE, D), row), pl.BlockSpec((ROW_TILE, D), row)],
        out_shape=[jax.ShapeDtypeStruct((R, D), f32), jax.ShapeDtypeStruct((R, D), bf16)],
        compiler_params=_params("arbitrary"),
        name="even_out",
    )(x0, g, yconv, p, att, p, p, hy_bias.reshape(1, W), w_out.astype(bf16), g_post.reshape(1, D), mods,
      x_lat, x_ctx, next_g_pre.reshape(1, D), next_mods)


def even_layer(x_lat, x_ctx, mods, next_g_pre, next_mods, B, n_lat, n_ctx, g_pre, g_post, w_in, conv_w, conv_b,
               f_w1, f_b1, f_freq, f_w2, f_b2, f_w3, hy_bias, q_norm, k_norm, w_out):
    T = n_lat + n_ctx
    tpb = T // ROW_TILE
    h = norm_mod(x_lat, x_ctx, g_pre, mods, tpb, B)
    p = matmul(h, w_in, tm=_proj_row_tile(B * T), tn=_proj_col_tile(w_in.shape[1]), out_dtype=bf16,
               name="even_in_proj")
    x0, g = hyena_pre(p, conv_w, conv_b, tpb)
    yconv = hyena_long_conv(g.reshape(B, T, HY_WIDTH), n_lat, n_ctx, f_w1, f_b1, f_freq, f_w2, f_b2, f_w3)
    q, kt, v = qkv_prep(p, q_norm, k_norm, B, n_lat, n_ctx)
    att = attention(q, kt, v, B, n_lat, n_ctx)
    return even_out(x0, g, yconv.reshape(B * T, HY_WIDTH), p, att, hy_bias, w_out, g_post, mods, x_lat, x_ctx,
                    next_g_pre, next_mods, tpb, B)


def _mlstm_prep_kernel(tiles_per_batch, q_ref, qp_ref, qn_ref, k_ref, kp_ref, kn_ref,
                       wq_ref, wk_ref, bq_ref, bk_ref, q_out, kt_out):
    first, last = _seq_edges(pl.program_id(0), tiles_per_batch)
    prev_row, next_row = _halo_rows(qp_ref, qn_ref, first, last)
    q = _silu(_conv3(q_ref[...].astype(f32), prev_row, next_row, wq_ref[...], bq_ref[...]))
    q_out[...] = q.astype(q_out.dtype)
    prev_row, next_row = _halo_rows(kp_ref, kn_ref, first, last)
    k = _silu(_conv3(k_ref[...].astype(f32), prev_row, next_row, wk_ref[...], bk_ref[...])) * (ML_QK ** -0.5)
    for h in range(k.shape[1] // ML_QK):
        sl = slice(h * ML_QK, (h + 1) * ML_QK)
        kt_out[0, sl, :] = k[:, sl].T.astype(kt_out.dtype)


def mlstm_prep(p, conv_w, conv_b, B, T, *, tc=1024):
    R = p.shape[0]
    tpb = T // ROW_TILE
    nb = ML_QK_WIDTH // tc
    qcol = lambda c: c
    kcol = lambda c: nb + c
    qprev, qnext = _halo_specs(qcol, tc, R)
    kprev, knext = _halo_specs(kcol, tc, R)
    return pl.pallas_call(
        functools.partial(_mlstm_prep_kernel, tpb),
        grid=(R // ROW_TILE, nb),
        in_specs=[pl.BlockSpec((ROW_TILE, tc), lambda i, c: (i, c)), qprev, qnext,
                  pl.BlockSpec((ROW_TILE, tc), lambda i, c: (i, nb + c)), kprev, knext,
                  pl.BlockSpec((3, tc), lambda i, c: (0, c)),
                  pl.BlockSpec((3, tc), lambda i, c: (0, nb + c)),
                  pl.BlockSpec((1, tc), lambda i, c: (0, c)),
                  pl.BlockSpec((1, tc), lambda i, c: (0, nb + c))],
        out_specs=[pl.BlockSpec((ROW_TILE, tc), lambda i, c: (i, c)),
                   pl.BlockSpec((1, tc, ROW_TILE), lambda i, c: (i // tpb, c, i % tpb))],
        out_shape=[jax.ShapeDtypeStruct((R, ML_QK_WIDTH), bf16),
                   jax.ShapeDtypeStruct((B, ML_QK_WIDTH, T), bf16)],
        compiler_params=_params("arbitrary", "arbitrary"),
        name="mlstm_prep",
    )(p, p, p, p, p, p, conv_w, conv_w, conv_b.reshape(1, -1), conv_b.reshape(1, -1))


def _log_sigmoid(x):
    return jnp.minimum(x, 0.0) - jnp.log(1.0 + jnp.exp(-jnp.abs(x)))


def _mlstm_gates_kernel(g_ref, b_ref, gc_out, gr_out):
    pre = g_ref[...] + b_ref[...]
    lane = lax.broadcasted_iota(jnp.int32, pre.shape, 1)
    is_forget = (lane // ML_HEADS) % 2 == 1
    gc = jnp.where(is_forget, _log_sigmoid(pre), pre)
    gc_out[...] = gc
    gr_out[0] = gc.T


def mlstm_gates(gates, gate_b, B, T):
    R = gates.shape[0]
    gb = jnp.pad(gate_b, (0, LANE - gate_b.shape[0])).reshape(1, LANE)
    return pl.pallas_call(
        _mlstm_gates_kernel,
        grid=(B,),
        in_specs=[pl.BlockSpec((T, LANE), lambda b: (b, 0)),
                  pl.BlockSpec((1, LANE), lambda b: (0, 0))],
        out_specs=[pl.BlockSpec((T, LANE), lambda b: (b, 0)),
                   pl.BlockSpec((1, LANE, T), lambda b: (b, 0, 0))],
        out_shape=[jax.ShapeDtypeStruct((R, LANE), f32), jax.ShapeDtypeStruct((B, LANE, T), f32)],
        compiler_params=_params("arbitrary"),
        name="mlstm_gates",
    )(gates, gb)


def _mlstm_chunk_setup(reverse, gc_ref, gr_ref, m_ref, ms_ref):
    Lc = gc_ref.shape[0]
    H = ML_HEADS
    i_off = 2 * H if reverse else 0
    f_off = i_off + H
    t_idx = lax.broadcasted_iota(jnp.int32, (Lc, Lc), 0)
    s_idx = lax.broadcasted_iota(jnp.int32, (Lc, Lc), 1)
    causal = (s_idx >= t_idx) if reverse else (s_idx <= t_idx)
    tri = causal.astype(f32)
    gc = gc_ref[...]
    gr = gr_ref[0]
    b_col_all = jnp.dot(tri, gc[:, f_off:f_off + H], preferred_element_type=f32, precision=HIGHEST)
    b_row_all = lax.dot_general(gr[f_off:f_off + H, :], tri, (((1,), (1,)), ((), ())),
                                preferred_element_type=f32, precision=HIGHEST)
    i_col_all = gc[:, i_off:i_off + H]
    i_row_all = gr[i_off:i_off + H, :]
    end = 0 if reverse else Lc - 1
    b_end = b_col_all[end:end + 1, :]
    m_prev = m_ref[0:1, 0:H]
    g_col = b_end - b_col_all + i_col_all
    m_new = jnp.maximum(b_end + m_prev, jnp.max(g_col, axis=0, keepdims=True))
    a_prev = jnp.exp(b_end + m_prev - m_new)
    m_ref[:, 0:H] = jnp.broadcast_to(m_new, (m_ref.shape[0], H))
    b_end_s = b_row_all[:, end:end + 1]
    m_prev_s = ms_ref[0:H, 0:1]
    g_row = b_end_s - b_row_all + i_row_all
    m_new_s = jnp.maximum(b_end_s + m_prev_s, jnp.max(g_row, axis=1, keepdims=True))
    a_row = jnp.exp(g_row - m_new_s)
    ms_ref[0:H, :] = jnp.broadcast_to(m_new_s, (H, ms_ref.shape[1]))
    return dict(causal=causal, i_row=i_row_all, b_col=b_col_all, b_row=b_row_all,
                m_prev=m_prev, a_row=a_row, a_prev=a_prev)


class _MlstmChain:
    def __init__(self, h, cs, q_ref, kt_ref, v_ref, o_ref, ct_ref):
        self.h, self.cs = h, cs
        self.q_ref, self.kt_ref, self.v_ref, self.o_ref, self.ct_ref = q_ref, kt_ref, v_ref, o_ref, ct_ref

    def _q(self):
        return self.q_ref[:, self.h * ML_QK:(self.h + 1) * ML_QK]

    def _kt(self):
        return self.kt_ref[0, self.h * ML_QK:(self.h + 1) * ML_QK, :]

    def _v(self):
        Lc = self.q_ref.shape[0]
        ones_col = (lax.broadcasted_iota(jnp.int32, (Lc, LANE), 1) == 0).astype(bf16)
        return jnp.concatenate([self.v_ref[:, self.h * ML_V:(self.h + 1) * ML_V], ones_col], axis=1)

    def scores(self):
        self.qk = jnp.dot(self._q(), self._kt(), preferred_element_type=f32)

    def gates(self):
        h, cs = self.h, self.cs
        i_row = cs["i_row"][h:h + 1, :]
        b_col = cs["b_col"][:, h:h + 1]
        b_row = cs["b_row"][h:h + 1, :]
        m_prev = cs["m_prev"][:, h:h + 1]
        d = jnp.where(cs["causal"], b_col + (i_row - b_row), -jnp.inf)
        inter = b_col + m_prev
        self.m_row = jnp.maximum(inter, jnp.max(d, axis=-1, keepdims=True))
        self.s = (self.qk * jnp.exp(d - self.m_row)).astype(bf16)
        self.w_prev = jnp.exp(inter - self.m_row)

    def values(self):
        h = self.h
        self.ct = self.ct_ref[h]
        qw = (self._q().astype(f32) * self.w_prev).astype(bf16)
        tot = jnp.dot(jnp.concatenate([self.s, qw], axis=1),
                      jnp.concatenate([self._v(), self.ct.astype(bf16)], axis=0),
                      preferred_element_type=f32)
        scale = 1.0 / jnp.maximum(jnp.abs(tot[:, ML_V:ML_V + 1]), jnp.exp(-self.m_row))
        self.o_ref[:, h * ML_V:(h + 1) * ML_V] = (tot[:, :ML_V] * scale).astype(self.o_ref.dtype)

    def update(self):
        h = self.h
        kta = (self._kt().astype(f32) * self.cs["a_row"][h:h + 1, :]).astype(bf16)
        self.ct_ref[h] = (self.cs["a_prev"][:, h:h + 1] * self.ct
                          + jnp.dot(kta, self._v(), preferred_element_type=f32))


def _mlstm_scan_kernel(qf_ref, ktf_ref, vf_ref, gcf_ref, grf_ref, qb_ref, ktb_ref, vb_ref, gcb_ref, grb_ref,
                       of_ref, ob_ref, ctf_ref, mf_ref, msf_ref, ctb_ref, mb_ref, msb_ref):
    @pl.when(pl.program_id(1) == 0)
    def _():
        for ref in (ctf_ref, mf_ref, msf_ref, ctb_ref, mb_ref, msb_ref):
            ref[...] = jnp.zeros_like(ref)

    fwd = _mlstm_chunk_setup(False, gcf_ref, grf_ref, mf_ref, msf_ref)
    bwd = _mlstm_chunk_setup(True, gcb_ref, grb_ref, mb_ref, msb_ref)
    chains = []
    for h in range(ML_HEADS):
        chains.append(_MlstmChain(h, fwd, qf_ref, ktf_ref, vf_ref, of_ref, ctf_ref))
        chains.append(_MlstmChain(h, bwd, qb_ref, ktb_ref, vb_ref, ob_ref, ctb_ref))
    stages = ("scores", "gates", "values", "update")
    for k in range(len(chains) + len(stages) - 1):
        for depth, stage in enumerate(stages):
            if 0 <= k - depth < len(chains):
                getattr(chains[k - depth], stage)()


def mlstm_scan(q, kt, p, gc, gr, B, n_lat, n_ctx):
    R = q.shape[0]
    Lc = ML_CHUNK
    tpb = (n_lat + n_ctx) // Lc
    lat = n_lat // Lc
    ctx = n_ctx // Lc

    def specs(reverse):
        def chunk(j):
            if reverse:
                return tpb - 1 - j
            return jnp.where(j < ctx, lat + j, j - ctx)
        ins = [pl.BlockSpec((Lc, ML_QK_WIDTH), lambda b, j: (b * tpb + chunk(j), 0)),
               pl.BlockSpec((1, ML_QK_WIDTH, Lc), lambda b, j: (b, 0, chunk(j))),
               pl.BlockSpec((Lc, ML_WIDTH), lambda b, j: (b * tpb + chunk(j), ODD_OFF_V // ML_WIDTH)),
               pl.BlockSpec((Lc, LANE), lambda b, j: (b * tpb + chunk(j), 0)),
               pl.BlockSpec((1, LANE, Lc), lambda b, j: (b, 0, chunk(j)))]
        out = pl.BlockSpec((Lc, ML_WIDTH), lambda b, j: (b * tpb + chunk(j), 0))
        return ins, out

    ins_f, out_f = specs(False)
    ins_b, out_b = specs(True)
    state = [pltpu.VMEM((ML_HEADS, ML_QK, ML_V + LANE), f32), pltpu.VMEM((SUB, LANE), f32),
             pltpu.VMEM((SUB, LANE), f32)]
    return pl.pallas_call(
        _mlstm_scan_kernel,
        grid=(B, tpb),
        in_specs=ins_f + ins_b,
        out_specs=[out_f, out_b],
        out_shape=[jax.ShapeDtypeStruct((R, ML_WIDTH), bf16), jax.ShapeDtypeStruct((R, ML_WIDTH), bf16)],
        scratch_shapes=state + state,
        compiler_params=_params("arbitrary", "arbitrary"),
        name="mlstm_scan",
    )(q, kt, p, gc, gr, q, kt, p, gc, gr)


def _odd_out_kernel(hf_ref, hb_ref, o_ref, z_ref, hn_ref, w_ref, gp_ref, m_ref, x_ref, out_ref):
    D = x_ref.shape[-1]
    hs = (hf_ref[0].astype(f32) + hb_ref[0].astype(f32)) * jax.nn.sigmoid(o_ref[0].astype(f32))
    segs = [hs[:, h * ML_V:(h + 1) * ML_V] for h in range(ML_HEADS)]
    ms = [jnp.mean(seg * seg, axis=-1, keepdims=True) for seg in segs]
    parts = [seg * lax.rsqrt(m + NORM_EPS) for seg, m in zip(segs, ms)]
    hn = jnp.concatenate(parts, axis=1) * hn_ref[...] * _silu(z_ref[0].astype(f32))
    y = jnp.dot(hn.astype(bf16), w_ref[...], preferred_element_type=f32)
    out_ref[0] = _post_residual(x_ref[0], y, gp_ref[...], m_ref[0, :, 2 * D:3 * D])


def odd_out(hf, hb, p, head_norm, w_out, g_post, mods, x_all, B, n_lat, T, *, tm=512):
    D = x_all.shape[1]
    view = lambda a: a.reshape(B, T, a.shape[1])
    row = lambda b, i: (b, i, 0)
    full = lambda b, i: (0, 0)
    return pl.pallas_call(
        _odd_out_kernel,
        grid=(B, n_lat // tm),
        in_specs=[pl.BlockSpec((1, tm, ML_WIDTH), row),
                  pl.BlockSpec((1, tm, ML_WIDTH), row),
                  pl.BlockSpec((1, tm, ML_WIDTH), lambda b, i: (b, i, ODD_OFF_O // ML_WIDTH)),
                  pl.BlockSpec((1, tm, ML_WIDTH), lambda b, i: (b, i, ODD_OFF_Z // ML_WIDTH)),
                  pl.BlockSpec((1, ML_WIDTH), full),
                  pl.BlockSpec((ML_WIDTH, D), full),
                  pl.BlockSpec((1, D), full),
                  pl.BlockSpec((1, 1, 3 * D), lambda b, i: (b, 0, 0)),
                  pl.BlockSpec((1, tm, D), row)],
        out_specs=pl.BlockSpec((1, tm, D), row),
        out_shape=jax.ShapeDtypeStruct((B, n_lat, D), f32),
        compiler_params=_params("arbitrary", "arbitrary"),
        name="odd_out",
    )(view(hf), view(hb), view(p), view(p), head_norm.reshape(1, ML_WIDTH), w_out.astype(bf16),
      g_post.reshape(1, D), mods, view(x_all))


def odd_layer_last(x_all, h, mods, B, n_lat, n_ctx, g_post, w_in, conv_w, conv_b, gate_b, head_norm, w_out):
    T = n_lat + n_ctx
    tm = _proj_row_tile(B * T)
    n_gates = w_in.shape[-1] - ODD_MAIN
    w_gates = jnp.pad(w_in[0, :, ODD_MAIN:], ((0, 0), (0, LANE - n_gates)))
    p, gates = matmul_with_side(h, w_in, w_gates, tm=tm, tn=_proj_col_tile(ODD_MAIN), n_cols=ODD_MAIN,
                                out_dtype=bf16, name="odd_in_proj")
    q, kt = mlstm_prep(p, conv_w, conv_b, B, T)
    gc, gr = mlstm_gates(gates, gate_b, B, T)
    hf, hb = mlstm_scan(q, kt, p, gc, gr, B, n_lat, n_ctx)
    return odd_out(hf, hb, p, head_norm, w_out, g_post, mods, x_all, B, n_lat, T)


def kernel(x, c, ctx, c_ctx, w_mod, b_mod, g_pre, g_post, e_w_in, e_conv_w, e_conv_b, e_filt_w1,
           e_filt_b1, e_filt_freq, e_filt_w2, e_filt_b2, e_filt_w3, e_hy_bias, e_q_norm, e_k_norm,
           e_w_out, o_w_in, o_conv_w, o_conv_b, o_gate_b, o_head_norm, o_w_out):
    B, n_lat, D = x.shape
    n_ctx = ctx.shape[1]
    T = n_lat + n_ctx
    depth = w_mod.shape[0]
    assert depth == 2 and B + 1 <= 8 and n_ctx == ROW_TILE and n_lat % ROW_TILE == 0
    cond = jnp.concatenate([c, c_ctx[None], jnp.zeros((8 - B - 1, D), f32)], axis=0)
    mods_all = adaln_all(cond, w_mod, b_mod)
    mods0 = mods_all[0].reshape(8, 1, 3 * D)
    mods1 = mods_all[1].reshape(8, 1, 3 * D)
    x_all, h1 = even_layer(x.reshape(B * n_lat, D), ctx.reshape(B * n_ctx, D), mods0, g_pre[1], mods1,
                           B, n_lat, n_ctx, g_pre[0], g_post[0], e_w_in[0],
                           e_conv_w[0], e_conv_b[0], e_filt_w1[0], e_filt_b1[0], e_filt_freq[0], e_filt_w2[0],
                           e_filt_b2[0], e_filt_w3[0], e_hy_bias[0], e_q_norm[0], e_k_norm[0], e_w_out[0])
    out = odd_layer_last(x_all, h1, mods1, B, n_lat, n_ctx, g_post[1], o_w_in,
                         o_conv_w[0], o_conv_b[0], o_gate_b[0], o_head_norm[0], o_w_out[0])
    return out.reshape(B, n_lat, D)
```
